```python
import math
import jax, jax.numpy as jnp
from jax import lax
import numpy as np

D_MODEL = 2048
BATCH = 4
SEQ = 2048
DEPTH = 2

D_MIX = D_MODEL
CONV_WIDTH = D_MODEL // 4
CONV_K = 3
RET_HEAD_DIM = 128
RET_WIDTH = 3 * D_MODEL // 8
RET_HEADS = RET_WIDTH // RET_HEAD_DIM
HG_HEAD_DIM = 128
HG_WIDTH = D_MIX - CONV_WIDTH - RET_WIDTH
HG_HEADS = HG_WIDTH // HG_HEAD_DIM
IN_COLS = 3 * CONV_WIDTH + 4 * RET_WIDTH + 5 * HG_WIDTH
RET_CHUNK = 128
HG_CHUNK = 64
ROPE_BASE = 10000.0
N_EXPERTS = 16
N_GROUPS = 4
EXPERTS_PER_GROUP = N_EXPERTS // N_GROUPS
TOP_K = 2
D_EXPERT = 1408
MASK_NEG = -1e9
DEEPNORM_ALPHA = (2.0 * DEPTH) ** 0.25
DEEPNORM_BETA = (8.0 * DEPTH) ** -0.25
LN_EPS = 1e-5
HEAD_EPS = 1e-6
FORGET_FLOOR = 1e-6

kernel_name = "hybrid_conv_retention_hgrn2_groupmoe_encoder"


def _layer_norm(x, g, b):
    xf = x.astype(jnp.float32)
    mu = jnp.mean(xf, axis=-1, keepdims=True)
    var = jnp.mean(jnp.square(xf - mu), axis=-1, keepdims=True)
    y = (xf - mu) * lax.rsqrt(var + LN_EPS)
    return (y * g.astype(jnp.float32) + b.astype(jnp.float32)).astype(x.dtype)


def _head_group_norm(x):
    xf = x.astype(jnp.float32)
    mu = jnp.mean(xf, axis=-1, keepdims=True)
    var = jnp.mean(jnp.square(xf - mu), axis=-1, keepdims=True)
    return ((xf - mu) * lax.rsqrt(var + HEAD_EPS)).astype(x.dtype)


def _head_rms_norm(x):
    xf = x.astype(jnp.float32)
    return (xf * lax.rsqrt(jnp.mean(jnp.square(xf), axis=-1, keepdims=True) + HEAD_EPS)).astype(x.dtype)


def _heads(x, n_heads):
    b, t, w = x.shape
    return x.reshape(b, t, n_heads, w // n_heads).transpose(0, 2, 1, 3)


def _merge_heads(x):
    b, h, t, d = x.shape
    return x.transpose(0, 2, 1, 3).reshape(b, t, h * d)


def _flip(x):
    return jnp.flip(x, axis=2)


def _rotary(x):
    t, d = x.shape[2], x.shape[3]
    half = d // 2
    inv_freq = ROPE_BASE ** (-jnp.arange(half, dtype=jnp.float32) / half)
    ang = jnp.arange(t, dtype=jnp.float32)[:, None] * inv_freq[None, :]
    cos = jnp.cos(ang).astype(x.dtype)
    sin = jnp.sin(ang).astype(x.dtype)
    x1, x2 = x[..., :half], x[..., half:]
    return jnp.concatenate([x1 * cos - x2 * sin, x1 * sin + x2 * cos], axis=-1)


def _retention_dir(q, k, v, log_gamma):
    b, h, t, d = q.shape
    L = RET_CHUNK
    n = t // L
    qc = q.reshape(b, h, n, L, d)
    kc = k.reshape(b, h, n, L, d)
    vc = v.reshape(b, h, n, L, d)
    pos = jnp.arange(L, dtype=jnp.float32)
    lg = log_gamma[:, None]
    rel = pos[:, None] - pos[None, :]
    decay = jnp.where(rel >= 0, jnp.exp(lg[:, :, None] * jnp.maximum(rel, 0.0)), 0.0)
    scores = jnp.einsum('bhntd,bhnsd->bhnts', qc, kc) * decay[None, :, None]
    intra = jnp.einsum('bhnts,bhnse->bhnte', scores, vc)
    k_tail = kc * jnp.exp(lg * (L - 1.0 - pos))[None, :, None, :, None]
    chunk_state = jnp.einsum('bhnsd,bhnse->nbhde', k_tail, vc)
    chunk_decay = jnp.exp(log_gamma * L)[None, :, None, None].astype(chunk_state.dtype)

    def step(state, cs):
        return chunk_decay * state + cs, state

    _, prev = lax.scan(step, jnp.zeros_like(chunk_state[0]), chunk_state)
    q_head = qc * jnp.exp(lg * (pos + 1.0))[None, :, None, :, None]
    inter = jnp.einsum('bhntd,nbhde->bhnte', q_head, prev)
    return (intra + inter).reshape(b, h, t, d)


def _hgrn2_dir(q, k, v, logf):
    b, h, t, dk = q.shape
    dv = v.shape[-1]
    L = HG_CHUNK
    n = t // L

    def chunks(a):
        return jnp.moveaxis(a.reshape(b, h, n, L, a.shape[-1]), 2, 0)

    incl = jnp.tril(jnp.ones((L, L), dtype=bool))[:, :, None]

    def step(state, inp):
        qc, kc, vc, gc = inp
        cum = jnp.cumsum(gc, axis=2)
        diff = cum[:, :, :, None, :] - cum[:, :, None, :, :]
        decay = jnp.where(incl, jnp.exp(jnp.where(incl, diff, 0.0)), 0.0)
        scores = jnp.einsum('bhtk,bhsk,bhtsk->bhts', qc, kc, decay)
        intra = jnp.einsum('bhts,bhsv->bhtv', scores, vc)
        inter = jnp.einsum('bhtk,bhkv->bhtv', qc * jnp.exp(cum), state)
        total = cum[:, :, -1:, :]
        k_tail = kc * jnp.exp(total - cum)
        new_state = jnp.exp(total[:, :, 0, :, None]) * state + jnp.einsum('bhsk,bhsv->bhkv', k_tail, vc)
        return new_state, intra + inter

    init = jnp.zeros((b, h, dk, dv), jnp.float32)
    _, out = lax.scan(step, init, (chunks(q), chunks(k), chunks(v), chunks(logf)))
    return jnp.moveaxis(out, 0, 2).reshape(b, h, t, dv)


def _log_forget(z, lb):
    lb = lb.astype(jnp.float32)
    f = lb + (1.0 - lb) * jax.nn.sigmoid(z.astype(jnp.float32))
    return jnp.log(jnp.maximum(f, FORGET_FLOOR))


def _split_cols(proj):
    widths = [CONV_WIDTH] * 3 + [RET_WIDTH] * 4 + [HG_WIDTH] * 5
    offsets = []
    acc = 0
    for w in widths[:-1]:
        acc += w
        offsets.append(acc)
    return jnp.split(proj, offsets, axis=-1)


def _mixer(h, w_in, conv_w, beta, w_out, lb):
    proj = h @ w_in
    (cb, cc, ch, rq, rk, rv, rg, hq, hf_fwd, hf_bwd, hi, hg) = _split_cols(proj)

    u = cc * ch
    up = jnp.pad(u, ((0, 0), (1, 1), (0, 0)))
    y_conv = cb * (up[:, :-2] * conv_w[0] + up[:, 1:-1] * conv_w[1] + up[:, 2:] * conv_w[2])

    head_idx = jnp.arange(RET_HEADS, dtype=jnp.float32)
    lg_fwd = jnp.log1p(-jnp.exp2(-5.0 - head_idx))
    lg_bwd = jnp.log1p(-jnp.exp2(-5.5 - head_idx))
    q = _rotary(_heads(rq, RET_HEADS))
    k = _rotary(_heads(rk, RET_HEADS)) * (RET_HEAD_DIM ** -0.5)
    v = _heads(rv, RET_HEADS)
    r_fwd = _retention_dir(q, k, v, lg_fwd)
    r_bwd = _flip(_retention_dir(_flip(q), _flip(k), _flip(v), lg_bwd))
    y_ret = _merge_heads(_head_group_norm(r_fwd + r_bwd)) * jax.nn.silu(rg)

    gq = _heads(hq, HG_HEADS)
    gv = _heads(hi, HG_HEADS)
    lf_f = _heads(_log_forget(hf_fwd, lb[0]), HG_HEADS)
    lf_b = _heads(_log_forget(hf_bwd, lb[1]), HG_HEADS)
    k_f = -jnp.expm1(lf_f)
    k_b = -jnp.expm1(lf_b)
    o_f = _hgrn2_dir(gq, k_f, gv, lf_f)
    o_b = _flip(_hgrn2_dir(_flip(gq), _flip(k_b), _flip(gv), _flip(lf_b)))
    y_hg = _merge_heads(_head_rms_norm(o_f + o_b)) * jax.nn.silu(hg)

    mixed = jnp.concatenate([y_conv, y_ret, y_hg], axis=-1) * beta
    return mixed @ w_out


def _moe(h, w_router, router_bias, w_gate, w_up, w_down):
    b, t, d = h.shape
    tok = h.reshape(b * t, d)
    scores = jax.nn.softmax((tok @ w_router).astype(jnp.float32), axis=-1)
    sel = scores + router_bias.astype(jnp.float32)
    grouped = sel.reshape(-1, N_GROUPS, EXPERTS_PER_GROUP)
    group_score = lax.top_k(grouped, TOP_K)[0].sum(-1)
    best_group = jnp.argmax(group_score, axis=-1)
    in_group = best_group[:, None] == (jnp.arange(N_EXPERTS) // EXPERTS_PER_GROUP)[None, :]
    _, idx = lax.top_k(jnp.where(in_group, sel, MASK_NEG), TOP_K)
    w = jnp.take_along_axis(scores, idx, axis=-1)
    w = w / jnp.sum(w, axis=-1, keepdims=True)
    gates = jnp.sum(jax.nn.one_hot(idx, N_EXPERTS, dtype=jnp.float32) * w[..., None], axis=1)
    out = jnp.zeros_like(tok)
    for e in range(N_EXPERTS):
        hidden = jax.nn.silu(tok @ w_gate[e]) * (tok @ w_up[e])
        out = out + gates[:, e:e + 1].astype(tok.dtype) * (hidden @ w_down[e])
    return out.reshape(b, t, d)


def setup_inputs(seed: int = 0) -> dict:
    key = jax.random.key(seed)
    ks = jax.random.split(key, 20)
    f32 = jnp.float32
    nrm = lambda k, s: jax.random.normal(k, s, f32)
    D = D_MODEL
    return {
        "x": nrm(ks[0], (BATCH, SEQ, D)),
        "c": nrm(ks[1], (BATCH, D)),
        "emb_ln_g": 1.0 + 0.02 * nrm(ks[2], (D,)),
        "emb_ln_b": 0.02 * nrm(ks[3], (D,)),
        "w_ada": nrm(ks[4], (DEPTH, D, 6 * D)) * (0.1 * D ** -0.5),
        "b_ada": 0.01 * nrm(ks[5], (DEPTH, 6 * D)),
        "w_in": nrm(ks[6], (DEPTH, D, IN_COLS)) * (D ** -0.5),
        "conv_w": nrm(ks[7], (DEPTH, CONV_K, CONV_WIDTH)) * (CONV_K ** -0.5),
        "mix_beta": 1.0 + 0.02 * nrm(ks[8], (DEPTH, D_MIX)),
        "w_out": nrm(ks[9], (DEPTH, D_MIX, D)) * (D_MIX ** -0.5 * DEEPNORM_BETA),
        "hg_lb_logits": nrm(ks[10], (DEPTH, 2, HG_WIDTH)),
        "ln_g": 1.0 + 0.02 * nrm(ks[11], (DEPTH, 2, D)),
        "ln_b": 0.02 * nrm(ks[12], (DEPTH, 2, D)),
        "w_router": nrm(ks[13], (D, N_EXPERTS)) * (D ** -0.5),
        "router_bias": 0.01 * nrm(ks[14], (N_EXPERTS,)),
        "w_gate": nrm(ks[15], (DEPTH, N_EXPERTS, D, D_EXPERT)) * (D ** -0.5),
        "w_up": nrm(ks[16], (DEPTH, N_EXPERTS, D, D_EXPERT)) * (D ** -0.5),
        "w_down": nrm(ks[17], (DEPTH, N_EXPERTS, D_EXPERT, D)) * (D_EXPERT ** -0.5 * DEEPNORM_BETA),
    }


def reference(x, c, emb_ln_g, emb_ln_b, w_ada, b_ada, w_in, conv_w, mix_beta, w_out,
              hg_lb_logits, ln_g, ln_b, w_router, router_bias, w_gate, w_up, w_down):
    p = jax.nn.softmax(hg_lb_logits.astype(jnp.float32), axis=0)
    lb_all = jnp.cumsum(p, axis=0) - p[0:1]
    cond = jax.nn.silu(c)
    x = _layer_norm(x, emb_ln_g, emb_ln_b)
    for l in range(DEPTH):
        mod = (cond @ w_ada[l] + b_ada[l])[:, None, :]
        shift1, scale1, gate1, shift2, scale2, gate2 = jnp.split(mod, 6, axis=-1)
        h = x * (1.0 + scale1) + shift1
        y = _mixer(h, w_in[l], conv_w[l], mix_beta[l], w_out[l], lb_all[l])
        x = _layer_norm(DEEPNORM_ALPHA * x + (1.0 + gate1) * y, ln_g[l, 0], ln_b[l, 0])
        h = x * (1.0 + scale2) + shift2
        y = _moe(h, w_router, router_bias, w_gate[l], w_up[l], w_down[l])
        x = _layer_norm(DEEPNORM_ALPHA * x + (1.0 + gate2) * y, ln_g[l, 1], ln_b[l, 1])
    return x
```

```python
import functools
import math

import jax
import jax.numpy as jnp
from jax import lax
from jax.experimental import pallas as pl
from jax.experimental.pallas import tpu as pltpu

F32 = jnp.float32
BF16 = jnp.bfloat16
I32 = jnp.int32

DEPTH = 2
CONV_WIDTH = 512
RET_WIDTH = 768
HG_WIDTH = 768
HEAD_DIM = 128
RET_HEADS = RET_WIDTH // HEAD_DIM
HG_HEADS = HG_WIDTH // HEAD_DIM
IN_COLS = 3 * CONV_WIDTH + 4 * RET_WIDTH + 5 * HG_WIDTH
ROPE_BASE = 10000.0
N_EXPERTS = 16
N_GROUPS = 4
EXPERTS_PER_GROUP = N_EXPERTS // N_GROUPS
MASK_NEG = -1e9
DEEPNORM_ALPHA = (2.0 * DEPTH) ** 0.25
LN_EPS = 1e-5
HEAD_EPS = 1e-6
FORGET_FLOOR = 1e-6

LANES = 128
SUBLANES = 8

_CB, _CC, _CH = 0, 4, 8
_RQ, _RK, _RV, _RG = 12, 18, 24, 30
_HQ, _HFF, _HFB, _HI, _HG = 36, 42, 48, 54, 60
_OUT_RET, _OUT_HG = 4, 10

SEQ_CHUNK = 128
MOE_TILE = 256
VMEM_LIMIT = 56 * 1024 * 1024


def _cparams(sem, vmem=None):
    return pltpu.CompilerParams(dimension_semantics=sem, vmem_limit_bytes=vmem)


def _silu(x):
    return x * jax.nn.sigmoid(x)


def _dot(a, b):
    return jnp.dot(a, b, preferred_element_type=F32)


def _dot_nt(a, b):
    return lax.dot_general(a, b, (((1,), (1,)), ((), ())), preferred_element_type=F32)


def _dot_tn(a, b):
    return lax.dot_general(a, b, (((0,), (0,)), ((), ())), preferred_element_type=F32)


def _ada_kernel(c_ref, w_ref, b_ref, o_ref):
    cond = _silu(c_ref[...])
    o_ref[0] = _dot(cond.astype(BF16), w_ref[0].astype(BF16)) + b_ref[0]


def _ada_mod(c, w_ada, b_ada):
    depth, d, n6 = w_ada.shape
    b = c.shape[0]
    bp = -(-b // SUBLANES) * SUBLANES
    cp = jnp.pad(c, ((0, bp - b), (0, 0)))
    tn = 512
    out = pl.pallas_call(
        _ada_kernel,
        grid=(depth, n6 // tn),
        in_specs=[
            pl.BlockSpec((bp, d), lambda l, j: (0, 0)),
            pl.BlockSpec((1, d, tn), lambda l, j: (l, 0, j)),
            pl.BlockSpec((1, 1, tn), lambda l, j: (l, 0, j)),
        ],
        out_specs=pl.BlockSpec((1, bp, tn), lambda l, j: (l, 0, j)),
        out_shape=jax.ShapeDtypeStruct((depth, bp, n6), F32),
        compiler_params=_cparams(("arbitrary", "arbitrary")),
    )(cp, w_ada, b_ada.reshape(depth, 1, n6))
    return out[:, :b, :]


def _layer_norm_rows(z, g, b):
    mu = jnp.mean(z, axis=-1, keepdims=True)
    zc = z - mu
    var = jnp.mean(zc * zc, axis=-1, keepdims=True)
    return zc * lax.rsqrt(var + LN_EPS) * g + b


def _ln_mod_kernel(x_ref, g_ref, b_ref, sc_ref, sh_ref, xo_ref, h_ref):
    y = _layer_norm_rows(x_ref[0], g_ref[...], b_ref[...])
    xo_ref[0] = y
    h_ref[0] = (y * (1.0 + sc_ref[0]) + sh_ref[0]).astype(BF16)


def _ln_mod(x, g, b, scale, shift):
    bsz, t, d = x.shape
    tr = 512
    row = pl.BlockSpec((1, tr, d), lambda i, j: (i, j, 0))
    vec = pl.BlockSpec((1, d), lambda i, j: (0, 0))
    mod = pl.BlockSpec((1, 1, d), lambda i, j: (i, 0, 0))
    return pl.pallas_call(
        _ln_mod_kernel,
        grid=(bsz, t // tr),
        in_specs=[row, vec, vec, mod, mod],
        out_specs=[row, row],
        out_shape=[jax.ShapeDtypeStruct((bsz, t, d), F32),
                   jax.ShapeDtypeStruct((bsz, t, d), BF16)],
        compiler_params=_cparams(("arbitrary", "arbitrary")),
    )(x, g.reshape(1, d), b.reshape(1, d), scale, shift)


def _mm_kernel(a_ref, b_ref, o_ref):
    o_ref[...] = _dot(a_ref[...], b_ref[...]).astype(o_ref.dtype)


def _in_proj(h, w, out_dtype):
    n, d = h.shape
    nc = w.shape[1]
    tm, tn = 1024, 768
    tm = min(tm, n)
    return pl.pallas_call(
        _mm_kernel,
        grid=(n // tm, nc // tn),
        in_specs=[pl.BlockSpec((tm, d), lambda i, j: (i, 0)),
                  pl.BlockSpec((d, tn), lambda i, j: (0, j))],
        out_specs=pl.BlockSpec((tm, tn), lambda i, j: (i, j)),
        out_shape=jax.ShapeDtypeStruct((n, nc), out_dtype),
        compiler_params=_cparams(("arbitrary", "arbitrary"), VMEM_LIMIT),
    )(h, w)


def _conv_kernel(cb_ref, cc_ref, ch_ref, w_ref, beta_ref, o_ref):
    u = cc_ref[0].astype(F32) * ch_ref[0].astype(F32)
    t = u.shape[0]
    row = lax.broadcasted_iota(I32, u.shape, 0)
    prev = jnp.where(row == 0, 0.0, pltpu.roll(u, 1, 0))
    nxt = jnp.where(row == t - 1, 0.0, pltpu.roll(u, t - 1, 0))
    w = w_ref[...]
    y = cb_ref[0].astype(F32) * (prev * w[0:1] + u * w[1:2] + nxt * w[2:3])
    o_ref[0] = (y * beta_ref[...]).astype(o_ref.dtype)


def _conv_mixer(proj, conv_w, beta):
    bsz, t, _ = proj.shape
    cw = 256
    nb = CONV_WIDTH // cw
    per = LANES * 1

    def col(off):
        return pl.BlockSpec((1, t, cw), lambda b, j, off=off: (b, 0, off * per // cw + j))

    return pl.pallas_call(
        _conv_kernel,
        grid=(bsz, nb),
        in_specs=[col(_CB), col(_CC), col(_CH),
                  pl.BlockSpec((3, cw), lambda b, j: (0, j)),
                  pl.BlockSpec((1, cw), lambda b, j: (0, j))],
        out_specs=pl.BlockSpec((1, t, cw), lambda b, j: (b, 0, j)),
        out_shape=jax.ShapeDtypeStruct((bsz, t, CONV_WIDTH), BF16),
        compiler_params=_cparams(("arbitrary", "arbitrary"), VMEM_LIMIT),
    )(proj, proj, proj, conv_w, beta)


def _ret_kernel(q_ref, k_ref, v_ref, g_ref, cos_ref, sin_ref, lg_ref, beta_ref,
                o_ref, of_scr, q_scr, k_scr):
    L = SEQ_CHUNK
    t = q_ref.shape[1]
    n = t // L
    lgf = lg_ref[0, 0:1, :]
    lgb = lg_ref[0, 1:2, :]
    r = lax.broadcasted_iota(I32, (L, L), 0).astype(F32)
    c = lax.broadcasted_iota(I32, (L, L), 1).astype(F32)
    rel = r - c
    dbi = jnp.where(rel > 0, jnp.exp(lgf * jnp.maximum(rel, 0.0)),
                    jnp.where(rel < 0, jnp.exp(lgb * jnp.maximum(-rel, 0.0)), 2.0))
    qf = jnp.exp(lgf * (r + 1.0))
    kf = jnp.exp(lgf * (L - 1.0 - r))
    qb = jnp.exp(lgb * (L - r))
    kb = jnp.exp(lgb * r)
    gf_l = jnp.exp(lgf * float(L))
    gb_l = jnp.exp(lgb * float(L))
    scale = HEAD_DIM ** -0.5

    def rot(x, cs, sn):
        return x * cs + pltpu.roll(x, HEAD_DIM // 2, 1) * sn

    def fwd_body(ci, state):
        sl = pl.ds(pl.multiple_of(ci * L, L), L)
        cs = cos_ref[sl, :]
        sn = sin_ref[sl, :]
        q = rot(q_ref[0, sl, :].astype(F32), cs, sn)
        k = rot(k_ref[0, sl, :].astype(F32), cs, sn) * scale
        v16 = v_ref[0, sl, :].astype(BF16)
        q_scr[sl, :] = q
        k_scr[sl, :] = k
        s = _dot_nt(q.astype(BF16), k.astype(BF16)) * dbi
        o = _dot(s.astype(BF16), v16) + _dot((q * qf).astype(BF16), state.astype(BF16))
        of_scr[sl, :] = o
        return gf_l * state + _dot_tn((k * kf).astype(BF16), v16)

    lax.fori_loop(0, n, fwd_body, jnp.zeros((HEAD_DIM, HEAD_DIM), F32))

    beta = beta_ref[...]

    def bwd_body(i, state):
        ci = n - 1 - i
        sl = pl.ds(pl.multiple_of(ci * L, L), L)
        q = q_scr[sl, :]
        k = k_scr[sl, :]
        v16 = v_ref[0, sl, :].astype(BF16)
        o = of_scr[sl, :] + _dot((q * qb).astype(BF16), state.astype(BF16))
        mu = jnp.mean(o, axis=-1, keepdims=True)
        oc = o - mu
        var = jnp.mean(oc * oc, axis=-1, keepdims=True)
        y = oc * lax.rsqrt(var + HEAD_EPS) * _silu(g_ref[0, sl, :].astype(F32)) * beta
        o_ref[0, sl, :] = y.astype(o_ref.dtype)
        return gb_l * state + _dot_tn((k * kb).astype(BF16), v16)

    lax.fori_loop(0, n, bwd_body, jnp.zeros((HEAD_DIM, HEAD_DIM), F32))


def _rotary_tables(t):
    half = HEAD_DIM // 2
    inv_freq = ROPE_BASE ** (-jnp.arange(half, dtype=F32) / half)
    ang = jnp.arange(t, dtype=F32)[:, None] * inv_freq[None, :]
    cos, sin = jnp.cos(ang), jnp.sin(ang)
    return jnp.concatenate([cos, cos], -1), jnp.concatenate([-sin, sin], -1)


def _ret_log_decays():
    head = jnp.arange(RET_HEADS, dtype=F32)
    lg_f = jnp.log1p(-jnp.exp2(-5.0 - head))
    lg_b = jnp.log1p(-jnp.exp2(-5.5 - head))
    lg = jnp.stack([lg_f, lg_b], axis=1)
    return jnp.broadcast_to(lg[:, :, None], (RET_HEADS, 2, LANES))


def _ret_mixer(proj, beta):
    bsz, t, _ = proj.shape
    cosf, sinf = _rotary_tables(t)

    def col(off):
        return pl.BlockSpec((1, t, HEAD_DIM), lambda b, h, off=off: (b, 0, off + h))

    tab = pl.BlockSpec((t, HEAD_DIM), lambda b, h: (0, 0))
    return pl.pallas_call(
        _ret_kernel,
        grid=(bsz, RET_HEADS),
        in_specs=[col(_RQ), col(_RK), col(_RV), col(_RG), tab, tab,
                  pl.BlockSpec((1, 2, LANES), lambda b, h: (h, 0, 0)),
                  pl.BlockSpec((1, HEAD_DIM), lambda b, h: (0, _OUT_RET + h))],
        out_specs=pl.BlockSpec((1, t, HEAD_DIM), lambda b, h: (b, 0, h)),
        out_shape=jax.ShapeDtypeStruct((bsz, t, RET_WIDTH), BF16),
        scratch_shapes=[pltpu.VMEM((t, HEAD_DIM), F32)] * 3,
        compiler_params=_cparams(("arbitrary", "arbitrary"), VMEM_LIMIT),
    )(proj, proj, proj, proj, cosf, sinf, _ret_log_decays(), beta)


_HG_BASE = SUBLANES
_HG_LEVELS = tuple(m for m in (8, 16, 32, 64) if m < SEQ_CHUNK)


def _cumsum_rows(x, reverse):
    n = x.shape[0]
    row = lax.broadcasted_iota(I32, x.shape, 0)
    sh = 1
    while sh < n:
        if reverse:
            x = x + jnp.where(row < n - sh, pltpu.roll(x, n - sh, 0), 0.0)
        else:
            x = x + jnp.where(row >= sh, pltpu.roll(x, sh, 0), 0.0)
        sh *= 2
    return x


def _hg_masks(reverse):
    C = SEQ_CHUNK
    r = lax.broadcasted_iota(I32, (C, C), 0)
    c = lax.broadcasted_iota(I32, (C, C), 1)
    if reverse:
        r, c = c, r
    sh = _HG_BASE.bit_length() - 1
    base = ((r >> sh) == (c >> sh)) & (c <= r)
    levels = []
    for m in _HG_LEVELS:
        sh = m.bit_length() - 1
        levels.append(((r >> (sh + 1)) == (c >> (sh + 1)))
                      & (((r >> sh) & 1) == 1) & (((c >> sh) & 1) == 0))
    return base, levels


def _hg_scores(q, kk, cum, masks, reverse):
    C = SEQ_CHUNK
    base_mask, level_masks = masks

    def split(x, blk):
        return x.reshape(C // blk, blk, LANES)

    b = _HG_BASE
    ref_row = b // 2 if reverse else b // 2 - 1
    cum3 = split(cum, b)
    ref = cum3[:, ref_row:ref_row + 1, :]
    qt = (split(q, b) * jnp.exp(cum3 - ref)).reshape(C, LANES).astype(BF16)
    kt = (split(kk, b) * jnp.exp(ref - cum3)).reshape(C, LANES).astype(BF16)
    a = jnp.where(base_mask, _dot_nt(qt, kt), 0.0)
    for m, mask in zip(_HG_LEVELS, level_masks):
        blk = 2 * m
        ref_row = m if reverse else m - 1
        cum3 = split(cum, blk)
        ref = cum3[:, ref_row:ref_row + 1, :]
        qt = (split(q, blk) * jnp.exp(jnp.minimum(cum3 - ref, 0.0))).reshape(C, LANES).astype(BF16)
        kt = (split(kk, blk) * jnp.exp(jnp.minimum(ref - cum3, 0.0))).reshape(C, LANES).astype(BF16)
        a = a + jnp.where(mask, _dot_nt(qt, kt), 0.0)
    return a


def _hg_kernel(layer, q_ref, zf_ref, zb_ref, v_ref, g_ref, lbl_ref, beta_ref, o_ref, of_scr):
    C = SEQ_CHUNK
    t = q_ref.shape[1]
    n = t // C
    logits = lbl_ref[...].astype(F32)
    e = jnp.exp(logits - jnp.max(logits, axis=0, keepdims=True))
    p = e / jnp.sum(e, axis=0, keepdims=True)
    lb = p[0]
    for l in range(1, layer + 1):
        lb = lb + p[l]
    lb = lb - p[0]

    def gates(z_ref, sl, lb_row):
        f = lb_row + (1.0 - lb_row) * jax.nn.sigmoid(z_ref[0, sl, :].astype(F32))
        f = jnp.maximum(f, FORGET_FLOOR)
        return jnp.log(f), 1.0 - f

    def chunk(ci, state_t, z_ref, lb_row, masks, reverse):
        sl = pl.ds(pl.multiple_of(ci * C, C), C)
        logf, kk = gates(z_ref, sl, lb_row)
        cum = _cumsum_rows(logf, reverse)
        q = q_ref[0, sl, :].astype(F32)
        v16 = v_ref[0, sl, :].astype(BF16)
        a = _hg_scores(q, kk, cum, masks, reverse)
        inter = _dot_nt((q * jnp.exp(cum)).astype(BF16), state_t.astype(BF16))
        o = _dot(a.astype(BF16), v16) + inter
        total = cum[0:1, :] if reverse else cum[C - 1:C, :]
        k_tail = kk * jnp.exp(total - cum)
        new_state = state_t * jnp.exp(total) + _dot_tn(v16, k_tail.astype(BF16))
        return sl, o, new_state

    masks_f = _hg_masks(False)

    def fwd_body(ci, state_t):
        sl, o, new_state = chunk(ci, state_t, zf_ref, lb[0:1, :], masks_f, False)
        of_scr[sl, :] = o
        return new_state

    lax.fori_loop(0, n, fwd_body, jnp.zeros((HEAD_DIM, HEAD_DIM), F32))

    masks_b = _hg_masks(True)
    beta = beta_ref[...]

    def bwd_body(i, state_t):
        sl, o, new_state = chunk(n - 1 - i, state_t, zb_ref, lb[1:2, :], masks_b, True)
        o = o + of_scr[sl, :]
        ms = jnp.mean(o * o, axis=-1, keepdims=True)
        y = o * lax.rsqrt(ms + HEAD_EPS) * _silu(g_ref[0, sl, :].astype(F32)) * beta
        o_ref[0, sl, :] = y.astype(o_ref.dtype)
        return new_state

    lax.fori_loop(0, n, bwd_body, jnp.zeros((HEAD_DIM, HEAD_DIM), F32))


def _hg_mixer(proj, lb_logits, beta, layer):
    bsz, t, _ = proj.shape

    def col(off):
        return pl.BlockSpec((1, t, HEAD_DIM), lambda b, h, off=off: (b, 0, off + h))

    return pl.pallas_call(
        functools.partial(_hg_kernel, layer),
        grid=(bsz, HG_HEADS),
        in_specs=[col(_HQ), col(_HFF), col(_HFB), col(_HI), col(_HG),
                  pl.BlockSpec((DEPTH, 2, HEAD_DIM), lambda b, h: (0, 0, h)),
                  pl.BlockSpec((1, HEAD_DIM), lambda b, h: (0, _OUT_HG + h))],
        out_specs=pl.BlockSpec((1, t, HEAD_DIM), lambda b, h: (b, 0, h)),
        out_shape=jax.ShapeDtypeStruct((bsz, t, HG_WIDTH), BF16),
        scratch_shapes=[pltpu.VMEM((t, HEAD_DIM), F32)],
        compiler_params=_cparams(("arbitrary", "arbitrary"), VMEM_LIMIT),
    )(proj, proj, proj, proj, proj, lb_logits, beta)


def _out_proj_kernel(yc_ref, yr_ref, yh_ref, w_ref, x_ref, gate_ref, g_ref, b_ref,
                     sc_ref, sh_ref, wrh_ref, wrl_ref, xo_ref, h_ref, lo_ref):
    c0, c1 = CONV_WIDTH, CONV_WIDTH + RET_WIDTH
    y = (_dot(yc_ref[0], w_ref[0:c0, :]) + _dot(yr_ref[0], w_ref[c0:c1, :])
         + _dot(yh_ref[0], w_ref[c1:, :]))
    z = DEEPNORM_ALPHA * x_ref[0] + (1.0 + gate_ref[0]) * y
    xn = _layer_norm_rows(z, g_ref[...], b_ref[...])
    xo_ref[0] = xn
    h = xn * (1.0 + sc_ref[0]) + sh_ref[0]
    h_ref[0] = h
    h_hi = h.astype(BF16)
    h_lo = (h - h_hi.astype(F32)).astype(BF16)
    wrh = wrh_ref[...]
    lo_ref[0] = _dot(h_hi, wrh) + _dot(h_lo, wrh) + _dot(h_hi, wrl_ref[...])


def _out_proj(yc, yr, yh, w_out16, x, gate, ln_g, ln_b, scale2, shift2, wr_hi, wr_lo):
    bsz, t, d = x.shape
    tm = 512

    def row(width):
        return pl.BlockSpec((1, tm, width), lambda b, i: (b, i, 0))

    vec = pl.BlockSpec((1, d), lambda b, i: (0, 0))
    mod = pl.BlockSpec((1, 1, d), lambda b, i: (b, 0, 0))
    wr = pl.BlockSpec((d, LANES), lambda b, i: (0, 0))
    return pl.pallas_call(
        _out_proj_kernel,
        grid=(bsz, t // tm),
        in_specs=[row(CONV_WIDTH), row(RET_WIDTH), row(HG_WIDTH),
                  pl.BlockSpec((d, d), lambda b, i: (0, 0)),
                  row(d), mod, vec, vec, mod, mod, wr, wr],
        out_specs=[row(d), row(d), row(LANES)],
        out_shape=[jax.ShapeDtypeStruct((bsz, t, d), F32),
                   jax.ShapeDtypeStruct((bsz, t, d), F32),
                   jax.ShapeDtypeStruct((bsz, t, LANES), F32)],
        compiler_params=_cparams(("arbitrary", "arbitrary"), VMEM_LIMIT),
    )(yc, yr, yh, w_out16, x, gate, ln_g.reshape(1, d), ln_b.reshape(1, d),
      scale2, shift2, wr_hi, wr_lo)


def _route_kernel(n_tiles, lt_ref, bias_ref, pos1_ref, pos2_ref, w1_ref, w2_ref, tile_ref):
    E = N_EXPERTS
    logits = [lt_ref[e] for e in range(E)]
    shape = logits[0].shape
    mx = functools.reduce(jnp.maximum, logits)
    ex = [jnp.exp(l - mx) for l in logits]
    den = functools.reduce(lambda a, b: a + b, ex)
    scores = [x / den for x in ex]
    sel = [scores[e] + bias_ref[e] for e in range(E)]

    best_g = jnp.zeros(shape, I32)
    best_v = None
    for g in range(N_GROUPS):
        a, b, c, d = sel[EXPERTS_PER_GROUP * g: EXPERTS_PER_GROUP * (g + 1)]
        hi1, lo1 = jnp.maximum(a, b), jnp.minimum(a, b)
        hi2, lo2 = jnp.maximum(c, d), jnp.minimum(c, d)
        gs = jnp.maximum(hi1, hi2) + jnp.maximum(jnp.minimum(hi1, hi2), jnp.maximum(lo1, lo2))
        if g == 0:
            best_v = gs
        else:
            better = gs > best_v
            best_g = jnp.where(better, g, best_g)
            best_v = jnp.where(better, gs, best_v)

    masked = [jnp.where(best_g == (e // EXPERTS_PER_GROUP), sel[e], MASK_NEG) for e in range(E)]

    def arg_top(vals, exclude):
        idx = jnp.zeros(shape, I32)
        val = None
        for e in range(E):
            v = vals[e] if exclude is None else jnp.where(exclude == e, -jnp.inf, vals[e])
            if e == 0:
                val = v
            else:
                better = v > val
                idx = jnp.where(better, e, idx)
                val = jnp.where(better, v, val)
        return idx

    idx1 = arg_top(masked, None)
    idx2 = arg_top(masked, idx1)
    s1 = functools.reduce(lambda a, b: a + b, [jnp.where(idx1 == e, scores[e], 0.0) for e in range(E)])
    s2 = functools.reduce(lambda a, b: a + b, [jnp.where(idx2 == e, scores[e], 0.0) for e in range(E)])
    w1_ref[...] = s1 / (s1 + s2)
    w2_ref[...] = s2 / (s1 + s2)

    rows = shape[0]
    li = lax.broadcasted_iota(I32, (LANES, LANES), 0)
    lj = lax.broadcasted_iota(I32, (LANES, LANES), 1)
    upper = (li <= lj).astype(BF16)
    ri = lax.broadcasted_iota(I32, (rows, rows), 0)
    rj = lax.broadcasted_iota(I32, (rows, rows), 1)
    lower = (rj < ri).astype(BF16)
    tile_start = (lax.broadcasted_iota(I32, (1, LANES), 1) * MOE_TILE).astype(F32)
    start = jnp.zeros((1, 1), F32)
    pos1 = jnp.zeros(shape, F32)
    pos2 = jnp.zeros(shape, F32)
    tile_e = jnp.zeros((1, LANES), F32)
    for e in range(E):
        hit1 = idx1 == e
        hit2 = idx2 == e
        onehot = jnp.where(hit1 | hit2, 1.0, 0.0)
        pref = _dot(onehot.astype(BF16), upper)
        row_tot = pref[:, LANES - 1:LANES]
        row_off = _dot(lower, jnp.broadcast_to(row_tot, shape).astype(BF16))[:, 0:1]
        rank = pref - onehot + row_off
        count = jnp.sum(onehot, keepdims=True)
        dest = start + rank
        pos1 = jnp.where(hit1, dest, pos1)
        pos2 = jnp.where(hit2, dest, pos2)
        start = start + jnp.ceil(count / MOE_TILE) * MOE_TILE
        tile_e = tile_e + jnp.where(tile_start >= start, 1.0, 0.0)
    pos1_ref[...] = pos1.astype(I32)
    pos2_ref[...] = pos2.astype(I32)
    n_valid = start / MOE_TILE
    tile_id = lax.broadcasted_iota(I32, (1, LANES), 1).astype(F32)
    last_e = jnp.sum(jnp.where(tile_id == n_valid - 1.0, tile_e, 0.0), keepdims=True)
    tile_e = jnp.where(tile_id < n_valid, tile_e, last_e)
    sub = lax.broadcasted_iota(I32, (SUBLANES, LANES), 0)
    tile_ref[...] = jnp.where(sub == 0, tile_e, jnp.broadcast_to(n_valid, (SUBLANES, LANES))).astype(I32)


def _route(logits_t, router_bias, n_tiles):
    e, rows, lanes = logits_t.shape
    tok = jax.ShapeDtypeStruct((rows, lanes), I32)
    tokf = jax.ShapeDtypeStruct((rows, lanes), F32)
    full = pl.BlockSpec((rows, lanes), lambda i: (0, 0))
    return pl.pallas_call(
        functools.partial(_route_kernel, n_tiles),
        grid=(1,),
        in_specs=[pl.BlockSpec((e, rows, lanes), lambda i: (0, 0, 0)),
                  pl.BlockSpec(memory_space=pltpu.SMEM)],
        out_specs=[full, full, full, full, pl.BlockSpec((SUBLANES, LANES), lambda i: (0, 0))],
        out_shape=[tok, tok, tokf, tokf, jax.ShapeDtypeStruct((SUBLANES, LANES), I32)],
        compiler_params=_cparams(("arbitrary",)),
    )(logits_t, router_bias)


_ROWS_PER_STEP = 512


def _scatter_rows_kernel(p1_ref, p2_ref, src_ref, dst_in_ref, dst_ref, sem):
    del dst_in_ref
    base = pl.program_id(0) * _ROWS_PER_STEP

    def copies(r):
        row = src_ref.at[pl.ds(base + r, 1)]
        return (pltpu.make_async_copy(row, dst_ref.at[pl.ds(p1_ref[0, 0, r], 1)], sem.at[0]),
                pltpu.make_async_copy(row, dst_ref.at[pl.ds(p2_ref[0, 0, r], 1)], sem.at[1]))

    def start(r, carry):
        for cp in copies(r):
            cp.start()
        return carry

    def wait(r, carry):
        for cp in copies(r):
            cp.wait()
        return carry

    lax.fori_loop(0, _ROWS_PER_STEP, start, 0)
    lax.fori_loop(0, _ROWS_PER_STEP, wait, 0)


def _scatter_rows(src, pos1, pos2, n_dst):
    n, d = src.shape
    steps = n // _ROWS_PER_STEP
    idx = pl.BlockSpec((1, 1, _ROWS_PER_STEP), lambda i: (i, 0, 0), memory_space=pltpu.SMEM)
    any_spec = pl.BlockSpec(memory_space=pl.ANY)
    dst0 = jnp.zeros((n_dst, d), src.dtype)
    return pl.pallas_call(
        _scatter_rows_kernel,
        grid=(steps,),
        in_specs=[idx, idx, any_spec, any_spec],
        out_specs=any_spec,
        out_shape=jax.ShapeDtypeStruct((n_dst, d), src.dtype),
        scratch_shapes=[pltpu.SemaphoreType.DMA((2,))],
        input_output_aliases={3: 0},
        compiler_params=_cparams(("arbitrary",)),
    )(pos1.reshape(steps, 1, _ROWS_PER_STEP), pos2.reshape(steps, 1, _ROWS_PER_STEP), src, dst0)


def _gather_rows_kernel(p1_ref, p2_ref, src_ref, dst_ref, sem):
    base = pl.program_id(0) * _ROWS_PER_STEP

    def copies(r):
        return (pltpu.make_async_copy(src_ref.at[pl.ds(p1_ref[0, 0, r], 1)],
                                      dst_ref.at[0, pl.ds(base + r, 1)], sem.at[0]),
                pltpu.make_async_copy(src_ref.at[pl.ds(p2_ref[0, 0, r], 1)],
                                      dst_ref.at[1, pl.ds(base + r, 1)], sem.at[1]))

    def start(r, carry):
        for cp in copies(r):
            cp.start()
        return carry

    def wait(r, carry):
        for cp in copies(r):
            cp.wait()
        return carry

    lax.fori_loop(0, _ROWS_PER_STEP, start, 0)
    lax.fori_loop(0, _ROWS_PER_STEP, wait, 0)


def _gather_rows(src, pos1, pos2):
    n = pos1.size
    d = src.shape[1]
    steps = n // _ROWS_PER_STEP
    idx = pl.BlockSpec((1, 1, _ROWS_PER_STEP), lambda i: (i, 0, 0), memory_space=pltpu.SMEM)
    any_spec = pl.BlockSpec(memory_space=pl.ANY)
    return pl.pallas_call(
        _gather_rows_kernel,
        grid=(steps,),
        in_specs=[idx, idx, any_spec],
        out_specs=any_spec,
        out_shape=jax.ShapeDtypeStruct((2, n, d), src.dtype),
        scratch_shapes=[pltpu.SemaphoreType.DMA((2,))],
        compiler_params=_cparams(("arbitrary",)),
    )(pos1.reshape(steps, 1, _ROWS_PER_STEP), pos2.reshape(steps, 1, _ROWS_PER_STEP), src)


def _moe_kernel(te_ref, nv_ref, x_ref, wg_ref, wu_ref, wd_ref, y_ref):
    i = pl.program_id(0)

    @pl.when(i < nv_ref[0])
    def _():
        x = x_ref[...].astype(BF16)
        a = _dot(x, wg_ref[0])
        b = _dot(x, wu_ref[0])
        hidden = (_silu(a) * b).astype(BF16)
        y_ref[...] = _dot(hidden, wd_ref[0])

    @pl.when(i >= nv_ref[0])
    def _():
        y_ref[...] = jnp.zeros_like(y_ref)


def _moe_experts(xs, tile_e, n_valid, wg16, wu16, wd16):
    n_rows, d = xs.shape
    f = wg16.shape[2]
    tm = MOE_TILE
    n_tiles = n_rows // tm

    def xmap(i, te, nv):
        return (jnp.minimum(i, nv[0] - 1), 0)

    grid_spec = pltpu.PrefetchScalarGridSpec(
        num_scalar_prefetch=2,
        grid=(n_tiles,),
        in_specs=[pl.BlockSpec((tm, d), xmap),
                  pl.BlockSpec((1, d, f), lambda i, te, nv: (te[i], 0, 0)),
                  pl.BlockSpec((1, d, f), lambda i, te, nv: (te[i], 0, 0)),
                  pl.BlockSpec((1, f, d), lambda i, te, nv: (te[i], 0, 0))],
        out_specs=pl.BlockSpec((tm, d), lambda i, te, nv: (i, 0)),
    )
    return pl.pallas_call(
        _moe_kernel,
        grid_spec=grid_spec,
        out_shape=jax.ShapeDtypeStruct((n_rows, d), F32),
        compiler_params=_cparams(("arbitrary",), VMEM_LIMIT),
    )(tile_e, n_valid, xs, wg16, wu16, wd16)


def _combine_kernel(with_next, y_ref, w1_ref, w2_ref, x_ref, gate_ref, g_ref, b_ref, *rest):
    if with_next:
        sc_ref, sh_ref, xo_ref, h_ref = rest
    else:
        (xo_ref,) = rest
    y = w1_ref[0] * y_ref[0, 0] + w2_ref[0] * y_ref[1, 0]
    z = DEEPNORM_ALPHA * x_ref[0] + (1.0 + gate_ref[0]) * y
    xn = _layer_norm_rows(z, g_ref[...], b_ref[...])
    xo_ref[0] = xn
    if with_next:
        h_ref[0] = (xn * (1.0 + sc_ref[0]) + sh_ref[0]).astype(BF16)


def _combine(y2, w1, w2, x, gate, ln_g, ln_b, next_mod):
    bsz, t, d = x.shape
    tr = 256
    row = pl.BlockSpec((1, tr, d), lambda b, i: (b, i, 0))
    vec = pl.BlockSpec((1, d), lambda b, i: (0, 0))
    mod = pl.BlockSpec((1, 1, d), lambda b, i: (b, 0, 0))
    wcol = pl.BlockSpec((1, tr, 1), lambda b, i: (b, i, 0))
    ysp = pl.BlockSpec((2, 1, tr, d), lambda b, i: (0, b, i, 0))
    with_next = next_mod is not None
    in_specs = [ysp, wcol, wcol, row, mod, vec, vec]
    args = [y2.reshape(2, bsz, t, d), w1.reshape(bsz, t, 1), w2.reshape(bsz, t, 1), x, gate,
            ln_g.reshape(1, d), ln_b.reshape(1, d)]
    out_specs = [row]
    out_shape = [jax.ShapeDtypeStruct((bsz, t, d), F32)]
    if with_next:
        in_specs += [mod, mod]
        args += list(next_mod)
        out_specs.append(row)
        out_shape.append(jax.ShapeDtypeStruct((bsz, t, d), BF16))
    res = pl.pallas_call(
        functools.partial(_combine_kernel, with_next),
        grid=(bsz, t // tr),
        in_specs=in_specs,
        out_specs=out_specs,
        out_shape=out_shape,
        compiler_params=_cparams(("arbitrary", "arbitrary"), VMEM_LIMIT),
    )(*args)
    return res if with_next else (res[0], None)


def kernel(x, c, emb_ln_g, emb_ln_b, w_ada, b_ada, w_in, conv_w, mix_beta, w_out, hg_lb_logits,
           ln_g, ln_b, w_router, router_bias, w_gate, w_up, w_down):
    bsz, t, d = x.shape
    n = bsz * t
    assert n % LANES == 0 and t % SEQ_CHUNK == 0
    n_tiles = 2 * n // MOE_TILE + N_EXPERTS
    assert n_tiles <= LANES

    mod = _ada_mod(c, w_ada, b_ada)
    mod = mod.reshape(DEPTH, bsz, 6, 1, d)

    def mods(l):
        return [mod[l, :, i] for i in range(6)]

    wr_hi = jnp.pad(w_router, ((0, 0), (0, LANES - N_EXPERTS)))
    wr_hi16 = wr_hi.astype(BF16)
    wr_lo16 = (wr_hi - wr_hi16.astype(F32)).astype(BF16)

    m = [mods(l) for l in range(DEPTH)]
    xcur, h = _ln_mod(x, emb_ln_g, emb_ln_b, m[0][1], m[0][0])
    for l in range(DEPTH):
        _, _, gate1, shift2, scale2, gate2 = m[l]
        beta = mix_beta[l].reshape(1, -1)
        proj = _in_proj(h.reshape(n, d), w_in[l].astype(BF16), F32).reshape(bsz, t, IN_COLS)
        yc = _conv_mixer(proj, conv_w[l], beta)
        yr = _ret_mixer(proj, beta)
        yh = _hg_mixer(proj, hg_lb_logits, beta, l)
        x1, h2, logits = _out_proj(yc, yr, yh, w_out[l].astype(BF16), xcur, gate1,
                                   ln_g[l, 0], ln_b[l, 0], scale2, shift2, wr_hi16, wr_lo16)
        logits_t = logits.reshape(n, LANES)[:, :N_EXPERTS].T.reshape(N_EXPERTS, n // LANES, LANES)
        pos1, pos2, w1, w2, tiles = _route(logits_t, router_bias, n_tiles)
        xs = _scatter_rows(h2.reshape(n, d), pos1.reshape(n), pos2.reshape(n), n_tiles * MOE_TILE)
        ys = _moe_experts(xs, tiles[0, :n_tiles], tiles[1, :1], w_gate[l].astype(BF16),
                          w_up[l].astype(BF16), w_down[l].astype(BF16))
        y2 = _gather_rows(ys, pos1.reshape(n), pos2.reshape(n))
        next_mod = (m[l + 1][1], m[l + 1][0]) if l + 1 < DEPTH else None
        xcur, h = _combine(y2, w1, w2, x1, gate2, ln_g[l, 1], ln_b[l, 1], next_mod)
    return xcur
```

```python
import functools
import math

import jax
import jax.numpy as jnp
from jax import lax
from jax.experimental import pallas as pl
from jax.experimental.pallas import tpu as pltpu

F32 = jnp.float32
BF16 = jnp.bfloat16
I32 = jnp.int32

DEPTH = 2
CONV_WIDTH = 512
RET_WIDTH = 768
HG_WIDTH = 768
HEAD_DIM = 128
RET_HEADS = RET_WIDTH // HEAD_DIM
HG_HEADS = HG_WIDTH // HEAD_DIM
IN_COLS = 3 * CONV_WIDTH + 4 * RET_WIDTH + 5 * HG_WIDTH
ROPE_BASE = 10000.0
N_EXPERTS = 16
N_GROUPS = 4
EXPERTS_PER_GROUP = N_EXPERTS // N_GROUPS
MASK_NEG = -1e9
DEEPNORM_ALPHA = (2.0 * DEPTH) ** 0.25
LN_EPS = 1e-5
HEAD_EPS = 1e-6
FORGET_FLOOR = 1e-6

LANES = 128
SUBLANES = 8

_CB, _CC, _CH = 0, 4, 8
_RQ, _RK, _RV, _RG = 12, 18, 24, 30
_HQ, _HFF, _HFB, _HI, _HG = 36, 42, 48, 54, 60
_OUT_RET, _OUT_HG = 4, 10

SEQ_CHUNK = 128
MOE_TILE = 256
VMEM_LIMIT = 56 * 1024 * 1024


def _cparams(sem, vmem=None):
    return pltpu.CompilerParams(dimension_semantics=sem, vmem_limit_bytes=vmem)


def _silu(x):
    return x * jax.nn.sigmoid(x)


def _dot(a, b):
    return jnp.dot(a, b, preferred_element_type=F32)


def _dot_nt(a, b):
    return lax.dot_general(a, b, (((1,), (1,)), ((), ())), preferred_element_type=F32)


def _dot_tn(a, b):
    return lax.dot_general(a, b, (((0,), (0,)), ((), ())), preferred_element_type=F32)


def _ada_kernel(c_ref, w_ref, b_ref, o_ref):
    cond = _silu(c_ref[...])
    o_ref[0] = _dot(cond.astype(BF16), w_ref[0].astype(BF16)) + b_ref[0]


def _ada_mod(c, w_ada, b_ada):
    depth, d, n6 = w_ada.shape
    b = c.shape[0]
    bp = -(-b // SUBLANES) * SUBLANES
    cp = jnp.pad(c, ((0, bp - b), (0, 0)))
    tn = 512
    out = pl.pallas_call(
        _ada_kernel,
        grid=(depth, n6 // tn),
        in_specs=[
            pl.BlockSpec((bp, d), lambda l, j: (0, 0)),
            pl.BlockSpec((1, d, tn), lambda l, j: (l, 0, j)),
            pl.BlockSpec((1, 1, tn), lambda l, j: (l, 0, j)),
        ],
        out_specs=pl.BlockSpec((1, bp, tn), lambda l, j: (l, 0, j)),
        out_shape=jax.ShapeDtypeStruct((depth, bp, n6), F32),
        compiler_params=_cparams(("arbitrary", "arbitrary")),
        name="ada_mod",
    )(cp, w_ada, b_ada.reshape(depth, 1, n6))
    return out[:, :b, :]


def _layer_norm_rows(z, g, b):
    mu = jnp.mean(z, axis=-1, keepdims=True)
    zc = z - mu
    var = jnp.mean(zc * zc, axis=-1, keepdims=True)
    return zc * lax.rsqrt(var + LN_EPS) * g + b


def _ln_mod_kernel(x_ref, g_ref, b_ref, sc_ref, sh_ref, xo_ref, h_ref):
    y = _layer_norm_rows(x_ref[0], g_ref[...], b_ref[...])
    xo_ref[0] = y
    h_ref[0] = (y * (1.0 + sc_ref[0]) + sh_ref[0]).astype(BF16)


def _ln_mod(x, g, b, scale, shift):
    bsz, t, d = x.shape
    tr = 512
    row = pl.BlockSpec((1, tr, d), lambda i, j: (i, j, 0))
    vec = pl.BlockSpec((1, d), lambda i, j: (0, 0))
    mod = pl.BlockSpec((1, 1, d), lambda i, j: (i, 0, 0))
    return pl.pallas_call(
        _ln_mod_kernel,
        grid=(bsz, t // tr),
        in_specs=[row, vec, vec, mod, mod],
        out_specs=[row, row],
        out_shape=[jax.ShapeDtypeStruct((bsz, t, d), F32),
                   jax.ShapeDtypeStruct((bsz, t, d), BF16)],
        compiler_params=_cparams(("arbitrary", "arbitrary")),
        name="ln_mod",
    )(x, g.reshape(1, d), b.reshape(1, d), scale, shift)


def _mm_kernel(a_ref, b_ref, o_ref):
    o_ref[...] = _dot(a_ref[...], b_ref[...]).astype(o_ref.dtype)


def _in_proj(h, w, out_dtype):
    n, d = h.shape
    nc = w.shape[1]
    tm, tn = 1024, 768
    tm = min(tm, n)
    return pl.pallas_call(
        _mm_kernel,
        grid=(n // tm, nc // tn),
        in_specs=[pl.BlockSpec((tm, d), lambda i, j: (i, 0)),
                  pl.BlockSpec((d, tn), lambda i, j: (0, j))],
        out_specs=pl.BlockSpec((tm, tn), lambda i, j: (i, j)),
        out_shape=jax.ShapeDtypeStruct((n, nc), out_dtype),
        compiler_params=_cparams(("arbitrary", "arbitrary"), VMEM_LIMIT),
        name="in_proj",
    )(h, w)


def _conv_kernel(cb_ref, cc_ref, ch_ref, w_ref, beta_ref, o_ref):
    u = cc_ref[0].astype(F32) * ch_ref[0].astype(F32)
    t = u.shape[0]
    row = lax.broadcasted_iota(I32, u.shape, 0)
    prev = jnp.where(row == 0, 0.0, pltpu.roll(u, 1, 0))
    nxt = jnp.where(row == t - 1, 0.0, pltpu.roll(u, t - 1, 0))
    w = w_ref[...]
    y = cb_ref[0].astype(F32) * (prev * w[0:1] + u * w[1:2] + nxt * w[2:3])
    o_ref[0] = (y * beta_ref[...]).astype(o_ref.dtype)


def _conv_mixer(proj, conv_w, beta):
    bsz, t, _ = proj.shape
    cw = 256
    nb = CONV_WIDTH // cw
    per = LANES * 1

    def col(off):
        return pl.BlockSpec((1, t, cw), lambda b, j, off=off: (b, 0, off * per // cw + j))

    return pl.pallas_call(
        _conv_kernel,
        grid=(bsz, nb),
        in_specs=[col(_CB), col(_CC), col(_CH),
                  pl.BlockSpec((3, cw), lambda b, j: (0, j)),
                  pl.BlockSpec((1, cw), lambda b, j: (0, j))],
        out_specs=pl.BlockSpec((1, t, cw), lambda b, j: (b, 0, j)),
        out_shape=jax.ShapeDtypeStruct((bsz, t, CONV_WIDTH), BF16),
        compiler_params=_cparams(("arbitrary", "arbitrary"), VMEM_LIMIT),
        name="conv_mixer",
    )(proj, proj, proj, conv_w, beta)


def _ret_kernel(q_ref, k_ref, v_ref, g_ref, cos_ref, sin_ref, lg_ref, beta_ref,
                o_ref, of_scr, q_scr, k_scr):
    L = SEQ_CHUNK
    t = q_ref.shape[1]
    n = t // L
    lgf = lg_ref[0, 0:1, :]
    lgb = lg_ref[0, 1:2, :]
    r = lax.broadcasted_iota(I32, (L, L), 0).astype(F32)
    c = lax.broadcasted_iota(I32, (L, L), 1).astype(F32)
    rel = r - c
    dbi = jnp.where(rel > 0, jnp.exp(lgf * jnp.maximum(rel, 0.0)),
                    jnp.where(rel < 0, jnp.exp(lgb * jnp.maximum(-rel, 0.0)), 2.0))
    qf = jnp.exp(lgf * (r + 1.0))
    kf = jnp.exp(lgf * (L - 1.0 - r))
    qb = jnp.exp(lgb * (L - r))
    kb = jnp.exp(lgb * r)
    gf_l = jnp.exp(lgf * float(L))
    gb_l = jnp.exp(lgb * float(L))
    scale = HEAD_DIM ** -0.5

    def rot(x, cs, sn):
        return x * cs + pltpu.roll(x, HEAD_DIM // 2, 1) * sn

    def fwd_body(ci, state):
        sl = pl.ds(pl.multiple_of(ci * L, L), L)
        cs = cos_ref[sl, :]
        sn = sin_ref[sl, :]
        q = rot(q_ref[0, sl, :].astype(F32), cs, sn)
        k = rot(k_ref[0, sl, :].astype(F32), cs, sn) * scale
        v16 = v_ref[0, sl, :].astype(BF16)
        q_scr[sl, :] = q
        k_scr[sl, :] = k
        s = _dot_nt(q.astype(BF16), k.astype(BF16)) * dbi
        o = _dot(s.astype(BF16), v16) + _dot((q * qf).astype(BF16), state.astype(BF16))
        of_scr[sl, :] = o
        return gf_l * state + _dot_tn((k * kf).astype(BF16), v16)

    lax.fori_loop(0, n, fwd_body, jnp.zeros((HEAD_DIM, HEAD_DIM), F32))

    beta = beta_ref[...]

    def bwd_body(i, state):
        ci = n - 1 - i
        sl = pl.ds(pl.multiple_of(ci * L, L), L)
        q = q_scr[sl, :]
        k = k_scr[sl, :]
        v16 = v_ref[0, sl, :].astype(BF16)
        o = of_scr[sl, :] + _dot((q * qb).astype(BF16), state.astype(BF16))
        mu = jnp.mean(o, axis=-1, keepdims=True)
        oc = o - mu
        var = jnp.mean(oc * oc, axis=-1, keepdims=True)
        y = oc * lax.rsqrt(var + HEAD_EPS) * _silu(g_ref[0, sl, :].astype(F32)) * beta
        o_ref[0, sl, :] = y.astype(o_ref.dtype)
        return gb_l * state + _dot_tn((k * kb).astype(BF16), v16)

    lax.fori_loop(0, n, bwd_body, jnp.zeros((HEAD_DIM, HEAD_DIM), F32))


def _rotary_tables(t):
    half = HEAD_DIM // 2
    inv_freq = ROPE_BASE ** (-jnp.arange(half, dtype=F32) / half)
    ang = jnp.arange(t, dtype=F32)[:, None] * inv_freq[None, :]
    cos, sin = jnp.cos(ang), jnp.sin(ang)
    return jnp.concatenate([cos, cos], -1), jnp.concatenate([-sin, sin], -1)


def _ret_log_decays():
    head = jnp.arange(RET_HEADS, dtype=F32)
    lg_f = jnp.log1p(-jnp.exp2(-5.0 - head))
    lg_b = jnp.log1p(-jnp.exp2(-5.5 - head))
    lg = jnp.stack([lg_f, lg_b], axis=1)
    return jnp.broadcast_to(lg[:, :, None], (RET_HEADS, 2, LANES))


def _ret_mixer(proj, beta):
    bsz, t, _ = proj.shape
    cosf, sinf = _rotary_tables(t)

    def col(off):
        return pl.BlockSpec((1, t, HEAD_DIM), lambda b, h, off=off: (b, 0, off + h))

    tab = pl.BlockSpec((t, HEAD_DIM), lambda b, h: (0, 0))
    return pl.pallas_call(
        _ret_kernel,
        grid=(bsz, RET_HEADS),
        in_specs=[col(_RQ), col(_RK), col(_RV), col(_RG), tab, tab,
                  pl.BlockSpec((1, 2, LANES), lambda b, h: (h, 0, 0)),
                  pl.BlockSpec((1, HEAD_DIM), lambda b, h: (0, _OUT_RET + h))],
        out_specs=pl.BlockSpec((1, t, HEAD_DIM), lambda b, h: (b, 0, h)),
        out_shape=jax.ShapeDtypeStruct((bsz, t, RET_WIDTH), BF16),
        scratch_shapes=[pltpu.VMEM((t, HEAD_DIM), F32)] * 3,
        compiler_params=_cparams(("arbitrary", "arbitrary"), VMEM_LIMIT),
        name="ret_mixer",
    )(proj, proj, proj, proj, cosf, sinf, _ret_log_decays(), beta)


_HG_BASE = SUBLANES
_HG_LEVELS = tuple(m for m in (8, 16, 32, 64) if m < SEQ_CHUNK)


def _cumsum_rows(x, reverse):
    n = x.shape[0]
    row = lax.broadcasted_iota(I32, x.shape, 0)
    sh = 1
    while sh < n:
        if reverse:
            x = x + jnp.where(row < n - sh, pltpu.roll(x, n - sh, 0), 0.0)
        else:
            x = x + jnp.where(row >= sh, pltpu.roll(x, sh, 0), 0.0)
        sh *= 2
    return x


def _hg_masks(reverse):
    C = SEQ_CHUNK
    r = lax.broadcasted_iota(I32, (C, C), 0)
    c = lax.broadcasted_iota(I32, (C, C), 1)
    if reverse:
        r, c = c, r
    sh = _HG_BASE.bit_length() - 1
    base = ((r >> sh) == (c >> sh)) & (c <= r)
    levels = []
    for m in _HG_LEVELS:
        sh = m.bit_length() - 1
        levels.append(((r >> (sh + 1)) == (c >> (sh + 1)))
                      & (((r >> sh) & 1) == 1) & (((c >> sh) & 1) == 0))
    return base, levels


def _hg_scores(q, kk, cum, masks, reverse):
    C = SEQ_CHUNK
    base_mask, level_masks = masks

    def split(x, blk):
        return x.reshape(C // blk, blk, LANES)

    b = _HG_BASE
    ref_row = b // 2 if reverse else b // 2 - 1
    cum3 = split(cum, b)
    ref = cum3[:, ref_row:ref_row + 1, :]
    qt = (split(q, b) * jnp.exp(cum3 - ref)).reshape(C, LANES).astype(BF16)
    kt = (split(kk, b) * jnp.exp(ref - cum3)).reshape(C, LANES).astype(BF16)
    a = jnp.where(base_mask, _dot_nt(qt, kt), 0.0)
    for m, mask in zip(_HG_LEVELS, level_masks):
        blk = 2 * m
        ref_row = m if reverse else m - 1
        cum3 = split(cum, blk)
        ref = cum3[:, ref_row:ref_row + 1, :]
        qt = (split(q, blk) * jnp.exp(jnp.minimum(cum3 - ref, 0.0))).reshape(C, LANES).astype(BF16)
        kt = (split(kk, blk) * jnp.exp(jnp.minimum(ref - cum3, 0.0))).reshape(C, LANES).astype(BF16)
        a = a + jnp.where(mask, _dot_nt(qt, kt), 0.0)
    return a


def _hg_kernel(layer, q_ref, zf_ref, zb_ref, v_ref, g_ref, lbl_ref, beta_ref, o_ref, of_scr):
    C = SEQ_CHUNK
    t = q_ref.shape[1]
    n = t // C
    logits = lbl_ref[...].astype(F32)
    e = jnp.exp(logits - jnp.max(logits, axis=0, keepdims=True))
    p = e / jnp.sum(e, axis=0, keepdims=True)
    lb = p[0]
    for l in range(1, layer + 1):
        lb = lb + p[l]
    lb = lb - p[0]

    def gates(z_ref, sl, lb_row):
        f = lb_row + (1.0 - lb_row) * jax.nn.sigmoid(z_ref[0, sl, :].astype(F32))
        f = jnp.maximum(f, FORGET_FLOOR)
        return jnp.log(f), 1.0 - f

    def chunk(ci, state_t, z_ref, lb_row, masks, reverse):
        sl = pl.ds(pl.multiple_of(ci * C, C), C)
        logf, kk = gates(z_ref, sl, lb_row)
        cum = _cumsum_rows(logf, reverse)
        q = q_ref[0, sl, :].astype(F32)
        v16 = v_ref[0, sl, :].astype(BF16)
        a = _hg_scores(q, kk, cum, masks, reverse)
        inter = _dot_nt((q * jnp.exp(cum)).astype(BF16), state_t.astype(BF16))
        o = _dot(a.astype(BF16), v16) + inter
        total = cum[0:1, :] if reverse else cum[C - 1:C, :]
        k_tail = kk * jnp.exp(total - cum)
        new_state = state_t * jnp.exp(total) + _dot_tn(v16, k_tail.astype(BF16))
        return sl, o, new_state

    masks_f = _hg_masks(False)

    def fwd_body(ci, state_t):
        sl, o, new_state = chunk(ci, state_t, zf_ref, lb[0:1, :], masks_f, False)
        of_scr[sl, :] = o
        return new_state

    lax.fori_loop(0, n, fwd_body, jnp.zeros((HEAD_DIM, HEAD_DIM), F32))

    masks_b = _hg_masks(True)
    beta = beta_ref[...]

    def bwd_body(i, state_t):
        sl, o, new_state = chunk(n - 1 - i, state_t, zb_ref, lb[1:2, :], masks_b, True)
        o = o + of_scr[sl, :]
        ms = jnp.mean(o * o, axis=-1, keepdims=True)
        y = o * lax.rsqrt(ms + HEAD_EPS) * _silu(g_ref[0, sl, :].astype(F32)) * beta
        o_ref[0, sl, :] = y.astype(o_ref.dtype)
        return new_state

    lax.fori_loop(0, n, bwd_body, jnp.zeros((HEAD_DIM, HEAD_DIM), F32))


def _hg_mixer(proj, lb_logits, beta, layer):
    bsz, t, _ = proj.shape

    def col(off):
        return pl.BlockSpec((1, t, HEAD_DIM), lambda b, h, off=off: (b, 0, off + h))

    return pl.pallas_call(
        functools.partial(_hg_kernel, layer),
        grid=(bsz, HG_HEADS),
        in_specs=[col(_HQ), col(_HFF), col(_HFB), col(_HI), col(_HG),
                  pl.BlockSpec((DEPTH, 2, HEAD_DIM), lambda b, h: (0, 0, h)),
                  pl.BlockSpec((1, HEAD_DIM), lambda b, h: (0, _OUT_HG + h))],
        out_specs=pl.BlockSpec((1, t, HEAD_DIM), lambda b, h: (b, 0, h)),
        out_shape=jax.ShapeDtypeStruct((bsz, t, HG_WIDTH), BF16),
        scratch_shapes=[pltpu.VMEM((t, HEAD_DIM), F32)],
        compiler_params=_cparams(("arbitrary", "arbitrary"), VMEM_LIMIT),
        name="hg_mixer",
    )(proj, proj, proj, proj, proj, lb_logits, beta)


def _out_proj_kernel(yc_ref, yr_ref, yh_ref, w_ref, x_ref, gate_ref, g_ref, b_ref,
                     sc_ref, sh_ref, wrh_ref, wrl_ref, xo_ref, h_ref, lo_ref):
    c0, c1 = CONV_WIDTH, CONV_WIDTH + RET_WIDTH
    y = (_dot(yc_ref[0], w_ref[0:c0, :]) + _dot(yr_ref[0], w_ref[c0:c1, :])
         + _dot(yh_ref[0], w_ref[c1:, :]))
    z = DEEPNORM_ALPHA * x_ref[0] + (1.0 + gate_ref[0]) * y
    xn = _layer_norm_rows(z, g_ref[...], b_ref[...])
    xo_ref[0] = xn
    h = xn * (1.0 + sc_ref[0]) + sh_ref[0]
    h_ref[0] = h
    h_hi = h.astype(BF16)
    h_lo = (h - h_hi.astype(F32)).astype(BF16)
    wrh = wrh_ref[...]
    lo_ref[0] = _dot(h_hi, wrh) + _dot(h_lo, wrh) + _dot(h_hi, wrl_ref[...])


def _out_proj(yc, yr, yh, w_out16, x, gate, ln_g, ln_b, scale2, shift2, wr_hi, wr_lo):
    bsz, t, d = x.shape
    tm = 512

    def row(width):
        return pl.BlockSpec((1, tm, width), lambda b, i: (b, i, 0))

    vec = pl.BlockSpec((1, d), lambda b, i: (0, 0))
    mod = pl.BlockSpec((1, 1, d), lambda b, i: (b, 0, 0))
    wr = pl.BlockSpec((d, LANES), lambda b, i: (0, 0))
    return pl.pallas_call(
        _out_proj_kernel,
        grid=(bsz, t // tm),
        in_specs=[row(CONV_WIDTH), row(RET_WIDTH), row(HG_WIDTH),
                  pl.BlockSpec((d, d), lambda b, i: (0, 0)),
                  row(d), mod, vec, vec, mod, mod, wr, wr],
        out_specs=[row(d), row(d), row(LANES)],
        out_shape=[jax.ShapeDtypeStruct((bsz, t, d), F32),
                   jax.ShapeDtypeStruct((bsz, t, d), F32),
                   jax.ShapeDtypeStruct((bsz, t, LANES), F32)],
        compiler_params=_cparams(("arbitrary", "arbitrary"), VMEM_LIMIT),
        name="out_proj",
    )(yc, yr, yh, w_out16, x, gate, ln_g.reshape(1, d), ln_b.reshape(1, d),
      scale2, shift2, wr_hi, wr_lo)


def _route_kernel(n_tiles, lt_ref, bias_ref, pos1_ref, pos2_ref, w1_ref, w2_ref, tile_ref):
    E = N_EXPERTS
    logits = [lt_ref[e] for e in range(E)]
    shape = logits[0].shape
    mx = functools.reduce(jnp.maximum, logits)
    ex = [jnp.exp(l - mx) for l in logits]
    den = functools.reduce(lambda a, b: a + b, ex)
    scores = [x / den for x in ex]
    sel = [scores[e] + bias_ref[e] for e in range(E)]

    best_g = jnp.zeros(shape, I32)
    best_v = None
    for g in range(N_GROUPS):
        a, b, c, d = sel[EXPERTS_PER_GROUP * g: EXPERTS_PER_GROUP * (g + 1)]
        hi1, lo1 = jnp.maximum(a, b), jnp.minimum(a, b)
        hi2, lo2 = jnp.maximum(c, d), jnp.minimum(c, d)
        gs = jnp.maximum(hi1, hi2) + jnp.maximum(jnp.minimum(hi1, hi2), jnp.maximum(lo1, lo2))
        if g == 0:
            best_v = gs
        else:
            better = gs > best_v
            best_g = jnp.where(better, g, best_g)
            best_v = jnp.where(better, gs, best_v)

    masked = [jnp.where(best_g == (e // EXPERTS_PER_GROUP), sel[e], MASK_NEG) for e in range(E)]

    def arg_top(vals, exclude):
        idx = jnp.zeros(shape, I32)
        val = None
        for e in range(E):
            v = vals[e] if exclude is None else jnp.where(exclude == e, -jnp.inf, vals[e])
            if e == 0:
                val = v
            else:
                better = v > val
                idx = jnp.where(better, e, idx)
                val = jnp.where(better, v, val)
        return idx

    idx1 = arg_top(masked, None)
    idx2 = arg_top(masked, idx1)
    s1 = functools.reduce(lambda a, b: a + b, [jnp.where(idx1 == e, scores[e], 0.0) for e in range(E)])
    s2 = functools.reduce(lambda a, b: a + b, [jnp.where(idx2 == e, scores[e], 0.0) for e in range(E)])
    w1_ref[...] = s1 / (s1 + s2)
    w2_ref[...] = s2 / (s1 + s2)

    rows = shape[0]
    li = lax.broadcasted_iota(I32, (LANES, LANES), 0)
    lj = lax.broadcasted_iota(I32, (LANES, LANES), 1)
    upper = (li <= lj).astype(BF16)
    ri = lax.broadcasted_iota(I32, (rows, rows), 0)
    rj = lax.broadcasted_iota(I32, (rows, rows), 1)
    lower = (rj < ri).astype(BF16)
    tile_start = (lax.broadcasted_iota(I32, (1, LANES), 1) * MOE_TILE).astype(F32)
    start = jnp.zeros((1, 1), F32)
    pos1 = jnp.zeros(shape, F32)
    pos2 = jnp.zeros(shape, F32)
    tile_e = jnp.zeros((1, LANES), F32)
    for e in range(E):
        hit1 = idx1 == e
        hit2 = idx2 == e
        onehot = jnp.where(hit1 | hit2, 1.0, 0.0)
        pref = _dot(onehot.astype(BF16), upper)
        row_tot = pref[:, LANES - 1:LANES]
        row_off = _dot(lower, jnp.broadcast_to(row_tot, shape).astype(BF16))[:, 0:1]
        rank = pref - onehot + row_off
        count = jnp.sum(onehot, keepdims=True)
        dest = start + rank
        pos1 = jnp.where(hit1, dest, pos1)
        pos2 = jnp.where(hit2, dest, pos2)
        start = start + jnp.ceil(count / MOE_TILE) * MOE_TILE
        tile_e = tile_e + jnp.where(tile_start >= start, 1.0, 0.0)
    pos1_ref[...] = pos1.astype(I32)
    pos2_ref[...] = pos2.astype(I32)
    n_valid = start / MOE_TILE
    tile_id = lax.broadcasted_iota(I32, (1, LANES), 1).astype(F32)
    last_e = jnp.sum(jnp.where(tile_id == n_valid - 1.0, tile_e, 0.0), keepdims=True)
    tile_e = jnp.where(tile_id < n_valid, tile_e, last_e)
    sub = lax.broadcasted_iota(I32, (SUBLANES, LANES), 0)
    tile_ref[...] = jnp.where(sub == 0, tile_e, jnp.broadcast_to(n_valid, (SUBLANES, LANES))).astype(I32)


def _route(logits_t, router_bias, n_tiles):
    e, rows, lanes = logits_t.shape
    tok = jax.ShapeDtypeStruct((rows, lanes), I32)
    tokf = jax.ShapeDtypeStruct((rows, lanes), F32)
    full = pl.BlockSpec((rows, lanes), lambda i: (0, 0))
    return pl.pallas_call(
        functools.partial(_route_kernel, n_tiles),
        grid=(1,),
        in_specs=[pl.BlockSpec((e, rows, lanes), lambda i: (0, 0, 0)),
                  pl.BlockSpec(memory_space=pltpu.SMEM)],
        out_specs=[full, full, full, full, pl.BlockSpec((SUBLANES, LANES), lambda i: (0, 0))],
        out_shape=[tok, tok, tokf, tokf, jax.ShapeDtypeStruct((SUBLANES, LANES), I32)],
        compiler_params=_cparams(("arbitrary",)),
        name="route",
    )(logits_t, router_bias)


_ROWS_PER_STEP = 512


_DMA_UNROLL = 8


def _scatter_rows_kernel(p1_ref, p2_ref, src_ref, dst_in_ref, dst_ref, sem):
    del dst_in_ref

    def copies(r):
        row = src_ref.at[pl.ds(r, 1)]
        return (pltpu.make_async_copy(row, dst_ref.at[pl.ds(p1_ref[0, 0, r], 1)], sem.at[0]),
                pltpu.make_async_copy(row, dst_ref.at[pl.ds(p2_ref[0, 0, r], 1)], sem.at[1]))

    def start(r, carry):
        for cp in copies(r):
            cp.start()
        return carry

    def wait(r, carry):
        for cp in copies(r):
            cp.wait()
        return carry

    lax.fori_loop(0, _ROWS_PER_STEP, start, 0, unroll=_DMA_UNROLL)
    lax.fori_loop(0, _ROWS_PER_STEP, wait, 0, unroll=_DMA_UNROLL)


def _scatter_rows(src, pos1, pos2, n_dst):
    n, d = src.shape
    steps = n // _ROWS_PER_STEP
    idx = pl.BlockSpec((1, 1, _ROWS_PER_STEP), lambda i: (i, 0, 0), memory_space=pltpu.SMEM)
    any_spec = pl.BlockSpec(memory_space=pl.ANY)
    dst0 = jnp.zeros((n_dst, d), src.dtype)
    return pl.pallas_call(
        _scatter_rows_kernel,
        grid=(steps,),
        in_specs=[idx, idx, pl.BlockSpec((_ROWS_PER_STEP, d), lambda i: (i, 0)), any_spec],
        out_specs=any_spec,
        out_shape=jax.ShapeDtypeStruct((n_dst, d), src.dtype),
        scratch_shapes=[pltpu.SemaphoreType.DMA((2,))],
        input_output_aliases={3: 0},
        compiler_params=_cparams(("arbitrary",), VMEM_LIMIT),
        name="scatter_rows",
    )(pos1.reshape(steps, 1, _ROWS_PER_STEP), pos2.reshape(steps, 1, _ROWS_PER_STEP), src, dst0)


def _moe_kernel(te_ref, nv_ref, x_ref, wg_ref, wu_ref, wd_ref, y_ref):
    i = pl.program_id(0)

    @pl.when(i < nv_ref[0])
    def _():
        x = x_ref[...].astype(BF16)
        a = _dot(x, wg_ref[0])
        b = _dot(x, wu_ref[0])
        hidden = (_silu(a) * b).astype(BF16)
        y_ref[...] = _dot(hidden, wd_ref[0])

    @pl.when(i >= nv_ref[0])
    def _():
        y_ref[...] = jnp.zeros_like(y_ref)


def _moe_experts(xs, tile_e, n_valid, wg16, wu16, wd16):
    n_rows, d = xs.shape
    f = wg16.shape[2]
    tm = MOE_TILE
    n_tiles = n_rows // tm

    def xmap(i, te, nv):
        return (jnp.minimum(i, nv[0] - 1), 0)

    grid_spec = pltpu.PrefetchScalarGridSpec(
        num_scalar_prefetch=2,
        grid=(n_tiles,),
        in_specs=[pl.BlockSpec((tm, d), xmap),
                  pl.BlockSpec((1, d, f), lambda i, te, nv: (te[i], 0, 0)),
                  pl.BlockSpec((1, d, f), lambda i, te, nv: (te[i], 0, 0)),
                  pl.BlockSpec((1, f, d), lambda i, te, nv: (te[i], 0, 0))],
        out_specs=pl.BlockSpec((tm, d), lambda i, te, nv: (i, 0)),
    )
    return pl.pallas_call(
        _moe_kernel,
        grid_spec=grid_spec,
        out_shape=jax.ShapeDtypeStruct((n_rows, d), F32),
        compiler_params=_cparams(("arbitrary",), VMEM_LIMIT),
        name="moe_experts",
    )(tile_e, n_valid, xs, wg16, wu16, wd16)


_COMBINE_ROWS = 256


def _combine_kernel(with_next, steps, p1_ref, p2_ref, p1n_ref, p2n_ref, ys_ref, w1_ref, w2_ref,
                    x_ref, gate_ref, g_ref, b_ref, *rest):
    if with_next:
        sc_ref, sh_ref, xo_ref, h_ref, ybuf, sem = rest
    else:
        xo_ref, ybuf, sem = rest
    step = pl.program_id(0)
    slot = step % 2

    def copies(pa, pb, s, r):
        return (pltpu.make_async_copy(ys_ref.at[pl.ds(pa[0, 0, r], 1)],
                                      ybuf.at[s, 0, pl.ds(r, 1)], sem.at[s, 0]),
                pltpu.make_async_copy(ys_ref.at[pl.ds(pb[0, 0, r], 1)],
                                      ybuf.at[s, 1, pl.ds(r, 1)], sem.at[s, 1]))

    def start_all(pa, pb, s):
        def body(r, carry):
            for cp in copies(pa, pb, s, r):
                cp.start()
            return carry
        lax.fori_loop(0, _COMBINE_ROWS, body, 0, unroll=_DMA_UNROLL)

    def wait_all(pa, pb, s):
        def body(r, carry):
            for cp in copies(pa, pb, s, r):
                cp.wait()
            return carry
        lax.fori_loop(0, _COMBINE_ROWS, body, 0, unroll=_DMA_UNROLL)

    @pl.when(step == 0)
    def _():
        start_all(p1_ref, p2_ref, 0)

    @pl.when(step + 1 < steps)
    def _():
        start_all(p1n_ref, p2n_ref, 1 - slot)

    wait_all(p1_ref, p2_ref, slot)
    y = w1_ref[0] * ybuf[slot, 0] + w2_ref[0] * ybuf[slot, 1]
    z = DEEPNORM_ALPHA * x_ref[0] + (1.0 + gate_ref[0]) * y
    xn = _layer_norm_rows(z, g_ref[...], b_ref[...])
    xo_ref[0] = xn
    if with_next:
        h_ref[0] = (xn * (1.0 + sc_ref[0]) + sh_ref[0]).astype(BF16)


def _combine(ys, pos1, pos2, w1, w2, x, gate, ln_g, ln_b, next_mod):
    bsz, t, d = x.shape
    tr = _COMBINE_ROWS
    per_b = t // tr
    steps = bsz * per_b
    row = pl.BlockSpec((1, tr, d), lambda s: (s // per_b, s % per_b, 0))
    vec = pl.BlockSpec((1, d), lambda s: (0, 0))
    mod = pl.BlockSpec((1, 1, d), lambda s: (s // per_b, 0, 0))
    wcol = pl.BlockSpec((1, tr, 1), lambda s: (s // per_b, s % per_b, 0))
    idx = pl.BlockSpec((1, 1, tr), lambda s: (s, 0, 0), memory_space=pltpu.SMEM)
    idx_next = pl.BlockSpec((1, 1, tr), lambda s: (jnp.minimum(s + 1, steps - 1), 0, 0),
                            memory_space=pltpu.SMEM)
    with_next = next_mod is not None
    p1 = pos1.reshape(steps, 1, tr)
    p2 = pos2.reshape(steps, 1, tr)
    in_specs = [idx, idx, idx_next, idx_next, pl.BlockSpec(memory_space=pl.ANY),
                wcol, wcol, row, mod, vec, vec]
    args = [p1, p2, p1, p2, ys, w1.reshape(bsz, t, 1), w2.reshape(bsz, t, 1), x, gate,
            ln_g.reshape(1, d), ln_b.reshape(1, d)]
    out_specs = [row]
    out_shape = [jax.ShapeDtypeStruct((bsz, t, d), F32)]
    if with_next:
        in_specs += [mod, mod]
        args += list(next_mod)
        out_specs.append(row)
        out_shape.append(jax.ShapeDtypeStruct((bsz, t, d), BF16))
    res = pl.pallas_call(
        functools.partial(_combine_kernel, with_next, steps),
        grid=(steps,),
        in_specs=in_specs,
        out_specs=out_specs,
        out_shape=out_shape,
        scratch_shapes=[pltpu.VMEM((2, 2, tr, d), F32), pltpu.SemaphoreType.DMA((2, 2))],
        compiler_params=_cparams(("arbitrary",), VMEM_LIMIT),
        name="combine_ln",
    )(*args)
    return res if with_next else (res[0], None)


def kernel(x, c, emb_ln_g, emb_ln_b, w_ada, b_ada, w_in, conv_w, mix_beta, w_out, hg_lb_logits,
           ln_g, ln_b, w_router, router_bias, w_gate, w_up, w_down):
    bsz, t, d = x.shape
    n = bsz * t
    assert n % LANES == 0 and t % SEQ_CHUNK == 0
    n_tiles = 2 * n // MOE_TILE + N_EXPERTS
    assert n_tiles <= LANES

    mod = _ada_mod(c, w_ada, b_ada)
    mod = mod.reshape(DEPTH, bsz, 6, 1, d)

    def mods(l):
        return [mod[l, :, i] for i in range(6)]

    wr_hi = jnp.pad(w_router, ((0, 0), (0, LANES - N_EXPERTS)))
    wr_hi16 = wr_hi.astype(BF16)
    wr_lo16 = (wr_hi - wr_hi16.astype(F32)).astype(BF16)

    m = [mods(l) for l in range(DEPTH)]
    xcur, h = _ln_mod(x, emb_ln_g, emb_ln_b, m[0][1], m[0][0])
    for l in range(DEPTH):
        _, _, gate1, shift2, scale2, gate2 = m[l]
        beta = mix_beta[l].reshape(1, -1)
        proj = _in_proj(h.reshape(n, d), w_in[l].astype(BF16), F32).reshape(bsz, t, IN_COLS)
        yc = _conv_mixer(proj, conv_w[l], beta)
        yr = _ret_mixer(proj, beta)
        yh = _hg_mixer(proj, hg_lb_logits, beta, l)
        x1, h2, logits = _out_proj(yc, yr, yh, w_out[l].astype(BF16), xcur, gate1,
                                   ln_g[l, 0], ln_b[l, 0], scale2, shift2, wr_hi16, wr_lo16)
        logits_t = logits.reshape(n, LANES)[:, :N_EXPERTS].T.reshape(N_EXPERTS, n // LANES, LANES)
        pos1, pos2, w1, w2, tiles = _route(logits_t, router_bias, n_tiles)
        xs = _scatter_rows(h2.reshape(n, d), pos1.reshape(n), pos2.reshape(n), n_tiles * MOE_TILE)
        ys = _moe_experts(xs, tiles[0, :n_tiles], tiles[1, :1], w_gate[l].astype(BF16),
                          w_up[l].astype(BF16), w_down[l].astype(BF16))
        next_mod = (m[l + 1][1], m[l + 1][0]) if l + 1 < DEPTH else None
        xcur, h = _combine(ys, pos1.reshape(n), pos2.reshape(n), w1, w2, x1, gate2,
                           ln_g[l, 1], ln_b[l, 1], next_mod)
    return xcur
```

```python
import functools
import math

import jax
import jax.numpy as jnp
from jax import lax
from jax.experimental import pallas as pl
from jax.experimental.pallas import tpu as pltpu

F32 = jnp.float32
BF16 = jnp.bfloat16
I32 = jnp.int32

DEPTH = 2
CONV_WIDTH = 512
RET_WIDTH = 768
HG_WIDTH = 768
HEAD_DIM = 128
RET_HEADS = RET_WIDTH // HEAD_DIM
HG_HEADS = HG_WIDTH // HEAD_DIM
IN_COLS = 3 * CONV_WIDTH + 4 * RET_WIDTH + 5 * HG_WIDTH
ROPE_BASE = 10000.0
N_EXPERTS = 16
N_GROUPS = 4
EXPERTS_PER_GROUP = N_EXPERTS // N_GROUPS
MASK_NEG = -1e9
DEEPNORM_ALPHA = (2.0 * DEPTH) ** 0.25
LN_EPS = 1e-5
HEAD_EPS = 1e-6
FORGET_FLOOR = 1e-6

LANES = 128
SUBLANES = 8

_CB, _CC, _CH = 0, 4, 8
_RQ, _RK, _RV, _RG = 12, 18, 24, 30
_HQ, _HFF, _HFB, _HI, _HG = 36, 42, 48, 54, 60
_OUT_RET, _OUT_HG = 4, 10

SEQ_CHUNK = 128
MOE_TILE = 256
VMEM_LIMIT = 56 * 1024 * 1024


def _cparams(sem, vmem=None):
    return pltpu.CompilerParams(dimension_semantics=sem, vmem_limit_bytes=vmem)


def _silu(x):
    return x * jax.nn.sigmoid(x)


def _dot(a, b):
    return jnp.dot(a, b, preferred_element_type=F32)


def _dot_nt(a, b):
    return lax.dot_general(a, b, (((1,), (1,)), ((), ())), preferred_element_type=F32)


def _dot_tn(a, b):
    return lax.dot_general(a, b, (((0,), (0,)), ((), ())), preferred_element_type=F32)


def _ada_kernel(c_ref, w_ref, b_ref, o_ref):
    cond = _silu(c_ref[...])
    o_ref[0] = _dot(cond.astype(BF16), w_ref[0].astype(BF16)) + b_ref[0]


def _ada_mod(c, w_ada, b_ada):
    depth, d, n6 = w_ada.shape
    b = c.shape[0]
    bp = -(-b // SUBLANES) * SUBLANES
    cp = jnp.pad(c, ((0, bp - b), (0, 0)))
    tn = 512
    out = pl.pallas_call(
        _ada_kernel,
        grid=(depth, n6 // tn),
        in_specs=[
            pl.BlockSpec((bp, d), lambda l, j: (0, 0)),
            pl.BlockSpec((1, d, tn), lambda l, j: (l, 0, j)),
            pl.BlockSpec((1, 1, tn), lambda l, j: (l, 0, j)),
        ],
        out_specs=pl.BlockSpec((1, bp, tn), lambda l, j: (l, 0, j)),
        out_shape=jax.ShapeDtypeStruct((depth, bp, n6), F32),
        compiler_params=_cparams(("arbitrary", "arbitrary")),
        name="ada_mod",
    )(cp, w_ada, b_ada.reshape(depth, 1, n6))
    return out[:, :b, :]


def _layer_norm_rows(z, g, b):
    mu = jnp.mean(z, axis=-1, keepdims=True)
    zc = z - mu
    var = jnp.mean(zc * zc, axis=-1, keepdims=True)
    return zc * lax.rsqrt(var + LN_EPS) * g + b


def _ln_mod_kernel(x_ref, g_ref, b_ref, sc_ref, sh_ref, xo_ref, h_ref):
    y = _layer_norm_rows(x_ref[0], g_ref[...], b_ref[...])
    xo_ref[0] = y
    h_ref[0] = (y * (1.0 + sc_ref[0]) + sh_ref[0]).astype(BF16)


def _ln_mod(x, g, b, scale, shift):
    bsz, t, d = x.shape
    tr = 512
    row = pl.BlockSpec((1, tr, d), lambda i, j: (i, j, 0))
    vec = pl.BlockSpec((1, d), lambda i, j: (0, 0))
    mod = pl.BlockSpec((1, 1, d), lambda i, j: (i, 0, 0))
    return pl.pallas_call(
        _ln_mod_kernel,
        grid=(bsz, t // tr),
        in_specs=[row, vec, vec, mod, mod],
        out_specs=[row, row],
        out_shape=[jax.ShapeDtypeStruct((bsz, t, d), F32),
                   jax.ShapeDtypeStruct((bsz, t, d), BF16)],
        compiler_params=_cparams(("arbitrary", "arbitrary")),
        name="ln_mod",
    )(x, g.reshape(1, d), b.reshape(1, d), scale, shift)


def _in_proj_kernel(a_ref, w_ref, o_ref, w16_scr):
    @pl.when(pl.program_id(1) == 0)
    def _():
        w16_scr[...] = w_ref[0].astype(BF16)

    o_ref[...] = _dot(a_ref[...], w16_scr[...]).astype(o_ref.dtype)


def _in_proj(h, w_in, layer, out_dtype):
    n, d = h.shape
    nc = w_in.shape[2]
    tm, tn = 1024, 768
    tm = min(tm, n)
    return pl.pallas_call(
        _in_proj_kernel,
        grid=(nc // tn, n // tm),
        in_specs=[pl.BlockSpec((tm, d), lambda j, i: (i, 0)),
                  pl.BlockSpec((1, d, tn), lambda j, i: (layer, 0, j))],
        out_specs=pl.BlockSpec((tm, tn), lambda j, i: (i, j)),
        out_shape=jax.ShapeDtypeStruct((n, nc), out_dtype),
        scratch_shapes=[pltpu.VMEM((d, tn), BF16)],
        compiler_params=_cparams(("arbitrary", "arbitrary"), VMEM_LIMIT),
        name="in_proj",
    )(h, w_in)


def _conv_kernel(cb_ref, cc_ref, ch_ref, w_ref, beta_ref, o_ref):
    u = cc_ref[0].astype(F32) * ch_ref[0].astype(F32)
    t = u.shape[0]
    row = lax.broadcasted_iota(I32, u.shape, 0)
    prev = jnp.where(row == 0, 0.0, pltpu.roll(u, 1, 0))
    nxt = jnp.where(row == t - 1, 0.0, pltpu.roll(u, t - 1, 0))
    w = w_ref[...]
    y = cb_ref[0].astype(F32) * (prev * w[0:1] + u * w[1:2] + nxt * w[2:3])
    o_ref[0] = (y * beta_ref[...]).astype(o_ref.dtype)


def _conv_mixer(proj, conv_w, beta):
    bsz, t, _ = proj.shape
    cw = 256
    nb = CONV_WIDTH // cw
    per = LANES * 1

    def col(off):
        return pl.BlockSpec((1, t, cw), lambda b, j, off=off: (b, 0, off * per // cw + j))

    return pl.pallas_call(
        _conv_kernel,
        grid=(bsz, nb),
        in_specs=[col(_CB), col(_CC), col(_CH),
                  pl.BlockSpec((3, cw), lambda b, j: (0, j)),
                  pl.BlockSpec((1, cw), lambda b, j: (0, j))],
        out_specs=pl.BlockSpec((1, t, cw), lambda b, j: (b, 0, j)),
        out_shape=jax.ShapeDtypeStruct((bsz, t, CONV_WIDTH), BF16),
        compiler_params=_cparams(("arbitrary", "arbitrary"), VMEM_LIMIT),
        name="conv_mixer",
    )(proj, proj, proj, conv_w, beta)


def _ret_kernel(q_ref, k_ref, v_ref, g_ref, cos_ref, sin_ref, lg_ref, beta_ref,
                o_ref, o_scr, qf_scr, qb_scr, uf_scr, ub_scr):
    L = SEQ_CHUNK
    t = q_ref.shape[1]
    n = t // L
    lgf = lg_ref[0, 0:1, :]
    lgb = lg_ref[0, 1:2, :]
    r = lax.broadcasted_iota(I32, (L, L), 0).astype(F32)
    c = lax.broadcasted_iota(I32, (L, L), 1).astype(F32)
    rel = r - c
    dbi = jnp.where(rel > 0, jnp.exp(lgf * jnp.maximum(rel, 0.0)),
                    jnp.where(rel < 0, jnp.exp(lgb * jnp.maximum(-rel, 0.0)), 2.0))
    qf = jnp.exp(lgf * (r + 1.0))
    kf = jnp.exp(lgf * (L - 1.0 - r))
    qb = jnp.exp(lgb * (L - r))
    kb = jnp.exp(lgb * r)
    gf_l = jnp.exp(lgf * float(L))
    gb_l = jnp.exp(lgb * float(L))
    scale = HEAD_DIM ** -0.5

    def rot(x, cs, sn):
        return x * cs + pltpu.roll(x, HEAD_DIM // 2, 1) * sn

    def rows(ci):
        return pl.ds(pl.multiple_of(ci * L, L), L)

    def local_body(ci, carry):
        sl = rows(ci)
        cs = cos_ref[sl, :]
        sn = sin_ref[sl, :]
        q = rot(q_ref[0, sl, :].astype(F32), cs, sn)
        k = rot(k_ref[0, sl, :].astype(F32), cs, sn) * scale
        v16 = v_ref[0, sl, :].astype(BF16)
        s = _dot_nt(q.astype(BF16), k.astype(BF16)) * dbi
        o_scr[sl, :] = _dot(s.astype(BF16), v16)
        qf_scr[sl, :] = (q * qf).astype(BF16)
        qb_scr[sl, :] = (q * qb).astype(BF16)
        uf_scr[ci] = _dot_tn((k * kf).astype(BF16), v16)
        ub_scr[ci] = _dot_tn((k * kb).astype(BF16), v16)
        return carry

    lax.fori_loop(0, n, local_body, 0, unroll=2)

    def scan_body(i, states):
        sf, sb = states
        cf, cb = i, n - 1 - i
        slf, slb = rows(cf), rows(cb)
        o_scr[slf, :] += _dot(qf_scr[slf, :], sf.astype(BF16))
        o_scr[slb, :] += _dot(qb_scr[slb, :], sb.astype(BF16))
        return gf_l * sf + uf_scr[cf], gb_l * sb + ub_scr[cb]

    zero = jnp.zeros((HEAD_DIM, HEAD_DIM), F32)
    lax.fori_loop(0, n, scan_body, (zero, zero), unroll=2)

    beta = beta_ref[...]

    def norm_body(ci, carry):
        sl = rows(ci)
        o = o_scr[sl, :]
        mu = jnp.mean(o, axis=-1, keepdims=True)
        oc = o - mu
        var = jnp.mean(oc * oc, axis=-1, keepdims=True)
        y = oc * lax.rsqrt(var + HEAD_EPS) * _silu(g_ref[0, sl, :].astype(F32)) * beta
        o_ref[0, sl, :] = y.astype(o_ref.dtype)
        return carry

    lax.fori_loop(0, n, norm_body, 0, unroll=2)


def _rotary_tables(t):
    half = HEAD_DIM // 2
    inv_freq = ROPE_BASE ** (-jnp.arange(half, dtype=F32) / half)
    ang = jnp.arange(t, dtype=F32)[:, None] * inv_freq[None, :]
    cos, sin = jnp.cos(ang), jnp.sin(ang)
    return jnp.concatenate([cos, cos], -1), jnp.concatenate([-sin, sin], -1)


def _ret_log_decays():
    head = jnp.arange(RET_HEADS, dtype=F32)
    lg_f = jnp.log1p(-jnp.exp2(-5.0 - head))
    lg_b = jnp.log1p(-jnp.exp2(-5.5 - head))
    lg = jnp.stack([lg_f, lg_b], axis=1)
    return jnp.broadcast_to(lg[:, :, None], (RET_HEADS, 2, LANES))


def _ret_mixer(proj, beta):
    bsz, t, _ = proj.shape
    n_chunks = t // SEQ_CHUNK
    assert n_chunks % 2 == 0
    cosf, sinf = _rotary_tables(t)

    def col(off):
        return pl.BlockSpec((1, t, HEAD_DIM), lambda b, h, off=off: (b, 0, off + h))

    tab = pl.BlockSpec((t, HEAD_DIM), lambda b, h: (0, 0))
    return pl.pallas_call(
        _ret_kernel,
        grid=(bsz, RET_HEADS),
        in_specs=[col(_RQ), col(_RK), col(_RV), col(_RG), tab, tab,
                  pl.BlockSpec((1, 2, LANES), lambda b, h: (h, 0, 0)),
                  pl.BlockSpec((1, HEAD_DIM), lambda b, h: (0, _OUT_RET + h))],
        out_specs=pl.BlockSpec((1, t, HEAD_DIM), lambda b, h: (b, 0, h)),
        out_shape=jax.ShapeDtypeStruct((bsz, t, RET_WIDTH), BF16),
        scratch_shapes=[pltpu.VMEM((t, HEAD_DIM), F32),
                        pltpu.VMEM((t, HEAD_DIM), BF16), pltpu.VMEM((t, HEAD_DIM), BF16),
                        pltpu.VMEM((n_chunks, HEAD_DIM, HEAD_DIM), F32),
                        pltpu.VMEM((n_chunks, HEAD_DIM, HEAD_DIM), F32)],
        compiler_params=_cparams(("arbitrary", "arbitrary"), VMEM_LIMIT),
        name="ret_mixer",
    )(proj, proj, proj, proj, cosf, sinf, _ret_log_decays(), beta)


_HG_BASE = SUBLANES
_HG_LEVELS = tuple(m for m in (8, 16, 32, 64) if m < SEQ_CHUNK)


def _cumsum_rows(x, reverse):
    n = x.shape[0]
    row = lax.broadcasted_iota(I32, x.shape, 0)
    sh = 1
    while sh < n:
        if reverse:
            x = x + jnp.where(row < n - sh, pltpu.roll(x, n - sh, 0), 0.0)
        else:
            x = x + jnp.where(row >= sh, pltpu.roll(x, sh, 0), 0.0)
        sh *= 2
    return x


def _hg_masks(reverse):
    C = SEQ_CHUNK
    r = lax.broadcasted_iota(I32, (C, C), 0)
    c = lax.broadcasted_iota(I32, (C, C), 1)
    if reverse:
        r, c = c, r
    sh = _HG_BASE.bit_length() - 1
    masks = [((r >> sh) == (c >> sh)) & (c <= r)]
    for m in _HG_LEVELS:
        sh = m.bit_length() - 1
        masks.append(((r >> (sh + 1)) == (c >> (sh + 1)))
                     & (((r >> sh) & 1) == 1) & (((c >> sh) & 1) == 0))
    return [jnp.where(m, 1.0, 0.0) for m in masks]


def _hg_scores(q, kk, cum, mask_ref, reverse):
    C = SEQ_CHUNK
    d = 1 if reverse else 0

    def split(x, blk):
        return x.reshape(C // blk, blk, LANES)

    b = _HG_BASE
    ref_row = b // 2 if reverse else b // 2 - 1
    cum3 = split(cum, b)
    rel = cum3 - cum3[:, ref_row:ref_row + 1, :]
    qt = (split(q, b) * jnp.exp(rel)).reshape(C, LANES).astype(BF16)
    kt = (split(kk, b) * jnp.exp(-rel)).reshape(C, LANES).astype(BF16)
    a = jnp.where(mask_ref[d, 0] > 0.5, _dot_nt(qt, kt), 0.0)
    for li, m in enumerate(_HG_LEVELS):
        blk = 2 * m
        ref_row = m if reverse else m - 1
        cum3 = split(cum, blk)
        rel = cum3 - cum3[:, ref_row:ref_row + 1, :]
        dec = jnp.exp(-jnp.abs(rel)).reshape(C, LANES)
        halves = []
        for j in range(C // m):
            is_query = (j % 2 == 1) != reverse
            halves.append((q if is_query else kk)[j * m:(j + 1) * m])
        x = (jnp.concatenate(halves, axis=0) * dec).astype(BF16)
        a = a + mask_ref[d, li + 1] * _dot_nt(x, x)
    return a


def _hg_kernel(layer, q_ref, zf_ref, zb_ref, v_ref, g_ref, lbl_ref, beta_ref, o_ref,
               o_scr, qe_scr, u_scr, dec_scr, mask_scr):
    C = SEQ_CHUNK
    t = q_ref.shape[1]
    n = t // C
    for d, reverse in enumerate((False, True)):
        for li, m in enumerate(_hg_masks(reverse)):
            mask_scr[d, li] = m
    logits = lbl_ref[...].astype(F32)
    e = jnp.exp(logits - jnp.max(logits, axis=0, keepdims=True))
    p = e / jnp.sum(e, axis=0, keepdims=True)
    lb = p[0]
    for l in range(1, layer + 1):
        lb = lb + p[l]
    lb = lb - p[0]

    def gates(z_ref, sl, lb_row):
        f = lb_row + (1.0 - lb_row) * jax.nn.sigmoid(z_ref[0, sl, :].astype(F32))
        f = jnp.maximum(f, FORGET_FLOOR)
        return jnp.log(f), 1.0 - f

    def rows(ci):
        return pl.ds(pl.multiple_of(ci * C, C), C)

    def local_dir(ci, sl, q, v16, z_ref, d):
        reverse = d == 1
        logf, kk = gates(z_ref, sl, lb[d:d + 1, :])
        cum = _cumsum_rows(logf, reverse)
        a = _hg_scores(q, kk, cum, mask_scr, reverse)
        total = cum[0:1, :] if reverse else cum[C - 1:C, :]
        qe_scr[d, sl, :] = (q * jnp.exp(cum)).astype(BF16)
        k_tail = kk * jnp.exp(total - cum)
        u_scr[d, ci] = _dot_tn(v16, k_tail.astype(BF16))
        dec_scr[d, ci] = jnp.broadcast_to(jnp.exp(total), (SUBLANES, LANES))
        return _dot(a.astype(BF16), v16)

    def local_body(ci, carry):
        sl = rows(ci)
        q = q_ref[0, sl, :].astype(F32)
        v16 = v_ref[0, sl, :].astype(BF16)
        o_scr[sl, :] = (local_dir(ci, sl, q, v16, zf_ref, 0)
                        + local_dir(ci, sl, q, v16, zb_ref, 1))
        return carry

    lax.fori_loop(0, n, local_body, 0)

    def scan_body(i, states):
        sf, sb = states
        cf, cb = i, n - 1 - i
        slf, slb = rows(cf), rows(cb)
        o_scr[slf, :] += _dot_nt(qe_scr[0, slf, :], sf.astype(BF16))
        o_scr[slb, :] += _dot_nt(qe_scr[1, slb, :], sb.astype(BF16))
        return (sf * dec_scr[0, cf, 0:1, :] + u_scr[0, cf],
                sb * dec_scr[1, cb, 0:1, :] + u_scr[1, cb])

    zero = jnp.zeros((HEAD_DIM, HEAD_DIM), F32)
    lax.fori_loop(0, n, scan_body, (zero, zero), unroll=2)

    beta = beta_ref[...]

    def norm_body(ci, carry):
        sl = rows(ci)
        o = o_scr[sl, :]
        ms = jnp.mean(o * o, axis=-1, keepdims=True)
        y = o * lax.rsqrt(ms + HEAD_EPS) * _silu(g_ref[0, sl, :].astype(F32)) * beta
        o_ref[0, sl, :] = y.astype(o_ref.dtype)
        return carry

    lax.fori_loop(0, n, norm_body, 0, unroll=2)


def _hg_mixer(proj, lb_logits, beta, layer):
    bsz, t, _ = proj.shape
    n_chunks = t // SEQ_CHUNK
    assert n_chunks % 2 == 0

    def col(off):
        return pl.BlockSpec((1, t, HEAD_DIM), lambda b, h, off=off: (b, 0, off + h))

    return pl.pallas_call(
        functools.partial(_hg_kernel, layer),
        grid=(bsz, HG_HEADS),
        in_specs=[col(_HQ), col(_HFF), col(_HFB), col(_HI), col(_HG),
                  pl.BlockSpec((DEPTH, 2, HEAD_DIM), lambda b, h: (0, 0, h)),
                  pl.BlockSpec((1, HEAD_DIM), lambda b, h: (0, _OUT_HG + h))],
        out_specs=pl.BlockSpec((1, t, HEAD_DIM), lambda b, h: (b, 0, h)),
        out_shape=jax.ShapeDtypeStruct((bsz, t, HG_WIDTH), BF16),
        scratch_shapes=[pltpu.VMEM((t, HEAD_DIM), F32),
                        pltpu.VMEM((2, t, HEAD_DIM), BF16),
                        pltpu.VMEM((2, n_chunks, HEAD_DIM, HEAD_DIM), F32),
                        pltpu.VMEM((2, n_chunks, SUBLANES, LANES), F32),
                        pltpu.VMEM((2, 1 + len(_HG_LEVELS), SEQ_CHUNK, SEQ_CHUNK), F32)],
        compiler_params=_cparams(("arbitrary", "arbitrary"), VMEM_LIMIT),
        name="hg_mixer",
    )(proj, proj, proj, proj, proj, lb_logits, beta)


def _out_proj_kernel(yc_ref, yr_ref, yh_ref, w_ref, x_ref, gate_ref, g_ref, b_ref,
                     sc_ref, sh_ref, wrh_ref, wrl_ref, xo_ref, h_ref, lo_ref):
    c0, c1 = CONV_WIDTH, CONV_WIDTH + RET_WIDTH
    y = (_dot(yc_ref[0], w_ref[0, 0:c0, :]) + _dot(yr_ref[0], w_ref[0, c0:c1, :])
         + _dot(yh_ref[0], w_ref[0, c1:, :]))
    z = DEEPNORM_ALPHA * x_ref[0] + (1.0 + gate_ref[0]) * y
    xn = _layer_norm_rows(z, g_ref[...], b_ref[...])
    xo_ref[0] = xn
    h = xn * (1.0 + sc_ref[0]) + sh_ref[0]
    h_ref[0] = h
    h_hi = h.astype(BF16)
    h_lo = (h - h_hi.astype(F32)).astype(BF16)
    wrh = wrh_ref[...]
    lo_ref[0] = _dot(h_hi, wrh) + _dot(h_lo, wrh) + _dot(h_hi, wrl_ref[...])


def _out_proj(yc, yr, yh, w_out16, layer, x, gate, ln_g, ln_b, scale2, shift2, wr_hi, wr_lo):
    bsz, t, d = x.shape
    tm = 512

    def row(width):
        return pl.BlockSpec((1, tm, width), lambda b, i: (b, i, 0))

    vec = pl.BlockSpec((1, d), lambda b, i: (0, 0))
    mod = pl.BlockSpec((1, 1, d), lambda b, i: (b, 0, 0))
    wr = pl.BlockSpec((d, LANES), lambda b, i: (0, 0))
    return pl.pallas_call(
        _out_proj_kernel,
        grid=(bsz, t // tm),
        in_specs=[row(CONV_WIDTH), row(RET_WIDTH), row(HG_WIDTH),
                  pl.BlockSpec((1, d, d), lambda b, i: (layer, 0, 0)),
                  row(d), mod, vec, vec, mod, mod, wr, wr],
        out_specs=[row(d), row(d), row(LANES)],
        out_shape=[jax.ShapeDtypeStruct((bsz, t, d), F32),
                   jax.ShapeDtypeStruct((bsz, t, d), F32),
                   jax.ShapeDtypeStruct((bsz, t, LANES), F32)],
        compiler_params=_cparams(("arbitrary", "arbitrary"), VMEM_LIMIT),
        name="out_proj",
    )(yc, yr, yh, w_out16, x, gate, ln_g.reshape(1, d), ln_b.reshape(1, d),
      scale2, shift2, wr_hi, wr_lo)


def _route_kernel(n_tiles, lt_ref, bias_ref, pos1_ref, pos2_ref, w1_ref, w2_ref, tile_ref):
    E = N_EXPERTS
    logits = [lt_ref[e] for e in range(E)]
    shape = logits[0].shape
    mx = functools.reduce(jnp.maximum, logits)
    ex = [jnp.exp(l - mx) for l in logits]
    den = functools.reduce(lambda a, b: a + b, ex)
    scores = [x / den for x in ex]
    sel = [scores[e] + bias_ref[e] for e in range(E)]

    best_g = jnp.zeros(shape, I32)
    best_v = None
    for g in range(N_GROUPS):
        a, b, c, d = sel[EXPERTS_PER_GROUP * g: EXPERTS_PER_GROUP * (g + 1)]
        hi1, lo1 = jnp.maximum(a, b), jnp.minimum(a, b)
        hi2, lo2 = jnp.maximum(c, d), jnp.minimum(c, d)
        gs = jnp.maximum(hi1, hi2) + jnp.maximum(jnp.minimum(hi1, hi2), jnp.maximum(lo1, lo2))
        if g == 0:
            best_v = gs
        else:
            better = gs > best_v
            best_g = jnp.where(better, g, best_g)
            best_v = jnp.where(better, gs, best_v)

    masked = [jnp.where(best_g == (e // EXPERTS_PER_GROUP), sel[e], MASK_NEG) for e in range(E)]

    def arg_top(vals, exclude):
        idx = jnp.zeros(shape, I32)
        val = None
        for e in range(E):
            v = vals[e] if exclude is None else jnp.where(exclude == e, -jnp.inf, vals[e])
            if e == 0:
                val = v
            else:
                better = v > val
                idx = jnp.where(better, e, idx)
                val = jnp.where(better, v, val)
        return idx

    idx1 = arg_top(masked, None)
    idx2 = arg_top(masked, idx1)
    s1 = functools.reduce(lambda a, b: a + b, [jnp.where(idx1 == e, scores[e], 0.0) for e in range(E)])
    s2 = functools.reduce(lambda a, b: a + b, [jnp.where(idx2 == e, scores[e], 0.0) for e in range(E)])
    w1_ref[...] = s1 / (s1 + s2)
    w2_ref[...] = s2 / (s1 + s2)

    rows = shape[0]
    li = lax.broadcasted_iota(I32, (LANES, LANES), 0)
    lj = lax.broadcasted_iota(I32, (LANES, LANES), 1)
    upper = (li <= lj).astype(BF16)
    ri = lax.broadcasted_iota(I32, (rows, rows), 0)
    rj = lax.broadcasted_iota(I32, (rows, rows), 1)
    lower = (rj < ri).astype(BF16)
    tile_start = (lax.broadcasted_iota(I32, (1, LANES), 1) * MOE_TILE).astype(F32)
    start = jnp.zeros((1, 1), F32)
    pos1 = jnp.zeros(shape, F32)
    pos2 = jnp.zeros(shape, F32)
    tile_e = jnp.zeros((1, LANES), F32)
    for e in range(E):
        hit1 = idx1 == e
        hit2 = idx2 == e
        onehot = jnp.where(hit1 | hit2, 1.0, 0.0)
        pref = _dot(onehot.astype(BF16), upper)
        row_tot = pref[:, LANES - 1:LANES]
        row_off = _dot(lower, jnp.broadcast_to(row_tot, shape).astype(BF16))[:, 0:1]
        rank = pref - onehot + row_off
        count = jnp.sum(onehot, keepdims=True)
        dest = start + rank
        pos1 = jnp.where(hit1, dest, pos1)
        pos2 = jnp.where(hit2, dest, pos2)
        start = start + jnp.ceil(count / MOE_TILE) * MOE_TILE
        tile_e = tile_e + jnp.where(tile_start >= start, 1.0, 0.0)
    pos1_ref[...] = pos1.astype(I32)
    pos2_ref[...] = pos2.astype(I32)
    n_valid = start / MOE_TILE
    tile_id = lax.broadcasted_iota(I32, (1, LANES), 1).astype(F32)
    last_e = jnp.sum(jnp.where(tile_id == n_valid - 1.0, tile_e, 0.0), keepdims=True)
    tile_e = jnp.where(tile_id < n_valid, tile_e, last_e)
    sub = lax.broadcasted_iota(I32, (SUBLANES, LANES), 0)
    tile_ref[...] = jnp.where(sub == 0, tile_e, jnp.broadcast_to(n_valid, (SUBLANES, LANES))).astype(I32)


def _route(logits_t, router_bias, n_tiles):
    e, rows, lanes = logits_t.shape
    tok = jax.ShapeDtypeStruct((rows, lanes), I32)
    tokf = jax.ShapeDtypeStruct((rows, lanes), F32)
    full = pl.BlockSpec((rows, lanes), lambda i: (0, 0))
    return pl.pallas_call(
        functools.partial(_route_kernel, n_tiles),
        grid=(1,),
        in_specs=[pl.BlockSpec((e, rows, lanes), lambda i: (0, 0, 0)),
                  pl.BlockSpec(memory_space=pltpu.SMEM)],
        out_specs=[full, full, full, full, pl.BlockSpec((SUBLANES, LANES), lambda i: (0, 0))],
        out_shape=[tok, tok, tokf, tokf, jax.ShapeDtypeStruct((SUBLANES, LANES), I32)],
        compiler_params=_cparams(("arbitrary",)),
        name="route",
    )(logits_t, router_bias)


_ROWS_PER_STEP = 512


_DMA_UNROLL = 8


def _scatter_rows_kernel(p1_ref, p2_ref, src_ref, dst_in_ref, dst_ref, sem):
    del dst_in_ref

    def copies(r):
        row = src_ref.at[pl.ds(r, 1)]
        return (pltpu.make_async_copy(row, dst_ref.at[pl.ds(p1_ref[0, 0, r], 1)], sem.at[0]),
                pltpu.make_async_copy(row, dst_ref.at[pl.ds(p2_ref[0, 0, r], 1)], sem.at[1]))

    def start(r, carry):
        for cp in copies(r):
            cp.start()
        return carry

    def wait(r, carry):
        for cp in copies(r):
            cp.wait()
        return carry

    lax.fori_loop(0, _ROWS_PER_STEP, start, 0, unroll=_DMA_UNROLL)
    lax.fori_loop(0, _ROWS_PER_STEP, wait, 0, unroll=_DMA_UNROLL)


def _scatter_rows(src, pos1, pos2, n_dst):
    n, d = src.shape
    steps = n // _ROWS_PER_STEP
    idx = pl.BlockSpec((1, 1, _ROWS_PER_STEP), lambda i: (i, 0, 0), memory_space=pltpu.SMEM)
    any_spec = pl.BlockSpec(memory_space=pl.ANY)
    dst0 = jnp.zeros((n_dst, d), src.dtype)
    return pl.pallas_call(
        _scatter_rows_kernel,
        grid=(steps,),
        in_specs=[idx, idx, pl.BlockSpec((_ROWS_PER_STEP, d), lambda i: (i, 0)), any_spec],
        out_specs=any_spec,
        out_shape=jax.ShapeDtypeStruct((n_dst, d), src.dtype),
        scratch_shapes=[pltpu.SemaphoreType.DMA((2,))],
        input_output_aliases={3: 0},
        compiler_params=_cparams(("arbitrary",), VMEM_LIMIT),
        name="scatter_rows",
    )(pos1.reshape(steps, 1, _ROWS_PER_STEP), pos2.reshape(steps, 1, _ROWS_PER_STEP), src, dst0)


def _moe_kernel(te_ref, nv_ref, x_ref, wg_ref, wu_ref, wd_ref, y_ref):
    i = pl.program_id(0)

    @pl.when(i < nv_ref[0])
    def _():
        x = x_ref[...].astype(BF16)
        a = _dot(x, wg_ref[0, 0])
        b = _dot(x, wu_ref[0, 0])
        hidden = (_silu(a) * b).astype(BF16)
        y_ref[...] = _dot(hidden, wd_ref[0, 0])

    @pl.when(i >= nv_ref[0])
    def _():
        y_ref[...] = jnp.zeros_like(y_ref)


def _moe_experts(xs, tile_e, n_valid, wg16, wu16, wd16, layer):
    n_rows, d = xs.shape
    f = wg16.shape[3]
    tm = MOE_TILE
    n_tiles = n_rows // tm

    def xmap(i, te, nv):
        return (jnp.minimum(i, nv[0] - 1), 0)

    def wmap(i, te, nv):
        return (layer, te[i], 0, 0)

    grid_spec = pltpu.PrefetchScalarGridSpec(
        num_scalar_prefetch=2,
        grid=(n_tiles,),
        in_specs=[pl.BlockSpec((tm, d), xmap),
                  pl.BlockSpec((1, 1, d, f), wmap),
                  pl.BlockSpec((1, 1, d, f), wmap),
                  pl.BlockSpec((1, 1, f, d), wmap)],
        out_specs=pl.BlockSpec((tm, d), lambda i, te, nv: (i, 0)),
    )
    return pl.pallas_call(
        _moe_kernel,
        grid_spec=grid_spec,
        out_shape=jax.ShapeDtypeStruct((n_rows, d), F32),
        compiler_params=_cparams(("arbitrary",), VMEM_LIMIT),
        name="moe_experts",
    )(tile_e, n_valid, xs, wg16, wu16, wd16)


_COMBINE_ROWS = 256


def _combine_kernel(with_next, steps, p1_ref, p2_ref, p1n_ref, p2n_ref, ys_ref, w1_ref, w2_ref,
                    x_ref, gate_ref, g_ref, b_ref, *rest):
    if with_next:
        sc_ref, sh_ref, xo_ref, h_ref, ybuf, sem = rest
    else:
        xo_ref, ybuf, sem = rest
    step = pl.program_id(0)
    slot = step % 2

    def copies(pa, pb, s, r):
        return (pltpu.make_async_copy(ys_ref.at[pl.ds(pa[0, 0, r], 1)],
                                      ybuf.at[s, 0, pl.ds(r, 1)], sem.at[s, 0]),
                pltpu.make_async_copy(ys_ref.at[pl.ds(pb[0, 0, r], 1)],
                                      ybuf.at[s, 1, pl.ds(r, 1)], sem.at[s, 1]))

    def start_all(pa, pb, s):
        def body(r, carry):
            for cp in copies(pa, pb, s, r):
                cp.start()
            return carry
        lax.fori_loop(0, _COMBINE_ROWS, body, 0, unroll=_DMA_UNROLL)

    def wait_all(pa, pb, s):
        def body(r, carry):
            for cp in copies(pa, pb, s, r):
                cp.wait()
            return carry
        lax.fori_loop(0, _COMBINE_ROWS, body, 0, unroll=_DMA_UNROLL)

    @pl.when(step == 0)
    def _():
        start_all(p1_ref, p2_ref, 0)

    @pl.when(step + 1 < steps)
    def _():
        start_all(p1n_ref, p2n_ref, 1 - slot)

    wait_all(p1_ref, p2_ref, slot)
    y = w1_ref[0] * ybuf[slot, 0] + w2_ref[0] * ybuf[slot, 1]
    z = DEEPNORM_ALPHA * x_ref[0] + (1.0 + gate_ref[0]) * y
    xn = _layer_norm_rows(z, g_ref[...], b_ref[...])
    xo_ref[0] = xn
    if with_next:
        h_ref[0] = (xn * (1.0 + sc_ref[0]) + sh_ref[0]).astype(BF16)


def _combine(ys, pos1, pos2, w1, w2, x, gate, ln_g, ln_b, next_mod):
    bsz, t, d = x.shape
    tr = _COMBINE_ROWS
    per_b = t // tr
    steps = bsz * per_b
    row = pl.BlockSpec((1, tr, d), lambda s: (s // per_b, s % per_b, 0))
    vec = pl.BlockSpec((1, d), lambda s: (0, 0))
    mod = pl.BlockSpec((1, 1, d), lambda s: (s // per_b, 0, 0))
    wcol = pl.BlockSpec((1, tr, 1), lambda s: (s // per_b, s % per_b, 0))
    idx = pl.BlockSpec((1, 1, tr), lambda s: (s, 0, 0), memory_space=pltpu.SMEM)
    idx_next = pl.BlockSpec((1, 1, tr), lambda s: (jnp.minimum(s + 1, steps - 1), 0, 0),
                            memory_space=pltpu.SMEM)
    with_next = next_mod is not None
    p1 = pos1.reshape(steps, 1, tr)
    p2 = pos2.reshape(steps, 1, tr)
    in_specs = [idx, idx, idx_next, idx_next, pl.BlockSpec(memory_space=pl.ANY),
                wcol, wcol, row, mod, vec, vec]
    args = [p1, p2, p1, p2, ys, w1.reshape(bsz, t, 1), w2.reshape(bsz, t, 1), x, gate,
            ln_g.reshape(1, d), ln_b.reshape(1, d)]
    out_specs = [row]
    out_shape = [jax.ShapeDtypeStruct((bsz, t, d), F32)]
    if with_next:
        in_specs += [mod, mod]
        args += list(next_mod)
        out_specs.append(row)
        out_shape.append(jax.ShapeDtypeStruct((bsz, t, d), BF16))
    res = pl.pallas_call(
        functools.partial(_combine_kernel, with_next, steps),
        grid=(steps,),
        in_specs=in_specs,
        out_specs=out_specs,
        out_shape=out_shape,
        scratch_shapes=[pltpu.VMEM((2, 2, tr, d), F32), pltpu.SemaphoreType.DMA((2, 2))],
        compiler_params=_cparams(("arbitrary",), VMEM_LIMIT),
        name="combine_ln",
    )(*args)
    return res if with_next else (res[0], None)


def kernel(x, c, emb_ln_g, emb_ln_b, w_ada, b_ada, w_in, conv_w, mix_beta, w_out, hg_lb_logits,
           ln_g, ln_b, w_router, router_bias, w_gate, w_up, w_down):
    bsz, t, d = x.shape
    n = bsz * t
    assert n % LANES == 0 and t % SEQ_CHUNK == 0
    n_tiles = 2 * n // MOE_TILE + N_EXPERTS
    assert n_tiles <= LANES

    mod = _ada_mod(c, w_ada, b_ada)
    mod = mod.reshape(DEPTH, bsz, 6, 1, d)

    def mods(l):
        return [mod[l, :, i] for i in range(6)]

    wr_hi = jnp.pad(w_router, ((0, 0), (0, LANES - N_EXPERTS)))
    wr_hi16 = wr_hi.astype(BF16)
    wr_lo16 = (wr_hi - wr_hi16.astype(F32)).astype(BF16)

    w_out16 = w_out.astype(BF16)
    wg16, wu16, wd16 = w_gate.astype(BF16), w_up.astype(BF16), w_down.astype(BF16)

    m = [mods(l) for l in range(DEPTH)]
    xcur, h = _ln_mod(x, emb_ln_g, emb_ln_b, m[0][1], m[0][0])
    for l in range(DEPTH):
        _, _, gate1, shift2, scale2, gate2 = m[l]
        beta = mix_beta[l].reshape(1, -1)
        proj = _in_proj(h.reshape(n, d), w_in, l, BF16).reshape(bsz, t, IN_COLS)
        yc = _conv_mixer(proj, conv_w[l], beta)
        yr = _ret_mixer(proj, beta)
        yh = _hg_mixer(proj, hg_lb_logits, beta, l)
        x1, h2, logits = _out_proj(yc, yr, yh, w_out16, l, xcur, gate1,
                                   ln_g[l, 0], ln_b[l, 0], scale2, shift2, wr_hi16, wr_lo16)
        logits_t = logits.reshape(n, LANES)[:, :N_EXPERTS].T.reshape(N_EXPERTS, n // LANES, LANES)
        pos1, pos2, w1, w2, tiles = _route(logits_t, router_bias, n_tiles)
        xs = _scatter_rows(h2.reshape(n, d), pos1.reshape(n), pos2.reshape(n), n_tiles * MOE_TILE)
        ys = _moe_experts(xs, tiles[0, :n_tiles], tiles[1, :1], wg16, wu16, wd16, l)
        next_mod = (m[l + 1][1], m[l + 1][0]) if l + 1 < DEPTH else None
        xcur, h = _combine(ys, pos1.reshape(n), pos2.reshape(n), w1, w2, x1, gate2,
                           ln_g[l, 1], ln_b[l, 1], next_mod)
    return xcur
```

```python
import functools
import math

import jax
import jax.numpy as jnp
from jax import lax
from jax.experimental import pallas as pl
from jax.experimental.pallas import tpu as pltpu

F32 = jnp.float32
BF16 = jnp.bfloat16
I32 = jnp.int32

DEPTH = 2
CONV_WIDTH = 512
RET_WIDTH = 768
HG_WIDTH = 768
HEAD_DIM = 128
RET_HEADS = RET_WIDTH // HEAD_DIM
HG_HEADS = HG_WIDTH // HEAD_DIM
IN_COLS = 3 * CONV_WIDTH + 4 * RET_WIDTH + 5 * HG_WIDTH
ROPE_BASE = 10000.0
N_EXPERTS = 16
N_GROUPS = 4
EXPERTS_PER_GROUP = N_EXPERTS // N_GROUPS
MASK_NEG = -1e9
DEEPNORM_ALPHA = (2.0 * DEPTH) ** 0.25
LN_EPS = 1e-5
HEAD_EPS = 1e-6
FORGET_FLOOR = 1e-6

LANES = 128
SUBLANES = 8

_CB, _CC, _CH = 0, 4, 8
_RQ, _RK, _RV, _RG = 12, 18, 24, 30
_HQ, _HFF, _HFB, _HI, _HG = 36, 42, 48, 54, 60
_OUT_RET, _OUT_HG = 4, 10

SEQ_CHUNK = 128
MOE_TILE = 256
VMEM_LIMIT = 56 * 1024 * 1024


def _cparams(sem, vmem=None):
    return pltpu.CompilerParams(dimension_semantics=sem, vmem_limit_bytes=vmem)


def _silu(x):
    return x * jax.nn.sigmoid(x)


def _dot(a, b):
    return jnp.dot(a, b, preferred_element_type=F32)


def _dot_nt(a, b):
    return lax.dot_general(a, b, (((1,), (1,)), ((), ())), preferred_element_type=F32)


def _dot_tn(a, b):
    return lax.dot_general(a, b, (((0,), (0,)), ((), ())), preferred_element_type=F32)


def _ada_kernel(c_ref, w_ref, b_ref, o_ref):
    cond = _silu(c_ref[...])
    o_ref[0] = _dot(cond.astype(BF16), w_ref[0].astype(BF16)) + b_ref[0]


def _ada_mod(c, w_ada, b_ada):
    depth, d, n6 = w_ada.shape
    b = c.shape[0]
    bp = -(-b // SUBLANES) * SUBLANES
    cp = jnp.pad(c, ((0, bp - b), (0, 0)))
    tn = 512
    out = pl.pallas_call(
        _ada_kernel,
        grid=(depth, n6 // tn),
        in_specs=[
            pl.BlockSpec((bp, d), lambda l, j: (0, 0)),
            pl.BlockSpec((1, d, tn), lambda l, j: (l, 0, j)),
            pl.BlockSpec((1, 1, tn), lambda l, j: (l, 0, j)),
        ],
        out_specs=pl.BlockSpec((1, bp, tn), lambda l, j: (l, 0, j)),
        out_shape=jax.ShapeDtypeStruct((depth, bp, n6), F32),
        compiler_params=_cparams(("arbitrary", "arbitrary")),
        name="ada_mod",
    )(cp, w_ada, b_ada.reshape(depth, 1, n6))
    return out[:, :b, :]


def _layer_norm_rows(z, g, b):
    mu = jnp.mean(z, axis=-1, keepdims=True)
    zc = z - mu
    var = jnp.mean(zc * zc, axis=-1, keepdims=True)
    return zc * lax.rsqrt(var + LN_EPS) * g + b


def _ln_mod_kernel(x_ref, g_ref, b_ref, sc_ref, sh_ref, xo_ref, h_ref):
    y = _layer_norm_rows(x_ref[0], g_ref[...], b_ref[...])
    xo_ref[0] = y
    h_ref[0] = (y * (1.0 + sc_ref[0]) + sh_ref[0]).astype(BF16)


def _ln_mod(x, g, b, scale, shift):
    bsz, t, d = x.shape
    tr = 512
    row = pl.BlockSpec((1, tr, d), lambda i, j: (i, j, 0))
    vec = pl.BlockSpec((1, d), lambda i, j: (0, 0))
    mod = pl.BlockSpec((1, 1, d), lambda i, j: (i, 0, 0))
    return pl.pallas_call(
        _ln_mod_kernel,
        grid=(bsz, t // tr),
        in_specs=[row, vec, vec, mod, mod],
        out_specs=[row, row],
        out_shape=[jax.ShapeDtypeStruct((bsz, t, d), F32),
                   jax.ShapeDtypeStruct((bsz, t, d), BF16)],
        compiler_params=_cparams(("arbitrary", "arbitrary")),
        name="ln_mod",
    )(x, g.reshape(1, d), b.reshape(1, d), scale, shift)


def _in_proj_kernel(a_ref, w_ref, o_ref, w16_scr):
    @pl.when(pl.program_id(1) == 0)
    def _():
        w16_scr[...] = w_ref[0].astype(BF16)

    o_ref[...] = _dot(a_ref[...], w16_scr[...]).astype(o_ref.dtype)


def _in_proj(h, w_in, layer, out_dtype):
    n, d = h.shape
    nc = w_in.shape[2]
    tm, tn = 1024, 1408
    tm = min(tm, n)
    return pl.pallas_call(
        _in_proj_kernel,
        grid=(nc // tn, n // tm),
        in_specs=[pl.BlockSpec((tm, d), lambda j, i: (i, 0)),
                  pl.BlockSpec((1, d, tn), lambda j, i: (layer, 0, j))],
        out_specs=pl.BlockSpec((tm, tn), lambda j, i: (i, j)),
        out_shape=jax.ShapeDtypeStruct((n, nc), out_dtype),
        scratch_shapes=[pltpu.VMEM((d, tn), BF16)],
        compiler_params=_cparams(("arbitrary", "arbitrary"), VMEM_LIMIT),
        name="in_proj",
    )(h, w_in)


def _conv_kernel(cb_ref, cc_ref, ch_ref, w_ref, beta_ref, o_ref):
    u = cc_ref[0].astype(F32) * ch_ref[0].astype(F32)
    t = u.shape[0]
    row = lax.broadcasted_iota(I32, u.shape, 0)
    prev = jnp.where(row == 0, 0.0, pltpu.roll(u, 1, 0))
    nxt = jnp.where(row == t - 1, 0.0, pltpu.roll(u, t - 1, 0))
    w = w_ref[...]
    y = cb_ref[0].astype(F32) * (prev * w[0:1] + u * w[1:2] + nxt * w[2:3])
    o_ref[0] = (y * beta_ref[...]).astype(o_ref.dtype)


def _conv_mixer(proj, conv_w, beta):
    bsz, t, _ = proj.shape
    cw = 256
    nb = CONV_WIDTH // cw
    per = LANES * 1

    def col(off):
        return pl.BlockSpec((1, t, cw), lambda b, j, off=off: (b, 0, off * per // cw + j))

    return pl.pallas_call(
        _conv_kernel,
        grid=(bsz, nb),
        in_specs=[col(_CB), col(_CC), col(_CH),
                  pl.BlockSpec((3, cw), lambda b, j: (0, j)),
                  pl.BlockSpec((1, cw), lambda b, j: (0, j))],
        out_specs=pl.BlockSpec((1, t, cw), lambda b, j: (b, 0, j)),
        out_shape=jax.ShapeDtypeStruct((bsz, t, CONV_WIDTH), BF16),
        compiler_params=_cparams(("arbitrary", "arbitrary"), VMEM_LIMIT),
        name="conv_mixer",
    )(proj, proj, proj, conv_w, beta)


def _ret_kernel(q_ref, k_ref, v_ref, g_ref, cos_ref, sin_ref, lg_ref, beta_ref,
                o_ref, o_scr, qf_scr, qb_scr, uf_scr, ub_scr):
    L = SEQ_CHUNK
    t = q_ref.shape[1]
    n = t // L
    lgf = lg_ref[0, 0:1, :]
    lgb = lg_ref[0, 1:2, :]
    r = lax.broadcasted_iota(I32, (L, L), 0).astype(F32)
    c = lax.broadcasted_iota(I32, (L, L), 1).astype(F32)
    rel = r - c
    dbi = jnp.where(rel > 0, jnp.exp(lgf * jnp.maximum(rel, 0.0)),
                    jnp.where(rel < 0, jnp.exp(lgb * jnp.maximum(-rel, 0.0)), 2.0))
    qf = jnp.exp(lgf * (r + 1.0))
    kf = jnp.exp(lgf * (L - 1.0 - r))
    qb = jnp.exp(lgb * (L - r))
    kb = jnp.exp(lgb * r)
    gf_l = jnp.exp(lgf * float(L))
    gb_l = jnp.exp(lgb * float(L))
    scale = HEAD_DIM ** -0.5

    def rot(x, cs, sn):
        return x * cs + pltpu.roll(x, HEAD_DIM // 2, 1) * sn

    def rows(ci):
        return pl.ds(pl.multiple_of(ci * L, L), L)

    def local_body(ci, carry):
        sl = rows(ci)
        cs = cos_ref[sl, :]
        sn = sin_ref[sl, :]
        q = rot(q_ref[0, sl, :].astype(F32), cs, sn)
        k = rot(k_ref[0, sl, :].astype(F32), cs, sn) * scale
        v16 = v_ref[0, sl, :].astype(BF16)
        s = _dot_nt(q.astype(BF16), k.astype(BF16)) * dbi
        o_scr[sl, :] = _dot(s.astype(BF16), v16)
        qf_scr[sl, :] = (q * qf).astype(BF16)
        qb_scr[sl, :] = (q * qb).astype(BF16)
        uf_scr[ci] = _dot_tn((k * kf).astype(BF16), v16)
        ub_scr[ci] = _dot_tn((k * kb).astype(BF16), v16)
        return carry

    lax.fori_loop(0, n, local_body, 0, unroll=4)

    def scan_body(i, states):
        sf, sb = states
        cf, cb = i, n - 1 - i
        slf, slb = rows(cf), rows(cb)
        o_scr[slf, :] += _dot(qf_scr[slf, :], sf.astype(BF16))
        o_scr[slb, :] += _dot(qb_scr[slb, :], sb.astype(BF16))
        return gf_l * sf + uf_scr[cf], gb_l * sb + ub_scr[cb]

    zero = jnp.zeros((HEAD_DIM, HEAD_DIM), F32)
    lax.fori_loop(0, n, scan_body, (zero, zero), unroll=2)

    beta = beta_ref[...]

    def norm_body(ci, carry):
        sl = rows(ci)
        o = o_scr[sl, :]
        mu = jnp.mean(o, axis=-1, keepdims=True)
        oc = o - mu
        var = jnp.mean(oc * oc, axis=-1, keepdims=True)
        y = oc * lax.rsqrt(var + HEAD_EPS) * _silu(g_ref[0, sl, :].astype(F32)) * beta
        o_ref[0, sl, :] = y.astype(o_ref.dtype)
        return carry

    lax.fori_loop(0, n, norm_body, 0, unroll=2)


def _rotary_tables(t):
    half = HEAD_DIM // 2
    inv_freq = ROPE_BASE ** (-jnp.arange(half, dtype=F32) / half)
    ang = jnp.arange(t, dtype=F32)[:, None] * inv_freq[None, :]
    cos, sin = jnp.cos(ang), jnp.sin(ang)
    return jnp.concatenate([cos, cos], -1), jnp.concatenate([-sin, sin], -1)


def _ret_log_decays():
    head = jnp.arange(RET_HEADS, dtype=F32)
    lg_f = jnp.log1p(-jnp.exp2(-5.0 - head))
    lg_b = jnp.log1p(-jnp.exp2(-5.5 - head))
    lg = jnp.stack([lg_f, lg_b], axis=1)
    return jnp.broadcast_to(lg[:, :, None], (RET_HEADS, 2, LANES))


def _ret_mixer(proj, beta):
    bsz, t, _ = proj.shape
    n_chunks = t // SEQ_CHUNK
    assert n_chunks % 2 == 0
    cosf, sinf = _rotary_tables(t)

    def col(off):
        return pl.BlockSpec((1, t, HEAD_DIM), lambda b, h, off=off: (b, 0, off + h))

    tab = pl.BlockSpec((t, HEAD_DIM), lambda b, h: (0, 0))
    return pl.pallas_call(
        _ret_kernel,
        grid=(bsz, RET_HEADS),
        in_specs=[col(_RQ), col(_RK), col(_RV), col(_RG), tab, tab,
                  pl.BlockSpec((1, 2, LANES), lambda b, h: (h, 0, 0)),
                  pl.BlockSpec((1, HEAD_DIM), lambda b, h: (0, _OUT_RET + h))],
        out_specs=pl.BlockSpec((1, t, HEAD_DIM), lambda b, h: (b, 0, h)),
        out_shape=jax.ShapeDtypeStruct((bsz, t, RET_WIDTH), BF16),
        scratch_shapes=[pltpu.VMEM((t, HEAD_DIM), F32),
                        pltpu.VMEM((t, HEAD_DIM), BF16), pltpu.VMEM((t, HEAD_DIM), BF16),
                        pltpu.VMEM((n_chunks, HEAD_DIM, HEAD_DIM), F32),
                        pltpu.VMEM((n_chunks, HEAD_DIM, HEAD_DIM), F32)],
        compiler_params=_cparams(("arbitrary", "arbitrary"), VMEM_LIMIT),
        name="ret_mixer",
    )(proj, proj, proj, proj, cosf, sinf, _ret_log_decays(), beta)


_HG_BASE = SUBLANES
_HG_LEVELS = tuple(m for m in (8, 16, 32, 64) if m < SEQ_CHUNK)


def _cumsum_rows(x, reverse):
    n = x.shape[0]
    row = lax.broadcasted_iota(I32, x.shape, 0)
    sh = 1
    while sh < n:
        if reverse:
            x = x + jnp.where(row < n - sh, pltpu.roll(x, n - sh, 0), 0.0)
        else:
            x = x + jnp.where(row >= sh, pltpu.roll(x, sh, 0), 0.0)
        sh *= 2
    return x


def _hg_masks(reverse):
    C = SEQ_CHUNK
    r = lax.broadcasted_iota(I32, (C, C), 0)
    c = lax.broadcasted_iota(I32, (C, C), 1)
    if reverse:
        r, c = c, r
    sh = _HG_BASE.bit_length() - 1
    masks = [((r >> sh) == (c >> sh)) & (c <= r)]
    for m in _HG_LEVELS:
        sh = m.bit_length() - 1
        masks.append(((r >> (sh + 1)) == (c >> (sh + 1)))
                     & (((r >> sh) & 1) == 1) & (((c >> sh) & 1) == 0))
    return [jnp.where(m, 1.0, 0.0) for m in masks]


def _hg_scores(q, kk, cum, mask_ref, reverse):
    C = SEQ_CHUNK
    d = 1 if reverse else 0

    def split(x, blk):
        return x.reshape(C // blk, blk, LANES)

    b = _HG_BASE
    ref_row = b // 2 if reverse else b // 2 - 1
    cum3 = split(cum, b)
    rel = cum3 - cum3[:, ref_row:ref_row + 1, :]
    qt = (split(q, b) * jnp.exp(rel)).reshape(C, LANES).astype(BF16)
    kt = (split(kk, b) * jnp.exp(-rel)).reshape(C, LANES).astype(BF16)
    a = jnp.where(mask_ref[d, 0] > 0.5, _dot_nt(qt, kt), 0.0)
    for li, m in enumerate(_HG_LEVELS):
        blk = 2 * m
        ref_row = m if reverse else m - 1
        cum3 = split(cum, blk)
        rel = cum3 - cum3[:, ref_row:ref_row + 1, :]
        dec = jnp.exp(-jnp.abs(rel)).reshape(C, LANES)
        halves = []
        for j in range(C // m):
            is_query = (j % 2 == 1) != reverse
            halves.append((q if is_query else kk)[j * m:(j + 1) * m])
        x = (jnp.concatenate(halves, axis=0) * dec).astype(BF16)
        a = a + mask_ref[d, li + 1] * _dot_nt(x, x)
    return a


def _hg_kernel(layer, q_ref, zf_ref, zb_ref, v_ref, g_ref, lbl_ref, beta_ref, o_ref,
               o_scr, qe_scr, u_scr, dec_scr, mask_scr):
    C = SEQ_CHUNK
    t = q_ref.shape[1]
    n = t // C
    for d, reverse in enumerate((False, True)):
        for li, m in enumerate(_hg_masks(reverse)):
            mask_scr[d, li] = m
    logits = lbl_ref[...].astype(F32)
    e = jnp.exp(logits - jnp.max(logits, axis=0, keepdims=True))
    p = e / jnp.sum(e, axis=0, keepdims=True)
    lb = p[0]
    for l in range(1, layer + 1):
        lb = lb + p[l]
    lb = lb - p[0]

    def gates(z_ref, sl, lb_row):
        f = lb_row + (1.0 - lb_row) * jax.nn.sigmoid(z_ref[0, sl, :].astype(F32))
        f = jnp.maximum(f, FORGET_FLOOR)
        return jnp.log(f), 1.0 - f

    def rows(ci):
        return pl.ds(pl.multiple_of(ci * C, C), C)

    def local_dir(ci, sl, q, v16, z_ref, d):
        reverse = d == 1
        logf, kk = gates(z_ref, sl, lb[d:d + 1, :])
        cum = _cumsum_rows(logf, reverse)
        a = _hg_scores(q, kk, cum, mask_scr, reverse)
        total = cum[0:1, :] if reverse else cum[C - 1:C, :]
        qe_scr[d, sl, :] = (q * jnp.exp(cum)).astype(BF16)
        k_tail = kk * jnp.exp(total - cum)
        u_scr[d, ci] = _dot_tn(v16, k_tail.astype(BF16))
        dec_scr[d, ci] = jnp.broadcast_to(jnp.exp(total), (SUBLANES, LANES))
        return _dot(a.astype(BF16), v16)

    def local_body(ci, carry):
        sl = rows(ci)
        q = q_ref[0, sl, :].astype(F32)
        v16 = v_ref[0, sl, :].astype(BF16)
        o_scr[sl, :] = (local_dir(ci, sl, q, v16, zf_ref, 0)
                        + local_dir(ci, sl, q, v16, zb_ref, 1))
        return carry

    lax.fori_loop(0, n, local_body, 0, unroll=2)

    def scan_body(i, states):
        sf, sb = states
        cf, cb = i, n - 1 - i
        slf, slb = rows(cf), rows(cb)
        o_scr[slf, :] += _dot_nt(qe_scr[0, slf, :], sf.astype(BF16))
        o_scr[slb, :] += _dot_nt(qe_scr[1, slb, :], sb.astype(BF16))
        return (sf * dec_scr[0, cf, 0:1, :] + u_scr[0, cf],
                sb * dec_scr[1, cb, 0:1, :] + u_scr[1, cb])

    zero = jnp.zeros((HEAD_DIM, HEAD_DIM), F32)
    lax.fori_loop(0, n, scan_body, (zero, zero), unroll=2)

    beta = beta_ref[...]

    def norm_body(ci, carry):
        sl = rows(ci)
        o = o_scr[sl, :]
        ms = jnp.mean(o * o, axis=-1, keepdims=True)
        y = o * lax.rsqrt(ms + HEAD_EPS) * _silu(g_ref[0, sl, :].astype(F32)) * beta
        o_ref[0, sl, :] = y.astype(o_ref.dtype)
        return carry

    lax.fori_loop(0, n, norm_body, 0, unroll=2)


def _hg_mixer(proj, lb_logits, beta, layer):
    bsz, t, _ = proj.shape
    n_chunks = t // SEQ_CHUNK
    assert n_chunks % 2 == 0

    def col(off):
        return pl.BlockSpec((1, t, HEAD_DIM), lambda b, h, off=off: (b, 0, off + h))

    return pl.pallas_call(
        functools.partial(_hg_kernel, layer),
        grid=(bsz, HG_HEADS),
        in_specs=[col(_HQ), col(_HFF), col(_HFB), col(_HI), col(_HG),
                  pl.BlockSpec((DEPTH, 2, HEAD_DIM), lambda b, h: (0, 0, h)),
                  pl.BlockSpec((1, HEAD_DIM), lambda b, h: (0, _OUT_HG + h))],
        out_specs=pl.BlockSpec((1, t, HEAD_DIM), lambda b, h: (b, 0, h)),
        out_shape=jax.ShapeDtypeStruct((bsz, t, HG_WIDTH), BF16),
        scratch_shapes=[pltpu.VMEM((t, HEAD_DIM), F32),
                        pltpu.VMEM((2, t, HEAD_DIM), BF16),
                        pltpu.VMEM((2, n_chunks, HEAD_DIM, HEAD_DIM), F32),
                        pltpu.VMEM((2, n_chunks, SUBLANES, LANES), F32),
                        pltpu.VMEM((2, 1 + len(_HG_LEVELS), SEQ_CHUNK, SEQ_CHUNK), F32)],
        compiler_params=_cparams(("arbitrary", "arbitrary"), VMEM_LIMIT),
        name="hg_mixer",
    )(proj, proj, proj, proj, proj, lb_logits, beta)


_OUT_PROJ_SPLIT = 2


def _out_proj_kernel(yc_ref, yr_ref, yh_ref, w_ref, x_ref, gate_ref, g_ref, b_ref,
                     sc_ref, sh_ref, wrh_ref, wrl_ref, xo_ref, h_ref, lo_ref):
    c0, c1 = CONV_WIDTH, CONV_WIDTH + RET_WIDTH
    tm = x_ref.shape[1]
    for half in range(_OUT_PROJ_SPLIT):
        sl = pl.ds(half * (tm // _OUT_PROJ_SPLIT), tm // _OUT_PROJ_SPLIT)
        y = (_dot(yc_ref[0, sl, :], w_ref[0, 0:c0, :]) + _dot(yr_ref[0, sl, :], w_ref[0, c0:c1, :])
             + _dot(yh_ref[0, sl, :], w_ref[0, c1:, :]))
        z = DEEPNORM_ALPHA * x_ref[0, sl, :] + (1.0 + gate_ref[0]) * y
        xn = _layer_norm_rows(z, g_ref[...], b_ref[...])
        xo_ref[0, sl, :] = xn
        h = xn * (1.0 + sc_ref[0]) + sh_ref[0]
        h_ref[0, sl, :] = h
        h_hi = h.astype(BF16)
        h_lo = (h - h_hi.astype(F32)).astype(BF16)
        wrh = wrh_ref[...]
        lo_ref[0, sl, :] = _dot(h_hi, wrh) + _dot(h_lo, wrh) + _dot(h_hi, wrl_ref[...])


def _out_proj(yc, yr, yh, w_out16, layer, x, gate, ln_g, ln_b, scale2, shift2, wr_hi, wr_lo):
    bsz, t, d = x.shape
    tm = 512

    def row(width):
        return pl.BlockSpec((1, tm, width), lambda b, i: (b, i, 0))

    vec = pl.BlockSpec((1, d), lambda b, i: (0, 0))
    mod = pl.BlockSpec((1, 1, d), lambda b, i: (b, 0, 0))
    wr = pl.BlockSpec((d, LANES), lambda b, i: (0, 0))
    return pl.pallas_call(
        _out_proj_kernel,
        grid=(bsz, t // tm),
        in_specs=[row(CONV_WIDTH), row(RET_WIDTH), row(HG_WIDTH),
                  pl.BlockSpec((1, d, d), lambda b, i: (layer, 0, 0)),
                  row(d), mod, vec, vec, mod, mod, wr, wr],
        out_specs=[row(d), row(d), row(LANES)],
        out_shape=[jax.ShapeDtypeStruct((bsz, t, d), F32),
                   jax.ShapeDtypeStruct((bsz, t, d), F32),
                   jax.ShapeDtypeStruct((bsz, t, LANES), F32)],
        compiler_params=_cparams(("arbitrary", "arbitrary"), VMEM_LIMIT),
        name="out_proj",
    )(yc, yr, yh, w_out16, x, gate, ln_g.reshape(1, d), ln_b.reshape(1, d),
      scale2, shift2, wr_hi, wr_lo)


def _route_kernel(n_tiles, lt_ref, bias_ref, pos1_ref, pos2_ref, w1_ref, w2_ref, tile_ref):
    E = N_EXPERTS
    logits = [lt_ref[e] for e in range(E)]
    shape = logits[0].shape
    mx = functools.reduce(jnp.maximum, logits)
    ex = [jnp.exp(l - mx) for l in logits]
    den = functools.reduce(lambda a, b: a + b, ex)
    scores = [x / den for x in ex]
    sel = [scores[e] + bias_ref[e] for e in range(E)]

    best_g = jnp.zeros(shape, I32)
    best_v = None
    for g in range(N_GROUPS):
        a, b, c, d = sel[EXPERTS_PER_GROUP * g: EXPERTS_PER_GROUP * (g + 1)]
        hi1, lo1 = jnp.maximum(a, b), jnp.minimum(a, b)
        hi2, lo2 = jnp.maximum(c, d), jnp.minimum(c, d)
        gs = jnp.maximum(hi1, hi2) + jnp.maximum(jnp.minimum(hi1, hi2), jnp.maximum(lo1, lo2))
        if g == 0:
            best_v = gs
        else:
            better = gs > best_v
            best_g = jnp.where(better, g, best_g)
            best_v = jnp.where(better, gs, best_v)

    masked = [jnp.where(best_g == (e // EXPERTS_PER_GROUP), sel[e], MASK_NEG) for e in range(E)]

    def arg_top(vals, exclude):
        idx = jnp.zeros(shape, I32)
        val = None
        for e in range(E):
            v = vals[e] if exclude is None else jnp.where(exclude == e, -jnp.inf, vals[e])
            if e == 0:
                val = v
            else:
                better = v > val
                idx = jnp.where(better, e, idx)
                val = jnp.where(better, v, val)
        return idx

    idx1 = arg_top(masked, None)
    idx2 = arg_top(masked, idx1)
    s1 = functools.reduce(lambda a, b: a + b, [jnp.where(idx1 == e, scores[e], 0.0) for e in range(E)])
    s2 = functools.reduce(lambda a, b: a + b, [jnp.where(idx2 == e, scores[e], 0.0) for e in range(E)])
    w1_ref[...] = s1 / (s1 + s2)
    w2_ref[...] = s2 / (s1 + s2)

    rows = shape[0]
    li = lax.broadcasted_iota(I32, (LANES, LANES), 0)
    lj = lax.broadcasted_iota(I32, (LANES, LANES), 1)
    upper = (li <= lj).astype(BF16)
    ri = lax.broadcasted_iota(I32, (rows, rows), 0)
    rj = lax.broadcasted_iota(I32, (rows, rows), 1)
    lower = (rj < ri).astype(BF16)
    tile_start = (lax.broadcasted_iota(I32, (1, LANES), 1) * MOE_TILE).astype(F32)
    start = jnp.zeros((1, 1), F32)
    pos1 = jnp.zeros(shape, F32)
    pos2 = jnp.zeros(shape, F32)
    tile_e = jnp.zeros((1, LANES), F32)
    for e in range(E):
        hit1 = idx1 == e
        hit2 = idx2 == e
        onehot = jnp.where(hit1 | hit2, 1.0, 0.0)
        pref = _dot(onehot.astype(BF16), upper)
        row_tot = pref[:, LANES - 1:LANES]
        row_off = _dot(lower, jnp.broadcast_to(row_tot, shape).astype(BF16))[:, 0:1]
        rank = pref - onehot + row_off
        count = jnp.sum(onehot, keepdims=True)
        dest = start + rank
        pos1 = jnp.where(hit1, dest, pos1)
        pos2 = jnp.where(hit2, dest, pos2)
        start = start + jnp.ceil(count / MOE_TILE) * MOE_TILE
        tile_e = tile_e + jnp.where(tile_start >= start, 1.0, 0.0)
    pos1_ref[...] = pos1.astype(I32)
    pos2_ref[...] = pos2.astype(I32)
    n_valid = start / MOE_TILE
    tile_id = lax.broadcasted_iota(I32, (1, LANES), 1).astype(F32)
    last_e = jnp.sum(jnp.where(tile_id == n_valid - 1.0, tile_e, 0.0), keepdims=True)
    tile_e = jnp.where(tile_id < n_valid, tile_e, last_e)
    sub = lax.broadcasted_iota(I32, (SUBLANES, LANES), 0)
    tile_ref[...] = jnp.where(sub == 0, tile_e, jnp.broadcast_to(n_valid, (SUBLANES, LANES))).astype(I32)


def _route(logits_t, router_bias, n_tiles):
    e, rows, lanes = logits_t.shape
    tok = jax.ShapeDtypeStruct((rows, lanes), I32)
    tokf = jax.ShapeDtypeStruct((rows, lanes), F32)
    full = pl.BlockSpec((rows, lanes), lambda i: (0, 0))
    return pl.pallas_call(
        functools.partial(_route_kernel, n_tiles),
        grid=(1,),
        in_specs=[pl.BlockSpec((e, rows, lanes), lambda i: (0, 0, 0)),
                  pl.BlockSpec(memory_space=pltpu.SMEM)],
        out_specs=[full, full, full, full, pl.BlockSpec((SUBLANES, LANES), lambda i: (0, 0))],
        out_shape=[tok, tok, tokf, tokf, jax.ShapeDtypeStruct((SUBLANES, LANES), I32)],
        compiler_params=_cparams(("arbitrary",)),
        name="route",
    )(logits_t, router_bias)


_ROWS_PER_STEP = 512


_DMA_UNROLL = 8


def _scatter_rows_kernel(p1_ref, p2_ref, src_ref, dst_in_ref, dst_ref, sem):
    del dst_in_ref

    def copies(r):
        row = src_ref.at[pl.ds(r, 1)]
        return (pltpu.make_async_copy(row, dst_ref.at[pl.ds(p1_ref[0, 0, r], 1)], sem.at[0]),
                pltpu.make_async_copy(row, dst_ref.at[pl.ds(p2_ref[0, 0, r], 1)], sem.at[1]))

    def start(r, carry):
        for cp in copies(r):
            cp.start()
        return carry

    def wait(r, carry):
        for cp in copies(r):
            cp.wait()
        return carry

    lax.fori_loop(0, _ROWS_PER_STEP, start, 0, unroll=_DMA_UNROLL)
    lax.fori_loop(0, _ROWS_PER_STEP, wait, 0, unroll=_DMA_UNROLL)


def _scatter_rows(src, pos1, pos2, n_dst):
    n, d = src.shape
    steps = n // _ROWS_PER_STEP
    idx = pl.BlockSpec((1, 1, _ROWS_PER_STEP), lambda i: (i, 0, 0), memory_space=pltpu.SMEM)
    any_spec = pl.BlockSpec(memory_space=pl.ANY)
    dst0 = jnp.zeros((n_dst, d), src.dtype)
    return pl.pallas_call(
        _scatter_rows_kernel,
        grid=(steps,),
        in_specs=[idx, idx, pl.BlockSpec((_ROWS_PER_STEP, d), lambda i: (i, 0)), any_spec],
        out_specs=any_spec,
        out_shape=jax.ShapeDtypeStruct((n_dst, d), src.dtype),
        scratch_shapes=[pltpu.SemaphoreType.DMA((2,))],
        input_output_aliases={3: 0},
        compiler_params=_cparams(("arbitrary",), VMEM_LIMIT),
        name="scatter_rows",
    )(pos1.reshape(steps, 1, _ROWS_PER_STEP), pos2.reshape(steps, 1, _ROWS_PER_STEP), src, dst0)


def _moe_kernel(te_ref, nv_ref, x_ref, wg_ref, wu_ref, wd_ref, y_ref):
    i = pl.program_id(0)

    @pl.when(i < nv_ref[0])
    def _():
        x = x_ref[...].astype(BF16)
        a = _dot(x, wg_ref[0, 0])
        b = _dot(x, wu_ref[0, 0])
        hidden = (_silu(a) * b).astype(BF16)
        y_ref[...] = _dot(hidden, wd_ref[0, 0])

    @pl.when(i >= nv_ref[0])
    def _():
        y_ref[...] = jnp.zeros_like(y_ref)


def _moe_experts(xs, tile_e, n_valid, wg16, wu16, wd16, layer):
    n_rows, d = xs.shape
    f = wg16.shape[3]
    tm = MOE_TILE
    n_tiles = n_rows // tm

    def xmap(i, te, nv):
        return (jnp.minimum(i, nv[0] - 1), 0)

    def wmap(i, te, nv):
        return (layer, te[i], 0, 0)

    grid_spec = pltpu.PrefetchScalarGridSpec(
        num_scalar_prefetch=2,
        grid=(n_tiles,),
        in_specs=[pl.BlockSpec((tm, d), xmap),
                  pl.BlockSpec((1, 1, d, f), wmap),
                  pl.BlockSpec((1, 1, d, f), wmap),
                  pl.BlockSpec((1, 1, f, d), wmap)],
        out_specs=pl.BlockSpec((tm, d), lambda i, te, nv: (i, 0)),
    )
    return pl.pallas_call(
        _moe_kernel,
        grid_spec=grid_spec,
        out_shape=jax.ShapeDtypeStruct((n_rows, d), F32),
        compiler_params=_cparams(("arbitrary",), VMEM_LIMIT),
        name="moe_experts",
    )(tile_e, n_valid, xs, wg16, wu16, wd16)


_COMBINE_ROWS = 256


def _combine_kernel(with_next, steps, p1_ref, p2_ref, p1n_ref, p2n_ref, ys_ref, w1_ref, w2_ref,
                    x_ref, gate_ref, g_ref, b_ref, *rest):
    if with_next:
        sc_ref, sh_ref, xo_ref, h_ref, ybuf, sem = rest
    else:
        xo_ref, ybuf, sem = rest
    step = pl.program_id(0)
    slot = step % 2

    def copies(pa, pb, s, r):
        return (pltpu.make_async_copy(ys_ref.at[pl.ds(pa[0, 0, r], 1)],
                                      ybuf.at[s, 0, pl.ds(r, 1)], sem.at[s, 0]),
                pltpu.make_async_copy(ys_ref.at[pl.ds(pb[0, 0, r], 1)],
                                      ybuf.at[s, 1, pl.ds(r, 1)], sem.at[s, 1]))

    def start_all(pa, pb, s):
        def body(r, carry):
            for cp in copies(pa, pb, s, r):
                cp.start()
            return carry
        lax.fori_loop(0, _COMBINE_ROWS, body, 0, unroll=_DMA_UNROLL)

    def wait_all(pa, pb, s):
        def body(r, carry):
            for cp in copies(pa, pb, s, r):
                cp.wait()
            return carry
        lax.fori_loop(0, _COMBINE_ROWS, body, 0, unroll=_DMA_UNROLL)

    @pl.when(step == 0)
    def _():
        start_all(p1_ref, p2_ref, 0)

    @pl.when(step + 1 < steps)
    def _():
        start_all(p1n_ref, p2n_ref, 1 - slot)

    wait_all(p1_ref, p2_ref, slot)
    y = w1_ref[0] * ybuf[slot, 0] + w2_ref[0] * ybuf[slot, 1]
    z = DEEPNORM_ALPHA * x_ref[0] + (1.0 + gate_ref[0]) * y
    xn = _layer_norm_rows(z, g_ref[...], b_ref[...])
    xo_ref[0] = xn
    if with_next:
        h_ref[0] = (xn * (1.0 + sc_ref[0]) + sh_ref[0]).astype(BF16)


def _combine(ys, pos1, pos2, w1, w2, x, gate, ln_g, ln_b, next_mod):
    bsz, t, d = x.shape
    tr = _COMBINE_ROWS
    per_b = t // tr
    steps = bsz * per_b
    row = pl.BlockSpec((1, tr, d), lambda s: (s // per_b, s % per_b, 0))
    vec = pl.BlockSpec((1, d), lambda s: (0, 0))
    mod = pl.BlockSpec((1, 1, d), lambda s: (s // per_b, 0, 0))
    wcol = pl.BlockSpec((1, tr, 1), lambda s: (s // per_b, s % per_b, 0))
    idx = pl.BlockSpec((1, 1, tr), lambda s: (s, 0, 0), memory_space=pltpu.SMEM)
    idx_next = pl.BlockSpec((1, 1, tr), lambda s: (jnp.minimum(s + 1, steps - 1), 0, 0),
                            memory_space=pltpu.SMEM)
    with_next = next_mod is not None
    p1 = pos1.reshape(steps, 1, tr)
    p2 = pos2.reshape(steps, 1, tr)
    in_specs = [idx, idx, idx_next, idx_next, pl.BlockSpec(memory_space=pl.ANY),
                wcol, wcol, row, mod, vec, vec]
    args = [p1, p2, p1, p2, ys, w1.reshape(bsz, t, 1), w2.reshape(bsz, t, 1), x, gate,
            ln_g.reshape(1, d), ln_b.reshape(1, d)]
    out_specs = [row]
    out_shape = [jax.ShapeDtypeStruct((bsz, t, d), F32)]
    if with_next:
        in_specs += [mod, mod]
        args += list(next_mod)
        out_specs.append(row)
        out_shape.append(jax.ShapeDtypeStruct((bsz, t, d), BF16))
    res = pl.pallas_call(
        functools.partial(_combine_kernel, with_next, steps),
        grid=(steps,),
        in_specs=in_specs,
        out_specs=out_specs,
        out_shape=out_shape,
        scratch_shapes=[pltpu.VMEM((2, 2, tr, d), F32), pltpu.SemaphoreType.DMA((2, 2))],
        compiler_params=_cparams(("arbitrary",), VMEM_LIMIT),
        name="combine_ln",
    )(*args)
    return res if with_next else (res[0], None)


def kernel(x, c, emb_ln_g, emb_ln_b, w_ada, b_ada, w_in, conv_w, mix_beta, w_out, hg_lb_logits,
           ln_g, ln_b, w_router, router_bias, w_gate, w_up, w_down):
    bsz, t, d = x.shape
    n = bsz * t
    assert n % LANES == 0 and t % SEQ_CHUNK == 0
    n_tiles = 2 * n // MOE_TILE + N_EXPERTS
    assert n_tiles <= LANES

    mod = _ada_mod(c, w_ada, b_ada)
    mod = mod.reshape(DEPTH, bsz, 6, 1, d)

    def mods(l):
        return [mod[l, :, i] for i in range(6)]

    wr_hi = jnp.pad(w_router, ((0, 0), (0, LANES - N_EXPERTS)))
    wr_hi16 = wr_hi.astype(BF16)
    wr_lo16 = (wr_hi - wr_hi16.astype(F32)).astype(BF16)

    w_out16 = w_out.astype(BF16)
    wg16, wu16, wd16 = w_gate.astype(BF16), w_up.astype(BF16), w_down.astype(BF16)

    m = [mods(l) for l in range(DEPTH)]
    xcur, h = _ln_mod(x, emb_ln_g, emb_ln_b, m[0][1], m[0][0])
    for l in range(DEPTH):
        _, _, gate1, shift2, scale2, gate2 = m[l]
        beta = mix_beta[l].reshape(1, -1)
        proj = _in_proj(h.reshape(n, d), w_in, l, BF16).reshape(bsz, t, IN_COLS)
        yc = _conv_mixer(proj, conv_w[l], beta)
        yr = _ret_mixer(proj, beta)
        yh = _hg_mixer(proj, hg_lb_logits, beta, l)
        x1, h2, logits = _out_proj(yc, yr, yh, w_out16, l, xcur, gate1,
                                   ln_g[l, 0], ln_b[l, 0], scale2, shift2, wr_hi16, wr_lo16)
        logits_t = logits.reshape(n, LANES)[:, :N_EXPERTS].T.reshape(N_EXPERTS, n // LANES, LANES)
        pos1, pos2, w1, w2, tiles = _route(logits_t, router_bias, n_tiles)
        xs = _scatter_rows(h2.reshape(n, d), pos1.reshape(n), pos2.reshape(n), n_tiles * MOE_TILE)
        ys = _moe_experts(xs, tiles[0, :n_tiles], tiles[1, :1], wg16, wu16, wd16, l)
        next_mod = (m[l + 1][1], m[l + 1][0]) if l + 1 < DEPTH else None
        xcur, h = _combine(ys, pos1.reshape(n), pos2.reshape(n), w1, w2, x1, gate2,
                           ln_g[l, 1], ln_b[l, 1], next_mod)
    return xcur
```

```python
import functools
import math

import jax
import jax.numpy as jnp
from jax import lax
from jax.experimental import pallas as pl
from jax.experimental.pallas import tpu as pltpu

F32 = jnp.float32
BF16 = jnp.bfloat16
I32 = jnp.int32

DEPTH = 2
CONV_WIDTH = 512
RET_WIDTH = 768
HG_WIDTH = 768
HEAD_DIM = 128
RET_HEADS = RET_WIDTH // HEAD_DIM
HG_HEADS = HG_WIDTH // HEAD_DIM
IN_COLS = 3 * CONV_WIDTH + 4 * RET_WIDTH + 5 * HG_WIDTH
ROPE_BASE = 10000.0
N_EXPERTS = 16
N_GROUPS = 4
EXPERTS_PER_GROUP = N_EXPERTS // N_GROUPS
MASK_NEG = -1e9
DEEPNORM_ALPHA = (2.0 * DEPTH) ** 0.25
LN_EPS = 1e-5
HEAD_EPS = 1e-6
FORGET_FLOOR = 1e-6

LANES = 128
SUBLANES = 8

_CB, _CC, _CH = 0, 4, 8
_RQ, _RK, _RV, _RG = 12, 18, 24, 30
_HQ, _HFF, _HFB, _HI, _HG = 36, 42, 48, 54, 60
_OUT_RET, _OUT_HG = 4, 10

SEQ_CHUNK = 128
MOE_TILE = 256
VMEM_LIMIT = 56 * 1024 * 1024
MOE_VMEM_LIMIT = 60 * 1024 * 1024


def _cparams(sem, vmem=None):
    return pltpu.CompilerParams(dimension_semantics=sem, vmem_limit_bytes=vmem)


def _silu(x):
    return x * jax.nn.sigmoid(x)


def _dot(a, b):
    return jnp.dot(a, b, preferred_element_type=F32)


def _dot_nt(a, b):
    return lax.dot_general(a, b, (((1,), (1,)), ((), ())), preferred_element_type=F32)


def _dot_tn(a, b):
    return lax.dot_general(a, b, (((0,), (0,)), ((), ())), preferred_element_type=F32)


def _ada_kernel(c_ref, w_ref, b_ref, o_ref):
    cond = _silu(c_ref[...])
    o_ref[0] = _dot(cond.astype(BF16), w_ref[0].astype(BF16)) + b_ref[0]


def _ada_mod(c, w_ada, b_ada):
    depth, d, n6 = w_ada.shape
    b = c.shape[0]
    bp = -(-b // SUBLANES) * SUBLANES
    cp = jnp.pad(c, ((0, bp - b), (0, 0)))
    tn = 512
    out = pl.pallas_call(
        _ada_kernel,
        grid=(depth, n6 // tn),
        in_specs=[
            pl.BlockSpec((bp, d), lambda l, j: (0, 0)),
            pl.BlockSpec((1, d, tn), lambda l, j: (l, 0, j)),
            pl.BlockSpec((1, 1, tn), lambda l, j: (l, 0, j)),
        ],
        out_specs=pl.BlockSpec((1, bp, tn), lambda l, j: (l, 0, j)),
        out_shape=jax.ShapeDtypeStruct((depth, bp, n6), F32),
        compiler_params=_cparams(("arbitrary", "arbitrary")),
        name="ada_mod",
    )(cp, w_ada, b_ada.reshape(depth, 1, n6))
    return out[:, :b, :]


def _layer_norm_rows(z, g, b):
    mu = jnp.mean(z, axis=-1, keepdims=True)
    zc = z - mu
    var = jnp.mean(zc * zc, axis=-1, keepdims=True)
    return zc * lax.rsqrt(var + LN_EPS) * g + b


def _ln_mod_kernel(x_ref, g_ref, b_ref, sc_ref, sh_ref, xo_ref, h_ref):
    y = _layer_norm_rows(x_ref[0], g_ref[...], b_ref[...])
    xo_ref[0] = y
    h_ref[0] = (y * (1.0 + sc_ref[0]) + sh_ref[0]).astype(BF16)


def _ln_mod(x, g, b, scale, shift):
    bsz, t, d = x.shape
    tr = 512
    row = pl.BlockSpec((1, tr, d), lambda i, j: (i, j, 0))
    vec = pl.BlockSpec((1, d), lambda i, j: (0, 0))
    mod = pl.BlockSpec((1, 1, d), lambda i, j: (i, 0, 0))
    return pl.pallas_call(
        _ln_mod_kernel,
        grid=(bsz, t // tr),
        in_specs=[row, vec, vec, mod, mod],
        out_specs=[row, row],
        out_shape=[jax.ShapeDtypeStruct((bsz, t, d), F32),
                   jax.ShapeDtypeStruct((bsz, t, d), BF16)],
        compiler_params=_cparams(("arbitrary", "arbitrary")),
        name="ln_mod",
    )(x, g.reshape(1, d), b.reshape(1, d), scale, shift)


def _in_proj_kernel(a_ref, w_ref, o_ref, w16_scr):
    @pl.when(pl.program_id(1) == 0)
    def _():
        w16_scr[...] = w_ref[0].astype(BF16)

    o_ref[...] = _dot(a_ref[...], w16_scr[...]).astype(o_ref.dtype)


def _in_proj(h, w_in, layer, out_dtype):
    n, d = h.shape
    nc = w_in.shape[2]
    tm, tn = 1024, 1408
    tm = min(tm, n)
    return pl.pallas_call(
        _in_proj_kernel,
        grid=(nc // tn, n // tm),
        in_specs=[pl.BlockSpec((tm, d), lambda j, i: (i, 0)),
                  pl.BlockSpec((1, d, tn), lambda j, i: (layer, 0, j))],
        out_specs=pl.BlockSpec((tm, tn), lambda j, i: (i, j)),
        out_shape=jax.ShapeDtypeStruct((n, nc), out_dtype),
        scratch_shapes=[pltpu.VMEM((d, tn), BF16)],
        compiler_params=_cparams(("arbitrary", "arbitrary"), VMEM_LIMIT),
        name="in_proj",
    )(h, w_in)


def _conv_kernel(cb_ref, cc_ref, ch_ref, w_ref, beta_ref, o_ref):
    u = cc_ref[0].astype(F32) * ch_ref[0].astype(F32)
    t = u.shape[0]
    row = lax.broadcasted_iota(I32, u.shape, 0)
    prev = jnp.where(row == 0, 0.0, pltpu.roll(u, 1, 0))
    nxt = jnp.where(row == t - 1, 0.0, pltpu.roll(u, t - 1, 0))
    w = w_ref[...]
    y = cb_ref[0].astype(F32) * (prev * w[0:1] + u * w[1:2] + nxt * w[2:3])
    o_ref[0] = (y * beta_ref[...]).astype(o_ref.dtype)


def _conv_mixer(proj, conv_w, beta):
    bsz, t, _ = proj.shape
    cw = 256
    nb = CONV_WIDTH // cw
    per = LANES * 1

    def col(off):
        return pl.BlockSpec((1, t, cw), lambda b, j, off=off: (b, 0, off * per // cw + j))

    return pl.pallas_call(
        _conv_kernel,
        grid=(bsz, nb),
        in_specs=[col(_CB), col(_CC), col(_CH),
                  pl.BlockSpec((3, cw), lambda b, j: (0, j)),
                  pl.BlockSpec((1, cw), lambda b, j: (0, j))],
        out_specs=pl.BlockSpec((1, t, cw), lambda b, j: (b, 0, j)),
        out_shape=jax.ShapeDtypeStruct((bsz, t, CONV_WIDTH), BF16),
        compiler_params=_cparams(("arbitrary", "arbitrary"), VMEM_LIMIT),
        name="conv_mixer",
    )(proj, proj, proj, conv_w, beta)


def _ret_kernel(q_ref, k_ref, v_ref, g_ref, cos_ref, sin_ref, lg_ref, beta_ref,
                o_ref, o_scr, qf_scr, qb_scr, uf_scr, ub_scr):
    L = SEQ_CHUNK
    t = q_ref.shape[1]
    n = t // L
    lgf = lg_ref[0, 0:1, :]
    lgb = lg_ref[0, 1:2, :]
    r = lax.broadcasted_iota(I32, (L, L), 0).astype(F32)
    c = lax.broadcasted_iota(I32, (L, L), 1).astype(F32)
    rel = r - c
    dbi = jnp.where(rel > 0, jnp.exp(lgf * jnp.maximum(rel, 0.0)),
                    jnp.where(rel < 0, jnp.exp(lgb * jnp.maximum(-rel, 0.0)), 2.0))
    qf = jnp.exp(lgf * (r + 1.0))
    kf = jnp.exp(lgf * (L - 1.0 - r))
    qb = jnp.exp(lgb * (L - r))
    kb = jnp.exp(lgb * r)
    gf_l = jnp.exp(lgf * float(L))
    gb_l = jnp.exp(lgb * float(L))
    scale = HEAD_DIM ** -0.5

    def rot(x, cs, sn):
        return x * cs + pltpu.roll(x, HEAD_DIM // 2, 1) * sn

    def rows(ci):
        return pl.ds(pl.multiple_of(ci * L, L), L)

    def local_body(ci, carry):
        sl = rows(ci)
        cs = cos_ref[sl, :]
        sn = sin_ref[sl, :]
        q = rot(q_ref[0, sl, :].astype(F32), cs, sn)
        k = rot(k_ref[0, sl, :].astype(F32), cs, sn) * scale
        v16 = v_ref[0, sl, :].astype(BF16)
        s = _dot_nt(q.astype(BF16), k.astype(BF16)) * dbi
        o_scr[sl, :] = _dot(s.astype(BF16), v16)
        qf_scr[sl, :] = (q * qf).astype(BF16)
        qb_scr[sl, :] = (q * qb).astype(BF16)
        uf_scr[ci] = _dot_tn((k * kf).astype(BF16), v16)
        ub_scr[ci] = _dot_tn((k * kb).astype(BF16), v16)
        return carry

    lax.fori_loop(0, n, local_body, 0, unroll=4)

    def scan_body(i, states):
        sf, sb = states
        cf, cb = i, n - 1 - i
        slf, slb = rows(cf), rows(cb)
        o_scr[slf, :] += _dot(qf_scr[slf, :], sf.astype(BF16))
        o_scr[slb, :] += _dot(qb_scr[slb, :], sb.astype(BF16))
        return gf_l * sf + uf_scr[cf], gb_l * sb + ub_scr[cb]

    zero = jnp.zeros((HEAD_DIM, HEAD_DIM), F32)
    lax.fori_loop(0, n, scan_body, (zero, zero), unroll=2)

    beta = beta_ref[...]

    def norm_body(ci, carry):
        sl = rows(ci)
        o = o_scr[sl, :]
        mu = jnp.mean(o, axis=-1, keepdims=True)
        oc = o - mu
        var = jnp.mean(oc * oc, axis=-1, keepdims=True)
        y = oc * lax.rsqrt(var + HEAD_EPS) * _silu(g_ref[0, sl, :].astype(F32)) * beta
        o_ref[0, sl, :] = y.astype(o_ref.dtype)
        return carry

    lax.fori_loop(0, n, norm_body, 0, unroll=2)


def _rotary_tables(t):
    half = HEAD_DIM // 2
    inv_freq = ROPE_BASE ** (-jnp.arange(half, dtype=F32) / half)
    ang = jnp.arange(t, dtype=F32)[:, None] * inv_freq[None, :]
    cos, sin = jnp.cos(ang), jnp.sin(ang)
    return jnp.concatenate([cos, cos], -1), jnp.concatenate([-sin, sin], -1)


def _ret_log_decays():
    head = jnp.arange(RET_HEADS, dtype=F32)
    lg_f = jnp.log1p(-jnp.exp2(-5.0 - head))
    lg_b = jnp.log1p(-jnp.exp2(-5.5 - head))
    lg = jnp.stack([lg_f, lg_b], axis=1)
    return jnp.broadcast_to(lg[:, :, None], (RET_HEADS, 2, LANES))


def _ret_mixer(proj, beta):
    bsz, t, _ = proj.shape
    n_chunks = t // SEQ_CHUNK
    assert n_chunks % 2 == 0
    cosf, sinf = _rotary_tables(t)

    def col(off):
        return pl.BlockSpec((1, t, HEAD_DIM), lambda b, h, off=off: (b, 0, off + h))

    tab = pl.BlockSpec((t, HEAD_DIM), lambda b, h: (0, 0))
    return pl.pallas_call(
        _ret_kernel,
        grid=(bsz, RET_HEADS),
        in_specs=[col(_RQ), col(_RK), col(_RV), col(_RG), tab, tab,
                  pl.BlockSpec((1, 2, LANES), lambda b, h: (h, 0, 0)),
                  pl.BlockSpec((1, HEAD_DIM), lambda b, h: (0, _OUT_RET + h))],
        out_specs=pl.BlockSpec((1, t, HEAD_DIM), lambda b, h: (b, 0, h)),
        out_shape=jax.ShapeDtypeStruct((bsz, t, RET_WIDTH), BF16),
        scratch_shapes=[pltpu.VMEM((t, HEAD_DIM), F32),
                        pltpu.VMEM((t, HEAD_DIM), BF16), pltpu.VMEM((t, HEAD_DIM), BF16),
                        pltpu.VMEM((n_chunks, HEAD_DIM, HEAD_DIM), F32),
                        pltpu.VMEM((n_chunks, HEAD_DIM, HEAD_DIM), F32)],
        compiler_params=_cparams(("arbitrary", "arbitrary"), VMEM_LIMIT),
        name="ret_mixer",
    )(proj, proj, proj, proj, cosf, sinf, _ret_log_decays(), beta)


_HG_BASE = SUBLANES
_HG_LEVELS = tuple(m for m in (8, 16, 32, 64) if m < SEQ_CHUNK)


def _cumsum_rows(x, reverse):
    n = x.shape[0]
    row = lax.broadcasted_iota(I32, x.shape, 0)
    sh = 1
    while sh < n:
        if reverse:
            x = x + jnp.where(row < n - sh, pltpu.roll(x, n - sh, 0), 0.0)
        else:
            x = x + jnp.where(row >= sh, pltpu.roll(x, sh, 0), 0.0)
        sh *= 2
    return x


def _hg_masks(reverse):
    C = SEQ_CHUNK
    r = lax.broadcasted_iota(I32, (C, C), 0)
    c = lax.broadcasted_iota(I32, (C, C), 1)
    if reverse:
        r, c = c, r
    sh = _HG_BASE.bit_length() - 1
    masks = [((r >> sh) == (c >> sh)) & (c <= r)]
    for m in _HG_LEVELS:
        sh = m.bit_length() - 1
        masks.append(((r >> (sh + 1)) == (c >> (sh + 1)))
                     & (((r >> sh) & 1) == 1) & (((c >> sh) & 1) == 0))
    return [jnp.where(m, 1.0, 0.0) for m in masks]


def _hg_scores(q, kk, cum, mask_ref, reverse):
    C = SEQ_CHUNK
    d = 1 if reverse else 0

    def split(x, blk):
        return x.reshape(C // blk, blk, LANES)

    b = _HG_BASE
    ref_row = b // 2 if reverse else b // 2 - 1
    cum3 = split(cum, b)
    rel = cum3 - cum3[:, ref_row:ref_row + 1, :]
    qt = (split(q, b) * jnp.exp(rel)).reshape(C, LANES).astype(BF16)
    kt = (split(kk, b) * jnp.exp(-rel)).reshape(C, LANES).astype(BF16)
    a = jnp.where(mask_ref[d, 0] > 0.5, _dot_nt(qt, kt), 0.0)
    for li, m in enumerate(_HG_LEVELS):
        blk = 2 * m
        ref_row = m if reverse else m - 1
        cum3 = split(cum, blk)
        rel = cum3 - cum3[:, ref_row:ref_row + 1, :]
        dec = jnp.exp(-jnp.abs(rel)).reshape(C, LANES)
        halves = []
        for j in range(C // m):
            is_query = (j % 2 == 1) != reverse
            halves.append((q if is_query else kk)[j * m:(j + 1) * m])
        x = (jnp.concatenate(halves, axis=0) * dec).astype(BF16)
        a = a + mask_ref[d, li + 1] * _dot_nt(x, x)
    return a


def _hg_kernel(layer, q_ref, zf_ref, zb_ref, v_ref, g_ref, lbl_ref, beta_ref, o_ref,
               o_scr, qe_scr, u_scr, dec_scr, mask_scr):
    C = SEQ_CHUNK
    t = q_ref.shape[1]
    n = t // C
    for d, reverse in enumerate((False, True)):
        for li, m in enumerate(_hg_masks(reverse)):
            mask_scr[d, li] = m
    logits = lbl_ref[...].astype(F32)
    e = jnp.exp(logits - jnp.max(logits, axis=0, keepdims=True))
    p = e / jnp.sum(e, axis=0, keepdims=True)
    lb = p[0]
    for l in range(1, layer + 1):
        lb = lb + p[l]
    lb = lb - p[0]

    def gates(z_ref, sl, lb_row):
        f = lb_row + (1.0 - lb_row) * jax.nn.sigmoid(z_ref[0, sl, :].astype(F32))
        f = jnp.maximum(f, FORGET_FLOOR)
        return jnp.log(f), 1.0 - f

    def rows(ci):
        return pl.ds(pl.multiple_of(ci * C, C), C)

    def local_dir(ci, sl, q, v16, z_ref, d):
        reverse = d == 1
        logf, kk = gates(z_ref, sl, lb[d:d + 1, :])
        cum = _cumsum_rows(logf, reverse)
        a = _hg_scores(q, kk, cum, mask_scr, reverse)
        total = cum[0:1, :] if reverse else cum[C - 1:C, :]
        qe_scr[d, sl, :] = (q * jnp.exp(cum)).astype(BF16)
        k_tail = kk * jnp.exp(total - cum)
        u_scr[d, ci] = _dot_tn(v16, k_tail.astype(BF16))
        dec_scr[d, ci] = jnp.broadcast_to(jnp.exp(total), (SUBLANES, LANES))
        return _dot(a.astype(BF16), v16)

    def local_body(ci, carry):
        sl = rows(ci)
        q = q_ref[0, sl, :].astype(F32)
        v16 = v_ref[0, sl, :].astype(BF16)
        o_scr[sl, :] = (local_dir(ci, sl, q, v16, zf_ref, 0)
                        + local_dir(ci, sl, q, v16, zb_ref, 1))
        return carry

    lax.fori_loop(0, n, local_body, 0, unroll=2)

    def scan_body(i, states):
        sf, sb = states
        cf, cb = i, n - 1 - i
        slf, slb = rows(cf), rows(cb)
        o_scr[slf, :] += _dot_nt(qe_scr[0, slf, :], sf.astype(BF16))
        o_scr[slb, :] += _dot_nt(qe_scr[1, slb, :], sb.astype(BF16))
        return (sf * dec_scr[0, cf, 0:1, :] + u_scr[0, cf],
                sb * dec_scr[1, cb, 0:1, :] + u_scr[1, cb])

    zero = jnp.zeros((HEAD_DIM, HEAD_DIM), F32)
    lax.fori_loop(0, n, scan_body, (zero, zero), unroll=2)

    beta = beta_ref[...]

    def norm_body(ci, carry):
        sl = rows(ci)
        o = o_scr[sl, :]
        ms = jnp.mean(o * o, axis=-1, keepdims=True)
        y = o * lax.rsqrt(ms + HEAD_EPS) * _silu(g_ref[0, sl, :].astype(F32)) * beta
        o_ref[0, sl, :] = y.astype(o_ref.dtype)
        return carry

    lax.fori_loop(0, n, norm_body, 0, unroll=2)


def _hg_mixer(proj, lb_logits, beta, layer):
    bsz, t, _ = proj.shape
    n_chunks = t // SEQ_CHUNK
    assert n_chunks % 2 == 0

    def col(off):
        return pl.BlockSpec((1, t, HEAD_DIM), lambda b, h, off=off: (b, 0, off + h))

    return pl.pallas_call(
        functools.partial(_hg_kernel, layer),
        grid=(bsz, HG_HEADS),
        in_specs=[col(_HQ), col(_HFF), col(_HFB), col(_HI), col(_HG),
                  pl.BlockSpec((DEPTH, 2, HEAD_DIM), lambda b, h: (0, 0, h)),
                  pl.BlockSpec((1, HEAD_DIM), lambda b, h: (0, _OUT_HG + h))],
        out_specs=pl.BlockSpec((1, t, HEAD_DIM), lambda b, h: (b, 0, h)),
        out_shape=jax.ShapeDtypeStruct((bsz, t, HG_WIDTH), BF16),
        scratch_shapes=[pltpu.VMEM((t, HEAD_DIM), F32),
                        pltpu.VMEM((2, t, HEAD_DIM), BF16),
                        pltpu.VMEM((2, n_chunks, HEAD_DIM, HEAD_DIM), F32),
                        pltpu.VMEM((2, n_chunks, SUBLANES, LANES), F32),
                        pltpu.VMEM((2, 1 + len(_HG_LEVELS), SEQ_CHUNK, SEQ_CHUNK), F32)],
        compiler_params=_cparams(("arbitrary", "arbitrary"), VMEM_LIMIT),
        name="hg_mixer",
    )(proj, proj, proj, proj, proj, lb_logits, beta)


_OUT_PROJ_SPLIT = 2


def _out_proj_kernel(yc_ref, yr_ref, yh_ref, w_ref, x_ref, gate_ref, g_ref, b_ref,
                     sc_ref, sh_ref, wrh_ref, wrl_ref, xo_ref, h_ref, lo_ref):
    c0, c1 = CONV_WIDTH, CONV_WIDTH + RET_WIDTH
    tm = x_ref.shape[1]
    for half in range(_OUT_PROJ_SPLIT):
        sl = pl.ds(half * (tm // _OUT_PROJ_SPLIT), tm // _OUT_PROJ_SPLIT)
        y = (_dot(yc_ref[0, sl, :], w_ref[0, 0:c0, :]) + _dot(yr_ref[0, sl, :], w_ref[0, c0:c1, :])
             + _dot(yh_ref[0, sl, :], w_ref[0, c1:, :]))
        z = DEEPNORM_ALPHA * x_ref[0, sl, :] + (1.0 + gate_ref[0]) * y
        xn = _layer_norm_rows(z, g_ref[...], b_ref[...])
        xo_ref[0, sl, :] = xn
        h = xn * (1.0 + sc_ref[0]) + sh_ref[0]
        h_ref[0, sl, :] = h
        h_hi = h.astype(BF16)
        h_lo = (h - h_hi.astype(F32)).astype(BF16)
        wrh = wrh_ref[...]
        lo_ref[0, sl, :] = _dot(h_hi, wrh) + _dot(h_lo, wrh) + _dot(h_hi, wrl_ref[...])


def _out_proj(yc, yr, yh, w_out16, layer, x, gate, ln_g, ln_b, scale2, shift2, wr_hi, wr_lo):
    bsz, t, d = x.shape
    tm = 512

    def row(width):
        return pl.BlockSpec((1, tm, width), lambda b, i: (b, i, 0))

    vec = pl.BlockSpec((1, d), lambda b, i: (0, 0))
    mod = pl.BlockSpec((1, 1, d), lambda b, i: (b, 0, 0))
    wr = pl.BlockSpec((d, LANES), lambda b, i: (0, 0))
    return pl.pallas_call(
        _out_proj_kernel,
        grid=(bsz, t // tm),
        in_specs=[row(CONV_WIDTH), row(RET_WIDTH), row(HG_WIDTH),
                  pl.BlockSpec((1, d, d), lambda b, i: (layer, 0, 0)),
                  row(d), mod, vec, vec, mod, mod, wr, wr],
        out_specs=[row(d), row(d), row(LANES)],
        out_shape=[jax.ShapeDtypeStruct((bsz, t, d), F32),
                   jax.ShapeDtypeStruct((bsz, t, d), F32),
                   jax.ShapeDtypeStruct((bsz, t, LANES), F32)],
        compiler_params=_cparams(("arbitrary", "arbitrary"), VMEM_LIMIT),
        name="out_proj",
    )(yc, yr, yh, w_out16, x, gate, ln_g.reshape(1, d), ln_b.reshape(1, d),
      scale2, shift2, wr_hi, wr_lo)


def _route_kernel(n_tiles, lt_ref, bias_ref, pos1_ref, pos2_ref, w1_ref, w2_ref, tile_ref):
    E = N_EXPERTS
    logits = [lt_ref[e] for e in range(E)]
    shape = logits[0].shape
    mx = functools.reduce(jnp.maximum, logits)
    ex = [jnp.exp(l - mx) for l in logits]
    den = functools.reduce(lambda a, b: a + b, ex)
    scores = [x / den for x in ex]
    sel = [scores[e] + bias_ref[e] for e in range(E)]

    best_g = jnp.zeros(shape, I32)
    best_v = None
    for g in range(N_GROUPS):
        a, b, c, d = sel[EXPERTS_PER_GROUP * g: EXPERTS_PER_GROUP * (g + 1)]
        hi1, lo1 = jnp.maximum(a, b), jnp.minimum(a, b)
        hi2, lo2 = jnp.maximum(c, d), jnp.minimum(c, d)
        gs = jnp.maximum(hi1, hi2) + jnp.maximum(jnp.minimum(hi1, hi2), jnp.maximum(lo1, lo2))
        if g == 0:
            best_v = gs
        else:
            better = gs > best_v
            best_g = jnp.where(better, g, best_g)
            best_v = jnp.where(better, gs, best_v)

    masked = [jnp.where(best_g == (e // EXPERTS_PER_GROUP), sel[e], MASK_NEG) for e in range(E)]

    def arg_top(vals, exclude):
        idx = jnp.zeros(shape, I32)
        val = None
        for e in range(E):
            v = vals[e] if exclude is None else jnp.where(exclude == e, -jnp.inf, vals[e])
            if e == 0:
                val = v
            else:
                better = v > val
                idx = jnp.where(better, e, idx)
                val = jnp.where(better, v, val)
        return idx

    idx1 = arg_top(masked, None)
    idx2 = arg_top(masked, idx1)
    s1 = functools.reduce(lambda a, b: a + b, [jnp.where(idx1 == e, scores[e], 0.0) for e in range(E)])
    s2 = functools.reduce(lambda a, b: a + b, [jnp.where(idx2 == e, scores[e], 0.0) for e in range(E)])
    w1_ref[...] = s1 / (s1 + s2)
    w2_ref[...] = s2 / (s1 + s2)

    rows = shape[0]
    li = lax.broadcasted_iota(I32, (LANES, LANES), 0)
    lj = lax.broadcasted_iota(I32, (LANES, LANES), 1)
    upper = (li <= lj).astype(BF16)
    ri = lax.broadcasted_iota(I32, (rows, rows), 0)
    rj = lax.broadcasted_iota(I32, (rows, rows), 1)
    lower = (rj < ri).astype(BF16)
    tile_start = (lax.broadcasted_iota(I32, (1, LANES), 1) * MOE_TILE).astype(F32)
    start = jnp.zeros((1, 1), F32)
    pos1 = jnp.zeros(shape, F32)
    pos2 = jnp.zeros(shape, F32)
    tile_e = jnp.zeros((1, LANES), F32)
    for e in range(E):
        hit1 = idx1 == e
        hit2 = idx2 == e
        onehot = jnp.where(hit1 | hit2, 1.0, 0.0)
        pref = _dot(onehot.astype(BF16), upper)
        row_tot = pref[:, LANES - 1:LANES]
        row_off = _dot(lower, jnp.broadcast_to(row_tot, shape).astype(BF16))[:, 0:1]
        rank = pref - onehot + row_off
        count = jnp.sum(onehot, keepdims=True)
        dest = start + rank
        pos1 = jnp.where(hit1, dest, pos1)
        pos2 = jnp.where(hit2, dest, pos2)
        start = start + jnp.ceil(count / MOE_TILE) * MOE_TILE
        tile_e = tile_e + jnp.where(tile_start >= start, 1.0, 0.0)
    pos1_ref[...] = pos1.astype(I32)
    pos2_ref[...] = pos2.astype(I32)
    n_valid = start / MOE_TILE
    tile_id = lax.broadcasted_iota(I32, (1, LANES), 1).astype(F32)
    last_e = jnp.sum(jnp.where(tile_id == n_valid - 1.0, tile_e, 0.0), keepdims=True)
    tile_e = jnp.where(tile_id < n_valid, tile_e, last_e)
    sub = lax.broadcasted_iota(I32, (SUBLANES, LANES), 0)
    tile_ref[...] = jnp.where(sub == 0, tile_e, jnp.broadcast_to(n_valid, (SUBLANES, LANES))).astype(I32)


def _route(logits_t, router_bias, n_tiles):
    e, rows, lanes = logits_t.shape
    tok = jax.ShapeDtypeStruct((rows, lanes), I32)
    tokf = jax.ShapeDtypeStruct((rows, lanes), F32)
    full = pl.BlockSpec((rows, lanes), lambda i: (0, 0))
    return pl.pallas_call(
        functools.partial(_route_kernel, n_tiles),
        grid=(1,),
        in_specs=[pl.BlockSpec((e, rows, lanes), lambda i: (0, 0, 0)),
                  pl.BlockSpec(memory_space=pltpu.SMEM)],
        out_specs=[full, full, full, full, pl.BlockSpec((SUBLANES, LANES), lambda i: (0, 0))],
        out_shape=[tok, tok, tokf, tokf, jax.ShapeDtypeStruct((SUBLANES, LANES), I32)],
        compiler_params=_cparams(("arbitrary",)),
        name="route",
    )(logits_t, router_bias)


_ROWS_PER_STEP = 512


_DMA_UNROLL = 8


def _scatter_rows_kernel(p1_ref, p2_ref, src_ref, dst_in_ref, dst_ref, sem):
    del dst_in_ref

    def copies(r):
        row = src_ref.at[pl.ds(r, 1)]
        return (pltpu.make_async_copy(row, dst_ref.at[pl.ds(p1_ref[0, 0, r], 1)], sem.at[0]),
                pltpu.make_async_copy(row, dst_ref.at[pl.ds(p2_ref[0, 0, r], 1)], sem.at[1]))

    def start(r, carry):
        for cp in copies(r):
            cp.start()
        return carry

    def wait(r, carry):
        for cp in copies(r):
            cp.wait()
        return carry

    lax.fori_loop(0, _ROWS_PER_STEP, start, 0, unroll=_DMA_UNROLL)
    lax.fori_loop(0, _ROWS_PER_STEP, wait, 0, unroll=_DMA_UNROLL)


def _scatter_rows(src, pos1, pos2, n_dst):
    n, d = src.shape
    steps = n // _ROWS_PER_STEP
    idx = pl.BlockSpec((1, 1, _ROWS_PER_STEP), lambda i: (i, 0, 0), memory_space=pltpu.SMEM)
    any_spec = pl.BlockSpec(memory_space=pl.ANY)
    dst0 = jnp.zeros((n_dst, d), src.dtype)
    return pl.pallas_call(
        _scatter_rows_kernel,
        grid=(steps,),
        in_specs=[idx, idx, pl.BlockSpec((_ROWS_PER_STEP, d), lambda i: (i, 0)), any_spec],
        out_specs=any_spec,
        out_shape=jax.ShapeDtypeStruct((n_dst, d), src.dtype),
        scratch_shapes=[pltpu.SemaphoreType.DMA((2,))],
        input_output_aliases={3: 0},
        compiler_params=_cparams(("arbitrary",), VMEM_LIMIT),
        name="scatter_rows",
    )(pos1.reshape(steps, 1, _ROWS_PER_STEP), pos2.reshape(steps, 1, _ROWS_PER_STEP), src, dst0)


_W_CHUNK = 128
_W_RING = 8


def _moe_kernel(layer, te_ref, nv_ref, first_ref, slot_ref, nxt_ref,
                x_ref, wg_hbm, wu_hbm, wd_hbm, y_ref,
                cg_scr, cu_scr, cd_scr, stage_scr, sem, cnt_ref):
    i = pl.program_id(0)
    d, f = cg_scr.shape[1], cg_scr.shape[2]
    n_g = d // _W_CHUNK
    n_chunks = 2 * n_g + f // _W_CHUNK

    def for_chunk(c, e, slot, fn):
        k = c % _W_RING

        @pl.when(c < n_g)
        def _():
            row = pl.multiple_of(c * _W_CHUNK, _W_CHUNK)
            fn(wg_hbm.at[layer, e, pl.ds(row, _W_CHUNK), :], stage_scr.at[k, :, pl.ds(0, f)],
               sem.at[k], cg_scr.at[slot, pl.ds(row, _W_CHUNK), :])

        @pl.when((c >= n_g) & (c < 2 * n_g))
        def _():
            row = pl.multiple_of((c - n_g) * _W_CHUNK, _W_CHUNK)
            fn(wu_hbm.at[layer, e, pl.ds(row, _W_CHUNK), :], stage_scr.at[k, :, pl.ds(0, f)],
               sem.at[k], cu_scr.at[slot, pl.ds(row, _W_CHUNK), :])

        @pl.when(c >= 2 * n_g)
        def _():
            row = pl.multiple_of((c - 2 * n_g) * _W_CHUNK, _W_CHUNK)
            fn(wd_hbm.at[layer, e, pl.ds(row, _W_CHUNK), :], stage_scr.at[k],
               sem.at[k], cd_scr.at[slot, pl.ds(row, _W_CHUNK), :])

    def start(src, stage, s, dst):
        pltpu.make_async_copy(src, stage, s).start()

    def finish(src, stage, s, dst):
        pltpu.make_async_copy(src, stage, s).wait()
        dst[...] = stage[...].astype(BF16)

    def start_upto(e, slot, hi):
        hi = jnp.minimum(hi, n_chunks)

        def body(c, carry):
            for_chunk(c, e, slot, start)
            return carry

        lax.fori_loop(cnt_ref[0], hi, body, 0)
        cnt_ref[0] = jnp.maximum(cnt_ref[0], hi)

    def finish_started(e, slot):
        def body(c, carry):
            for_chunk(c, e, slot, finish)
            return carry

        lax.fori_loop(cnt_ref[1], cnt_ref[0], body, 0)
        cnt_ref[1] = cnt_ref[0]

    valid = i < nv_ref[0]

    @pl.when(i == 0)
    def _():
        cnt_ref[0] = 0
        cnt_ref[1] = 0

    @pl.when(valid & (first_ref[i] == 1))
    def _():
        e, slot = te_ref[i], slot_ref[i]

        def body(c, carry):
            start_upto(e, slot, c + _W_RING)
            for_chunk(c, e, slot, finish)
            return carry

        lax.fori_loop(cnt_ref[1], n_chunks, body, 0)
        cnt_ref[0] = 0
        cnt_ref[1] = 0

    @pl.when(valid & (nxt_ref[i] >= 0))
    def _():
        e, slot = nxt_ref[i], 1 - slot_ref[i]
        finish_started(e, slot)
        start_upto(e, slot, cnt_ref[0] + _W_RING)

    @pl.when(valid)
    def _():
        slot = slot_ref[i]
        x = x_ref[...].astype(BF16)
        a = _dot(x, cg_scr[slot])
        b = _dot(x, cu_scr[slot])
        hidden = (_silu(a) * b).astype(BF16)
        y_ref[...] = _dot(hidden, cd_scr[slot])

    @pl.when(jnp.logical_not(valid))
    def _():
        y_ref[...] = jnp.zeros_like(y_ref)


def _moe_schedule(tile_e, n_valid):
    n_tiles = tile_e.shape[0]
    idx = jnp.arange(n_tiles, dtype=I32)
    valid = idx < n_valid[0]
    first = valid & ((idx == 0) | (tile_e != jnp.roll(tile_e, 1)))
    run = jnp.cumsum(first.astype(I32)) - 1
    next_first = lax.cummin(jnp.where(first, idx, n_tiles), reverse=True)
    after = jnp.concatenate([next_first[1:], jnp.full((1,), n_tiles, I32)])
    nxt = jnp.where(valid & (after < n_tiles), tile_e[jnp.minimum(after, n_tiles - 1)], -1)
    return first.astype(I32), (run % 2).astype(I32), nxt.astype(I32)


def _moe_experts(xs, tile_e, n_valid, w_gate, w_up, w_down, layer):
    n_rows, d = xs.shape
    f = w_gate.shape[3]
    tm = MOE_TILE
    n_tiles = n_rows // tm
    assert d % _W_CHUNK == 0 and f % _W_CHUNK == 0
    first, slot, nxt = _moe_schedule(tile_e, n_valid)

    def xmap(i, te, nv, *_):
        return (jnp.minimum(i, nv[0] - 1), 0)

    any_spec = pl.BlockSpec(memory_space=pl.ANY)
    grid_spec = pltpu.PrefetchScalarGridSpec(
        num_scalar_prefetch=5,
        grid=(n_tiles,),
        in_specs=[pl.BlockSpec((tm, d), xmap), any_spec, any_spec, any_spec],
        out_specs=pl.BlockSpec((tm, d), lambda i, *_: (i, 0)),
        scratch_shapes=[pltpu.VMEM((2, d, f), BF16), pltpu.VMEM((2, d, f), BF16),
                        pltpu.VMEM((2, f, d), BF16),
                        pltpu.VMEM((_W_RING, _W_CHUNK, d), F32),
                        pltpu.SemaphoreType.DMA((_W_RING,)),
                        pltpu.SMEM((2,), I32)],
    )
    return pl.pallas_call(
        functools.partial(_moe_kernel, layer),
        grid_spec=grid_spec,
        out_shape=jax.ShapeDtypeStruct((n_rows, d), F32),
        compiler_params=_cparams(("arbitrary",), MOE_VMEM_LIMIT),
        name="moe_experts",
    )(tile_e, n_valid, first, slot, nxt, xs, w_gate, w_up, w_down)


_COMBINE_ROWS = 256


def _combine_kernel(with_next, steps, p1_ref, p2_ref, p1n_ref, p2n_ref, ys_ref, w1_ref, w2_ref,
                    x_ref, gate_ref, g_ref, b_ref, *rest):
    if with_next:
        sc_ref, sh_ref, xo_ref, h_ref, ybuf, sem = rest
    else:
        xo_ref, ybuf, sem = rest
    step = pl.program_id(0)
    slot = step % 2

    def copies(pa, pb, s, r):
        return (pltpu.make_async_copy(ys_ref.at[pl.ds(pa[0, 0, r], 1)],
                                      ybuf.at[s, 0, pl.ds(r, 1)], sem.at[s, 0]),
                pltpu.make_async_copy(ys_ref.at[pl.ds(pb[0, 0, r], 1)],
                                      ybuf.at[s, 1, pl.ds(r, 1)], sem.at[s, 1]))

    def start_all(pa, pb, s):
        def body(r, carry):
            for cp in copies(pa, pb, s, r):
                cp.start()
            return carry
        lax.fori_loop(0, _COMBINE_ROWS, body, 0, unroll=_DMA_UNROLL)

    def wait_all(pa, pb, s):
        def body(r, carry):
            for cp in copies(pa, pb, s, r):
                cp.wait()
            return carry
        lax.fori_loop(0, _COMBINE_ROWS, body, 0, unroll=_DMA_UNROLL)

    @pl.when(step == 0)
    def _():
        start_all(p1_ref, p2_ref, 0)

    @pl.when(step + 1 < steps)
    def _():
        start_all(p1n_ref, p2n_ref, 1 - slot)

    wait_all(p1_ref, p2_ref, slot)
    y = w1_ref[0] * ybuf[slot, 0] + w2_ref[0] * ybuf[slot, 1]
    z = DEEPNORM_ALPHA * x_ref[0] + (1.0 + gate_ref[0]) * y
    xn = _layer_norm_rows(z, g_ref[...], b_ref[...])
    xo_ref[0] = xn
    if with_next:
        h_ref[0] = (xn * (1.0 + sc_ref[0]) + sh_ref[0]).astype(BF16)


def _combine(ys, pos1, pos2, w1, w2, x, gate, ln_g, ln_b, next_mod):
    bsz, t, d = x.shape
    tr = _COMBINE_ROWS
    per_b = t // tr
    steps = bsz * per_b
    row = pl.BlockSpec((1, tr, d), lambda s: (s // per_b, s % per_b, 0))
    vec = pl.BlockSpec((1, d), lambda s: (0, 0))
    mod = pl.BlockSpec((1, 1, d), lambda s: (s // per_b, 0, 0))
    wcol = pl.BlockSpec((1, tr, 1), lambda s: (s // per_b, s % per_b, 0))
    idx = pl.BlockSpec((1, 1, tr), lambda s: (s, 0, 0), memory_space=pltpu.SMEM)
    idx_next = pl.BlockSpec((1, 1, tr), lambda s: (jnp.minimum(s + 1, steps - 1), 0, 0),
                            memory_space=pltpu.SMEM)
    with_next = next_mod is not None
    p1 = pos1.reshape(steps, 1, tr)
    p2 = pos2.reshape(steps, 1, tr)
    in_specs = [idx, idx, idx_next, idx_next, pl.BlockSpec(memory_space=pl.ANY),
                wcol, wcol, row, mod, vec, vec]
    args = [p1, p2, p1, p2, ys, w1.reshape(bsz, t, 1), w2.reshape(bsz, t, 1), x, gate,
            ln_g.reshape(1, d), ln_b.reshape(1, d)]
    out_specs = [row]
    out_shape = [jax.ShapeDtypeStruct((bsz, t, d), F32)]
    if with_next:
        in_specs += [mod, mod]
        args += list(next_mod)
        out_specs.append(row)
        out_shape.append(jax.ShapeDtypeStruct((bsz, t, d), BF16))
    res = pl.pallas_call(
        functools.partial(_combine_kernel, with_next, steps),
        grid=(steps,),
        in_specs=in_specs,
        out_specs=out_specs,
        out_shape=out_shape,
        scratch_shapes=[pltpu.VMEM((2, 2, tr, d), F32), pltpu.SemaphoreType.DMA((2, 2))],
        compiler_params=_cparams(("arbitrary",), VMEM_LIMIT),
        name="combine_ln",
    )(*args)
    return res if with_next else (res[0], None)


def kernel(x, c, emb_ln_g, emb_ln_b, w_ada, b_ada, w_in, conv_w, mix_beta, w_out, hg_lb_logits,
           ln_g, ln_b, w_router, router_bias, w_gate, w_up, w_down):
    bsz, t, d = x.shape
    n = bsz * t
    assert n % LANES == 0 and t % SEQ_CHUNK == 0
    n_tiles = 2 * n // MOE_TILE + N_EXPERTS
    assert n_tiles <= LANES

    mod = _ada_mod(c, w_ada, b_ada)
    mod = mod.reshape(DEPTH, bsz, 6, 1, d)

    def mods(l):
        return [mod[l, :, i] for i in range(6)]

    wr_hi = jnp.pad(w_router, ((0, 0), (0, LANES - N_EXPERTS)))
    wr_hi16 = wr_hi.astype(BF16)
    wr_lo16 = (wr_hi - wr_hi16.astype(F32)).astype(BF16)

    w_out16 = w_out.astype(BF16)

    m = [mods(l) for l in range(DEPTH)]
    xcur, h = _ln_mod(x, emb_ln_g, emb_ln_b, m[0][1], m[0][0])
    for l in range(DEPTH):
        _, _, gate1, shift2, scale2, gate2 = m[l]
        beta = mix_beta[l].reshape(1, -1)
        proj = _in_proj(h.reshape(n, d), w_in, l, BF16).reshape(bsz, t, IN_COLS)
        yc = _conv_mixer(proj, conv_w[l], beta)
        yr = _ret_mixer(proj, beta)
        yh = _hg_mixer(proj, hg_lb_logits, beta, l)
        x1, h2, logits = _out_proj(yc, yr, yh, w_out16, l, xcur, gate1,
                                   ln_g[l, 0], ln_b[l, 0], scale2, shift2, wr_hi16, wr_lo16)
        logits_t = logits.reshape(n, LANES)[:, :N_EXPERTS].T.reshape(N_EXPERTS, n // LANES, LANES)
        pos1, pos2, w1, w2, tiles = _route(logits_t, router_bias, n_tiles)
        xs = _scatter_rows(h2.reshape(n, d), pos1.reshape(n), pos2.reshape(n), n_tiles * MOE_TILE)
        ys = _moe_experts(xs, tiles[0, :n_tiles], tiles[1, :1], w_gate, w_up, w_down, l)
        next_mod = (m[l + 1][1], m[l + 1][0]) if l + 1 < DEPTH else None
        xcur, h = _combine(ys, pos1.reshape(n), pos2.reshape(n), w1, w2, x1, gate2,
                           ln_g[l, 1], ln_b[l, 1], next_mod)
    return xcur
```

```python
import functools
import math

import jax
import jax.numpy as jnp
from jax import lax
from jax.experimental import pallas as pl
from jax.experimental.pallas import tpu as pltpu

F32 = jnp.float32
BF16 = jnp.bfloat16
I32 = jnp.int32

DEPTH = 2
CONV_WIDTH = 512
RET_WIDTH = 768
HG_WIDTH = 768
HEAD_DIM = 128
RET_HEADS = RET_WIDTH // HEAD_DIM
HG_HEADS = HG_WIDTH // HEAD_DIM
IN_COLS = 3 * CONV_WIDTH + 4 * RET_WIDTH + 5 * HG_WIDTH
ROPE_BASE = 10000.0
N_EXPERTS = 16
N_GROUPS = 4
EXPERTS_PER_GROUP = N_EXPERTS // N_GROUPS
MASK_NEG = -1e9
DEEPNORM_ALPHA = (2.0 * DEPTH) ** 0.25
LN_EPS = 1e-5
HEAD_EPS = 1e-6
FORGET_FLOOR = 1e-6

LANES = 128
SUBLANES = 8

_CB, _CC, _CH = 0, 4, 8
_RQ, _RK, _RV, _RG = 12, 18, 24, 30
_HQ, _HFF, _HFB, _HI, _HG = 36, 42, 48, 54, 60
_OUT_RET, _OUT_HG = 4, 10

SEQ_CHUNK = 128
MOE_TILE = 256
VMEM_LIMIT = 56 * 1024 * 1024
MOE_VMEM_LIMIT = 60 * 1024 * 1024


def _cparams(sem, vmem=None):
    return pltpu.CompilerParams(dimension_semantics=sem, vmem_limit_bytes=vmem)


def _silu(x):
    return x * jax.nn.sigmoid(x)


def _dot(a, b):
    return jnp.dot(a, b, preferred_element_type=F32)


def _dot_nt(a, b):
    return lax.dot_general(a, b, (((1,), (1,)), ((), ())), preferred_element_type=F32)


def _dot_tn(a, b):
    return lax.dot_general(a, b, (((0,), (0,)), ((), ())), preferred_element_type=F32)


def _ada_kernel(c_ref, w_ref, b_ref, o_ref):
    cond = _silu(c_ref[...])
    o_ref[0] = _dot(cond.astype(BF16), w_ref[0].astype(BF16)) + b_ref[0]


def _ada_mod(c, w_ada, b_ada):
    depth, d, n6 = w_ada.shape
    b = c.shape[0]
    bp = -(-b // SUBLANES) * SUBLANES
    cp = jnp.pad(c, ((0, bp - b), (0, 0)))
    tn = 512
    out = pl.pallas_call(
        _ada_kernel,
        grid=(depth, n6 // tn),
        in_specs=[
            pl.BlockSpec((bp, d), lambda l, j: (0, 0)),
            pl.BlockSpec((1, d, tn), lambda l, j: (l, 0, j)),
            pl.BlockSpec((1, 1, tn), lambda l, j: (l, 0, j)),
        ],
        out_specs=pl.BlockSpec((1, bp, tn), lambda l, j: (l, 0, j)),
        out_shape=jax.ShapeDtypeStruct((depth, bp, n6), F32),
        compiler_params=_cparams(("arbitrary", "arbitrary")),
        name="ada_mod",
    )(cp, w_ada, b_ada.reshape(depth, 1, n6))
    return out[:, :b, :]


def _layer_norm_rows(z, g, b):
    mu = jnp.mean(z, axis=-1, keepdims=True)
    zc = z - mu
    var = jnp.mean(zc * zc, axis=-1, keepdims=True)
    return zc * lax.rsqrt(var + LN_EPS) * g + b


def _ln_mod_kernel(x_ref, g_ref, b_ref, sc_ref, sh_ref, xo_ref, h_ref):
    y = _layer_norm_rows(x_ref[0], g_ref[...], b_ref[...])
    xo_ref[0] = y
    h_ref[0] = (y * (1.0 + sc_ref[0]) + sh_ref[0]).astype(BF16)


def _ln_mod(x, g, b, scale, shift):
    bsz, t, d = x.shape
    tr = 512
    row = pl.BlockSpec((1, tr, d), lambda i, j: (i, j, 0))
    vec = pl.BlockSpec((1, d), lambda i, j: (0, 0))
    mod = pl.BlockSpec((1, 1, d), lambda i, j: (i, 0, 0))
    return pl.pallas_call(
        _ln_mod_kernel,
        grid=(bsz, t // tr),
        in_specs=[row, vec, vec, mod, mod],
        out_specs=[row, row],
        out_shape=[jax.ShapeDtypeStruct((bsz, t, d), F32),
                   jax.ShapeDtypeStruct((bsz, t, d), BF16)],
        compiler_params=_cparams(("arbitrary", "arbitrary")),
        name="ln_mod",
    )(x, g.reshape(1, d), b.reshape(1, d), scale, shift)


def _in_proj_kernel(a_ref, w_ref, o_ref, w16_scr):
    @pl.when(pl.program_id(1) == 0)
    def _():
        w16_scr[...] = w_ref[0].astype(BF16)

    o_ref[...] = _dot(a_ref[...], w16_scr[...]).astype(o_ref.dtype)


def _in_proj(h, w_in, layer, out_dtype):
    n, d = h.shape
    nc = w_in.shape[2]
    tm, tn = 1024, 1408
    tm = min(tm, n)
    return pl.pallas_call(
        _in_proj_kernel,
        grid=(nc // tn, n // tm),
        in_specs=[pl.BlockSpec((tm, d), lambda j, i: (i, 0)),
                  pl.BlockSpec((1, d, tn), lambda j, i: (layer, 0, j))],
        out_specs=pl.BlockSpec((tm, tn), lambda j, i: (i, j)),
        out_shape=jax.ShapeDtypeStruct((n, nc), out_dtype),
        scratch_shapes=[pltpu.VMEM((d, tn), BF16)],
        compiler_params=_cparams(("arbitrary", "arbitrary"), VMEM_LIMIT),
        name="in_proj",
    )(h, w_in)


def _conv_kernel(cb_ref, cc_ref, ch_ref, w_ref, beta_ref, o_ref):
    u = cc_ref[0].astype(F32) * ch_ref[0].astype(F32)
    t = u.shape[0]
    row = lax.broadcasted_iota(I32, u.shape, 0)
    prev = jnp.where(row == 0, 0.0, pltpu.roll(u, 1, 0))
    nxt = jnp.where(row == t - 1, 0.0, pltpu.roll(u, t - 1, 0))
    w = w_ref[...]
    y = cb_ref[0].astype(F32) * (prev * w[0:1] + u * w[1:2] + nxt * w[2:3])
    o_ref[0] = (y * beta_ref[...]).astype(o_ref.dtype)


def _conv_mixer(proj, conv_w, beta):
    bsz, t, _ = proj.shape
    cw = 256
    nb = CONV_WIDTH // cw
    per = LANES * 1

    def col(off):
        return pl.BlockSpec((1, t, cw), lambda b, j, off=off: (b, 0, off * per // cw + j))

    return pl.pallas_call(
        _conv_kernel,
        grid=(bsz, nb),
        in_specs=[col(_CB), col(_CC), col(_CH),
                  pl.BlockSpec((3, cw), lambda b, j: (0, j)),
                  pl.BlockSpec((1, cw), lambda b, j: (0, j))],
        out_specs=pl.BlockSpec((1, t, cw), lambda b, j: (b, 0, j)),
        out_shape=jax.ShapeDtypeStruct((bsz, t, CONV_WIDTH), BF16),
        compiler_params=_cparams(("arbitrary", "arbitrary"), VMEM_LIMIT),
        name="conv_mixer",
    )(proj, proj, proj, conv_w, beta)


def _ret_kernel(q_ref, k_ref, v_ref, g_ref, cos_ref, sin_ref, lg_ref, beta_ref,
                o_ref, o_scr, qf_scr, qb_scr, uf_scr, ub_scr):
    L = SEQ_CHUNK
    t = q_ref.shape[1]
    n = t // L
    lgf = lg_ref[0, 0:1, :]
    lgb = lg_ref[0, 1:2, :]
    r = lax.broadcasted_iota(I32, (L, L), 0).astype(F32)
    c = lax.broadcasted_iota(I32, (L, L), 1).astype(F32)
    rel = r - c
    dbi = jnp.where(rel > 0, jnp.exp(lgf * jnp.maximum(rel, 0.0)),
                    jnp.where(rel < 0, jnp.exp(lgb * jnp.maximum(-rel, 0.0)), 2.0))
    qf = jnp.exp(lgf * (r + 1.0))
    kf = jnp.exp(lgf * (L - 1.0 - r))
    qb = jnp.exp(lgb * (L - r))
    kb = jnp.exp(lgb * r)
    gf_l = jnp.exp(lgf * float(L))
    gb_l = jnp.exp(lgb * float(L))
    scale = HEAD_DIM ** -0.5

    def rot(x, cs, sn):
        return x * cs + pltpu.roll(x, HEAD_DIM // 2, 1) * sn

    def rows(ci):
        return pl.ds(pl.multiple_of(ci * L, L), L)

    def local_body(ci, carry):
        sl = rows(ci)
        cs = cos_ref[sl, :]
        sn = sin_ref[sl, :]
        q = rot(q_ref[0, sl, :].astype(F32), cs, sn)
        k = rot(k_ref[0, sl, :].astype(F32), cs, sn) * scale
        v16 = v_ref[0, sl, :].astype(BF16)
        s = _dot_nt(q.astype(BF16), k.astype(BF16)) * dbi
        o_scr[sl, :] = _dot(s.astype(BF16), v16)
        qf_scr[sl, :] = (q * qf).astype(BF16)
        qb_scr[sl, :] = (q * qb).astype(BF16)
        uf_scr[ci] = _dot_tn((k * kf).astype(BF16), v16)
        ub_scr[ci] = _dot_tn((k * kb).astype(BF16), v16)
        return carry

    lax.fori_loop(0, n, local_body, 0, unroll=4)

    def scan_body(i, states):
        sf, sb = states
        cf, cb = i, n - 1 - i
        slf, slb = rows(cf), rows(cb)
        o_scr[slf, :] += _dot(qf_scr[slf, :], sf.astype(BF16))
        o_scr[slb, :] += _dot(qb_scr[slb, :], sb.astype(BF16))
        return gf_l * sf + uf_scr[cf], gb_l * sb + ub_scr[cb]

    zero = jnp.zeros((HEAD_DIM, HEAD_DIM), F32)
    lax.fori_loop(0, n, scan_body, (zero, zero), unroll=2)

    beta = beta_ref[...]

    def norm_body(ci, carry):
        sl = rows(ci)
        o = o_scr[sl, :]
        mu = jnp.mean(o, axis=-1, keepdims=True)
        oc = o - mu
        var = jnp.mean(oc * oc, axis=-1, keepdims=True)
        y = oc * lax.rsqrt(var + HEAD_EPS) * _silu(g_ref[0, sl, :].astype(F32)) * beta
        o_ref[0, sl, :] = y.astype(o_ref.dtype)
        return carry

    lax.fori_loop(0, n, norm_body, 0, unroll=2)


def _rotary_tables(t):
    half = HEAD_DIM // 2
    inv_freq = ROPE_BASE ** (-jnp.arange(half, dtype=F32) / half)
    ang = jnp.arange(t, dtype=F32)[:, None] * inv_freq[None, :]
    cos, sin = jnp.cos(ang), jnp.sin(ang)
    return jnp.concatenate([cos, cos], -1), jnp.concatenate([-sin, sin], -1)


def _ret_log_decays():
    head = jnp.arange(RET_HEADS, dtype=F32)
    lg_f = jnp.log1p(-jnp.exp2(-5.0 - head))
    lg_b = jnp.log1p(-jnp.exp2(-5.5 - head))
    lg = jnp.stack([lg_f, lg_b], axis=1)
    return jnp.broadcast_to(lg[:, :, None], (RET_HEADS, 2, LANES))


def _ret_mixer(proj, beta):
    bsz, t, _ = proj.shape
    n_chunks = t // SEQ_CHUNK
    assert n_chunks % 2 == 0
    cosf, sinf = _rotary_tables(t)

    def col(off):
        return pl.BlockSpec((1, t, HEAD_DIM), lambda b, h, off=off: (b, 0, off + h))

    tab = pl.BlockSpec((t, HEAD_DIM), lambda b, h: (0, 0))
    return pl.pallas_call(
        _ret_kernel,
        grid=(bsz, RET_HEADS),
        in_specs=[col(_RQ), col(_RK), col(_RV), col(_RG), tab, tab,
                  pl.BlockSpec((1, 2, LANES), lambda b, h: (h, 0, 0)),
                  pl.BlockSpec((1, HEAD_DIM), lambda b, h: (0, _OUT_RET + h))],
        out_specs=pl.BlockSpec((1, t, HEAD_DIM), lambda b, h: (b, 0, h)),
        out_shape=jax.ShapeDtypeStruct((bsz, t, RET_WIDTH), BF16),
        scratch_shapes=[pltpu.VMEM((t, HEAD_DIM), F32),
                        pltpu.VMEM((t, HEAD_DIM), BF16), pltpu.VMEM((t, HEAD_DIM), BF16),
                        pltpu.VMEM((n_chunks, HEAD_DIM, HEAD_DIM), F32),
                        pltpu.VMEM((n_chunks, HEAD_DIM, HEAD_DIM), F32)],
        compiler_params=_cparams(("arbitrary", "arbitrary"), VMEM_LIMIT),
        name="ret_mixer",
    )(proj, proj, proj, proj, cosf, sinf, _ret_log_decays(), beta)


_HG_BASE = SUBLANES
_HG_LEVELS = tuple(m for m in (8, 16, 32, 64) if m < SEQ_CHUNK)


def _cumsum_rows(x, reverse):
    n = x.shape[0]
    row = lax.broadcasted_iota(I32, x.shape, 0)
    sh = 1
    while sh < n:
        if reverse:
            x = x + jnp.where(row < n - sh, pltpu.roll(x, n - sh, 0), 0.0)
        else:
            x = x + jnp.where(row >= sh, pltpu.roll(x, sh, 0), 0.0)
        sh *= 2
    return x


def _hg_masks(reverse):
    C = SEQ_CHUNK
    r = lax.broadcasted_iota(I32, (C, C), 0)
    c = lax.broadcasted_iota(I32, (C, C), 1)
    if reverse:
        r, c = c, r
    sh = _HG_BASE.bit_length() - 1
    masks = [((r >> sh) == (c >> sh)) & (c <= r)]
    for m in _HG_LEVELS:
        sh = m.bit_length() - 1
        masks.append(((r >> (sh + 1)) == (c >> (sh + 1)))
                     & (((r >> sh) & 1) == 1) & (((c >> sh) & 1) == 0))
    return [jnp.where(m, 1.0, 0.0) for m in masks]


def _hg_scores(q, kk, cum, mask_ref, reverse):
    C = SEQ_CHUNK
    d = 1 if reverse else 0

    def split(x, blk):
        return x.reshape(C // blk, blk, LANES)

    b = _HG_BASE
    ref_row = b // 2 if reverse else b // 2 - 1
    cum3 = split(cum, b)
    rel = cum3 - cum3[:, ref_row:ref_row + 1, :]
    qt = (split(q, b) * jnp.exp(rel)).reshape(C, LANES).astype(BF16)
    kt = (split(kk, b) * jnp.exp(-rel)).reshape(C, LANES).astype(BF16)
    a = jnp.where(mask_ref[d, 0] > 0.5, _dot_nt(qt, kt), 0.0)
    for li, m in enumerate(_HG_LEVELS):
        blk = 2 * m
        ref_row = m if reverse else m - 1
        cum3 = split(cum, blk)
        rel = cum3 - cum3[:, ref_row:ref_row + 1, :]
        dec = jnp.exp(-jnp.abs(rel)).reshape(C, LANES)
        halves = []
        for j in range(C // m):
            is_query = (j % 2 == 1) != reverse
            halves.append((q if is_query else kk)[j * m:(j + 1) * m])
        x = (jnp.concatenate(halves, axis=0) * dec).astype(BF16)
        a = a + mask_ref[d, li + 1] * _dot_nt(x, x)
    return a


def _hg_kernel(layer, q_ref, zf_ref, zb_ref, v_ref, g_ref, lbl_ref, beta_ref, o_ref,
               o_scr, qe_scr, u_scr, dec_scr, mask_scr):
    C = SEQ_CHUNK
    t = q_ref.shape[1]
    n = t // C
    for d, reverse in enumerate((False, True)):
        for li, m in enumerate(_hg_masks(reverse)):
            mask_scr[d, li] = m
    logits = lbl_ref[...].astype(F32)
    e = jnp.exp(logits - jnp.max(logits, axis=0, keepdims=True))
    p = e / jnp.sum(e, axis=0, keepdims=True)
    lb = p[0]
    for l in range(1, layer + 1):
        lb = lb + p[l]
    lb = lb - p[0]

    def gates(z_ref, sl, lb_row):
        f = lb_row + (1.0 - lb_row) * jax.nn.sigmoid(z_ref[0, sl, :].astype(F32))
        f = jnp.maximum(f, FORGET_FLOOR)
        return jnp.log(f), 1.0 - f

    def rows(ci):
        return pl.ds(pl.multiple_of(ci * C, C), C)

    def local_dir(ci, sl, q, v16, z_ref, d):
        reverse = d == 1
        logf, kk = gates(z_ref, sl, lb[d:d + 1, :])
        cum = _cumsum_rows(logf, reverse)
        a = _hg_scores(q, kk, cum, mask_scr, reverse)
        total = cum[0:1, :] if reverse else cum[C - 1:C, :]
        qe_scr[d, sl, :] = (q * jnp.exp(cum)).astype(BF16)
        k_tail = kk * jnp.exp(total - cum)
        u_scr[d, ci] = _dot_tn(v16, k_tail.astype(BF16))
        dec_scr[d, ci] = jnp.broadcast_to(jnp.exp(total), (SUBLANES, LANES))
        return _dot(a.astype(BF16), v16)

    def local_body(ci, carry):
        sl = rows(ci)
        q = q_ref[0, sl, :].astype(F32)
        v16 = v_ref[0, sl, :].astype(BF16)
        o_scr[sl, :] = (local_dir(ci, sl, q, v16, zf_ref, 0)
                        + local_dir(ci, sl, q, v16, zb_ref, 1))
        return carry

    lax.fori_loop(0, n, local_body, 0, unroll=2)

    def scan_body(i, states):
        sf, sb = states
        cf, cb = i, n - 1 - i
        slf, slb = rows(cf), rows(cb)
        o_scr[slf, :] += _dot_nt(qe_scr[0, slf, :], sf.astype(BF16))
        o_scr[slb, :] += _dot_nt(qe_scr[1, slb, :], sb.astype(BF16))
        return (sf * dec_scr[0, cf, 0:1, :] + u_scr[0, cf],
                sb * dec_scr[1, cb, 0:1, :] + u_scr[1, cb])

    zero = jnp.zeros((HEAD_DIM, HEAD_DIM), F32)
    lax.fori_loop(0, n, scan_body, (zero, zero), unroll=2)

    beta = beta_ref[...]

    def norm_body(ci, carry):
        sl = rows(ci)
        o = o_scr[sl, :]
        ms = jnp.mean(o * o, axis=-1, keepdims=True)
        y = o * lax.rsqrt(ms + HEAD_EPS) * _silu(g_ref[0, sl, :].astype(F32)) * beta
        o_ref[0, sl, :] = y.astype(o_ref.dtype)
        return carry

    lax.fori_loop(0, n, norm_body, 0, unroll=2)


def _hg_mixer(proj, lb_logits, beta, layer):
    bsz, t, _ = proj.shape
    n_chunks = t // SEQ_CHUNK
    assert n_chunks % 2 == 0

    def col(off):
        return pl.BlockSpec((1, t, HEAD_DIM), lambda b, h, off=off: (b, 0, off + h))

    return pl.pallas_call(
        functools.partial(_hg_kernel, layer),
        grid=(bsz, HG_HEADS),
        in_specs=[col(_HQ), col(_HFF), col(_HFB), col(_HI), col(_HG),
                  pl.BlockSpec((DEPTH, 2, HEAD_DIM), lambda b, h: (0, 0, h)),
                  pl.BlockSpec((1, HEAD_DIM), lambda b, h: (0, _OUT_HG + h))],
        out_specs=pl.BlockSpec((1, t, HEAD_DIM), lambda b, h: (b, 0, h)),
        out_shape=jax.ShapeDtypeStruct((bsz, t, HG_WIDTH), BF16),
        scratch_shapes=[pltpu.VMEM((t, HEAD_DIM), F32),
                        pltpu.VMEM((2, t, HEAD_DIM), BF16),
                        pltpu.VMEM((2, n_chunks, HEAD_DIM, HEAD_DIM), F32),
                        pltpu.VMEM((2, n_chunks, SUBLANES, LANES), F32),
                        pltpu.VMEM((2, 1 + len(_HG_LEVELS), SEQ_CHUNK, SEQ_CHUNK), F32)],
        compiler_params=_cparams(("arbitrary", "arbitrary"), VMEM_LIMIT),
        name="hg_mixer",
    )(proj, proj, proj, proj, proj, lb_logits, beta)


_OUT_PROJ_SPLIT = 2


def _out_proj_kernel(yc_ref, yr_ref, yh_ref, w_ref, x_ref, gate_ref, g_ref, b_ref,
                     sc_ref, sh_ref, wrh_ref, wrl_ref, xo_ref, h_ref, lo_ref):
    c0, c1 = CONV_WIDTH, CONV_WIDTH + RET_WIDTH
    tm = x_ref.shape[1]
    for half in range(_OUT_PROJ_SPLIT):
        sl = pl.ds(half * (tm // _OUT_PROJ_SPLIT), tm // _OUT_PROJ_SPLIT)
        y = (_dot(yc_ref[0, sl, :], w_ref[0, 0:c0, :]) + _dot(yr_ref[0, sl, :], w_ref[0, c0:c1, :])
             + _dot(yh_ref[0, sl, :], w_ref[0, c1:, :]))
        z = DEEPNORM_ALPHA * x_ref[0, sl, :] + (1.0 + gate_ref[0]) * y
        xn = _layer_norm_rows(z, g_ref[...], b_ref[...])
        xo_ref[0, sl, :] = xn
        h = xn * (1.0 + sc_ref[0]) + sh_ref[0]
        h_ref[0, sl, :] = h
        h_hi = h.astype(BF16)
        h_lo = (h - h_hi.astype(F32)).astype(BF16)
        wrh = wrh_ref[...]
        lo_ref[0, sl, :] = _dot(h_hi, wrh) + _dot(h_lo, wrh) + _dot(h_hi, wrl_ref[...])


def _out_proj(yc, yr, yh, w_out16, layer, x, gate, ln_g, ln_b, scale2, shift2, wr_hi, wr_lo):
    bsz, t, d = x.shape
    tm = 512

    def row(width):
        return pl.BlockSpec((1, tm, width), lambda b, i: (b, i, 0))

    vec = pl.BlockSpec((1, d), lambda b, i: (0, 0))
    mod = pl.BlockSpec((1, 1, d), lambda b, i: (b, 0, 0))
    wr = pl.BlockSpec((d, LANES), lambda b, i: (0, 0))
    return pl.pallas_call(
        _out_proj_kernel,
        grid=(bsz, t // tm),
        in_specs=[row(CONV_WIDTH), row(RET_WIDTH), row(HG_WIDTH),
                  pl.BlockSpec((1, d, d), lambda b, i: (layer, 0, 0)),
                  row(d), mod, vec, vec, mod, mod, wr, wr],
        out_specs=[row(d), row(d), row(LANES)],
        out_shape=[jax.ShapeDtypeStruct((bsz, t, d), F32),
                   jax.ShapeDtypeStruct((bsz, t, d), F32),
                   jax.ShapeDtypeStruct((bsz, t, LANES), F32)],
        compiler_params=_cparams(("arbitrary", "arbitrary"), VMEM_LIMIT),
        name="out_proj",
    )(yc, yr, yh, w_out16, x, gate, ln_g.reshape(1, d), ln_b.reshape(1, d),
      scale2, shift2, wr_hi, wr_lo)


def _route_kernel(n_tiles, lt_ref, bias_ref, pos1_ref, pos2_ref, w1_ref, w2_ref, tile_ref):
    E = N_EXPERTS
    logits = [lt_ref[e] for e in range(E)]
    shape = logits[0].shape
    mx = functools.reduce(jnp.maximum, logits)
    ex = [jnp.exp(l - mx) for l in logits]
    den = functools.reduce(lambda a, b: a + b, ex)
    scores = [x / den for x in ex]
    sel = [scores[e] + bias_ref[e] for e in range(E)]

    best_g = jnp.zeros(shape, I32)
    best_v = None
    for g in range(N_GROUPS):
        a, b, c, d = sel[EXPERTS_PER_GROUP * g: EXPERTS_PER_GROUP * (g + 1)]
        hi1, lo1 = jnp.maximum(a, b), jnp.minimum(a, b)
        hi2, lo2 = jnp.maximum(c, d), jnp.minimum(c, d)
        gs = jnp.maximum(hi1, hi2) + jnp.maximum(jnp.minimum(hi1, hi2), jnp.maximum(lo1, lo2))
        if g == 0:
            best_v = gs
        else:
            better = gs > best_v
            best_g = jnp.where(better, g, best_g)
            best_v = jnp.where(better, gs, best_v)

    masked = [jnp.where(best_g == (e // EXPERTS_PER_GROUP), sel[e], MASK_NEG) for e in range(E)]

    def arg_top(vals, exclude):
        idx = jnp.zeros(shape, I32)
        val = None
        for e in range(E):
            v = vals[e] if exclude is None else jnp.where(exclude == e, -jnp.inf, vals[e])
            if e == 0:
                val = v
            else:
                better = v > val
                idx = jnp.where(better, e, idx)
                val = jnp.where(better, v, val)
        return idx

    idx1 = arg_top(masked, None)
    idx2 = arg_top(masked, idx1)
    s1 = functools.reduce(lambda a, b: a + b, [jnp.where(idx1 == e, scores[e], 0.0) for e in range(E)])
    s2 = functools.reduce(lambda a, b: a + b, [jnp.where(idx2 == e, scores[e], 0.0) for e in range(E)])
    w1_ref[...] = s1 / (s1 + s2)
    w2_ref[...] = s2 / (s1 + s2)

    rows = shape[0]
    li = lax.broadcasted_iota(I32, (LANES, LANES), 0)
    lj = lax.broadcasted_iota(I32, (LANES, LANES), 1)
    upper = (li <= lj).astype(BF16)
    ri = lax.broadcasted_iota(I32, (rows, rows), 0)
    rj = lax.broadcasted_iota(I32, (rows, rows), 1)
    lower = (rj < ri).astype(BF16)
    tile_start = (lax.broadcasted_iota(I32, (1, LANES), 1) * MOE_TILE).astype(F32)
    start = jnp.zeros((1, 1), F32)
    pos1 = jnp.zeros(shape, F32)
    pos2 = jnp.zeros(shape, F32)
    tile_e = jnp.zeros((1, LANES), F32)
    for e in range(E):
        hit1 = idx1 == e
        hit2 = idx2 == e
        onehot = jnp.where(hit1 | hit2, 1.0, 0.0)
        pref = _dot(onehot.astype(BF16), upper)
        row_tot = pref[:, LANES - 1:LANES]
        row_off = _dot(lower, jnp.broadcast_to(row_tot, shape).astype(BF16))[:, 0:1]
        rank = pref - onehot + row_off
        count = jnp.sum(onehot, keepdims=True)
        dest = start + rank
        pos1 = jnp.where(hit1, dest, pos1)
        pos2 = jnp.where(hit2, dest, pos2)
        start = start + jnp.ceil(count / MOE_TILE) * MOE_TILE
        tile_e = tile_e + jnp.where(tile_start >= start, 1.0, 0.0)
    pos1_ref[...] = pos1.astype(I32)
    pos2_ref[...] = pos2.astype(I32)
    n_valid = start / MOE_TILE
    tile_id = lax.broadcasted_iota(I32, (1, LANES), 1).astype(F32)
    last_e = jnp.sum(jnp.where(tile_id == n_valid - 1.0, tile_e, 0.0), keepdims=True)
    tile_e = jnp.where(tile_id < n_valid, tile_e, last_e)
    sub = lax.broadcasted_iota(I32, (SUBLANES, LANES), 0)
    tile_ref[...] = jnp.where(sub == 0, tile_e, jnp.broadcast_to(n_valid, (SUBLANES, LANES))).astype(I32)


def _route(logits_t, router_bias, n_tiles):
    e, rows, lanes = logits_t.shape
    tok = jax.ShapeDtypeStruct((rows, lanes), I32)
    tokf = jax.ShapeDtypeStruct((rows, lanes), F32)
    full = pl.BlockSpec((rows, lanes), lambda i: (0, 0))
    return pl.pallas_call(
        functools.partial(_route_kernel, n_tiles),
        grid=(1,),
        in_specs=[pl.BlockSpec((e, rows, lanes), lambda i: (0, 0, 0)),
                  pl.BlockSpec(memory_space=pltpu.SMEM)],
        out_specs=[full, full, full, full, pl.BlockSpec((SUBLANES, LANES), lambda i: (0, 0))],
        out_shape=[tok, tok, tokf, tokf, jax.ShapeDtypeStruct((SUBLANES, LANES), I32)],
        compiler_params=_cparams(("arbitrary",)),
        name="route",
    )(logits_t, router_bias)


_ROWS_PER_STEP = 512


_DMA_UNROLL = 8


def _scatter_rows_kernel(n_tiles, p1_ref, p2_ref, fill_ref, src_ref, dst_ref, zero_scr, sem, zsem):
    def zero_copy(i):
        row0 = pl.multiple_of(i * MOE_TILE, MOE_TILE)
        return pltpu.make_async_copy(zero_scr, dst_ref.at[pl.ds(row0, MOE_TILE)], zsem)

    @pl.when(pl.program_id(0) == 0)
    def _():
        zero_scr[...] = jnp.zeros_like(zero_scr)

        def zstart(i, carry):
            @pl.when(fill_ref[i] == 1)
            def _():
                zero_copy(i).start()
            return carry

        def zwait(i, carry):
            @pl.when(fill_ref[i] == 1)
            def _():
                zero_copy(i).wait()
            return carry

        lax.fori_loop(0, n_tiles, zstart, 0)
        lax.fori_loop(0, n_tiles, zwait, 0)

    def copies(r):
        row = src_ref.at[pl.ds(r, 1)]
        return (pltpu.make_async_copy(row, dst_ref.at[pl.ds(p1_ref[0, 0, r], 1)], sem.at[0]),
                pltpu.make_async_copy(row, dst_ref.at[pl.ds(p2_ref[0, 0, r], 1)], sem.at[1]))

    def start(r, carry):
        for cp in copies(r):
            cp.start()
        return carry

    def wait(r, carry):
        for cp in copies(r):
            cp.wait()
        return carry

    lax.fori_loop(0, _ROWS_PER_STEP, start, 0, unroll=_DMA_UNROLL)
    lax.fori_loop(0, _ROWS_PER_STEP, wait, 0, unroll=_DMA_UNROLL)


def _scatter_rows(src, pos1, pos2, fill_tile, n_tiles):
    n, d = src.shape
    steps = n // _ROWS_PER_STEP
    idx = pl.BlockSpec((1, 1, _ROWS_PER_STEP), lambda i: (i, 0, 0), memory_space=pltpu.SMEM)
    return pl.pallas_call(
        functools.partial(_scatter_rows_kernel, n_tiles),
        grid=(steps,),
        in_specs=[idx, idx, pl.BlockSpec(memory_space=pltpu.SMEM),
                  pl.BlockSpec((_ROWS_PER_STEP, d), lambda i: (i, 0))],
        out_specs=pl.BlockSpec(memory_space=pl.ANY),
        out_shape=jax.ShapeDtypeStruct((n_tiles * MOE_TILE, d), src.dtype),
        scratch_shapes=[pltpu.VMEM((MOE_TILE, d), src.dtype),
                        pltpu.SemaphoreType.DMA((2,)), pltpu.SemaphoreType.DMA],
        compiler_params=_cparams(("arbitrary",), VMEM_LIMIT),
        name="scatter_rows",
    )(pos1.reshape(steps, 1, _ROWS_PER_STEP), pos2.reshape(steps, 1, _ROWS_PER_STEP), fill_tile, src)


_W_CHUNK = 128
_W_RING = 11


def _moe_kernel(layer, te_ref, nv_ref, first_ref, slot_ref, nxt_ref,
                x_ref, wg_hbm, wu_hbm, wd_hbm, y_ref,
                cg_scr, cu_scr, cd_scr, stage_scr, sem, cnt_ref):
    i = pl.program_id(0)
    d, f = cg_scr.shape[1], cg_scr.shape[2]
    n_g = d // _W_CHUNK
    n_chunks = 2 * n_g + f // _W_CHUNK

    def for_chunk(c, e, slot, fn):
        k = c % _W_RING

        @pl.when(c < n_g)
        def _():
            row = pl.multiple_of(c * _W_CHUNK, _W_CHUNK)
            fn(wg_hbm.at[layer, e, pl.ds(row, _W_CHUNK), :], stage_scr.at[k, :, pl.ds(0, f)],
               sem.at[k], cg_scr.at[slot, pl.ds(row, _W_CHUNK), :])

        @pl.when((c >= n_g) & (c < 2 * n_g))
        def _():
            row = pl.multiple_of((c - n_g) * _W_CHUNK, _W_CHUNK)
            fn(wu_hbm.at[layer, e, pl.ds(row, _W_CHUNK), :], stage_scr.at[k, :, pl.ds(0, f)],
               sem.at[k], cu_scr.at[slot, pl.ds(row, _W_CHUNK), :])

        @pl.when(c >= 2 * n_g)
        def _():
            row = pl.multiple_of((c - 2 * n_g) * _W_CHUNK, _W_CHUNK)
            fn(wd_hbm.at[layer, e, pl.ds(row, _W_CHUNK), :], stage_scr.at[k],
               sem.at[k], cd_scr.at[slot, pl.ds(row, _W_CHUNK), :])

    def start(src, stage, s, dst):
        pltpu.make_async_copy(src, stage, s).start()

    def finish(src, stage, s, dst):
        pltpu.make_async_copy(src, stage, s).wait()
        dst[...] = stage[...].astype(BF16)

    def start_upto(e, slot, hi):
        hi = jnp.minimum(hi, n_chunks)

        def body(c, carry):
            for_chunk(c, e, slot, start)
            return carry

        lax.fori_loop(cnt_ref[0], hi, body, 0)
        cnt_ref[0] = jnp.maximum(cnt_ref[0], hi)

    def finish_started(e, slot):
        def body(c, carry):
            for_chunk(c, e, slot, finish)
            return carry

        lax.fori_loop(cnt_ref[1], cnt_ref[0], body, 0)
        cnt_ref[1] = cnt_ref[0]

    valid = i < nv_ref[0]

    @pl.when(i == 0)
    def _():
        cnt_ref[0] = 0
        cnt_ref[1] = 0

    @pl.when(valid & (first_ref[i] == 1))
    def _():
        e, slot = te_ref[i], slot_ref[i]

        def body(c, carry):
            start_upto(e, slot, c + _W_RING)
            for_chunk(c, e, slot, finish)
            return carry

        lax.fori_loop(cnt_ref[1], n_chunks, body, 0)
        cnt_ref[0] = 0
        cnt_ref[1] = 0

    @pl.when(valid & (nxt_ref[i] >= 0))
    def _():
        e, slot = nxt_ref[i], 1 - slot_ref[i]
        finish_started(e, slot)
        start_upto(e, slot, cnt_ref[0] + _W_RING)

    @pl.when(valid)
    def _():
        slot = slot_ref[i]
        x = x_ref[...].astype(BF16)
        a = _dot(x, cg_scr[slot])
        b = _dot(x, cu_scr[slot])
        hidden = (_silu(a) * b).astype(BF16)
        y_ref[...] = _dot(hidden, cd_scr[slot])

    @pl.when(jnp.logical_not(valid))
    def _():
        y_ref[...] = jnp.zeros_like(y_ref)


def _moe_schedule(tile_e, n_valid):
    n_tiles = tile_e.shape[0]
    idx = jnp.arange(n_tiles, dtype=I32)
    valid = idx < n_valid[0]
    first = valid & ((idx == 0) | (tile_e != jnp.roll(tile_e, 1)))
    run = jnp.cumsum(first.astype(I32)) - 1
    next_first = lax.cummin(jnp.where(first, idx, n_tiles), reverse=True)
    after = jnp.concatenate([next_first[1:], jnp.full((1,), n_tiles, I32)])
    nxt = jnp.where(valid & (after < n_tiles), tile_e[jnp.minimum(after, n_tiles - 1)], -1)
    last = valid & ((after == idx + 1) | (idx == n_valid[0] - 1))
    fill = last | jnp.logical_not(valid)
    return first.astype(I32), (run % 2).astype(I32), nxt.astype(I32), fill.astype(I32)


def _moe_experts(xs, tile_e, n_valid, schedule, w_gate, w_up, w_down, layer):
    n_rows, d = xs.shape
    f = w_gate.shape[3]
    tm = MOE_TILE
    n_tiles = n_rows // tm
    assert d % _W_CHUNK == 0 and f % _W_CHUNK == 0
    first, slot, nxt = schedule

    def xmap(i, te, nv, *_):
        return (jnp.minimum(i, nv[0] - 1), 0)

    any_spec = pl.BlockSpec(memory_space=pl.ANY)
    grid_spec = pltpu.PrefetchScalarGridSpec(
        num_scalar_prefetch=5,
        grid=(n_tiles,),
        in_specs=[pl.BlockSpec((tm, d), xmap), any_spec, any_spec, any_spec],
        out_specs=pl.BlockSpec((tm, d), lambda i, *_: (i, 0)),
        scratch_shapes=[pltpu.VMEM((2, d, f), BF16), pltpu.VMEM((2, d, f), BF16),
                        pltpu.VMEM((2, f, d), BF16),
                        pltpu.VMEM((_W_RING, _W_CHUNK, d), F32),
                        pltpu.SemaphoreType.DMA((_W_RING,)),
                        pltpu.SMEM((2,), I32)],
    )
    return pl.pallas_call(
        functools.partial(_moe_kernel, layer),
        grid_spec=grid_spec,
        out_shape=jax.ShapeDtypeStruct((n_rows, d), F32),
        compiler_params=_cparams(("arbitrary",), MOE_VMEM_LIMIT),
        name="moe_experts",
    )(tile_e, n_valid, first, slot, nxt, xs, w_gate, w_up, w_down)


_COMBINE_ROWS = 256


def _combine_kernel(with_next, steps, p1_ref, p2_ref, p1n_ref, p2n_ref, ys_ref, w1_ref, w2_ref,
                    x_ref, gate_ref, g_ref, b_ref, *rest):
    if with_next:
        sc_ref, sh_ref, xo_ref, h_ref, ybuf, sem = rest
    else:
        xo_ref, ybuf, sem = rest
    step = pl.program_id(0)
    slot = step % 2

    def copies(pa, pb, s, r):
        return (pltpu.make_async_copy(ys_ref.at[pl.ds(pa[0, 0, r], 1)],
                                      ybuf.at[s, 0, pl.ds(r, 1)], sem.at[s, 0]),
                pltpu.make_async_copy(ys_ref.at[pl.ds(pb[0, 0, r], 1)],
                                      ybuf.at[s, 1, pl.ds(r, 1)], sem.at[s, 1]))

    def start_all(pa, pb, s):
        def body(r, carry):
            for cp in copies(pa, pb, s, r):
                cp.start()
            return carry
        lax.fori_loop(0, _COMBINE_ROWS, body, 0, unroll=_DMA_UNROLL)

    def wait_all(pa, pb, s):
        def body(r, carry):
            for cp in copies(pa, pb, s, r):
                cp.wait()
            return carry
        lax.fori_loop(0, _COMBINE_ROWS, body, 0, unroll=_DMA_UNROLL)

    @pl.when(step == 0)
    def _():
        start_all(p1_ref, p2_ref, 0)

    @pl.when(step + 1 < steps)
    def _():
        start_all(p1n_ref, p2n_ref, 1 - slot)

    wait_all(p1_ref, p2_ref, slot)
    y = w1_ref[0] * ybuf[slot, 0] + w2_ref[0] * ybuf[slot, 1]
    z = DEEPNORM_ALPHA * x_ref[0] + (1.0 + gate_ref[0]) * y
    xn = _layer_norm_rows(z, g_ref[...], b_ref[...])
    xo_ref[0] = xn
    if with_next:
        h_ref[0] = (xn * (1.0 + sc_ref[0]) + sh_ref[0]).astype(BF16)


def _combine(ys, pos1, pos2, w1, w2, x, gate, ln_g, ln_b, next_mod):
    bsz, t, d = x.shape
    tr = _COMBINE_ROWS
    per_b = t // tr
    steps = bsz * per_b
    row = pl.BlockSpec((1, tr, d), lambda s: (s // per_b, s % per_b, 0))
    vec = pl.BlockSpec((1, d), lambda s: (0, 0))
    mod = pl.BlockSpec((1, 1, d), lambda s: (s // per_b, 0, 0))
    wcol = pl.BlockSpec((1, tr, 1), lambda s: (s // per_b, s % per_b, 0))
    idx = pl.BlockSpec((1, 1, tr), lambda s: (s, 0, 0), memory_space=pltpu.SMEM)
    idx_next = pl.BlockSpec((1, 1, tr), lambda s: (jnp.minimum(s + 1, steps - 1), 0, 0),
                            memory_space=pltpu.SMEM)
    with_next = next_mod is not None
    p1 = pos1.reshape(steps, 1, tr)
    p2 = pos2.reshape(steps, 1, tr)
    in_specs = [idx, idx, idx_next, idx_next, pl.BlockSpec(memory_space=pl.ANY),
                wcol, wcol, row, mod, vec, vec]
    args = [p1, p2, p1, p2, ys, w1.reshape(bsz, t, 1), w2.reshape(bsz, t, 1), x, gate,
            ln_g.reshape(1, d), ln_b.reshape(1, d)]
    out_specs = [row]
    out_shape = [jax.ShapeDtypeStruct((bsz, t, d), F32)]
    if with_next:
        in_specs += [mod, mod]
        args += list(next_mod)
        out_specs.append(row)
        out_shape.append(jax.ShapeDtypeStruct((bsz, t, d), BF16))
    res = pl.pallas_call(
        functools.partial(_combine_kernel, with_next, steps),
        grid=(steps,),
        in_specs=in_specs,
        out_specs=out_specs,
        out_shape=out_shape,
        scratch_shapes=[pltpu.VMEM((2, 2, tr, d), F32), pltpu.SemaphoreType.DMA((2, 2))],
        compiler_params=_cparams(("arbitrary",), VMEM_LIMIT),
        name="combine_ln",
    )(*args)
    return res if with_next else (res[0], None)


def kernel(x, c, emb_ln_g, emb_ln_b, w_ada, b_ada, w_in, conv_w, mix_beta, w_out, hg_lb_logits,
           ln_g, ln_b, w_router, router_bias, w_gate, w_up, w_down):
    bsz, t, d = x.shape
    n = bsz * t
    assert n % LANES == 0 and t % SEQ_CHUNK == 0
    n_tiles = 2 * n // MOE_TILE + N_EXPERTS
    assert n_tiles <= LANES

    mod = _ada_mod(c, w_ada, b_ada)
    mod = mod.reshape(DEPTH, bsz, 6, 1, d)

    def mods(l):
        return [mod[l, :, i] for i in range(6)]

    wr_hi = jnp.pad(w_router, ((0, 0), (0, LANES - N_EXPERTS)))
    wr_hi16 = wr_hi.astype(BF16)
    wr_lo16 = (wr_hi - wr_hi16.astype(F32)).astype(BF16)

    w_out16 = w_out.astype(BF16)

    m = [mods(l) for l in range(DEPTH)]
    xcur, h = _ln_mod(x, emb_ln_g, emb_ln_b, m[0][1], m[0][0])
    for l in range(DEPTH):
        _, _, gate1, shift2, scale2, gate2 = m[l]
        beta = mix_beta[l].reshape(1, -1)
        proj = _in_proj(h.reshape(n, d), w_in, l, BF16).reshape(bsz, t, IN_COLS)
        yc = _conv_mixer(proj, conv_w[l], beta)
        yr = _ret_mixer(proj, beta)
        yh = _hg_mixer(proj, hg_lb_logits, beta, l)
        x1, h2, logits = _out_proj(yc, yr, yh, w_out16, l, xcur, gate1,
                                   ln_g[l, 0], ln_b[l, 0], scale2, shift2, wr_hi16, wr_lo16)
        logits_t = logits.reshape(n, LANES)[:, :N_EXPERTS].T.reshape(N_EXPERTS, n // LANES, LANES)
        pos1, pos2, w1, w2, tiles = _route(logits_t, router_bias, n_tiles)
        tile_e, n_valid = tiles[0, :n_tiles], tiles[1, :1]
        first, slot, nxt, fill = _moe_schedule(tile_e, n_valid)
        xs = _scatter_rows(h2.reshape(n, d), pos1.reshape(n), pos2.reshape(n), fill, n_tiles)
        ys = _moe_experts(xs, tile_e, n_valid, (first, slot, nxt), w_gate, w_up, w_down, l)
        next_mod = (m[l + 1][1], m[l + 1][0]) if l + 1 < DEPTH else None
        xcur, h = _combine(ys, pos1.reshape(n), pos2.reshape(n), w1, w2, x1, gate2,
                           ln_g[l, 1], ln_b[l, 1], next_mod)
    return xcur
```

```python
import functools
import math

import jax
import jax.numpy as jnp
from jax import lax
from jax.experimental import pallas as pl
from jax.experimental.pallas import tpu as pltpu

F32 = jnp.float32
BF16 = jnp.bfloat16
I32 = jnp.int32

DEPTH = 2
CONV_WIDTH = 512
RET_WIDTH = 768
HG_WIDTH = 768
HEAD_DIM = 128
RET_HEADS = RET_WIDTH // HEAD_DIM
HG_HEADS = HG_WIDTH // HEAD_DIM
IN_COLS = 3 * CONV_WIDTH + 4 * RET_WIDTH + 5 * HG_WIDTH
ROPE_BASE = 10000.0
N_EXPERTS = 16
N_GROUPS = 4
EXPERTS_PER_GROUP = N_EXPERTS // N_GROUPS
MASK_NEG = -1e9
DEEPNORM_ALPHA = (2.0 * DEPTH) ** 0.25
LN_EPS = 1e-5
HEAD_EPS = 1e-6
FORGET_FLOOR = 1e-6

LANES = 128
SUBLANES = 8

_CB, _CC, _CH = 0, 4, 8
_RQ, _RK, _RV, _RG = 12, 18, 24, 30
_HQ, _HFF, _HFB, _HI, _HG = 36, 42, 48, 54, 60
_OUT_RET, _OUT_HG = 4, 10

SEQ_CHUNK = 128
MOE_TILE = 256
VMEM_LIMIT = 56 * 1024 * 1024
MOE_VMEM_LIMIT = 60 * 1024 * 1024


def _cparams(sem, vmem=None):
    return pltpu.CompilerParams(dimension_semantics=sem, vmem_limit_bytes=vmem)


def _silu(x):
    return x * jax.nn.sigmoid(x)


def _dot(a, b):
    return jnp.dot(a, b, preferred_element_type=F32)


def _dot_nt(a, b):
    return lax.dot_general(a, b, (((1,), (1,)), ((), ())), preferred_element_type=F32)


def _dot_tn(a, b):
    return lax.dot_general(a, b, (((0,), (0,)), ((), ())), preferred_element_type=F32)


def _ada_kernel(c_ref, w_ref, b_ref, o_ref):
    cond = _silu(c_ref[...])
    o_ref[0] = _dot(cond.astype(BF16), w_ref[0].astype(BF16)) + b_ref[0]


def _ada_mod(c, w_ada, b_ada):
    depth, d, n6 = w_ada.shape
    b = c.shape[0]
    bp = -(-b // SUBLANES) * SUBLANES
    cp = jnp.pad(c, ((0, bp - b), (0, 0)))
    tn = 512
    out = pl.pallas_call(
        _ada_kernel,
        grid=(depth, n6 // tn),
        in_specs=[
            pl.BlockSpec((bp, d), lambda l, j: (0, 0)),
            pl.BlockSpec((1, d, tn), lambda l, j: (l, 0, j)),
            pl.BlockSpec((1, 1, tn), lambda l, j: (l, 0, j)),
        ],
        out_specs=pl.BlockSpec((1, bp, tn), lambda l, j: (l, 0, j)),
        out_shape=jax.ShapeDtypeStruct((depth, bp, n6), F32),
        compiler_params=_cparams(("arbitrary", "arbitrary")),
        name="ada_mod",
    )(cp, w_ada, b_ada.reshape(depth, 1, n6))
    return out[:, :b, :]


def _layer_norm_rows(z, g, b):
    mu = jnp.mean(z, axis=-1, keepdims=True)
    zc = z - mu
    var = jnp.mean(zc * zc, axis=-1, keepdims=True)
    return zc * lax.rsqrt(var + LN_EPS) * g + b


def _ln_mod_kernel(x_ref, g_ref, b_ref, sc_ref, sh_ref, xo_ref, h_ref):
    y = _layer_norm_rows(x_ref[0], g_ref[...], b_ref[...])
    xo_ref[0] = y
    h_ref[0] = (y * (1.0 + sc_ref[0]) + sh_ref[0]).astype(BF16)


def _ln_mod(x, g, b, scale, shift):
    bsz, t, d = x.shape
    tr = 512
    row = pl.BlockSpec((1, tr, d), lambda i, j: (i, j, 0))
    vec = pl.BlockSpec((1, d), lambda i, j: (0, 0))
    mod = pl.BlockSpec((1, 1, d), lambda i, j: (i, 0, 0))
    return pl.pallas_call(
        _ln_mod_kernel,
        grid=(bsz, t // tr),
        in_specs=[row, vec, vec, mod, mod],
        out_specs=[row, row],
        out_shape=[jax.ShapeDtypeStruct((bsz, t, d), F32),
                   jax.ShapeDtypeStruct((bsz, t, d), BF16)],
        compiler_params=_cparams(("arbitrary", "arbitrary")),
        name="ln_mod",
    )(x, g.reshape(1, d), b.reshape(1, d), scale, shift)


def _in_proj_kernel(a_ref, w_ref, o_ref, w16_scr):
    @pl.when(pl.program_id(1) == 0)
    def _():
        w16_scr[...] = w_ref[0].astype(BF16)

    o_ref[...] = _dot(a_ref[...], w16_scr[...]).astype(o_ref.dtype)


def _in_proj(h, w_in, layer, out_dtype):
    n, d = h.shape
    nc = w_in.shape[2]
    tm, tn = 1024, 1408
    tm = min(tm, n)
    return pl.pallas_call(
        _in_proj_kernel,
        grid=(nc // tn, n // tm),
        in_specs=[pl.BlockSpec((tm, d), lambda j, i: (i, 0)),
                  pl.BlockSpec((1, d, tn), lambda j, i: (layer, 0, j))],
        out_specs=pl.BlockSpec((tm, tn), lambda j, i: (i, j)),
        out_shape=jax.ShapeDtypeStruct((n, nc), out_dtype),
        scratch_shapes=[pltpu.VMEM((d, tn), BF16)],
        compiler_params=_cparams(("arbitrary", "arbitrary"), VMEM_LIMIT),
        name="in_proj",
    )(h, w_in)


def _conv_kernel(cb_ref, cc_ref, ch_ref, w_ref, beta_ref, o_ref):
    u = cc_ref[0].astype(F32) * ch_ref[0].astype(F32)
    t = u.shape[0]
    row = lax.broadcasted_iota(I32, u.shape, 0)
    prev = jnp.where(row == 0, 0.0, pltpu.roll(u, 1, 0))
    nxt = jnp.where(row == t - 1, 0.0, pltpu.roll(u, t - 1, 0))
    w = w_ref[...]
    y = cb_ref[0].astype(F32) * (prev * w[0:1] + u * w[1:2] + nxt * w[2:3])
    o_ref[0] = (y * beta_ref[...]).astype(o_ref.dtype)


def _conv_mixer(proj, conv_w, beta):
    bsz, t, _ = proj.shape
    cw = 256
    nb = CONV_WIDTH // cw
    per = LANES * 1

    def col(off):
        return pl.BlockSpec((1, t, cw), lambda b, j, off=off: (b, 0, off * per // cw + j))

    return pl.pallas_call(
        _conv_kernel,
        grid=(bsz, nb),
        in_specs=[col(_CB), col(_CC), col(_CH),
                  pl.BlockSpec((3, cw), lambda b, j: (0, j)),
                  pl.BlockSpec((1, cw), lambda b, j: (0, j))],
        out_specs=pl.BlockSpec((1, t, cw), lambda b, j: (b, 0, j)),
        out_shape=jax.ShapeDtypeStruct((bsz, t, CONV_WIDTH), BF16),
        compiler_params=_cparams(("arbitrary", "arbitrary"), VMEM_LIMIT),
        name="conv_mixer",
    )(proj, proj, proj, conv_w, beta)


def _ret_kernel(q_ref, k_ref, v_ref, g_ref, cos_ref, sin_ref, lg_ref, beta_ref,
                o_ref, o_scr, qf_scr, qb_scr, uf_scr, ub_scr):
    L = SEQ_CHUNK
    t = q_ref.shape[1]
    n = t // L
    lgf = lg_ref[0, 0:1, :]
    lgb = lg_ref[0, 1:2, :]
    r = lax.broadcasted_iota(I32, (L, L), 0).astype(F32)
    c = lax.broadcasted_iota(I32, (L, L), 1).astype(F32)
    rel = r - c
    dbi = jnp.where(rel > 0, jnp.exp(lgf * jnp.maximum(rel, 0.0)),
                    jnp.where(rel < 0, jnp.exp(lgb * jnp.maximum(-rel, 0.0)), 2.0))
    qf = jnp.exp(lgf * (r + 1.0))
    kf = jnp.exp(lgf * (L - 1.0 - r))
    qb = jnp.exp(lgb * (L - r))
    kb = jnp.exp(lgb * r)
    gf_l = jnp.exp(lgf * float(L))
    gb_l = jnp.exp(lgb * float(L))
    scale = HEAD_DIM ** -0.5

    def rot(x, cs, sn):
        return x * cs + pltpu.roll(x, HEAD_DIM // 2, 1) * sn

    def rows(ci):
        return pl.ds(pl.multiple_of(ci * L, L), L)

    def local_body(ci, carry):
        sl = rows(ci)
        cs = cos_ref[sl, :]
        sn = sin_ref[sl, :]
        q = rot(q_ref[0, sl, :].astype(F32), cs, sn)
        k = rot(k_ref[0, sl, :].astype(F32), cs, sn) * scale
        v16 = v_ref[0, sl, :].astype(BF16)
        s = _dot_nt(q.astype(BF16), k.astype(BF16)) * dbi
        o_scr[sl, :] = _dot(s.astype(BF16), v16)
        qf_scr[sl, :] = (q * qf).astype(BF16)
        qb_scr[sl, :] = (q * qb).astype(BF16)
        uf_scr[ci] = _dot_tn((k * kf).astype(BF16), v16)
        ub_scr[ci] = _dot_tn((k * kb).astype(BF16), v16)
        return carry

    lax.fori_loop(0, n, local_body, 0, unroll=4)

    def scan_body(i, states):
        sf, sb = states
        cf, cb = i, n - 1 - i
        slf, slb = rows(cf), rows(cb)
        o_scr[slf, :] += _dot(qf_scr[slf, :], sf.astype(BF16))
        o_scr[slb, :] += _dot(qb_scr[slb, :], sb.astype(BF16))
        return gf_l * sf + uf_scr[cf], gb_l * sb + ub_scr[cb]

    zero = jnp.zeros((HEAD_DIM, HEAD_DIM), F32)
    lax.fori_loop(0, n, scan_body, (zero, zero), unroll=8)

    beta = beta_ref[...]

    def norm_body(ci, carry):
        sl = rows(ci)
        o = o_scr[sl, :]
        mu = jnp.mean(o, axis=-1, keepdims=True)
        oc = o - mu
        var = jnp.mean(oc * oc, axis=-1, keepdims=True)
        y = oc * lax.rsqrt(var + HEAD_EPS) * _silu(g_ref[0, sl, :].astype(F32)) * beta
        o_ref[0, sl, :] = y.astype(o_ref.dtype)
        return carry

    lax.fori_loop(0, n, norm_body, 0, unroll=4)


def _rotary_tables(t):
    half = HEAD_DIM // 2
    inv_freq = ROPE_BASE ** (-jnp.arange(half, dtype=F32) / half)
    ang = jnp.arange(t, dtype=F32)[:, None] * inv_freq[None, :]
    cos, sin = jnp.cos(ang), jnp.sin(ang)
    return jnp.concatenate([cos, cos], -1), jnp.concatenate([-sin, sin], -1)


def _ret_log_decays():
    head = jnp.arange(RET_HEADS, dtype=F32)
    lg_f = jnp.log1p(-jnp.exp2(-5.0 - head))
    lg_b = jnp.log1p(-jnp.exp2(-5.5 - head))
    lg = jnp.stack([lg_f, lg_b], axis=1)
    return jnp.broadcast_to(lg[:, :, None], (RET_HEADS, 2, LANES))


def _ret_mixer(proj, beta):
    bsz, t, _ = proj.shape
    n_chunks = t // SEQ_CHUNK
    assert n_chunks % 2 == 0
    cosf, sinf = _rotary_tables(t)

    def col(off):
        return pl.BlockSpec((1, t, HEAD_DIM), lambda b, h, off=off: (b, 0, off + h))

    tab = pl.BlockSpec((t, HEAD_DIM), lambda b, h: (0, 0))
    return pl.pallas_call(
        _ret_kernel,
        grid=(bsz, RET_HEADS),
        in_specs=[col(_RQ), col(_RK), col(_RV), col(_RG), tab, tab,
                  pl.BlockSpec((1, 2, LANES), lambda b, h: (h, 0, 0)),
                  pl.BlockSpec((1, HEAD_DIM), lambda b, h: (0, _OUT_RET + h))],
        out_specs=pl.BlockSpec((1, t, HEAD_DIM), lambda b, h: (b, 0, h)),
        out_shape=jax.ShapeDtypeStruct((bsz, t, RET_WIDTH), BF16),
        scratch_shapes=[pltpu.VMEM((t, HEAD_DIM), F32),
                        pltpu.VMEM((t, HEAD_DIM), BF16), pltpu.VMEM((t, HEAD_DIM), BF16),
                        pltpu.VMEM((n_chunks, HEAD_DIM, HEAD_DIM), F32),
                        pltpu.VMEM((n_chunks, HEAD_DIM, HEAD_DIM), F32)],
        compiler_params=_cparams(("arbitrary", "arbitrary"), VMEM_LIMIT),
        name="ret_mixer",
    )(proj, proj, proj, proj, cosf, sinf, _ret_log_decays(), beta)


_HG_BASE = SUBLANES
_HG_LEVELS = tuple(m for m in (8, 16, 32, 64) if m < SEQ_CHUNK)


def _cumsum_rows(x, reverse):
    n = x.shape[0]
    row = lax.broadcasted_iota(I32, x.shape, 0)
    sh = 1
    while sh < n:
        if reverse:
            x = x + jnp.where(row < n - sh, pltpu.roll(x, n - sh, 0), 0.0)
        else:
            x = x + jnp.where(row >= sh, pltpu.roll(x, sh, 0), 0.0)
        sh *= 2
    return x


def _hg_masks(reverse):
    C = SEQ_CHUNK
    r = lax.broadcasted_iota(I32, (C, C), 0)
    c = lax.broadcasted_iota(I32, (C, C), 1)
    if reverse:
        r, c = c, r
    sh = _HG_BASE.bit_length() - 1
    masks = [((r >> sh) == (c >> sh)) & (c <= r)]
    for m in _HG_LEVELS:
        sh = m.bit_length() - 1
        masks.append(((r >> (sh + 1)) == (c >> (sh + 1)))
                     & (((r >> sh) & 1) == 1) & (((c >> sh) & 1) == 0))
    return [jnp.where(m, 1.0, 0.0) for m in masks]


def _hg_scores(q, kk, cum, mask_ref, reverse):
    C = SEQ_CHUNK
    d = 1 if reverse else 0

    def split(x, blk):
        return x.reshape(C // blk, blk, LANES)

    b = _HG_BASE
    ref_row = b // 2 if reverse else b // 2 - 1
    cum3 = split(cum, b)
    rel = cum3 - cum3[:, ref_row:ref_row + 1, :]
    qt = (split(q, b) * jnp.exp(rel)).reshape(C, LANES).astype(BF16)
    kt = (split(kk, b) * jnp.exp(-rel)).reshape(C, LANES).astype(BF16)
    a = jnp.where(mask_ref[d, 0] > 0.5, _dot_nt(qt, kt), 0.0)
    for li, m in enumerate(_HG_LEVELS):
        blk = 2 * m
        ref_row = m if reverse else m - 1
        cum3 = split(cum, blk)
        rel = cum3 - cum3[:, ref_row:ref_row + 1, :]
        dec = jnp.exp(-jnp.abs(rel)).reshape(C, LANES)
        halves = []
        for j in range(C // m):
            is_query = (j % 2 == 1) != reverse
            halves.append((q if is_query else kk)[j * m:(j + 1) * m])
        x = (jnp.concatenate(halves, axis=0) * dec).astype(BF16)
        a = a + mask_ref[d, li + 1] * _dot_nt(x, x)
    return a


def _hg_kernel(layer, q_ref, zf_ref, zb_ref, v_ref, g_ref, lbl_ref, beta_ref, o_ref,
               o_scr, qe_scr, u_scr, dec_scr, mask_scr):
    C = SEQ_CHUNK
    t = q_ref.shape[1]
    n = t // C
    for d, reverse in enumerate((False, True)):
        for li, m in enumerate(_hg_masks(reverse)):
            mask_scr[d, li] = m
    logits = lbl_ref[...].astype(F32)
    e = jnp.exp(logits - jnp.max(logits, axis=0, keepdims=True))
    p = e / jnp.sum(e, axis=0, keepdims=True)
    lb = p[0]
    for l in range(1, layer + 1):
        lb = lb + p[l]
    lb = lb - p[0]

    def gates(z_ref, sl, lb_row):
        f = lb_row + (1.0 - lb_row) * jax.nn.sigmoid(z_ref[0, sl, :].astype(F32))
        f = jnp.maximum(f, FORGET_FLOOR)
        return jnp.log(f), 1.0 - f

    def rows(ci):
        return pl.ds(pl.multiple_of(ci * C, C), C)

    def local_dir(ci, sl, q, v16, z_ref, d):
        reverse = d == 1
        logf, kk = gates(z_ref, sl, lb[d:d + 1, :])
        cum = _cumsum_rows(logf, reverse)
        a = _hg_scores(q, kk, cum, mask_scr, reverse)
        total = cum[0:1, :] if reverse else cum[C - 1:C, :]
        qe_scr[d, sl, :] = (q * jnp.exp(cum)).astype(BF16)
        k_tail = kk * jnp.exp(total - cum)
        u_scr[d, ci] = _dot_tn(v16, k_tail.astype(BF16))
        dec_scr[d, ci] = jnp.broadcast_to(jnp.exp(total), (SUBLANES, LANES))
        return _dot(a.astype(BF16), v16)

    def local_body(ci, carry):
        sl = rows(ci)
        q = q_ref[0, sl, :].astype(F32)
        v16 = v_ref[0, sl, :].astype(BF16)
        o_scr[sl, :] = (local_dir(ci, sl, q, v16, zf_ref, 0)
                        + local_dir(ci, sl, q, v16, zb_ref, 1))
        return carry

    lax.fori_loop(0, n, local_body, 0, unroll=2)

    def scan_body(i, states):
        sf, sb = states
        cf, cb = i, n - 1 - i
        slf, slb = rows(cf), rows(cb)
        o_scr[slf, :] += _dot_nt(qe_scr[0, slf, :], sf.astype(BF16))
        o_scr[slb, :] += _dot_nt(qe_scr[1, slb, :], sb.astype(BF16))
        return (sf * dec_scr[0, cf, 0:1, :] + u_scr[0, cf],
                sb * dec_scr[1, cb, 0:1, :] + u_scr[1, cb])

    zero = jnp.zeros((HEAD_DIM, HEAD_DIM), F32)
    lax.fori_loop(0, n, scan_body, (zero, zero), unroll=8)

    beta = beta_ref[...]

    def norm_body(ci, carry):
        sl = rows(ci)
        o = o_scr[sl, :]
        ms = jnp.mean(o * o, axis=-1, keepdims=True)
        y = o * lax.rsqrt(ms + HEAD_EPS) * _silu(g_ref[0, sl, :].astype(F32)) * beta
        o_ref[0, sl, :] = y.astype(o_ref.dtype)
        return carry

    lax.fori_loop(0, n, norm_body, 0, unroll=8)


def _hg_mixer(proj, lb_logits, beta, layer):
    bsz, t, _ = proj.shape
    n_chunks = t // SEQ_CHUNK
    assert n_chunks % 2 == 0

    def col(off):
        return pl.BlockSpec((1, t, HEAD_DIM), lambda b, h, off=off: (b, 0, off + h))

    return pl.pallas_call(
        functools.partial(_hg_kernel, layer),
        grid=(bsz, HG_HEADS),
        in_specs=[col(_HQ), col(_HFF), col(_HFB), col(_HI), col(_HG),
                  pl.BlockSpec((DEPTH, 2, HEAD_DIM), lambda b, h: (0, 0, h)),
                  pl.BlockSpec((1, HEAD_DIM), lambda b, h: (0, _OUT_HG + h))],
        out_specs=pl.BlockSpec((1, t, HEAD_DIM), lambda b, h: (b, 0, h)),
        out_shape=jax.ShapeDtypeStruct((bsz, t, HG_WIDTH), BF16),
        scratch_shapes=[pltpu.VMEM((t, HEAD_DIM), F32),
                        pltpu.VMEM((2, t, HEAD_DIM), BF16),
                        pltpu.VMEM((2, n_chunks, HEAD_DIM, HEAD_DIM), F32),
                        pltpu.VMEM((2, n_chunks, SUBLANES, LANES), F32),
                        pltpu.VMEM((2, 1 + len(_HG_LEVELS), SEQ_CHUNK, SEQ_CHUNK), F32)],
        compiler_params=_cparams(("arbitrary", "arbitrary"), VMEM_LIMIT),
        name="hg_mixer",
    )(proj, proj, proj, proj, proj, lb_logits, beta)


_OUT_PROJ_SPLIT = 2


def _out_proj_kernel(yc_ref, yr_ref, yh_ref, w_ref, x_ref, gate_ref, g_ref, b_ref,
                     sc_ref, sh_ref, wrh_ref, wrl_ref, xo_ref, h_ref, lo_ref):
    c0, c1 = CONV_WIDTH, CONV_WIDTH + RET_WIDTH
    tm = x_ref.shape[1]
    for half in range(_OUT_PROJ_SPLIT):
        sl = pl.ds(half * (tm // _OUT_PROJ_SPLIT), tm // _OUT_PROJ_SPLIT)
        y = (_dot(yc_ref[0, sl, :], w_ref[0, 0:c0, :]) + _dot(yr_ref[0, sl, :], w_ref[0, c0:c1, :])
             + _dot(yh_ref[0, sl, :], w_ref[0, c1:, :]))
        z = DEEPNORM_ALPHA * x_ref[0, sl, :] + (1.0 + gate_ref[0]) * y
        xn = _layer_norm_rows(z, g_ref[...], b_ref[...])
        xo_ref[0, sl, :] = xn
        h = xn * (1.0 + sc_ref[0]) + sh_ref[0]
        h_ref[0, sl, :] = h
        h_hi = h.astype(BF16)
        h_lo = (h - h_hi.astype(F32)).astype(BF16)
        wrh = wrh_ref[...]
        lo_ref[0, sl, :] = _dot(h_hi, wrh) + _dot(h_lo, wrh) + _dot(h_hi, wrl_ref[...])


def _out_proj(yc, yr, yh, w_out16, layer, x, gate, ln_g, ln_b, scale2, shift2, wr_hi, wr_lo):
    bsz, t, d = x.shape
    tm = 512

    def row(width):
        return pl.BlockSpec((1, tm, width), lambda b, i: (b, i, 0))

    vec = pl.BlockSpec((1, d), lambda b, i: (0, 0))
    mod = pl.BlockSpec((1, 1, d), lambda b, i: (b, 0, 0))
    wr = pl.BlockSpec((d, LANES), lambda b, i: (0, 0))
    return pl.pallas_call(
        _out_proj_kernel,
        grid=(bsz, t // tm),
        in_specs=[row(CONV_WIDTH), row(RET_WIDTH), row(HG_WIDTH),
                  pl.BlockSpec((1, d, d), lambda b, i: (layer, 0, 0)),
                  row(d), mod, vec, vec, mod, mod, wr, wr],
        out_specs=[row(d), row(d), row(LANES)],
        out_shape=[jax.ShapeDtypeStruct((bsz, t, d), F32),
                   jax.ShapeDtypeStruct((bsz, t, d), F32),
                   jax.ShapeDtypeStruct((bsz, t, LANES), F32)],
        compiler_params=_cparams(("arbitrary", "arbitrary"), VMEM_LIMIT),
        name="out_proj",
    )(yc, yr, yh, w_out16, x, gate, ln_g.reshape(1, d), ln_b.reshape(1, d),
      scale2, shift2, wr_hi, wr_lo)


def _route_kernel(n_tiles, lt_ref, bias_ref, pos1_ref, pos2_ref, w1_ref, w2_ref, tile_ref):
    E = N_EXPERTS
    logits = [lt_ref[e] for e in range(E)]
    shape = logits[0].shape
    mx = functools.reduce(jnp.maximum, logits)
    ex = [jnp.exp(l - mx) for l in logits]
    den = functools.reduce(lambda a, b: a + b, ex)
    scores = [x / den for x in ex]
    sel = [scores[e] + bias_ref[e] for e in range(E)]

    best_g = jnp.zeros(shape, I32)
    best_v = None
    for g in range(N_GROUPS):
        a, b, c, d = sel[EXPERTS_PER_GROUP * g: EXPERTS_PER_GROUP * (g + 1)]
        hi1, lo1 = jnp.maximum(a, b), jnp.minimum(a, b)
        hi2, lo2 = jnp.maximum(c, d), jnp.minimum(c, d)
        gs = jnp.maximum(hi1, hi2) + jnp.maximum(jnp.minimum(hi1, hi2), jnp.maximum(lo1, lo2))
        if g == 0:
            best_v = gs
        else:
            better = gs > best_v
            best_g = jnp.where(better, g, best_g)
            best_v = jnp.where(better, gs, best_v)

    masked = [jnp.where(best_g == (e // EXPERTS_PER_GROUP), sel[e], MASK_NEG) for e in range(E)]

    def arg_top(vals, exclude):
        idx = jnp.zeros(shape, I32)
        val = None
        for e in range(E):
            v = vals[e] if exclude is None else jnp.where(exclude == e, -jnp.inf, vals[e])
            if e == 0:
                val = v
            else:
                better = v > val
                idx = jnp.where(better, e, idx)
                val = jnp.where(better, v, val)
        return idx

    idx1 = arg_top(masked, None)
    idx2 = arg_top(masked, idx1)
    s1 = functools.reduce(lambda a, b: a + b, [jnp.where(idx1 == e, scores[e], 0.0) for e in range(E)])
    s2 = functools.reduce(lambda a, b: a + b, [jnp.where(idx2 == e, scores[e], 0.0) for e in range(E)])
    w1_ref[...] = s1 / (s1 + s2)
    w2_ref[...] = s2 / (s1 + s2)

    rows = shape[0]
    li = lax.broadcasted_iota(I32, (LANES, LANES), 0)
    lj = lax.broadcasted_iota(I32, (LANES, LANES), 1)
    upper = (li <= lj).astype(BF16)
    ri = lax.broadcasted_iota(I32, (rows, rows), 0)
    rj = lax.broadcasted_iota(I32, (rows, rows), 1)
    lower = (rj < ri).astype(BF16)
    tile_start = (lax.broadcasted_iota(I32, (1, LANES), 1) * MOE_TILE).astype(F32)
    start = jnp.zeros((1, 1), F32)
    pos1 = jnp.zeros(shape, F32)
    pos2 = jnp.zeros(shape, F32)
    tile_e = jnp.zeros((1, LANES), F32)
    for e in range(E):
        hit1 = idx1 == e
        hit2 = idx2 == e
        onehot = jnp.where(hit1 | hit2, 1.0, 0.0)
        pref = _dot(onehot.astype(BF16), upper)
        row_tot = pref[:, LANES - 1:LANES]
        row_off = _dot(lower, jnp.broadcast_to(row_tot, shape).astype(BF16))[:, 0:1]
        rank = pref - onehot + row_off
        count = jnp.sum(onehot, keepdims=True)
        dest = start + rank
        pos1 = jnp.where(hit1, dest, pos1)
        pos2 = jnp.where(hit2, dest, pos2)
        start = start + jnp.ceil(count / MOE_TILE) * MOE_TILE
        tile_e = tile_e + jnp.where(tile_start >= start, 1.0, 0.0)
    pos1_ref[...] = pos1.astype(I32)
    pos2_ref[...] = pos2.astype(I32)
    n_valid = start / MOE_TILE
    tile_id = lax.broadcasted_iota(I32, (1, LANES), 1).astype(F32)
    last_e = jnp.sum(jnp.where(tile_id == n_valid - 1.0, tile_e, 0.0), keepdims=True)
    tile_e = jnp.where(tile_id < n_valid, tile_e, last_e)
    sub = lax.broadcasted_iota(I32, (SUBLANES, LANES), 0)
    tile_ref[...] = jnp.where(sub == 0, tile_e, jnp.broadcast_to(n_valid, (SUBLANES, LANES))).astype(I32)


def _route(logits_t, router_bias, n_tiles):
    e, rows, lanes = logits_t.shape
    tok = jax.ShapeDtypeStruct((rows, lanes), I32)
    tokf = jax.ShapeDtypeStruct((rows, lanes), F32)
    full = pl.BlockSpec((rows, lanes), lambda i: (0, 0))
    return pl.pallas_call(
        functools.partial(_route_kernel, n_tiles),
        grid=(1,),
        in_specs=[pl.BlockSpec((e, rows, lanes), lambda i: (0, 0, 0)),
                  pl.BlockSpec(memory_space=pltpu.SMEM)],
        out_specs=[full, full, full, full, pl.BlockSpec((SUBLANES, LANES), lambda i: (0, 0))],
        out_shape=[tok, tok, tokf, tokf, jax.ShapeDtypeStruct((SUBLANES, LANES), I32)],
        compiler_params=_cparams(("arbitrary",)),
        name="route",
    )(logits_t, router_bias)


_ROWS_PER_STEP = 512


_DMA_UNROLL = 8


def _scatter_rows_kernel(n_tiles, p1_ref, p2_ref, fill_ref, src_ref, dst_ref, zero_scr, sem, zsem):
    def zero_copy(i):
        row0 = pl.multiple_of(i * MOE_TILE, MOE_TILE)
        return pltpu.make_async_copy(zero_scr, dst_ref.at[pl.ds(row0, MOE_TILE)], zsem)

    @pl.when(pl.program_id(0) == 0)
    def _():
        zero_scr[...] = jnp.zeros_like(zero_scr)

        def zstart(i, carry):
            @pl.when(fill_ref[i] == 1)
            def _():
                zero_copy(i).start()
            return carry

        def zwait(i, carry):
            @pl.when(fill_ref[i] == 1)
            def _():
                zero_copy(i).wait()
            return carry

        lax.fori_loop(0, n_tiles, zstart, 0)
        lax.fori_loop(0, n_tiles, zwait, 0)

    def copies(r):
        row = src_ref.at[pl.ds(r, 1)]
        return (pltpu.make_async_copy(row, dst_ref.at[pl.ds(p1_ref[0, 0, r], 1)], sem.at[0]),
                pltpu.make_async_copy(row, dst_ref.at[pl.ds(p2_ref[0, 0, r], 1)], sem.at[1]))

    def start(r, carry):
        for cp in copies(r):
            cp.start()
        return carry

    def wait(r, carry):
        for cp in copies(r):
            cp.wait()
        return carry

    lax.fori_loop(0, _ROWS_PER_STEP, start, 0, unroll=_DMA_UNROLL)
    lax.fori_loop(0, _ROWS_PER_STEP, wait, 0, unroll=_DMA_UNROLL)


def _scatter_rows(src, pos1, pos2, fill_tile, n_tiles):
    n, d = src.shape
    steps = n // _ROWS_PER_STEP
    idx = pl.BlockSpec((1, 1, _ROWS_PER_STEP), lambda i: (i, 0, 0), memory_space=pltpu.SMEM)
    return pl.pallas_call(
        functools.partial(_scatter_rows_kernel, n_tiles),
        grid=(steps,),
        in_specs=[idx, idx, pl.BlockSpec(memory_space=pltpu.SMEM),
                  pl.BlockSpec((_ROWS_PER_STEP, d), lambda i: (i, 0))],
        out_specs=pl.BlockSpec(memory_space=pl.ANY),
        out_shape=jax.ShapeDtypeStruct((n_tiles * MOE_TILE, d), src.dtype),
        scratch_shapes=[pltpu.VMEM((MOE_TILE, d), src.dtype),
                        pltpu.SemaphoreType.DMA((2,)), pltpu.SemaphoreType.DMA],
        compiler_params=_cparams(("arbitrary",), VMEM_LIMIT),
        name="scatter_rows",
    )(pos1.reshape(steps, 1, _ROWS_PER_STEP), pos2.reshape(steps, 1, _ROWS_PER_STEP), fill_tile, src)


_W_CHUNK = 128
_W_RING = 11


def _moe_kernel(layer, te_ref, nv_ref, first_ref, slot_ref, nxt_ref,
                x_ref, wg_hbm, wu_hbm, wd_hbm, y_ref,
                cg_scr, cu_scr, cd_scr, stage_scr, sem, cnt_ref):
    i = pl.program_id(0)
    d, f = cg_scr.shape[1], cg_scr.shape[2]
    n_g = d // _W_CHUNK
    n_chunks = 2 * n_g + f // _W_CHUNK

    def for_chunk(c, e, slot, fn):
        k = c % _W_RING

        @pl.when(c < n_g)
        def _():
            row = pl.multiple_of(c * _W_CHUNK, _W_CHUNK)
            fn(wg_hbm.at[layer, e, pl.ds(row, _W_CHUNK), :], stage_scr.at[k, :, pl.ds(0, f)],
               sem.at[k], cg_scr.at[slot, pl.ds(row, _W_CHUNK), :])

        @pl.when((c >= n_g) & (c < 2 * n_g))
        def _():
            row = pl.multiple_of((c - n_g) * _W_CHUNK, _W_CHUNK)
            fn(wu_hbm.at[layer, e, pl.ds(row, _W_CHUNK), :], stage_scr.at[k, :, pl.ds(0, f)],
               sem.at[k], cu_scr.at[slot, pl.ds(row, _W_CHUNK), :])

        @pl.when(c >= 2 * n_g)
        def _():
            row = pl.multiple_of((c - 2 * n_g) * _W_CHUNK, _W_CHUNK)
            fn(wd_hbm.at[layer, e, pl.ds(row, _W_CHUNK), :], stage_scr.at[k],
               sem.at[k], cd_scr.at[slot, pl.ds(row, _W_CHUNK), :])

    def start(src, stage, s, dst):
        pltpu.make_async_copy(src, stage, s).start()

    def finish(src, stage, s, dst):
        pltpu.make_async_copy(src, stage, s).wait()
        dst[...] = stage[...].astype(BF16)

    def start_upto(e, slot, hi):
        hi = jnp.minimum(hi, n_chunks)

        def body(c, carry):
            for_chunk(c, e, slot, start)
            return carry

        lax.fori_loop(cnt_ref[0], hi, body, 0)
        cnt_ref[0] = jnp.maximum(cnt_ref[0], hi)

    def finish_started(e, slot):
        def body(c, carry):
            for_chunk(c, e, slot, finish)
            return carry

        lax.fori_loop(cnt_ref[1], cnt_ref[0], body, 0)
        cnt_ref[1] = cnt_ref[0]

    valid = i < nv_ref[0]

    @pl.when(i == 0)
    def _():
        cnt_ref[0] = 0
        cnt_ref[1] = 0

    @pl.when(valid & (first_ref[i] == 1))
    def _():
        e, slot = te_ref[i], slot_ref[i]

        def body(c, carry):
            start_upto(e, slot, c + _W_RING)
            for_chunk(c, e, slot, finish)
            return carry

        lax.fori_loop(cnt_ref[1], n_chunks, body, 0)
        cnt_ref[0] = 0
        cnt_ref[1] = 0

    @pl.when(valid & (nxt_ref[i] >= 0))
    def _():
        e, slot = nxt_ref[i], 1 - slot_ref[i]
        finish_started(e, slot)
        start_upto(e, slot, cnt_ref[0] + _W_RING)

    @pl.when(valid)
    def _():
        slot = slot_ref[i]
        x = x_ref[...].astype(BF16)
        a = _dot(x, cg_scr[slot])
        b = _dot(x, cu_scr[slot])
        hidden = (_silu(a) * b).astype(BF16)
        y_ref[...] = _dot(hidden, cd_scr[slot])

    @pl.when(jnp.logical_not(valid))
    def _():
        y_ref[...] = jnp.zeros_like(y_ref)


def _moe_schedule(tile_e, n_valid):
    n_tiles = tile_e.shape[0]
    idx = jnp.arange(n_tiles, dtype=I32)
    valid = idx < n_valid[0]
    first = valid & ((idx == 0) | (tile_e != jnp.roll(tile_e, 1)))
    run = jnp.cumsum(first.astype(I32)) - 1
    next_first = lax.cummin(jnp.where(first, idx, n_tiles), reverse=True)
    after = jnp.concatenate([next_first[1:], jnp.full((1,), n_tiles, I32)])
    nxt = jnp.where(valid & (after < n_tiles), tile_e[jnp.minimum(after, n_tiles - 1)], -1)
    last = valid & ((after == idx + 1) | (idx == n_valid[0] - 1))
    fill = last | jnp.logical_not(valid)
    return first.astype(I32), (run % 2).astype(I32), nxt.astype(I32), fill.astype(I32)


def _moe_experts(xs, tile_e, n_valid, schedule, w_gate, w_up, w_down, layer):
    n_rows, d = xs.shape
    f = w_gate.shape[3]
    tm = MOE_TILE
    n_tiles = n_rows // tm
    assert d % _W_CHUNK == 0 and f % _W_CHUNK == 0
    first, slot, nxt = schedule

    def xmap(i, te, nv, *_):
        return (jnp.minimum(i, nv[0] - 1), 0)

    any_spec = pl.BlockSpec(memory_space=pl.ANY)
    grid_spec = pltpu.PrefetchScalarGridSpec(
        num_scalar_prefetch=5,
        grid=(n_tiles,),
        in_specs=[pl.BlockSpec((tm, d), xmap), any_spec, any_spec, any_spec],
        out_specs=pl.BlockSpec((tm, d), lambda i, *_: (i, 0)),
        scratch_shapes=[pltpu.VMEM((2, d, f), BF16), pltpu.VMEM((2, d, f), BF16),
                        pltpu.VMEM((2, f, d), BF16),
                        pltpu.VMEM((_W_RING, _W_CHUNK, d), F32),
                        pltpu.SemaphoreType.DMA((_W_RING,)),
                        pltpu.SMEM((2,), I32)],
    )
    return pl.pallas_call(
        functools.partial(_moe_kernel, layer),
        grid_spec=grid_spec,
        out_shape=jax.ShapeDtypeStruct((n_rows, d), F32),
        compiler_params=_cparams(("arbitrary",), MOE_VMEM_LIMIT),
        name="moe_experts",
    )(tile_e, n_valid, first, slot, nxt, xs, w_gate, w_up, w_down)


_COMBINE_ROWS = 256


def _combine_kernel(with_next, steps, p1_ref, p2_ref, p1n_ref, p2n_ref, ys_ref, w1_ref, w2_ref,
                    x_ref, gate_ref, g_ref, b_ref, *rest):
    if with_next:
        sc_ref, sh_ref, xo_ref, h_ref, ybuf, sem = rest
    else:
        xo_ref, ybuf, sem = rest
    step = pl.program_id(0)
    slot = step % 2

    def copies(pa, pb, s, r):
        return (pltpu.make_async_copy(ys_ref.at[pl.ds(pa[0, 0, r], 1)],
                                      ybuf.at[s, 0, pl.ds(r, 1)], sem.at[s, 0]),
                pltpu.make_async_copy(ys_ref.at[pl.ds(pb[0, 0, r], 1)],
                                      ybuf.at[s, 1, pl.ds(r, 1)], sem.at[s, 1]))

    def start_all(pa, pb, s):
        def body(r, carry):
            for cp in copies(pa, pb, s, r):
                cp.start()
            return carry
        lax.fori_loop(0, _COMBINE_ROWS, body, 0, unroll=_DMA_UNROLL)

    def wait_all(pa, pb, s):
        def body(r, carry):
            for cp in copies(pa, pb, s, r):
                cp.wait()
            return carry
        lax.fori_loop(0, _COMBINE_ROWS, body, 0, unroll=_DMA_UNROLL)

    @pl.when(step == 0)
    def _():
        start_all(p1_ref, p2_ref, 0)

    @pl.when(step + 1 < steps)
    def _():
        start_all(p1n_ref, p2n_ref, 1 - slot)

    wait_all(p1_ref, p2_ref, slot)
    y = w1_ref[0] * ybuf[slot, 0] + w2_ref[0] * ybuf[slot, 1]
    z = DEEPNORM_ALPHA * x_ref[0] + (1.0 + gate_ref[0]) * y
    xn = _layer_norm_rows(z, g_ref[...], b_ref[...])
    xo_ref[0] = xn
    if with_next:
        h_ref[0] = (xn * (1.0 + sc_ref[0]) + sh_ref[0]).astype(BF16)


def _combine(ys, pos1, pos2, w1, w2, x, gate, ln_g, ln_b, next_mod):
    bsz, t, d = x.shape
    tr = _COMBINE_ROWS
    per_b = t // tr
    steps = bsz * per_b
    row = pl.BlockSpec((1, tr, d), lambda s: (s // per_b, s % per_b, 0))
    vec = pl.BlockSpec((1, d), lambda s: (0, 0))
    mod = pl.BlockSpec((1, 1, d), lambda s: (s // per_b, 0, 0))
    wcol = pl.BlockSpec((1, tr, 1), lambda s: (s // per_b, s % per_b, 0))
    idx = pl.BlockSpec((1, 1, tr), lambda s: (s, 0, 0), memory_space=pltpu.SMEM)
    idx_next = pl.BlockSpec((1, 1, tr), lambda s: (jnp.minimum(s + 1, steps - 1), 0, 0),
                            memory_space=pltpu.SMEM)
    with_next = next_mod is not None
    p1 = pos1.reshape(steps, 1, tr)
    p2 = pos2.reshape(steps, 1, tr)
    in_specs = [idx, idx, idx_next, idx_next, pl.BlockSpec(memory_space=pl.ANY),
                wcol, wcol, row, mod, vec, vec]
    args = [p1, p2, p1, p2, ys, w1.reshape(bsz, t, 1), w2.reshape(bsz, t, 1), x, gate,
            ln_g.reshape(1, d), ln_b.reshape(1, d)]
    out_specs = [row]
    out_shape = [jax.ShapeDtypeStruct((bsz, t, d), F32)]
    if with_next:
        in_specs += [mod, mod]
        args += list(next_mod)
        out_specs.append(row)
        out_shape.append(jax.ShapeDtypeStruct((bsz, t, d), BF16))
    res = pl.pallas_call(
        functools.partial(_combine_kernel, with_next, steps),
        grid=(steps,),
        in_specs=in_specs,
        out_specs=out_specs,
        out_shape=out_shape,
        scratch_shapes=[pltpu.VMEM((2, 2, tr, d), F32), pltpu.SemaphoreType.DMA((2, 2))],
        compiler_params=_cparams(("arbitrary",), VMEM_LIMIT),
        name="combine_ln",
    )(*args)
    return res if with_next else (res[0], None)


def kernel(x, c, emb_ln_g, emb_ln_b, w_ada, b_ada, w_in, conv_w, mix_beta, w_out, hg_lb_logits,
           ln_g, ln_b, w_router, router_bias, w_gate, w_up, w_down):
    bsz, t, d = x.shape
    n = bsz * t
    assert n % LANES == 0 and t % SEQ_CHUNK == 0
    n_tiles = 2 * n // MOE_TILE + N_EXPERTS
    assert n_tiles <= LANES

    mod = _ada_mod(c, w_ada, b_ada)
    mod = mod.reshape(DEPTH, bsz, 6, 1, d)

    def mods(l):
        return [mod[l, :, i] for i in range(6)]

    wr_hi = jnp.pad(w_router, ((0, 0), (0, LANES - N_EXPERTS)))
    wr_hi16 = wr_hi.astype(BF16)
    wr_lo16 = (wr_hi - wr_hi16.astype(F32)).astype(BF16)

    w_out16 = w_out.astype(BF16)

    m = [mods(l) for l in range(DEPTH)]
    xcur, h = _ln_mod(x, emb_ln_g, emb_ln_b, m[0][1], m[0][0])
    for l in range(DEPTH):
        _, _, gate1, shift2, scale2, gate2 = m[l]
        beta = mix_beta[l].reshape(1, -1)
        proj = _in_proj(h.reshape(n, d), w_in, l, BF16).reshape(bsz, t, IN_COLS)
        yc = _conv_mixer(proj, conv_w[l], beta)
        yr = _ret_mixer(proj, beta)
        yh = _hg_mixer(proj, hg_lb_logits, beta, l)
        x1, h2, logits = _out_proj(yc, yr, yh, w_out16, l, xcur, gate1,
                                   ln_g[l, 0], ln_b[l, 0], scale2, shift2, wr_hi16, wr_lo16)
        logits_t = logits.reshape(n, LANES)[:, :N_EXPERTS].T.reshape(N_EXPERTS, n // LANES, LANES)
        pos1, pos2, w1, w2, tiles = _route(logits_t, router_bias, n_tiles)
        tile_e, n_valid = tiles[0, :n_tiles], tiles[1, :1]
        first, slot, nxt, fill = _moe_schedule(tile_e, n_valid)
        xs = _scatter_rows(h2.reshape(n, d), pos1.reshape(n), pos2.reshape(n), fill, n_tiles)
        ys = _moe_experts(xs, tile_e, n_valid, (first, slot, nxt), w_gate, w_up, w_down, l)
        next_mod = (m[l + 1][1], m[l + 1][0]) if l + 1 < DEPTH else None
        xcur, h = _combine(ys, pos1.reshape(n), pos2.reshape(n), w1, w2, x1, gate2,
                           ln_g[l, 1], ln_b[l, 1], next_mod)
    return xcur
```

```python
import functools
import math

import jax
import jax.numpy as jnp
from jax import lax
from jax.experimental import pallas as pl
from jax.experimental.pallas import tpu as pltpu

F32 = jnp.float32
BF16 = jnp.bfloat16
I32 = jnp.int32

DEPTH = 2
CONV_WIDTH = 512
RET_WIDTH = 768
HG_WIDTH = 768
HEAD_DIM = 128
RET_HEADS = RET_WIDTH // HEAD_DIM
HG_HEADS = HG_WIDTH // HEAD_DIM
IN_COLS = 3 * CONV_WIDTH + 4 * RET_WIDTH + 5 * HG_WIDTH
ROPE_BASE = 10000.0
N_EXPERTS = 16
N_GROUPS = 4
EXPERTS_PER_GROUP = N_EXPERTS // N_GROUPS
MASK_NEG = -1e9
DEEPNORM_ALPHA = (2.0 * DEPTH) ** 0.25
LN_EPS = 1e-5
HEAD_EPS = 1e-6
FORGET_FLOOR = 1e-6

LANES = 128
SUBLANES = 8

_CB, _CC, _CH = 0, 4, 8
_RQ, _RK, _RV, _RG = 12, 18, 24, 30
_HQ, _HFF, _HFB, _HI, _HG = 36, 42, 48, 54, 60
_OUT_RET, _OUT_HG = 4, 10

SEQ_CHUNK = 128
MOE_TILE = 256
VMEM_LIMIT = 56 * 1024 * 1024
MOE_VMEM_LIMIT = 60 * 1024 * 1024


def _cparams(sem, vmem=None):
    return pltpu.CompilerParams(dimension_semantics=sem, vmem_limit_bytes=vmem)


def _silu(x):
    return x * jax.nn.sigmoid(x)


def _dot(a, b):
    return jnp.dot(a, b, preferred_element_type=F32)


def _dot_nt(a, b):
    return lax.dot_general(a, b, (((1,), (1,)), ((), ())), preferred_element_type=F32)


def _dot_tn(a, b):
    return lax.dot_general(a, b, (((0,), (0,)), ((), ())), preferred_element_type=F32)


def _ada_kernel(c_ref, w_ref, b_ref, o_ref):
    cond = _silu(c_ref[...])
    o_ref[0] = _dot(cond.astype(BF16), w_ref[0].astype(BF16)) + b_ref[0]


def _ada_mod(c, w_ada, b_ada):
    depth, d, n6 = w_ada.shape
    b = c.shape[0]
    bp = -(-b // SUBLANES) * SUBLANES
    cp = jnp.pad(c, ((0, bp - b), (0, 0)))
    tn = 512
    out = pl.pallas_call(
        _ada_kernel,
        grid=(depth, n6 // tn),
        in_specs=[
            pl.BlockSpec((bp, d), lambda l, j: (0, 0)),
            pl.BlockSpec((1, d, tn), lambda l, j: (l, 0, j)),
            pl.BlockSpec((1, 1, tn), lambda l, j: (l, 0, j)),
        ],
        out_specs=pl.BlockSpec((1, bp, tn), lambda l, j: (l, 0, j)),
        out_shape=jax.ShapeDtypeStruct((depth, bp, n6), F32),
        compiler_params=_cparams(("arbitrary", "arbitrary")),
        name="ada_mod",
    )(cp, w_ada, b_ada.reshape(depth, 1, n6))
    return out[:, :b, :]


def _layer_norm_rows(z, g, b):
    mu = jnp.mean(z, axis=-1, keepdims=True)
    zc = z - mu
    var = jnp.mean(zc * zc, axis=-1, keepdims=True)
    return zc * lax.rsqrt(var + LN_EPS) * g + b


def _ln_mod_kernel(x_ref, g_ref, b_ref, sc_ref, sh_ref, xo_ref, h_ref):
    y = _layer_norm_rows(x_ref[0], g_ref[...], b_ref[...])
    xo_ref[0] = y
    h_ref[0] = (y * (1.0 + sc_ref[0]) + sh_ref[0]).astype(BF16)


def _ln_mod(x, g, b, scale, shift):
    bsz, t, d = x.shape
    tr = 512
    row = pl.BlockSpec((1, tr, d), lambda i, j: (i, j, 0))
    vec = pl.BlockSpec((1, d), lambda i, j: (0, 0))
    mod = pl.BlockSpec((1, 1, d), lambda i, j: (i, 0, 0))
    return pl.pallas_call(
        _ln_mod_kernel,
        grid=(bsz, t // tr),
        in_specs=[row, vec, vec, mod, mod],
        out_specs=[row, row],
        out_shape=[jax.ShapeDtypeStruct((bsz, t, d), F32),
                   jax.ShapeDtypeStruct((bsz, t, d), BF16)],
        compiler_params=_cparams(("arbitrary", "arbitrary")),
        name="ln_mod",
    )(x, g.reshape(1, d), b.reshape(1, d), scale, shift)


def _in_proj_kernel(a_ref, w_ref, o_ref, w16_scr):
    @pl.when(pl.program_id(1) == 0)
    def _():
        w16_scr[...] = w_ref[0].astype(BF16)

    o_ref[...] = _dot(a_ref[...], w16_scr[...]).astype(o_ref.dtype)


def _in_proj(h, w_in, layer, out_dtype):
    n, d = h.shape
    nc = w_in.shape[2]
    tm, tn = 1024, 1408
    tm = min(tm, n)
    return pl.pallas_call(
        _in_proj_kernel,
        grid=(nc // tn, n // tm),
        in_specs=[pl.BlockSpec((tm, d), lambda j, i: (i, 0)),
                  pl.BlockSpec((1, d, tn), lambda j, i: (layer, 0, j))],
        out_specs=pl.BlockSpec((tm, tn), lambda j, i: (i, j)),
        out_shape=jax.ShapeDtypeStruct((n, nc), out_dtype),
        scratch_shapes=[pltpu.VMEM((d, tn), BF16)],
        compiler_params=_cparams(("arbitrary", "arbitrary"), VMEM_LIMIT),
        name="in_proj",
    )(h, w_in)


def _conv_kernel(cb_ref, cc_ref, ch_ref, w_ref, beta_ref, o_ref):
    u = cc_ref[0].astype(F32) * ch_ref[0].astype(F32)
    t = u.shape[0]
    row = lax.broadcasted_iota(I32, u.shape, 0)
    prev = jnp.where(row == 0, 0.0, pltpu.roll(u, 1, 0))
    nxt = jnp.where(row == t - 1, 0.0, pltpu.roll(u, t - 1, 0))
    w = w_ref[...]
    y = cb_ref[0].astype(F32) * (prev * w[0:1] + u * w[1:2] + nxt * w[2:3])
    o_ref[0] = (y * beta_ref[...]).astype(o_ref.dtype)


def _conv_mixer(proj, conv_w, beta):
    bsz, t, _ = proj.shape
    cw = 256
    nb = CONV_WIDTH // cw
    per = LANES * 1

    def col(off):
        return pl.BlockSpec((1, t, cw), lambda b, j, off=off: (b, 0, off * per // cw + j))

    return pl.pallas_call(
        _conv_kernel,
        grid=(bsz, nb),
        in_specs=[col(_CB), col(_CC), col(_CH),
                  pl.BlockSpec((3, cw), lambda b, j: (0, j)),
                  pl.BlockSpec((1, cw), lambda b, j: (0, j))],
        out_specs=pl.BlockSpec((1, t, cw), lambda b, j: (b, 0, j)),
        out_shape=jax.ShapeDtypeStruct((bsz, t, CONV_WIDTH), BF16),
        compiler_params=_cparams(("arbitrary", "arbitrary"), VMEM_LIMIT),
        name="conv_mixer",
    )(proj, proj, proj, conv_w, beta)


def _ret_kernel(q_ref, k_ref, v_ref, g_ref, cos_ref, sin_ref, lg_ref, beta_ref,
                o_ref, o_scr, qf_scr, qb_scr, uf_scr, ub_scr):
    L = SEQ_CHUNK
    t = q_ref.shape[1]
    n = t // L
    lgf = lg_ref[0, 0:1, :]
    lgb = lg_ref[0, 1:2, :]
    r = lax.broadcasted_iota(I32, (L, L), 0).astype(F32)
    c = lax.broadcasted_iota(I32, (L, L), 1).astype(F32)
    rel = r - c
    dbi = jnp.where(rel > 0, jnp.exp(lgf * jnp.maximum(rel, 0.0)),
                    jnp.where(rel < 0, jnp.exp(lgb * jnp.maximum(-rel, 0.0)), 2.0))
    qf = jnp.exp(lgf * (r + 1.0))
    kf = jnp.exp(lgf * (L - 1.0 - r))
    qb = jnp.exp(lgb * (L - r))
    kb = jnp.exp(lgb * r)
    gf_l = jnp.exp(lgf * float(L))
    gb_l = jnp.exp(lgb * float(L))
    scale = HEAD_DIM ** -0.5

    def rot(x, cs, sn):
        return x * cs + pltpu.roll(x, HEAD_DIM // 2, 1) * sn

    def rows(ci):
        return pl.ds(pl.multiple_of(ci * L, L), L)

    def local_body(ci, carry):
        sl = rows(ci)
        cs = cos_ref[sl, :]
        sn = sin_ref[sl, :]
        q = rot(q_ref[0, sl, :].astype(F32), cs, sn)
        k = rot(k_ref[0, sl, :].astype(F32), cs, sn) * scale
        v16 = v_ref[0, sl, :].astype(BF16)
        s = _dot_nt(q.astype(BF16), k.astype(BF16)) * dbi
        o_scr[sl, :] = _dot(s.astype(BF16), v16)
        qf_scr[sl, :] = (q * qf).astype(BF16)
        qb_scr[sl, :] = (q * qb).astype(BF16)
        uf_scr[ci] = _dot_tn((k * kf).astype(BF16), v16)
        ub_scr[ci] = _dot_tn((k * kb).astype(BF16), v16)
        return carry

    lax.fori_loop(0, n, local_body, 0, unroll=4)

    def scan_body(i, states):
        sf, sb = states
        cf, cb = i, n - 1 - i
        slf, slb = rows(cf), rows(cb)
        o_scr[slf, :] += _dot(qf_scr[slf, :], sf.astype(BF16))
        o_scr[slb, :] += _dot(qb_scr[slb, :], sb.astype(BF16))
        return gf_l * sf + uf_scr[cf], gb_l * sb + ub_scr[cb]

    zero = jnp.zeros((HEAD_DIM, HEAD_DIM), F32)
    lax.fori_loop(0, n, scan_body, (zero, zero), unroll=8)

    beta = beta_ref[...]

    def norm_body(ci, carry):
        sl = rows(ci)
        o = o_scr[sl, :]
        mu = jnp.mean(o, axis=-1, keepdims=True)
        oc = o - mu
        var = jnp.mean(oc * oc, axis=-1, keepdims=True)
        y = oc * lax.rsqrt(var + HEAD_EPS) * _silu(g_ref[0, sl, :].astype(F32)) * beta
        o_ref[0, sl, :] = y.astype(o_ref.dtype)
        return carry

    lax.fori_loop(0, n, norm_body, 0, unroll=4)


def _rotary_tables(t):
    half = HEAD_DIM // 2
    inv_freq = ROPE_BASE ** (-jnp.arange(half, dtype=F32) / half)
    ang = jnp.arange(t, dtype=F32)[:, None] * inv_freq[None, :]
    cos, sin = jnp.cos(ang), jnp.sin(ang)
    return jnp.concatenate([cos, cos], -1), jnp.concatenate([-sin, sin], -1)


def _ret_log_decays():
    head = jnp.arange(RET_HEADS, dtype=F32)
    lg_f = jnp.log1p(-jnp.exp2(-5.0 - head))
    lg_b = jnp.log1p(-jnp.exp2(-5.5 - head))
    lg = jnp.stack([lg_f, lg_b], axis=1)
    return jnp.broadcast_to(lg[:, :, None], (RET_HEADS, 2, LANES))


def _ret_mixer(proj, beta):
    bsz, t, _ = proj.shape
    n_chunks = t // SEQ_CHUNK
    assert n_chunks % 2 == 0
    cosf, sinf = _rotary_tables(t)

    def col(off):
        return pl.BlockSpec((1, t, HEAD_DIM), lambda b, h, off=off: (b, 0, off + h))

    tab = pl.BlockSpec((t, HEAD_DIM), lambda b, h: (0, 0))
    return pl.pallas_call(
        _ret_kernel,
        grid=(bsz, RET_HEADS),
        in_specs=[col(_RQ), col(_RK), col(_RV), col(_RG), tab, tab,
                  pl.BlockSpec((1, 2, LANES), lambda b, h: (h, 0, 0)),
                  pl.BlockSpec((1, HEAD_DIM), lambda b, h: (0, _OUT_RET + h))],
        out_specs=pl.BlockSpec((1, t, HEAD_DIM), lambda b, h: (b, 0, h)),
        out_shape=jax.ShapeDtypeStruct((bsz, t, RET_WIDTH), BF16),
        scratch_shapes=[pltpu.VMEM((t, HEAD_DIM), F32),
                        pltpu.VMEM((t, HEAD_DIM), BF16), pltpu.VMEM((t, HEAD_DIM), BF16),
                        pltpu.VMEM((n_chunks, HEAD_DIM, HEAD_DIM), F32),
                        pltpu.VMEM((n_chunks, HEAD_DIM, HEAD_DIM), F32)],
        compiler_params=_cparams(("arbitrary", "arbitrary"), VMEM_LIMIT),
        name="ret_mixer",
    )(proj, proj, proj, proj, cosf, sinf, _ret_log_decays(), beta)


_HG_BASE = SUBLANES
_HG_LEVELS = tuple(m for m in (8, 16, 32, 64) if m < SEQ_CHUNK)


def _cumsum_rows(x, reverse):
    n = x.shape[0]
    row = lax.broadcasted_iota(I32, x.shape, 0)
    sh = 1
    while sh < n:
        if reverse:
            x = x + jnp.where(row < n - sh, pltpu.roll(x, n - sh, 0), 0.0)
        else:
            x = x + jnp.where(row >= sh, pltpu.roll(x, sh, 0), 0.0)
        sh *= 2
    return x


def _hg_masks(reverse):
    C = SEQ_CHUNK
    r = lax.broadcasted_iota(I32, (C, C), 0)
    c = lax.broadcasted_iota(I32, (C, C), 1)
    if reverse:
        r, c = c, r
    sh = _HG_BASE.bit_length() - 1
    masks = [((r >> sh) == (c >> sh)) & (c <= r)]
    for m in _HG_LEVELS:
        sh = m.bit_length() - 1
        masks.append(((r >> (sh + 1)) == (c >> (sh + 1)))
                     & (((r >> sh) & 1) == 1) & (((c >> sh) & 1) == 0))
    return [jnp.where(m, 1.0, 0.0) for m in masks]


def _hg_scores(q, kk, cum, mask_ref, reverse):
    C = SEQ_CHUNK
    d = 1 if reverse else 0

    def split(x, blk):
        return x.reshape(C // blk, blk, LANES)

    b = _HG_BASE
    ref_row = b // 2 if reverse else b // 2 - 1
    cum3 = split(cum, b)
    rel = cum3 - cum3[:, ref_row:ref_row + 1, :]
    qt = (split(q, b) * jnp.exp(rel)).reshape(C, LANES).astype(BF16)
    kt = (split(kk, b) * jnp.exp(-rel)).reshape(C, LANES).astype(BF16)
    a = jnp.where(mask_ref[d, 0] > 0.5, _dot_nt(qt, kt), 0.0)
    for li, m in enumerate(_HG_LEVELS):
        blk = 2 * m
        ref_row = m if reverse else m - 1
        cum3 = split(cum, blk)
        rel = cum3 - cum3[:, ref_row:ref_row + 1, :]
        dec = jnp.exp(-jnp.abs(rel)).reshape(C, LANES)
        halves = []
        for j in range(C // m):
            is_query = (j % 2 == 1) != reverse
            halves.append((q if is_query else kk)[j * m:(j + 1) * m])
        x = (jnp.concatenate(halves, axis=0) * dec).astype(BF16)
        a = a + mask_ref[d, li + 1] * _dot_nt(x, x)
    return a


def _hg_kernel(layer, q_ref, zf_ref, zb_ref, v_ref, g_ref, lbl_ref, beta_ref, o_ref,
               o_scr, qe_scr, u_scr, dec_scr, mask_scr):
    C = SEQ_CHUNK
    t = q_ref.shape[1]
    n = t // C
    for d, reverse in enumerate((False, True)):
        for li, m in enumerate(_hg_masks(reverse)):
            mask_scr[d, li] = m
    logits = lbl_ref[...].astype(F32)
    e = jnp.exp(logits - jnp.max(logits, axis=0, keepdims=True))
    p = e / jnp.sum(e, axis=0, keepdims=True)
    lb = p[0]
    for l in range(1, layer + 1):
        lb = lb + p[l]
    lb = lb - p[0]

    def gates(z_ref, sl, lb_row):
        f = lb_row + (1.0 - lb_row) * jax.nn.sigmoid(z_ref[0, sl, :].astype(F32))
        f = jnp.maximum(f, FORGET_FLOOR)
        return jnp.log(f), 1.0 - f

    def rows(ci):
        return pl.ds(pl.multiple_of(ci * C, C), C)

    def local_dir(ci, sl, q, v16, z_ref, d):
        reverse = d == 1
        logf, kk = gates(z_ref, sl, lb[d:d + 1, :])
        cum = _cumsum_rows(logf, reverse)
        a = _hg_scores(q, kk, cum, mask_scr, reverse)
        total = cum[0:1, :] if reverse else cum[C - 1:C, :]
        qe_scr[d, sl, :] = (q * jnp.exp(cum)).astype(BF16)
        k_tail = kk * jnp.exp(total - cum)
        u_scr[d, ci] = _dot_tn(v16, k_tail.astype(BF16))
        dec_scr[d, ci] = jnp.broadcast_to(jnp.exp(total), (SUBLANES, LANES))
        return _dot(a.astype(BF16), v16)

    def local_body(ci, carry):
        sl = rows(ci)
        q = q_ref[0, sl, :].astype(F32)
        v16 = v_ref[0, sl, :].astype(BF16)
        o_scr[sl, :] = (local_dir(ci, sl, q, v16, zf_ref, 0)
                        + local_dir(ci, sl, q, v16, zb_ref, 1))
        return carry

    lax.fori_loop(0, n, local_body, 0, unroll=2)

    def scan_body(i, states):
        sf, sb = states
        cf, cb = i, n - 1 - i
        slf, slb = rows(cf), rows(cb)
        o_scr[slf, :] += _dot_nt(qe_scr[0, slf, :], sf.astype(BF16))
        o_scr[slb, :] += _dot_nt(qe_scr[1, slb, :], sb.astype(BF16))
        return (sf * dec_scr[0, cf, 0:1, :] + u_scr[0, cf],
                sb * dec_scr[1, cb, 0:1, :] + u_scr[1, cb])

    zero = jnp.zeros((HEAD_DIM, HEAD_DIM), F32)
    lax.fori_loop(0, n, scan_body, (zero, zero), unroll=8)

    beta = beta_ref[...]

    def norm_body(ci, carry):
        sl = rows(ci)
        o = o_scr[sl, :]
        ms = jnp.mean(o * o, axis=-1, keepdims=True)
        y = o * lax.rsqrt(ms + HEAD_EPS) * _silu(g_ref[0, sl, :].astype(F32)) * beta
        o_ref[0, sl, :] = y.astype(o_ref.dtype)
        return carry

    lax.fori_loop(0, n, norm_body, 0, unroll=8)


def _hg_mixer(proj, lb_logits, beta, layer):
    bsz, t, _ = proj.shape
    n_chunks = t // SEQ_CHUNK
    assert n_chunks % 2 == 0

    def col(off):
        return pl.BlockSpec((1, t, HEAD_DIM), lambda b, h, off=off: (b, 0, off + h))

    return pl.pallas_call(
        functools.partial(_hg_kernel, layer),
        grid=(bsz, HG_HEADS),
        in_specs=[col(_HQ), col(_HFF), col(_HFB), col(_HI), col(_HG),
                  pl.BlockSpec((DEPTH, 2, HEAD_DIM), lambda b, h: (0, 0, h)),
                  pl.BlockSpec((1, HEAD_DIM), lambda b, h: (0, _OUT_HG + h))],
        out_specs=pl.BlockSpec((1, t, HEAD_DIM), lambda b, h: (b, 0, h)),
        out_shape=jax.ShapeDtypeStruct((bsz, t, HG_WIDTH), BF16),
        scratch_shapes=[pltpu.VMEM((t, HEAD_DIM), F32),
                        pltpu.VMEM((2, t, HEAD_DIM), BF16),
                        pltpu.VMEM((2, n_chunks, HEAD_DIM, HEAD_DIM), F32),
                        pltpu.VMEM((2, n_chunks, SUBLANES, LANES), F32),
                        pltpu.VMEM((2, 1 + len(_HG_LEVELS), SEQ_CHUNK, SEQ_CHUNK), F32)],
        compiler_params=_cparams(("arbitrary", "arbitrary"), VMEM_LIMIT),
        name="hg_mixer",
    )(proj, proj, proj, proj, proj, lb_logits, beta)


_OUT_PROJ_SPLIT = 2


def _out_proj_kernel(yc_ref, yr_ref, yh_ref, w_ref, x_ref, gate_ref, g_ref, b_ref,
                     sc_ref, sh_ref, wrh_ref, wrl_ref, xo_ref, h_ref, lo_ref):
    c0, c1 = CONV_WIDTH, CONV_WIDTH + RET_WIDTH
    tm = x_ref.shape[1]
    for half in range(_OUT_PROJ_SPLIT):
        sl = pl.ds(half * (tm // _OUT_PROJ_SPLIT), tm // _OUT_PROJ_SPLIT)
        y = (_dot(yc_ref[0, sl, :], w_ref[0, 0:c0, :]) + _dot(yr_ref[0, sl, :], w_ref[0, c0:c1, :])
             + _dot(yh_ref[0, sl, :], w_ref[0, c1:, :]))
        z = DEEPNORM_ALPHA * x_ref[0, sl, :] + (1.0 + gate_ref[0]) * y
        xn = _layer_norm_rows(z, g_ref[...], b_ref[...])
        xo_ref[0, sl, :] = xn
        h = xn * (1.0 + sc_ref[0]) + sh_ref[0]
        h_ref[0, sl, :] = h
        h_hi = h.astype(BF16)
        h_lo = (h - h_hi.astype(F32)).astype(BF16)
        wrh = wrh_ref[...]
        lo_ref[0, sl, :] = _dot(h_hi, wrh) + _dot(h_lo, wrh) + _dot(h_hi, wrl_ref[...])


def _out_proj(yc, yr, yh, w_out16, layer, x, gate, ln_g, ln_b, scale2, shift2, wr_hi, wr_lo):
    bsz, t, d = x.shape
    tm = 512

    def row(width):
        return pl.BlockSpec((1, tm, width), lambda b, i: (b, i, 0))

    vec = pl.BlockSpec((1, d), lambda b, i: (0, 0))
    mod = pl.BlockSpec((1, 1, d), lambda b, i: (b, 0, 0))
    wr = pl.BlockSpec((d, LANES), lambda b, i: (0, 0))
    return pl.pallas_call(
        _out_proj_kernel,
        grid=(bsz, t // tm),
        in_specs=[row(CONV_WIDTH), row(RET_WIDTH), row(HG_WIDTH),
                  pl.BlockSpec((1, d, d), lambda b, i: (layer, 0, 0)),
                  row(d), mod, vec, vec, mod, mod, wr, wr],
        out_specs=[row(d), row(d), row(LANES)],
        out_shape=[jax.ShapeDtypeStruct((bsz, t, d), F32),
                   jax.ShapeDtypeStruct((bsz, t, d), F32),
                   jax.ShapeDtypeStruct((bsz, t, LANES), F32)],
        compiler_params=_cparams(("arbitrary", "arbitrary"), VMEM_LIMIT),
        name="out_proj",
    )(yc, yr, yh, w_out16, x, gate, ln_g.reshape(1, d), ln_b.reshape(1, d),
      scale2, shift2, wr_hi, wr_lo)


def _route_kernel(n_tiles, lt_ref, bias_ref, pos1_ref, pos2_ref, w1_ref, w2_ref, tile_ref):
    E = N_EXPERTS
    logits = [lt_ref[e] for e in range(E)]
    shape = logits[0].shape
    mx = functools.reduce(jnp.maximum, logits)
    ex = [jnp.exp(l - mx) for l in logits]
    den = functools.reduce(lambda a, b: a + b, ex)
    scores = [x / den for x in ex]
    sel = [scores[e] + bias_ref[e] for e in range(E)]

    best_g = jnp.zeros(shape, I32)
    best_v = None
    for g in range(N_GROUPS):
        a, b, c, d = sel[EXPERTS_PER_GROUP * g: EXPERTS_PER_GROUP * (g + 1)]
        hi1, lo1 = jnp.maximum(a, b), jnp.minimum(a, b)
        hi2, lo2 = jnp.maximum(c, d), jnp.minimum(c, d)
        gs = jnp.maximum(hi1, hi2) + jnp.maximum(jnp.minimum(hi1, hi2), jnp.maximum(lo1, lo2))
        if g == 0:
            best_v = gs
        else:
            better = gs > best_v
            best_g = jnp.where(better, g, best_g)
            best_v = jnp.where(better, gs, best_v)

    masked = [jnp.where(best_g == (e // EXPERTS_PER_GROUP), sel[e], MASK_NEG) for e in range(E)]

    def arg_top(vals, exclude):
        idx = jnp.zeros(shape, I32)
        val = None
        for e in range(E):
            v = vals[e] if exclude is None else jnp.where(exclude == e, -jnp.inf, vals[e])
            if e == 0:
                val = v
            else:
                better = v > val
                idx = jnp.where(better, e, idx)
                val = jnp.where(better, v, val)
        return idx

    idx1 = arg_top(masked, None)
    idx2 = arg_top(masked, idx1)
    s1 = functools.reduce(lambda a, b: a + b, [jnp.where(idx1 == e, scores[e], 0.0) for e in range(E)])
    s2 = functools.reduce(lambda a, b: a + b, [jnp.where(idx2 == e, scores[e], 0.0) for e in range(E)])
    w1_ref[...] = s1 / (s1 + s2)
    w2_ref[...] = s2 / (s1 + s2)

    rows = shape[0]
    li = lax.broadcasted_iota(I32, (LANES, LANES), 0)
    lj = lax.broadcasted_iota(I32, (LANES, LANES), 1)
    upper = (li <= lj).astype(BF16)
    ri = lax.broadcasted_iota(I32, (rows, rows), 0)
    rj = lax.broadcasted_iota(I32, (rows, rows), 1)
    lower = (rj < ri).astype(BF16)
    tile_start = (lax.broadcasted_iota(I32, (1, LANES), 1) * MOE_TILE).astype(F32)
    start = jnp.zeros((1, 1), F32)
    pos1 = jnp.zeros(shape, F32)
    pos2 = jnp.zeros(shape, F32)
    tile_e = jnp.zeros((1, LANES), F32)
    for e in range(E):
        hit1 = idx1 == e
        hit2 = idx2 == e
        onehot = jnp.where(hit1 | hit2, 1.0, 0.0)
        pref = _dot(onehot.astype(BF16), upper)
        row_tot = pref[:, LANES - 1:LANES]
        row_off = _dot(lower, jnp.broadcast_to(row_tot, shape).astype(BF16))[:, 0:1]
        rank = pref - onehot + row_off
        count = jnp.sum(onehot, keepdims=True)
        dest = start + rank
        pos1 = jnp.where(hit1, dest, pos1)
        pos2 = jnp.where(hit2, dest, pos2)
        start = start + jnp.ceil(count / MOE_TILE) * MOE_TILE
        tile_e = tile_e + jnp.where(tile_start >= start, 1.0, 0.0)
    pos1_ref[...] = pos1.astype(I32)
    pos2_ref[...] = pos2.astype(I32)
    n_valid = start / MOE_TILE
    tile_id = lax.broadcasted_iota(I32, (1, LANES), 1).astype(F32)
    last_e = jnp.sum(jnp.where(tile_id == n_valid - 1.0, tile_e, 0.0), keepdims=True)
    tile_e = jnp.where(tile_id < n_valid, tile_e, last_e)
    sub = lax.broadcasted_iota(I32, (SUBLANES, LANES), 0)
    tile_ref[...] = jnp.where(sub == 0, tile_e, jnp.broadcast_to(n_valid, (SUBLANES, LANES))).astype(I32)


def _route(logits_t, router_bias, n_tiles):
    e, rows, lanes = logits_t.shape
    tok = jax.ShapeDtypeStruct((rows, lanes), I32)
    tokf = jax.ShapeDtypeStruct((rows, lanes), F32)
    full = pl.BlockSpec((rows, lanes), lambda i: (0, 0))
    return pl.pallas_call(
        functools.partial(_route_kernel, n_tiles),
        grid=(1,),
        in_specs=[pl.BlockSpec((e, rows, lanes), lambda i: (0, 0, 0)),
                  pl.BlockSpec(memory_space=pltpu.SMEM)],
        out_specs=[full, full, full, full, pl.BlockSpec((SUBLANES, LANES), lambda i: (0, 0))],
        out_shape=[tok, tok, tokf, tokf, jax.ShapeDtypeStruct((SUBLANES, LANES), I32)],
        compiler_params=_cparams(("arbitrary",)),
        name="route",
    )(logits_t, router_bias)


_DMA_UNROLL = 8


_W_CHUNK = 128
_W_RING = 11


def _sorted_sources_kernel(n_tok, n_rows, p1_ref, p2_ref, src_ref):
    def clear(p, carry):
        src_ref[p] = 0
        return carry

    def place(t, carry):
        src_ref[p1_ref[t]] = t
        src_ref[p2_ref[t]] = t
        return carry

    lax.fori_loop(0, n_rows, clear, 0, unroll=8)
    lax.fori_loop(0, n_tok, place, 0, unroll=8)


def _sorted_sources(pos1, pos2, n_rows):
    n_tok = pos1.shape[0]
    smem = pl.BlockSpec(memory_space=pltpu.SMEM)
    return pl.pallas_call(
        functools.partial(_sorted_sources_kernel, n_tok, n_rows),
        in_specs=[smem, smem],
        out_specs=smem,
        out_shape=jax.ShapeDtypeStruct((n_rows,), I32),
        name="sorted_sources",
    )(pos1, pos2)


def _moe_kernel(layer, n_tiles, te_ref, nv_ref, first_ref, slot_ref, nxt_ref,
                src_ref, srcn_ref, h_hbm, wg_hbm, wu_hbm, wd_hbm, y_ref,
                cg_scr, cu_scr, cd_scr, stage_scr, xa_scr, xb_scr, sem, xsem, cnt_ref):
    i = pl.program_id(0)
    tm = xa_scr.shape[0]

    def row_copy(idx_ref, r, buf, s):
        return pltpu.make_async_copy(h_hbm.at[pl.ds(idx_ref[0, 0, r], 1)], buf.at[pl.ds(r, 1)],
                                     xsem.at[s])
    d, f = cg_scr.shape[1], cg_scr.shape[2]
    n_g = d // _W_CHUNK
    n_chunks = 2 * n_g + f // _W_CHUNK

    def for_chunk(c, e, slot, fn):
        k = c % _W_RING

        @pl.when(c < n_g)
        def _():
            row = pl.multiple_of(c * _W_CHUNK, _W_CHUNK)
            fn(wg_hbm.at[layer, e, pl.ds(row, _W_CHUNK), :], stage_scr.at[k, :, pl.ds(0, f)],
               sem.at[k], cg_scr.at[slot, pl.ds(row, _W_CHUNK), :])

        @pl.when((c >= n_g) & (c < 2 * n_g))
        def _():
            row = pl.multiple_of((c - n_g) * _W_CHUNK, _W_CHUNK)
            fn(wu_hbm.at[layer, e, pl.ds(row, _W_CHUNK), :], stage_scr.at[k, :, pl.ds(0, f)],
               sem.at[k], cu_scr.at[slot, pl.ds(row, _W_CHUNK), :])

        @pl.when(c >= 2 * n_g)
        def _():
            row = pl.multiple_of((c - 2 * n_g) * _W_CHUNK, _W_CHUNK)
            fn(wd_hbm.at[layer, e, pl.ds(row, _W_CHUNK), :], stage_scr.at[k],
               sem.at[k], cd_scr.at[slot, pl.ds(row, _W_CHUNK), :])

    def start(src, stage, s, dst):
        pltpu.make_async_copy(src, stage, s).start()

    def finish(src, stage, s, dst):
        pltpu.make_async_copy(src, stage, s).wait()
        dst[...] = stage[...].astype(BF16)

    def start_upto(e, slot, hi):
        hi = jnp.minimum(hi, n_chunks)

        def body(c, carry):
            for_chunk(c, e, slot, start)
            return carry

        lax.fori_loop(cnt_ref[0], hi, body, 0)
        cnt_ref[0] = jnp.maximum(cnt_ref[0], hi)

    def finish_started(e, slot):
        def body(c, carry):
            for_chunk(c, e, slot, finish)
            return carry

        lax.fori_loop(cnt_ref[1], cnt_ref[0], body, 0)
        cnt_ref[1] = cnt_ref[0]

    valid = i < nv_ref[0]

    @pl.when(i == 0)
    def _():
        cnt_ref[0] = 0
        cnt_ref[1] = 0

    @pl.when(valid & (first_ref[i] == 1))
    def _():
        e, slot = te_ref[i], slot_ref[i]

        def body(c, carry):
            start_upto(e, slot, c + _W_RING)
            for_chunk(c, e, slot, finish)
            return carry

        lax.fori_loop(cnt_ref[1], n_chunks, body, 0)
        cnt_ref[0] = 0
        cnt_ref[1] = 0

    @pl.when(valid & (nxt_ref[i] >= 0))
    def _():
        e, slot = nxt_ref[i], 1 - slot_ref[i]
        finish_started(e, slot)
        start_upto(e, slot, cnt_ref[0] + _W_RING)

    @pl.when(i == 0)
    def _():
        def body(r, carry):
            row_copy(src_ref, r, xa_scr, 0).start()
            return carry

        lax.fori_loop(0, tm, body, 0, unroll=_DMA_UNROLL)

    for par, (cur, nxt_buf) in enumerate(((xa_scr, xb_scr), (xb_scr, xa_scr))):
        @pl.when(valid & (i % 2 == par))
        def _(par=par, cur=cur, nxt_buf=nxt_buf):
            def wait_body(r, carry):
                row_copy(src_ref, r, cur, par).wait()
                return carry

            lax.fori_loop(0, tm, wait_body, 0, unroll=_DMA_UNROLL)
            slot = slot_ref[i]
            x = cur[...].astype(BF16)
            for r in range(tm):
                row_copy(srcn_ref, r, nxt_buf, 1 - par).start()
            a = _dot(x, cg_scr[slot])
            b = _dot(x, cu_scr[slot])
            hidden = (_silu(a) * b).astype(BF16)
            y_ref[...] = _dot(hidden, cd_scr[slot])

            @pl.when(i == nv_ref[0] - 1)
            def _():
                def drain_body(r, carry):
                    row_copy(srcn_ref, r, nxt_buf, 1 - par).wait()
                    return carry

                lax.fori_loop(0, tm, drain_body, 0, unroll=_DMA_UNROLL)

    @pl.when(jnp.logical_not(valid))
    def _():
        y_ref[...] = jnp.zeros_like(y_ref)


def _moe_schedule(tile_e, n_valid):
    n_tiles = tile_e.shape[0]
    idx = jnp.arange(n_tiles, dtype=I32)
    valid = idx < n_valid[0]
    first = valid & ((idx == 0) | (tile_e != jnp.roll(tile_e, 1)))
    run = jnp.cumsum(first.astype(I32)) - 1
    next_first = lax.cummin(jnp.where(first, idx, n_tiles), reverse=True)
    after = jnp.concatenate([next_first[1:], jnp.full((1,), n_tiles, I32)])
    nxt = jnp.where(valid & (after < n_tiles), tile_e[jnp.minimum(after, n_tiles - 1)], -1)
    return first.astype(I32), (run % 2).astype(I32), nxt.astype(I32)


def _moe_experts(h, src_rows, tile_e, n_valid, schedule, w_gate, w_up, w_down, layer):
    d = h.shape[1]
    f = w_gate.shape[3]
    tm = MOE_TILE
    n_tiles = tile_e.shape[0]
    assert d % _W_CHUNK == 0 and f % _W_CHUNK == 0
    first, slot, nxt = schedule
    src3 = src_rows.reshape(n_tiles, 1, tm)

    any_spec = pl.BlockSpec(memory_space=pl.ANY)
    grid_spec = pltpu.PrefetchScalarGridSpec(
        num_scalar_prefetch=5,
        grid=(n_tiles,),
        in_specs=[pl.BlockSpec((1, 1, tm), lambda i, *_: (i, 0, 0), memory_space=pltpu.SMEM),
                  pl.BlockSpec((1, 1, tm), lambda i, *_: (jnp.minimum(i + 1, n_tiles - 1), 0, 0),
                               memory_space=pltpu.SMEM),
                  any_spec, any_spec, any_spec, any_spec],
        out_specs=pl.BlockSpec((tm, d), lambda i, *_: (i, 0)),
        scratch_shapes=[pltpu.VMEM((2, d, f), BF16), pltpu.VMEM((2, d, f), BF16),
                        pltpu.VMEM((2, f, d), BF16),
                        pltpu.VMEM((_W_RING, _W_CHUNK, d), F32),
                        pltpu.VMEM((tm, d), F32), pltpu.VMEM((tm, d), F32),
                        pltpu.SemaphoreType.DMA((_W_RING,)), pltpu.SemaphoreType.DMA((2,)),
                        pltpu.SMEM((2,), I32)],
    )
    return pl.pallas_call(
        functools.partial(_moe_kernel, layer, n_tiles),
        grid_spec=grid_spec,
        out_shape=jax.ShapeDtypeStruct((n_tiles * tm, d), F32),
        compiler_params=_cparams(("arbitrary",), MOE_VMEM_LIMIT),
        name="moe_experts",
    )(tile_e, n_valid, first, slot, nxt, src3, src3, h, w_gate, w_up, w_down)


_COMBINE_ROWS = 256


def _combine_kernel(with_next, steps, p1_ref, p2_ref, p1n_ref, p2n_ref, ys_ref, w1_ref, w2_ref,
                    x_ref, gate_ref, g_ref, b_ref, *rest):
    if with_next:
        sc_ref, sh_ref, xo_ref, h_ref, ya_scr, yb_scr, sem = rest
    else:
        xo_ref, ya_scr, yb_scr, sem = rest
    step = pl.program_id(0)

    def copies(pa, pb, buf, s, r):
        return (pltpu.make_async_copy(ys_ref.at[pl.ds(pa[0, 0, r], 1)],
                                      buf.at[0, pl.ds(r, 1)], sem.at[s, 0]),
                pltpu.make_async_copy(ys_ref.at[pl.ds(pb[0, 0, r], 1)],
                                      buf.at[1, pl.ds(r, 1)], sem.at[s, 1]))

    def wait_all(pa, pb, buf, s):
        def body(r, carry):
            for cp in copies(pa, pb, buf, s, r):
                cp.wait()
            return carry
        lax.fori_loop(0, _COMBINE_ROWS, body, 0, unroll=_DMA_UNROLL)

    @pl.when(step == 0)
    def _():
        def body(r, carry):
            for cp in copies(p1_ref, p2_ref, ya_scr, 0, r):
                cp.start()
            return carry
        lax.fori_loop(0, _COMBINE_ROWS, body, 0, unroll=_DMA_UNROLL)

    for par, (cur, nxt) in enumerate(((ya_scr, yb_scr), (yb_scr, ya_scr))):
        @pl.when(step % 2 == par)
        def _(par=par, cur=cur, nxt=nxt):
            wait_all(p1_ref, p2_ref, cur, par)
            for r in range(_COMBINE_ROWS):
                for cp in copies(p1n_ref, p2n_ref, nxt, 1 - par, r):
                    cp.start()
            y = w1_ref[0] * cur[0] + w2_ref[0] * cur[1]
            z = DEEPNORM_ALPHA * x_ref[0] + (1.0 + gate_ref[0]) * y
            xn = _layer_norm_rows(z, g_ref[...], b_ref[...])
            xo_ref[0] = xn
            if with_next:
                h_ref[0] = (xn * (1.0 + sc_ref[0]) + sh_ref[0]).astype(BF16)

            @pl.when(step == steps - 1)
            def _():
                wait_all(p1n_ref, p2n_ref, nxt, 1 - par)


def _combine(ys, pos1, pos2, w1, w2, x, gate, ln_g, ln_b, next_mod):
    bsz, t, d = x.shape
    tr = _COMBINE_ROWS
    per_b = t // tr
    steps = bsz * per_b
    row = pl.BlockSpec((1, tr, d), lambda s: (s // per_b, s % per_b, 0))
    vec = pl.BlockSpec((1, d), lambda s: (0, 0))
    mod = pl.BlockSpec((1, 1, d), lambda s: (s // per_b, 0, 0))
    wcol = pl.BlockSpec((1, tr, 1), lambda s: (s // per_b, s % per_b, 0))
    idx = pl.BlockSpec((1, 1, tr), lambda s: (s, 0, 0), memory_space=pltpu.SMEM)
    idx_next = pl.BlockSpec((1, 1, tr), lambda s: (jnp.minimum(s + 1, steps - 1), 0, 0),
                            memory_space=pltpu.SMEM)
    with_next = next_mod is not None
    p1 = pos1.reshape(steps, 1, tr)
    p2 = pos2.reshape(steps, 1, tr)
    in_specs = [idx, idx, idx_next, idx_next, pl.BlockSpec(memory_space=pl.ANY),
                wcol, wcol, row, mod, vec, vec]
    args = [p1, p2, p1, p2, ys, w1.reshape(bsz, t, 1), w2.reshape(bsz, t, 1), x, gate,
            ln_g.reshape(1, d), ln_b.reshape(1, d)]
    out_specs = [row]
    out_shape = [jax.ShapeDtypeStruct((bsz, t, d), F32)]
    if with_next:
        in_specs += [mod, mod]
        args += list(next_mod)
        out_specs.append(row)
        out_shape.append(jax.ShapeDtypeStruct((bsz, t, d), BF16))
    res = pl.pallas_call(
        functools.partial(_combine_kernel, with_next, steps),
        grid=(steps,),
        in_specs=in_specs,
        out_specs=out_specs,
        out_shape=out_shape,
        scratch_shapes=[pltpu.VMEM((2, tr, d), F32), pltpu.VMEM((2, tr, d), F32),
                        pltpu.SemaphoreType.DMA((2, 2))],
        compiler_params=_cparams(("arbitrary",), VMEM_LIMIT),
        name="combine_ln",
    )(*args)
    return res if with_next else (res[0], None)


def kernel(x, c, emb_ln_g, emb_ln_b, w_ada, b_ada, w_in, conv_w, mix_beta, w_out, hg_lb_logits,
           ln_g, ln_b, w_router, router_bias, w_gate, w_up, w_down):
    bsz, t, d = x.shape
    n = bsz * t
    assert n % LANES == 0 and t % SEQ_CHUNK == 0
    n_tiles = 2 * n // MOE_TILE + N_EXPERTS
    assert n_tiles <= LANES

    mod = _ada_mod(c, w_ada, b_ada)
    mod = mod.reshape(DEPTH, bsz, 6, 1, d)

    def mods(l):
        return [mod[l, :, i] for i in range(6)]

    wr_hi = jnp.pad(w_router, ((0, 0), (0, LANES - N_EXPERTS)))
    wr_hi16 = wr_hi.astype(BF16)
    wr_lo16 = (wr_hi - wr_hi16.astype(F32)).astype(BF16)

    w_out16 = w_out.astype(BF16)

    m = [mods(l) for l in range(DEPTH)]
    xcur, h = _ln_mod(x, emb_ln_g, emb_ln_b, m[0][1], m[0][0])
    for l in range(DEPTH):
        _, _, gate1, shift2, scale2, gate2 = m[l]
        beta = mix_beta[l].reshape(1, -1)
        proj = _in_proj(h.reshape(n, d), w_in, l, BF16).reshape(bsz, t, IN_COLS)
        yc = _conv_mixer(proj, conv_w[l], beta)
        yr = _ret_mixer(proj, beta)
        yh = _hg_mixer(proj, hg_lb_logits, beta, l)
        x1, h2, logits = _out_proj(yc, yr, yh, w_out16, l, xcur, gate1,
                                   ln_g[l, 0], ln_b[l, 0], scale2, shift2, wr_hi16, wr_lo16)
        logits_t = logits.reshape(n, LANES)[:, :N_EXPERTS].T.reshape(N_EXPERTS, n // LANES, LANES)
        pos1, pos2, w1, w2, tiles = _route(logits_t, router_bias, n_tiles)
        tile_e, n_valid = tiles[0, :n_tiles], tiles[1, :1]
        src_rows = _sorted_sources(pos1.reshape(n), pos2.reshape(n), n_tiles * MOE_TILE)
        ys = _moe_experts(h2.reshape(n, d), src_rows, tile_e, n_valid,
                          _moe_schedule(tile_e, n_valid), w_gate, w_up, w_down, l)
        next_mod = (m[l + 1][1], m[l + 1][0]) if l + 1 < DEPTH else None
        xcur, h = _combine(ys, pos1.reshape(n), pos2.reshape(n), w1, w2, x1, gate2,
                           ln_g[l, 1], ln_b[l, 1], next_mod)
    return xcur
```

```python
import functools
import math

import jax
import jax.numpy as jnp
from jax import lax
from jax.experimental import pallas as pl
from jax.experimental.pallas import tpu as pltpu

F32 = jnp.float32
BF16 = jnp.bfloat16
I32 = jnp.int32

DEPTH = 2
CONV_WIDTH = 512
RET_WIDTH = 768
HG_WIDTH = 768
HEAD_DIM = 128
RET_HEADS = RET_WIDTH // HEAD_DIM
HG_HEADS = HG_WIDTH // HEAD_DIM
IN_COLS = 3 * CONV_WIDTH + 4 * RET_WIDTH + 5 * HG_WIDTH
ROPE_BASE = 10000.0
N_EXPERTS = 16
N_GROUPS = 4
EXPERTS_PER_GROUP = N_EXPERTS // N_GROUPS
MASK_NEG = -1e9
DEEPNORM_ALPHA = (2.0 * DEPTH) ** 0.25
LN_EPS = 1e-5
HEAD_EPS = 1e-6
FORGET_FLOOR = 1e-6

LANES = 128
SUBLANES = 8

_CB, _CC, _CH = 0, 4, 8
_RQ, _RK, _RV, _RG = 12, 18, 24, 30
_HQ, _HFF, _HFB, _HI, _HG = 36, 42, 48, 54, 60
_OUT_RET, _OUT_HG = 4, 10

SEQ_CHUNK = 128
MOE_TILE = 256
VMEM_LIMIT = 56 * 1024 * 1024
MOE_VMEM_LIMIT = 60 * 1024 * 1024


def _cparams(sem, vmem=None):
    return pltpu.CompilerParams(dimension_semantics=sem, vmem_limit_bytes=vmem)


def _silu(x):
    return x * jax.nn.sigmoid(x)


def _dot(a, b):
    return jnp.dot(a, b, preferred_element_type=F32)


def _dot_nt(a, b):
    return lax.dot_general(a, b, (((1,), (1,)), ((), ())), preferred_element_type=F32)


def _dot_tn(a, b):
    return lax.dot_general(a, b, (((0,), (0,)), ((), ())), preferred_element_type=F32)


def _ada_kernel(c_ref, w_ref, b_ref, o_ref):
    cond = _silu(c_ref[...])
    o_ref[0] = _dot(cond.astype(BF16), w_ref[0].astype(BF16)) + b_ref[0]


def _ada_mod(c, w_ada, b_ada):
    depth, d, n6 = w_ada.shape
    b = c.shape[0]
    bp = -(-b // SUBLANES) * SUBLANES
    cp = jnp.pad(c, ((0, bp - b), (0, 0)))
    tn = 512
    out = pl.pallas_call(
        _ada_kernel,
        grid=(depth, n6 // tn),
        in_specs=[
            pl.BlockSpec((bp, d), lambda l, j: (0, 0)),
            pl.BlockSpec((1, d, tn), lambda l, j: (l, 0, j)),
            pl.BlockSpec((1, 1, tn), lambda l, j: (l, 0, j)),
        ],
        out_specs=pl.BlockSpec((1, bp, tn), lambda l, j: (l, 0, j)),
        out_shape=jax.ShapeDtypeStruct((depth, bp, n6), F32),
        compiler_params=_cparams(("arbitrary", "arbitrary")),
        name="ada_mod",
    )(cp, w_ada, b_ada.reshape(depth, 1, n6))
    return out[:, :b, :]


def _layer_norm_rows(z, g, b):
    mu = jnp.mean(z, axis=-1, keepdims=True)
    zc = z - mu
    var = jnp.mean(zc * zc, axis=-1, keepdims=True)
    return zc * lax.rsqrt(var + LN_EPS) * g + b


def _ln_mod_kernel(x_ref, g_ref, b_ref, sc_ref, sh_ref, xo_ref, h_ref):
    y = _layer_norm_rows(x_ref[0], g_ref[...], b_ref[...])
    xo_ref[0] = y
    h_ref[0] = (y * (1.0 + sc_ref[0]) + sh_ref[0]).astype(BF16)


def _ln_mod(x, g, b, scale, shift):
    bsz, t, d = x.shape
    tr = 512
    row = pl.BlockSpec((1, tr, d), lambda i, j: (i, j, 0))
    vec = pl.BlockSpec((1, d), lambda i, j: (0, 0))
    mod = pl.BlockSpec((1, 1, d), lambda i, j: (i, 0, 0))
    return pl.pallas_call(
        _ln_mod_kernel,
        grid=(bsz, t // tr),
        in_specs=[row, vec, vec, mod, mod],
        out_specs=[row, row],
        out_shape=[jax.ShapeDtypeStruct((bsz, t, d), F32),
                   jax.ShapeDtypeStruct((bsz, t, d), BF16)],
        compiler_params=_cparams(("arbitrary", "arbitrary")),
        name="ln_mod",
    )(x, g.reshape(1, d), b.reshape(1, d), scale, shift)


def _in_proj_kernel(a_ref, w_ref, o_ref, w16_scr):
    @pl.when(pl.program_id(1) == 0)
    def _():
        w16_scr[...] = w_ref[0].astype(BF16)

    o_ref[...] = _dot(a_ref[...], w16_scr[...]).astype(o_ref.dtype)


def _in_proj(h, w_in, layer, out_dtype):
    n, d = h.shape
    nc = w_in.shape[2]
    tm, tn = 1024, 1408
    tm = min(tm, n)
    return pl.pallas_call(
        _in_proj_kernel,
        grid=(nc // tn, n // tm),
        in_specs=[pl.BlockSpec((tm, d), lambda j, i: (i, 0)),
                  pl.BlockSpec((1, d, tn), lambda j, i: (layer, 0, j))],
        out_specs=pl.BlockSpec((tm, tn), lambda j, i: (i, j)),
        out_shape=jax.ShapeDtypeStruct((n, nc), out_dtype),
        scratch_shapes=[pltpu.VMEM((d, tn), BF16)],
        compiler_params=_cparams(("arbitrary", "arbitrary"), VMEM_LIMIT),
        name="in_proj",
    )(h, w_in)


def _conv_kernel(cb_ref, cc_ref, ch_ref, w_ref, beta_ref, o_ref):
    u = cc_ref[0].astype(F32) * ch_ref[0].astype(F32)
    t = u.shape[0]
    row = lax.broadcasted_iota(I32, u.shape, 0)
    prev = jnp.where(row == 0, 0.0, pltpu.roll(u, 1, 0))
    nxt = jnp.where(row == t - 1, 0.0, pltpu.roll(u, t - 1, 0))
    w = w_ref[...]
    y = cb_ref[0].astype(F32) * (prev * w[0:1] + u * w[1:2] + nxt * w[2:3])
    o_ref[0] = (y * beta_ref[...]).astype(o_ref.dtype)


def _conv_mixer(proj, conv_w, beta):
    bsz, t, _ = proj.shape
    cw = 256
    nb = CONV_WIDTH // cw
    per = LANES * 1

    def col(off):
        return pl.BlockSpec((1, t, cw), lambda b, j, off=off: (b, 0, off * per // cw + j))

    return pl.pallas_call(
        _conv_kernel,
        grid=(bsz, nb),
        in_specs=[col(_CB), col(_CC), col(_CH),
                  pl.BlockSpec((3, cw), lambda b, j: (0, j)),
                  pl.BlockSpec((1, cw), lambda b, j: (0, j))],
        out_specs=pl.BlockSpec((1, t, cw), lambda b, j: (b, 0, j)),
        out_shape=jax.ShapeDtypeStruct((bsz, t, CONV_WIDTH), BF16),
        compiler_params=_cparams(("arbitrary", "arbitrary"), VMEM_LIMIT),
        name="conv_mixer",
    )(proj, proj, proj, conv_w, beta)


def _ret_kernel(q_ref, k_ref, v_ref, g_ref, cos_ref, sin_ref, lg_ref, beta_ref,
                o_ref, o_scr, qf_scr, qb_scr, uf_scr, ub_scr):
    L = SEQ_CHUNK
    t = q_ref.shape[1]
    n = t // L
    lgf = lg_ref[0, 0:1, :]
    lgb = lg_ref[0, 1:2, :]
    r = lax.broadcasted_iota(I32, (L, L), 0).astype(F32)
    c = lax.broadcasted_iota(I32, (L, L), 1).astype(F32)
    rel = r - c
    dbi = jnp.where(rel > 0, jnp.exp(lgf * jnp.maximum(rel, 0.0)),
                    jnp.where(rel < 0, jnp.exp(lgb * jnp.maximum(-rel, 0.0)), 2.0))
    qf = jnp.exp(lgf * (r + 1.0))
    kf = jnp.exp(lgf * (L - 1.0 - r))
    qb = jnp.exp(lgb * (L - r))
    kb = jnp.exp(lgb * r)
    gf_l = jnp.exp(lgf * float(L))
    gb_l = jnp.exp(lgb * float(L))
    scale = HEAD_DIM ** -0.5

    def rot(x, cs, sn):
        return x * cs + pltpu.roll(x, HEAD_DIM // 2, 1) * sn

    def rows(ci):
        return pl.ds(pl.multiple_of(ci * L, L), L)

    def local_body(ci, carry):
        sl = rows(ci)
        cs = cos_ref[sl, :]
        sn = sin_ref[sl, :]
        q = rot(q_ref[0, sl, :].astype(F32), cs, sn)
        k = rot(k_ref[0, sl, :].astype(F32), cs, sn) * scale
        v16 = v_ref[0, sl, :].astype(BF16)
        s = _dot_nt(q.astype(BF16), k.astype(BF16)) * dbi
        o_scr[sl, :] = _dot(s.astype(BF16), v16)
        qf_scr[sl, :] = (q * qf).astype(BF16)
        qb_scr[sl, :] = (q * qb).astype(BF16)
        uf_scr[ci] = _dot_tn((k * kf).astype(BF16), v16)
        ub_scr[ci] = _dot_tn((k * kb).astype(BF16), v16)
        return carry

    lax.fori_loop(0, n, local_body, 0, unroll=4)

    def scan_body(i, states):
        sf, sb = states
        cf, cb = i, n - 1 - i
        slf, slb = rows(cf), rows(cb)
        o_scr[slf, :] += _dot(qf_scr[slf, :], sf.astype(BF16))
        o_scr[slb, :] += _dot(qb_scr[slb, :], sb.astype(BF16))
        return gf_l * sf + uf_scr[cf], gb_l * sb + ub_scr[cb]

    zero = jnp.zeros((HEAD_DIM, HEAD_DIM), F32)
    lax.fori_loop(0, n, scan_body, (zero, zero), unroll=8)

    beta = beta_ref[...]

    def norm_body(ci, carry):
        sl = rows(ci)
        o = o_scr[sl, :]
        mu = jnp.mean(o, axis=-1, keepdims=True)
        oc = o - mu
        var = jnp.mean(oc * oc, axis=-1, keepdims=True)
        y = oc * lax.rsqrt(var + HEAD_EPS) * _silu(g_ref[0, sl, :].astype(F32)) * beta
        o_ref[0, sl, :] = y.astype(o_ref.dtype)
        return carry

    lax.fori_loop(0, n, norm_body, 0, unroll=4)


def _rotary_tables(t):
    half = HEAD_DIM // 2
    inv_freq = ROPE_BASE ** (-jnp.arange(half, dtype=F32) / half)
    ang = jnp.arange(t, dtype=F32)[:, None] * inv_freq[None, :]
    cos, sin = jnp.cos(ang), jnp.sin(ang)
    return jnp.concatenate([cos, cos], -1), jnp.concatenate([-sin, sin], -1)


def _ret_log_decays():
    head = jnp.arange(RET_HEADS, dtype=F32)
    lg_f = jnp.log1p(-jnp.exp2(-5.0 - head))
    lg_b = jnp.log1p(-jnp.exp2(-5.5 - head))
    lg = jnp.stack([lg_f, lg_b], axis=1)
    return jnp.broadcast_to(lg[:, :, None], (RET_HEADS, 2, LANES))


def _ret_mixer(proj, beta):
    bsz, t, _ = proj.shape
    n_chunks = t // SEQ_CHUNK
    assert n_chunks % 2 == 0
    cosf, sinf = _rotary_tables(t)

    def col(off):
        return pl.BlockSpec((1, t, HEAD_DIM), lambda b, h, off=off: (b, 0, off + h))

    tab = pl.BlockSpec((t, HEAD_DIM), lambda b, h: (0, 0))
    return pl.pallas_call(
        _ret_kernel,
        grid=(bsz, RET_HEADS),
        in_specs=[col(_RQ), col(_RK), col(_RV), col(_RG), tab, tab,
                  pl.BlockSpec((1, 2, LANES), lambda b, h: (h, 0, 0)),
                  pl.BlockSpec((1, HEAD_DIM), lambda b, h: (0, _OUT_RET + h))],
        out_specs=pl.BlockSpec((1, t, HEAD_DIM), lambda b, h: (b, 0, h)),
        out_shape=jax.ShapeDtypeStruct((bsz, t, RET_WIDTH), BF16),
        scratch_shapes=[pltpu.VMEM((t, HEAD_DIM), F32),
                        pltpu.VMEM((t, HEAD_DIM), BF16), pltpu.VMEM((t, HEAD_DIM), BF16),
                        pltpu.VMEM((n_chunks, HEAD_DIM, HEAD_DIM), F32),
                        pltpu.VMEM((n_chunks, HEAD_DIM, HEAD_DIM), F32)],
        compiler_params=_cparams(("arbitrary", "arbitrary"), VMEM_LIMIT),
        name="ret_mixer",
    )(proj, proj, proj, proj, cosf, sinf, _ret_log_decays(), beta)


_HG_BASE = SUBLANES
_HG_LEVELS = tuple(m for m in (8, 16, 32, 64) if m < SEQ_CHUNK)


def _cumsum_rows(x, reverse):
    n = x.shape[0]
    row = lax.broadcasted_iota(I32, x.shape, 0)
    sh = 1
    while sh < n:
        if reverse:
            x = x + jnp.where(row < n - sh, pltpu.roll(x, n - sh, 0), 0.0)
        else:
            x = x + jnp.where(row >= sh, pltpu.roll(x, sh, 0), 0.0)
        sh *= 2
    return x


def _hg_masks(reverse):
    C = SEQ_CHUNK
    r = lax.broadcasted_iota(I32, (C, C), 0)
    c = lax.broadcasted_iota(I32, (C, C), 1)
    if reverse:
        r, c = c, r
    sh = _HG_BASE.bit_length() - 1
    masks = [((r >> sh) == (c >> sh)) & (c <= r)]
    for m in _HG_LEVELS:
        sh = m.bit_length() - 1
        masks.append(((r >> (sh + 1)) == (c >> (sh + 1)))
                     & (((r >> sh) & 1) == 1) & (((c >> sh) & 1) == 0))
    return [jnp.where(m, 1.0, 0.0) for m in masks]


def _hg_scores(q, kk, cum, mask_ref, reverse):
    C = SEQ_CHUNK
    d = 1 if reverse else 0

    def split(x, blk):
        return x.reshape(C // blk, blk, LANES)

    b = _HG_BASE
    ref_row = b // 2 if reverse else b // 2 - 1
    cum3 = split(cum, b)
    rel = cum3 - cum3[:, ref_row:ref_row + 1, :]
    qt = (split(q, b) * jnp.exp(rel)).reshape(C, LANES).astype(BF16)
    kt = (split(kk, b) * jnp.exp(-rel)).reshape(C, LANES).astype(BF16)
    a = jnp.where(mask_ref[d, 0] > 0.5, _dot_nt(qt, kt), 0.0)
    for li, m in enumerate(_HG_LEVELS):
        blk = 2 * m
        ref_row = m if reverse else m - 1
        cum3 = split(cum, blk)
        rel = cum3 - cum3[:, ref_row:ref_row + 1, :]
        dec = jnp.exp(-jnp.abs(rel)).reshape(C, LANES)
        halves = []
        for j in range(C // m):
            is_query = (j % 2 == 1) != reverse
            halves.append((q if is_query else kk)[j * m:(j + 1) * m])
        x = (jnp.concatenate(halves, axis=0) * dec).astype(BF16)
        a = a + mask_ref[d, li + 1] * _dot_nt(x, x)
    return a


def _hg_kernel(layer, q_ref, zf_ref, zb_ref, v_ref, g_ref, lbl_ref, beta_ref, o_ref,
               o_scr, qe_scr, u_scr, dec_scr, mask_scr):
    C = SEQ_CHUNK
    t = q_ref.shape[1]
    n = t // C
    for d, reverse in enumerate((False, True)):
        for li, m in enumerate(_hg_masks(reverse)):
            mask_scr[d, li] = m
    logits = lbl_ref[...].astype(F32)
    e = jnp.exp(logits - jnp.max(logits, axis=0, keepdims=True))
    p = e / jnp.sum(e, axis=0, keepdims=True)
    lb = p[0]
    for l in range(1, layer + 1):
        lb = lb + p[l]
    lb = lb - p[0]

    def gates(z_ref, sl, lb_row):
        f = lb_row + (1.0 - lb_row) * jax.nn.sigmoid(z_ref[0, sl, :].astype(F32))
        f = jnp.maximum(f, FORGET_FLOOR)
        return jnp.log(f), 1.0 - f

    def rows(ci):
        return pl.ds(pl.multiple_of(ci * C, C), C)

    def local_dir(ci, sl, q, v16, z_ref, d):
        reverse = d == 1
        logf, kk = gates(z_ref, sl, lb[d:d + 1, :])
        cum = _cumsum_rows(logf, reverse)
        a = _hg_scores(q, kk, cum, mask_scr, reverse)
        total = cum[0:1, :] if reverse else cum[C - 1:C, :]
        qe_scr[d, sl, :] = (q * jnp.exp(cum)).astype(BF16)
        k_tail = kk * jnp.exp(total - cum)
        u_scr[d, ci] = _dot_tn(v16, k_tail.astype(BF16))
        dec_scr[d, ci] = jnp.broadcast_to(jnp.exp(total), (SUBLANES, LANES))
        return _dot(a.astype(BF16), v16)

    def local_body(ci, carry):
        sl = rows(ci)
        q = q_ref[0, sl, :].astype(F32)
        v16 = v_ref[0, sl, :].astype(BF16)
        o_scr[sl, :] = (local_dir(ci, sl, q, v16, zf_ref, 0)
                        + local_dir(ci, sl, q, v16, zb_ref, 1))
        return carry

    lax.fori_loop(0, n, local_body, 0, unroll=2)

    def scan_body(i, states):
        sf, sb = states
        cf, cb = i, n - 1 - i
        slf, slb = rows(cf), rows(cb)
        o_scr[slf, :] += _dot_nt(qe_scr[0, slf, :], sf.astype(BF16))
        o_scr[slb, :] += _dot_nt(qe_scr[1, slb, :], sb.astype(BF16))
        return (sf * dec_scr[0, cf, 0:1, :] + u_scr[0, cf],
                sb * dec_scr[1, cb, 0:1, :] + u_scr[1, cb])

    zero = jnp.zeros((HEAD_DIM, HEAD_DIM), F32)
    lax.fori_loop(0, n, scan_body, (zero, zero), unroll=8)

    beta = beta_ref[...]

    def norm_body(ci, carry):
        sl = rows(ci)
        o = o_scr[sl, :]
        ms = jnp.mean(o * o, axis=-1, keepdims=True)
        y = o * lax.rsqrt(ms + HEAD_EPS) * _silu(g_ref[0, sl, :].astype(F32)) * beta
        o_ref[0, sl, :] = y.astype(o_ref.dtype)
        return carry

    lax.fori_loop(0, n, norm_body, 0, unroll=8)


def _hg_mixer(proj, lb_logits, beta, layer):
    bsz, t, _ = proj.shape
    n_chunks = t // SEQ_CHUNK
    assert n_chunks % 2 == 0

    def col(off):
        return pl.BlockSpec((1, t, HEAD_DIM), lambda b, h, off=off: (b, 0, off + h))

    return pl.pallas_call(
        functools.partial(_hg_kernel, layer),
        grid=(bsz, HG_HEADS),
        in_specs=[col(_HQ), col(_HFF), col(_HFB), col(_HI), col(_HG),
                  pl.BlockSpec((DEPTH, 2, HEAD_DIM), lambda b, h: (0, 0, h)),
                  pl.BlockSpec((1, HEAD_DIM), lambda b, h: (0, _OUT_HG + h))],
        out_specs=pl.BlockSpec((1, t, HEAD_DIM), lambda b, h: (b, 0, h)),
        out_shape=jax.ShapeDtypeStruct((bsz, t, HG_WIDTH), BF16),
        scratch_shapes=[pltpu.VMEM((t, HEAD_DIM), F32),
                        pltpu.VMEM((2, t, HEAD_DIM), BF16),
                        pltpu.VMEM((2, n_chunks, HEAD_DIM, HEAD_DIM), F32),
                        pltpu.VMEM((2, n_chunks, SUBLANES, LANES), F32),
                        pltpu.VMEM((2, 1 + len(_HG_LEVELS), SEQ_CHUNK, SEQ_CHUNK), F32)],
        compiler_params=_cparams(("arbitrary", "arbitrary"), VMEM_LIMIT),
        name="hg_mixer",
    )(proj, proj, proj, proj, proj, lb_logits, beta)


_OUT_PROJ_SPLIT = 2


def _out_proj_kernel(yc_ref, yr_ref, yh_ref, w_ref, x_ref, gate_ref, g_ref, b_ref,
                     sc_ref, sh_ref, wrh_ref, wrl_ref, xo_ref, h_ref, lo_ref):
    c0, c1 = CONV_WIDTH, CONV_WIDTH + RET_WIDTH
    tm = x_ref.shape[1]
    for half in range(_OUT_PROJ_SPLIT):
        sl = pl.ds(half * (tm // _OUT_PROJ_SPLIT), tm // _OUT_PROJ_SPLIT)
        y = (_dot(yc_ref[0, sl, :], w_ref[0, 0:c0, :]) + _dot(yr_ref[0, sl, :], w_ref[0, c0:c1, :])
             + _dot(yh_ref[0, sl, :], w_ref[0, c1:, :]))
        z = DEEPNORM_ALPHA * x_ref[0, sl, :] + (1.0 + gate_ref[0]) * y
        xn = _layer_norm_rows(z, g_ref[...], b_ref[...])
        xo_ref[0, sl, :] = xn
        h = xn * (1.0 + sc_ref[0]) + sh_ref[0]
        h_ref[0, sl, :] = h
        h_hi = h.astype(BF16)
        h_lo = (h - h_hi.astype(F32)).astype(BF16)
        wrh = wrh_ref[...]
        lo_ref[0, sl, :] = _dot(h_hi, wrh) + _dot(h_lo, wrh) + _dot(h_hi, wrl_ref[...])


def _out_proj(yc, yr, yh, w_out16, layer, x, gate, ln_g, ln_b, scale2, shift2, wr_hi, wr_lo):
    bsz, t, d = x.shape
    tm = 512

    def row(width):
        return pl.BlockSpec((1, tm, width), lambda b, i: (b, i, 0))

    vec = pl.BlockSpec((1, d), lambda b, i: (0, 0))
    mod = pl.BlockSpec((1, 1, d), lambda b, i: (b, 0, 0))
    wr = pl.BlockSpec((d, LANES), lambda b, i: (0, 0))
    return pl.pallas_call(
        _out_proj_kernel,
        grid=(bsz, t // tm),
        in_specs=[row(CONV_WIDTH), row(RET_WIDTH), row(HG_WIDTH),
                  pl.BlockSpec((1, d, d), lambda b, i: (layer, 0, 0)),
                  row(d), mod, vec, vec, mod, mod, wr, wr],
        out_specs=[row(d), row(d), row(LANES)],
        out_shape=[jax.ShapeDtypeStruct((bsz, t, d), F32),
                   jax.ShapeDtypeStruct((bsz, t, d), F32),
                   jax.ShapeDtypeStruct((bsz, t, LANES), F32)],
        compiler_params=_cparams(("arbitrary", "arbitrary"), VMEM_LIMIT),
        name="out_proj",
    )(yc, yr, yh, w_out16, x, gate, ln_g.reshape(1, d), ln_b.reshape(1, d),
      scale2, shift2, wr_hi, wr_lo)


def _route_kernel(n_tiles, lt_ref, bias_ref, pos1_ref, pos2_ref, w1_ref, w2_ref, tile_ref):
    E = N_EXPERTS
    logits = [lt_ref[e] for e in range(E)]
    shape = logits[0].shape
    mx = functools.reduce(jnp.maximum, logits)
    ex = [jnp.exp(l - mx) for l in logits]
    den = functools.reduce(lambda a, b: a + b, ex)
    scores = [x / den for x in ex]
    sel = [scores[e] + bias_ref[e] for e in range(E)]

    best_g = jnp.zeros(shape, I32)
    best_v = None
    for g in range(N_GROUPS):
        a, b, c, d = sel[EXPERTS_PER_GROUP * g: EXPERTS_PER_GROUP * (g + 1)]
        hi1, lo1 = jnp.maximum(a, b), jnp.minimum(a, b)
        hi2, lo2 = jnp.maximum(c, d), jnp.minimum(c, d)
        gs = jnp.maximum(hi1, hi2) + jnp.maximum(jnp.minimum(hi1, hi2), jnp.maximum(lo1, lo2))
        if g == 0:
            best_v = gs
        else:
            better = gs > best_v
            best_g = jnp.where(better, g, best_g)
            best_v = jnp.where(better, gs, best_v)

    masked = [jnp.where(best_g == (e // EXPERTS_PER_GROUP), sel[e], MASK_NEG) for e in range(E)]

    def arg_top(vals, exclude):
        idx = jnp.zeros(shape, I32)
        val = None
        for e in range(E):
            v = vals[e] if exclude is None else jnp.where(exclude == e, -jnp.inf, vals[e])
            if e == 0:
                val = v
            else:
                better = v > val
                idx = jnp.where(better, e, idx)
                val = jnp.where(better, v, val)
        return idx

    idx1 = arg_top(masked, None)
    idx2 = arg_top(masked, idx1)
    s1 = functools.reduce(lambda a, b: a + b, [jnp.where(idx1 == e, scores[e], 0.0) for e in range(E)])
    s2 = functools.reduce(lambda a, b: a + b, [jnp.where(idx2 == e, scores[e], 0.0) for e in range(E)])
    w1_ref[...] = s1 / (s1 + s2)
    w2_ref[...] = s2 / (s1 + s2)

    rows = shape[0]
    li = lax.broadcasted_iota(I32, (LANES, LANES), 0)
    lj = lax.broadcasted_iota(I32, (LANES, LANES), 1)
    upper = (li <= lj).astype(BF16)
    ri = lax.broadcasted_iota(I32, (rows, rows), 0)
    rj = lax.broadcasted_iota(I32, (rows, rows), 1)
    lower = (rj < ri).astype(BF16)
    tile_start = (lax.broadcasted_iota(I32, (1, LANES), 1) * MOE_TILE).astype(F32)
    start = jnp.zeros((1, 1), F32)
    pos1 = jnp.zeros(shape, F32)
    pos2 = jnp.zeros(shape, F32)
    tile_e = jnp.zeros((1, LANES), F32)
    for e in range(E):
        hit1 = idx1 == e
        hit2 = idx2 == e
        onehot = jnp.where(hit1 | hit2, 1.0, 0.0)
        pref = _dot(onehot.astype(BF16), upper)
        row_tot = pref[:, LANES - 1:LANES]
        row_off = _dot(lower, jnp.broadcast_to(row_tot, shape).astype(BF16))[:, 0:1]
        rank = pref - onehot + row_off
        count = jnp.sum(onehot, keepdims=True)
        dest = start + rank
        pos1 = jnp.where(hit1, dest, pos1)
        pos2 = jnp.where(hit2, dest, pos2)
        start = start + jnp.ceil(count / MOE_TILE) * MOE_TILE
        tile_e = tile_e + jnp.where(tile_start >= start, 1.0, 0.0)
    pos1_ref[...] = pos1.astype(I32)
    pos2_ref[...] = pos2.astype(I32)
    n_valid = start / MOE_TILE
    tile_id = lax.broadcasted_iota(I32, (1, LANES), 1).astype(F32)
    last_e = jnp.sum(jnp.where(tile_id == n_valid - 1.0, tile_e, 0.0), keepdims=True)
    tile_e = jnp.where(tile_id < n_valid, tile_e, last_e)
    sub = lax.broadcasted_iota(I32, (SUBLANES, LANES), 0)
    tile_ref[...] = jnp.where(sub == 0, tile_e, jnp.broadcast_to(n_valid, (SUBLANES, LANES))).astype(I32)


def _route(logits_t, router_bias, n_tiles):
    e, rows, lanes = logits_t.shape
    tok = jax.ShapeDtypeStruct((rows, lanes), I32)
    tokf = jax.ShapeDtypeStruct((rows, lanes), F32)
    full = pl.BlockSpec((rows, lanes), lambda i: (0, 0))
    return pl.pallas_call(
        functools.partial(_route_kernel, n_tiles),
        grid=(1,),
        in_specs=[pl.BlockSpec((e, rows, lanes), lambda i: (0, 0, 0)),
                  pl.BlockSpec(memory_space=pltpu.SMEM)],
        out_specs=[full, full, full, full, pl.BlockSpec((SUBLANES, LANES), lambda i: (0, 0))],
        out_shape=[tok, tok, tokf, tokf, jax.ShapeDtypeStruct((SUBLANES, LANES), I32)],
        compiler_params=_cparams(("arbitrary",)),
        name="route",
    )(logits_t, router_bias)


_DMA_UNROLL = 8


_W_CHUNK = 128
_W_RING = 11


def _sorted_sources_kernel(n_tok, n_rows, p1_ref, p2_ref, src_ref):
    def clear(p, carry):
        src_ref[p] = 0
        return carry

    def place(t, carry):
        src_ref[p1_ref[t]] = t
        src_ref[p2_ref[t]] = t
        return carry

    lax.fori_loop(0, n_rows, clear, 0, unroll=8)
    lax.fori_loop(0, n_tok, place, 0, unroll=8)


def _sorted_sources(pos1, pos2, n_rows):
    n_tok = pos1.shape[0]
    smem = pl.BlockSpec(memory_space=pltpu.SMEM)
    return pl.pallas_call(
        functools.partial(_sorted_sources_kernel, n_tok, n_rows),
        in_specs=[smem, smem],
        out_specs=smem,
        out_shape=jax.ShapeDtypeStruct((n_rows,), I32),
        name="sorted_sources",
    )(pos1, pos2)


def _moe_kernel(layer, n_tiles, te_ref, nv_ref, first_ref, slot_ref, nxt_ref,
                src_ref, srcn_ref, h_hbm, wg_hbm, wu_hbm, wd_hbm, y_ref,
                cg_scr, cu_scr, cd_scr, stage_scr, xa_scr, xb_scr, sem, xsem, cnt_ref):
    i = pl.program_id(0)
    tm = xa_scr.shape[0]

    def row_copy(idx_ref, r, buf, s):
        return pltpu.make_async_copy(h_hbm.at[pl.ds(idx_ref[0, 0, r], 1)], buf.at[pl.ds(r, 1)],
                                     xsem.at[s])
    d, f = cg_scr.shape[1], cg_scr.shape[2]
    n_g = d // _W_CHUNK
    n_chunks = 2 * n_g + f // _W_CHUNK

    def for_chunk(c, e, slot, fn):
        k = c % _W_RING

        @pl.when(c < n_g)
        def _():
            row = pl.multiple_of(c * _W_CHUNK, _W_CHUNK)
            fn(wg_hbm.at[layer, e, pl.ds(row, _W_CHUNK), :], stage_scr.at[k, :, pl.ds(0, f)],
               sem.at[k], cg_scr.at[slot, pl.ds(row, _W_CHUNK), :])

        @pl.when((c >= n_g) & (c < 2 * n_g))
        def _():
            row = pl.multiple_of((c - n_g) * _W_CHUNK, _W_CHUNK)
            fn(wu_hbm.at[layer, e, pl.ds(row, _W_CHUNK), :], stage_scr.at[k, :, pl.ds(0, f)],
               sem.at[k], cu_scr.at[slot, pl.ds(row, _W_CHUNK), :])

        @pl.when(c >= 2 * n_g)
        def _():
            row = pl.multiple_of((c - 2 * n_g) * _W_CHUNK, _W_CHUNK)
            fn(wd_hbm.at[layer, e, pl.ds(row, _W_CHUNK), :], stage_scr.at[k],
               sem.at[k], cd_scr.at[slot, pl.ds(row, _W_CHUNK), :])

    def start(src, stage, s, dst):
        pltpu.make_async_copy(src, stage, s).start(priority=1)

    def finish(src, stage, s, dst):
        pltpu.make_async_copy(src, stage, s).wait()
        dst[...] = stage[...].astype(BF16)

    def start_upto(e, slot, hi):
        hi = jnp.minimum(hi, n_chunks)

        def body(c, carry):
            for_chunk(c, e, slot, start)
            return carry

        lax.fori_loop(cnt_ref[0], hi, body, 0)
        cnt_ref[0] = jnp.maximum(cnt_ref[0], hi)

    def finish_started(e, slot):
        def body(c, carry):
            for_chunk(c, e, slot, finish)
            return carry

        lax.fori_loop(cnt_ref[1], cnt_ref[0], body, 0)
        cnt_ref[1] = cnt_ref[0]

    valid = i < nv_ref[0]

    @pl.when(i == 0)
    def _():
        cnt_ref[0] = 0
        cnt_ref[1] = 0

    @pl.when(valid & (first_ref[i] == 1))
    def _():
        e, slot = te_ref[i], slot_ref[i]

        def body(c, carry):
            start_upto(e, slot, c + _W_RING)
            for_chunk(c, e, slot, finish)
            return carry

        lax.fori_loop(cnt_ref[1], n_chunks, body, 0)
        cnt_ref[0] = 0
        cnt_ref[1] = 0

    @pl.when(valid & (nxt_ref[i] >= 0))
    def _():
        e, slot = nxt_ref[i], 1 - slot_ref[i]
        finish_started(e, slot)
        start_upto(e, slot, cnt_ref[0] + _W_RING)

    @pl.when(i == 0)
    def _():
        def body(r, carry):
            row_copy(src_ref, r, xa_scr, 0).start()
            return carry

        lax.fori_loop(0, tm, body, 0, unroll=_DMA_UNROLL)

    for par, (cur, nxt_buf) in enumerate(((xa_scr, xb_scr), (xb_scr, xa_scr))):
        @pl.when(valid & (i % 2 == par))
        def _(par=par, cur=cur, nxt_buf=nxt_buf):
            def wait_body(r, carry):
                row_copy(src_ref, r, cur, par).wait()
                return carry

            lax.fori_loop(0, tm, wait_body, 0, unroll=_DMA_UNROLL)
            slot = slot_ref[i]
            x = cur[...].astype(BF16)
            for r in range(tm):
                row_copy(srcn_ref, r, nxt_buf, 1 - par).start()
            a = _dot(x, cg_scr[slot])
            b = _dot(x, cu_scr[slot])
            hidden = (_silu(a) * b).astype(BF16)
            y_ref[...] = _dot(hidden, cd_scr[slot])

            @pl.when(i == nv_ref[0] - 1)
            def _():
                def drain_body(r, carry):
                    row_copy(srcn_ref, r, nxt_buf, 1 - par).wait()
                    return carry

                lax.fori_loop(0, tm, drain_body, 0, unroll=_DMA_UNROLL)

    @pl.when(jnp.logical_not(valid))
    def _():
        y_ref[...] = jnp.zeros_like(y_ref)


def _moe_schedule(tile_e, n_valid):
    n_tiles = tile_e.shape[0]
    idx = jnp.arange(n_tiles, dtype=I32)
    valid = idx < n_valid[0]
    first = valid & ((idx == 0) | (tile_e != jnp.roll(tile_e, 1)))
    run = jnp.cumsum(first.astype(I32)) - 1
    next_first = lax.cummin(jnp.where(first, idx, n_tiles), reverse=True)
    after = jnp.concatenate([next_first[1:], jnp.full((1,), n_tiles, I32)])
    nxt = jnp.where(valid & (after < n_tiles), tile_e[jnp.minimum(after, n_tiles - 1)], -1)
    return first.astype(I32), (run % 2).astype(I32), nxt.astype(I32)


def _moe_experts(h, src_rows, tile_e, n_valid, schedule, w_gate, w_up, w_down, layer):
    d = h.shape[1]
    f = w_gate.shape[3]
    tm = MOE_TILE
    n_tiles = tile_e.shape[0]
    assert d % _W_CHUNK == 0 and f % _W_CHUNK == 0
    first, slot, nxt = schedule
    src3 = src_rows.reshape(n_tiles, 1, tm)

    any_spec = pl.BlockSpec(memory_space=pl.ANY)
    grid_spec = pltpu.PrefetchScalarGridSpec(
        num_scalar_prefetch=5,
        grid=(n_tiles,),
        in_specs=[pl.BlockSpec((1, 1, tm), lambda i, *_: (i, 0, 0), memory_space=pltpu.SMEM),
                  pl.BlockSpec((1, 1, tm), lambda i, *_: (jnp.minimum(i + 1, n_tiles - 1), 0, 0),
                               memory_space=pltpu.SMEM),
                  any_spec, any_spec, any_spec, any_spec],
        out_specs=pl.BlockSpec((tm, d), lambda i, *_: (i, 0)),
        scratch_shapes=[pltpu.VMEM((2, d, f), BF16), pltpu.VMEM((2, d, f), BF16),
                        pltpu.VMEM((2, f, d), BF16),
                        pltpu.VMEM((_W_RING, _W_CHUNK, d), F32),
                        pltpu.VMEM((tm, d), F32), pltpu.VMEM((tm, d), F32),
                        pltpu.SemaphoreType.DMA((_W_RING,)), pltpu.SemaphoreType.DMA((2,)),
                        pltpu.SMEM((2,), I32)],
    )
    return pl.pallas_call(
        functools.partial(_moe_kernel, layer, n_tiles),
        grid_spec=grid_spec,
        out_shape=jax.ShapeDtypeStruct((n_tiles * tm, d), F32),
        compiler_params=_cparams(("arbitrary",), MOE_VMEM_LIMIT),
        name="moe_experts",
    )(tile_e, n_valid, first, slot, nxt, src3, src3, h, w_gate, w_up, w_down)


_COMBINE_ROWS = 256


def _combine_kernel(with_next, steps, p1_ref, p2_ref, p1n_ref, p2n_ref, ys_ref, w1_ref, w2_ref,
                    x_ref, gate_ref, g_ref, b_ref, *rest):
    if with_next:
        sc_ref, sh_ref, xo_ref, h_ref, ya_scr, yb_scr, sem = rest
    else:
        xo_ref, ya_scr, yb_scr, sem = rest
    step = pl.program_id(0)

    def copies(pa, pb, buf, s, r):
        return (pltpu.make_async_copy(ys_ref.at[pl.ds(pa[0, 0, r], 1)],
                                      buf.at[0, pl.ds(r, 1)], sem.at[s, 0]),
                pltpu.make_async_copy(ys_ref.at[pl.ds(pb[0, 0, r], 1)],
                                      buf.at[1, pl.ds(r, 1)], sem.at[s, 1]))

    def wait_all(pa, pb, buf, s):
        def body(r, carry):
            for cp in copies(pa, pb, buf, s, r):
                cp.wait()
            return carry
        lax.fori_loop(0, _COMBINE_ROWS, body, 0, unroll=_DMA_UNROLL)

    @pl.when(step == 0)
    def _():
        def body(r, carry):
            for cp in copies(p1_ref, p2_ref, ya_scr, 0, r):
                cp.start()
            return carry
        lax.fori_loop(0, _COMBINE_ROWS, body, 0, unroll=_DMA_UNROLL)

    for par, (cur, nxt) in enumerate(((ya_scr, yb_scr), (yb_scr, ya_scr))):
        @pl.when(step % 2 == par)
        def _(par=par, cur=cur, nxt=nxt):
            wait_all(p1_ref, p2_ref, cur, par)
            for r in range(_COMBINE_ROWS):
                for cp in copies(p1n_ref, p2n_ref, nxt, 1 - par, r):
                    cp.start()
            y = w1_ref[0] * cur[0] + w2_ref[0] * cur[1]
            z = DEEPNORM_ALPHA * x_ref[0] + (1.0 + gate_ref[0]) * y
            xn = _layer_norm_rows(z, g_ref[...], b_ref[...])
            xo_ref[0] = xn
            if with_next:
                h_ref[0] = (xn * (1.0 + sc_ref[0]) + sh_ref[0]).astype(BF16)

            @pl.when(step == steps - 1)
            def _():
                wait_all(p1n_ref, p2n_ref, nxt, 1 - par)


def _combine(ys, pos1, pos2, w1, w2, x, gate, ln_g, ln_b, next_mod):
    bsz, t, d = x.shape
    tr = _COMBINE_ROWS
    per_b = t // tr
    steps = bsz * per_b
    row = pl.BlockSpec((1, tr, d), lambda s: (s // per_b, s % per_b, 0))
    vec = pl.BlockSpec((1, d), lambda s: (0, 0))
    mod = pl.BlockSpec((1, 1, d), lambda s: (s // per_b, 0, 0))
    wcol = pl.BlockSpec((1, tr, 1), lambda s: (s // per_b, s % per_b, 0))
    idx = pl.BlockSpec((1, 1, tr), lambda s: (s, 0, 0), memory_space=pltpu.SMEM)
    idx_next = pl.BlockSpec((1, 1, tr), lambda s: (jnp.minimum(s + 1, steps - 1), 0, 0),
                            memory_space=pltpu.SMEM)
    with_next = next_mod is not None
    p1 = pos1.reshape(steps, 1, tr)
    p2 = pos2.reshape(steps, 1, tr)
    in_specs = [idx, idx, idx_next, idx_next, pl.BlockSpec(memory_space=pl.ANY),
                wcol, wcol, row, mod, vec, vec]
    args = [p1, p2, p1, p2, ys, w1.reshape(bsz, t, 1), w2.reshape(bsz, t, 1), x, gate,
            ln_g.reshape(1, d), ln_b.reshape(1, d)]
    out_specs = [row]
    out_shape = [jax.ShapeDtypeStruct((bsz, t, d), F32)]
    if with_next:
        in_specs += [mod, mod]
        args += list(next_mod)
        out_specs.append(row)
        out_shape.append(jax.ShapeDtypeStruct((bsz, t, d), BF16))
    res = pl.pallas_call(
        functools.partial(_combine_kernel, with_next, steps),
        grid=(steps,),
        in_specs=in_specs,
        out_specs=out_specs,
        out_shape=out_shape,
        scratch_shapes=[pltpu.VMEM((2, tr, d), F32), pltpu.VMEM((2, tr, d), F32),
                        pltpu.SemaphoreType.DMA((2, 2))],
        compiler_params=_cparams(("arbitrary",), VMEM_LIMIT),
        name="combine_ln",
    )(*args)
    return res if with_next else (res[0], None)


def kernel(x, c, emb_ln_g, emb_ln_b, w_ada, b_ada, w_in, conv_w, mix_beta, w_out, hg_lb_logits,
           ln_g, ln_b, w_router, router_bias, w_gate, w_up, w_down):
    bsz, t, d = x.shape
    n = bsz * t
    assert n % LANES == 0 and t % SEQ_CHUNK == 0
    n_tiles = 2 * n // MOE_TILE + N_EXPERTS
    assert n_tiles <= LANES

    mod = _ada_mod(c, w_ada, b_ada)
    mod = mod.reshape(DEPTH, bsz, 6, 1, d)

    def mods(l):
        return [mod[l, :, i] for i in range(6)]

    wr_hi = jnp.pad(w_router, ((0, 0), (0, LANES - N_EXPERTS)))
    wr_hi16 = wr_hi.astype(BF16)
    wr_lo16 = (wr_hi - wr_hi16.astype(F32)).astype(BF16)

    w_out16 = w_out.astype(BF16)

    m = [mods(l) for l in range(DEPTH)]
    xcur, h = _ln_mod(x, emb_ln_g, emb_ln_b, m[0][1], m[0][0])
    for l in range(DEPTH):
        _, _, gate1, shift2, scale2, gate2 = m[l]
        beta = mix_beta[l].reshape(1, -1)
        proj = _in_proj(h.reshape(n, d), w_in, l, BF16).reshape(bsz, t, IN_COLS)
        yc = _conv_mixer(proj, conv_w[l], beta)
        yr = _ret_mixer(proj, beta)
        yh = _hg_mixer(proj, hg_lb_logits, beta, l)
        x1, h2, logits = _out_proj(yc, yr, yh, w_out16, l, xcur, gate1,
                                   ln_g[l, 0], ln_b[l, 0], scale2, shift2, wr_hi16, wr_lo16)
        logits_t = logits.reshape(n, LANES)[:, :N_EXPERTS].T.reshape(N_EXPERTS, n // LANES, LANES)
        pos1, pos2, w1, w2, tiles = _route(logits_t, router_bias, n_tiles)
        tile_e, n_valid = tiles[0, :n_tiles], tiles[1, :1]
        src_rows = _sorted_sources(pos1.reshape(n), pos2.reshape(n), n_tiles * MOE_TILE)
        ys = _moe_experts(h2.reshape(n, d), src_rows, tile_e, n_valid,
                          _moe_schedule(tile_e, n_valid), w_gate, w_up, w_down, l)
        next_mod = (m[l + 1][1], m[l + 1][0]) if l + 1 < DEPTH else None
        xcur, h = _combine(ys, pos1.reshape(n), pos2.reshape(n), w1, w2, x1, gate2,
                           ln_g[l, 1], ln_b[l, 1], next_mod)
    return xcur
```

```python
import functools
import math

import jax
import jax.numpy as jnp
from jax import lax
from jax.experimental import pallas as pl
from jax.experimental.pallas import tpu as pltpu

F32 = jnp.float32
BF16 = jnp.bfloat16
I32 = jnp.int32

DEPTH = 2
CONV_WIDTH = 512
RET_WIDTH = 768
HG_WIDTH = 768
HEAD_DIM = 128
RET_HEADS = RET_WIDTH // HEAD_DIM
HG_HEADS = HG_WIDTH // HEAD_DIM
IN_COLS = 3 * CONV_WIDTH + 4 * RET_WIDTH + 5 * HG_WIDTH
ROPE_BASE = 10000.0
N_EXPERTS = 16
N_GROUPS = 4
EXPERTS_PER_GROUP = N_EXPERTS // N_GROUPS
MASK_NEG = -1e9
DEEPNORM_ALPHA = (2.0 * DEPTH) ** 0.25
LN_EPS = 1e-5
HEAD_EPS = 1e-6
FORGET_FLOOR = 1e-6

LANES = 128
SUBLANES = 8

_CB, _CC, _CH = 0, 4, 8
_RQ, _RK, _RV, _RG = 12, 18, 24, 30
_HQ, _HFF, _HFB, _HI, _HG = 36, 42, 48, 54, 60
_OUT_RET, _OUT_HG = 4, 10

SEQ_CHUNK = 128
MOE_TILE = 256
VMEM_LIMIT = 56 * 1024 * 1024
MOE_VMEM_LIMIT = 60 * 1024 * 1024


def _cparams(sem, vmem=None):
    return pltpu.CompilerParams(dimension_semantics=sem, vmem_limit_bytes=vmem)


def _silu(x):
    return x * jax.nn.sigmoid(x)


def _dot(a, b):
    return jnp.dot(a, b, preferred_element_type=F32)


def _dot_nt(a, b):
    return lax.dot_general(a, b, (((1,), (1,)), ((), ())), preferred_element_type=F32)


def _dot_tn(a, b):
    return lax.dot_general(a, b, (((0,), (0,)), ((), ())), preferred_element_type=F32)


def _ada_kernel(c_ref, w_ref, b_ref, o_ref):
    cond = _silu(c_ref[...])
    o_ref[0] = _dot(cond.astype(BF16), w_ref[0].astype(BF16)) + b_ref[0]


def _ada_mod(c, w_ada, b_ada):
    depth, d, n6 = w_ada.shape
    b = c.shape[0]
    bp = -(-b // SUBLANES) * SUBLANES
    cp = jnp.pad(c, ((0, bp - b), (0, 0)))
    tn = 512
    out = pl.pallas_call(
        _ada_kernel,
        grid=(depth, n6 // tn),
        in_specs=[
            pl.BlockSpec((bp, d), lambda l, j: (0, 0)),
            pl.BlockSpec((1, d, tn), lambda l, j: (l, 0, j)),
            pl.BlockSpec((1, 1, tn), lambda l, j: (l, 0, j)),
        ],
        out_specs=pl.BlockSpec((1, bp, tn), lambda l, j: (l, 0, j)),
        out_shape=jax.ShapeDtypeStruct((depth, bp, n6), F32),
        compiler_params=_cparams(("arbitrary", "arbitrary")),
        name="ada_mod",
    )(cp, w_ada, b_ada.reshape(depth, 1, n6))
    return out[:, :b, :]


def _layer_norm_rows(z, g, b):
    mu = jnp.mean(z, axis=-1, keepdims=True)
    zc = z - mu
    var = jnp.mean(zc * zc, axis=-1, keepdims=True)
    return zc * lax.rsqrt(var + LN_EPS) * g + b


def _ln_mod_kernel(x_ref, g_ref, b_ref, sc_ref, sh_ref, xo_ref, h_ref):
    y = _layer_norm_rows(x_ref[0], g_ref[...], b_ref[...])
    xo_ref[0] = y
    h_ref[0] = (y * (1.0 + sc_ref[0]) + sh_ref[0]).astype(BF16)


def _ln_mod(x, g, b, scale, shift):
    bsz, t, d = x.shape
    tr = 512
    row = pl.BlockSpec((1, tr, d), lambda i, j: (i, j, 0))
    vec = pl.BlockSpec((1, d), lambda i, j: (0, 0))
    mod = pl.BlockSpec((1, 1, d), lambda i, j: (i, 0, 0))
    return pl.pallas_call(
        _ln_mod_kernel,
        grid=(bsz, t // tr),
        in_specs=[row, vec, vec, mod, mod],
        out_specs=[row, row],
        out_shape=[jax.ShapeDtypeStruct((bsz, t, d), F32),
                   jax.ShapeDtypeStruct((bsz, t, d), BF16)],
        compiler_params=_cparams(("arbitrary", "arbitrary")),
        name="ln_mod",
    )(x, g.reshape(1, d), b.reshape(1, d), scale, shift)


def _in_proj_kernel(a_ref, w_ref, o_ref, w16_scr):
    @pl.when(pl.program_id(1) == 0)
    def _():
        w16_scr[...] = w_ref[0].astype(BF16)

    o_ref[...] = _dot(a_ref[...], w16_scr[...]).astype(o_ref.dtype)


def _in_proj(h, w_in, layer, out_dtype):
    n, d = h.shape
    nc = w_in.shape[2]
    tm, tn = 1024, 1408
    tm = min(tm, n)
    return pl.pallas_call(
        _in_proj_kernel,
        grid=(nc // tn, n // tm),
        in_specs=[pl.BlockSpec((tm, d), lambda j, i: (i, 0)),
                  pl.BlockSpec((1, d, tn), lambda j, i: (layer, 0, j))],
        out_specs=pl.BlockSpec((tm, tn), lambda j, i: (i, j)),
        out_shape=jax.ShapeDtypeStruct((n, nc), out_dtype),
        scratch_shapes=[pltpu.VMEM((d, tn), BF16)],
        compiler_params=_cparams(("arbitrary", "arbitrary"), VMEM_LIMIT),
        name="in_proj",
    )(h, w_in)


def _conv_kernel(cb_ref, cc_ref, ch_ref, w_ref, beta_ref, o_ref):
    u = cc_ref[0].astype(F32) * ch_ref[0].astype(F32)
    t = u.shape[0]
    row = lax.broadcasted_iota(I32, u.shape, 0)
    prev = jnp.where(row == 0, 0.0, pltpu.roll(u, 1, 0))
    nxt = jnp.where(row == t - 1, 0.0, pltpu.roll(u, t - 1, 0))
    w = w_ref[...]
    y = cb_ref[0].astype(F32) * (prev * w[0:1] + u * w[1:2] + nxt * w[2:3])
    o_ref[0] = (y * beta_ref[...]).astype(o_ref.dtype)


def _conv_mixer(proj, conv_w, beta):
    bsz, t, _ = proj.shape
    cw = 256
    nb = CONV_WIDTH // cw
    per = LANES * 1

    def col(off):
        return pl.BlockSpec((1, t, cw), lambda b, j, off=off: (b, 0, off * per // cw + j))

    return pl.pallas_call(
        _conv_kernel,
        grid=(bsz, nb),
        in_specs=[col(_CB), col(_CC), col(_CH),
                  pl.BlockSpec((3, cw), lambda b, j: (0, j)),
                  pl.BlockSpec((1, cw), lambda b, j: (0, j))],
        out_specs=pl.BlockSpec((1, t, cw), lambda b, j: (b, 0, j)),
        out_shape=jax.ShapeDtypeStruct((bsz, t, CONV_WIDTH), BF16),
        compiler_params=_cparams(("arbitrary", "arbitrary"), VMEM_LIMIT),
        name="conv_mixer",
    )(proj, proj, proj, conv_w, beta)


def _ret_kernel(q_ref, k_ref, v_ref, g_ref, cos_ref, sin_ref, lg_ref, beta_ref,
                o_ref, o_scr, qf_scr, qb_scr, uf_scr, ub_scr):
    L = SEQ_CHUNK
    t = q_ref.shape[1]
    n = t // L
    lgf = lg_ref[0, 0:1, :]
    lgb = lg_ref[0, 1:2, :]
    r = lax.broadcasted_iota(I32, (L, L), 0).astype(F32)
    c = lax.broadcasted_iota(I32, (L, L), 1).astype(F32)
    rel = r - c
    dbi = jnp.where(rel > 0, jnp.exp(lgf * jnp.maximum(rel, 0.0)),
                    jnp.where(rel < 0, jnp.exp(lgb * jnp.maximum(-rel, 0.0)), 2.0))
    qf = jnp.exp(lgf * (r + 1.0))
    kf = jnp.exp(lgf * (L - 1.0 - r))
    qb = jnp.exp(lgb * (L - r))
    kb = jnp.exp(lgb * r)
    gf_l = jnp.exp(lgf * float(L))
    gb_l = jnp.exp(lgb * float(L))
    scale = HEAD_DIM ** -0.5

    def rot(x, cs, sn):
        return x * cs + pltpu.roll(x, HEAD_DIM // 2, 1) * sn

    def rows(ci):
        return pl.ds(pl.multiple_of(ci * L, L), L)

    def local_body(ci, carry):
        sl = rows(ci)
        cs = cos_ref[sl, :]
        sn = sin_ref[sl, :]
        q = rot(q_ref[0, sl, :].astype(F32), cs, sn)
        k = rot(k_ref[0, sl, :].astype(F32), cs, sn) * scale
        v16 = v_ref[0, sl, :].astype(BF16)
        s = _dot_nt(q.astype(BF16), k.astype(BF16)) * dbi
        o_scr[sl, :] = _dot(s.astype(BF16), v16)
        qf_scr[sl, :] = (q * qf).astype(BF16)
        qb_scr[sl, :] = (q * qb).astype(BF16)
        uf_scr[ci] = _dot_tn((k * kf).astype(BF16), v16)
        ub_scr[ci] = _dot_tn((k * kb).astype(BF16), v16)
        return carry

    lax.fori_loop(0, n, local_body, 0, unroll=4)

    def scan_body(i, states):
        sf, sb = states
        cf, cb = i, n - 1 - i
        slf, slb = rows(cf), rows(cb)
        o_scr[slf, :] += _dot(qf_scr[slf, :], sf.astype(BF16))
        o_scr[slb, :] += _dot(qb_scr[slb, :], sb.astype(BF16))
        return gf_l * sf + uf_scr[cf], gb_l * sb + ub_scr[cb]

    zero = jnp.zeros((HEAD_DIM, HEAD_DIM), F32)
    lax.fori_loop(0, n, scan_body, (zero, zero), unroll=8)

    beta = beta_ref[...]

    def norm_body(ci, carry):
        sl = rows(ci)
        o = o_scr[sl, :]
        mu = jnp.mean(o, axis=-1, keepdims=True)
        oc = o - mu
        var = jnp.mean(oc * oc, axis=-1, keepdims=True)
        y = oc * lax.rsqrt(var + HEAD_EPS) * _silu(g_ref[0, sl, :].astype(F32)) * beta
        o_ref[0, sl, :] = y.astype(o_ref.dtype)
        return carry

    lax.fori_loop(0, n, norm_body, 0, unroll=4)


def _rotary_tables(t):
    half = HEAD_DIM // 2
    inv_freq = ROPE_BASE ** (-jnp.arange(half, dtype=F32) / half)
    ang = jnp.arange(t, dtype=F32)[:, None] * inv_freq[None, :]
    cos, sin = jnp.cos(ang), jnp.sin(ang)
    return jnp.concatenate([cos, cos], -1), jnp.concatenate([-sin, sin], -1)


def _ret_log_decays():
    head = jnp.arange(RET_HEADS, dtype=F32)
    lg_f = jnp.log1p(-jnp.exp2(-5.0 - head))
    lg_b = jnp.log1p(-jnp.exp2(-5.5 - head))
    lg = jnp.stack([lg_f, lg_b], axis=1)
    return jnp.broadcast_to(lg[:, :, None], (RET_HEADS, 2, LANES))


def _ret_mixer(proj, beta):
    bsz, t, _ = proj.shape
    n_chunks = t // SEQ_CHUNK
    assert n_chunks % 2 == 0
    cosf, sinf = _rotary_tables(t)

    def col(off):
        return pl.BlockSpec((1, t, HEAD_DIM), lambda b, h, off=off: (b, 0, off + h))

    tab = pl.BlockSpec((t, HEAD_DIM), lambda b, h: (0, 0))
    return pl.pallas_call(
        _ret_kernel,
        grid=(bsz, RET_HEADS),
        in_specs=[col(_RQ), col(_RK), col(_RV), col(_RG), tab, tab,
                  pl.BlockSpec((1, 2, LANES), lambda b, h: (h, 0, 0)),
                  pl.BlockSpec((1, HEAD_DIM), lambda b, h: (0, _OUT_RET + h))],
        out_specs=pl.BlockSpec((1, t, HEAD_DIM), lambda b, h: (b, 0, h)),
        out_shape=jax.ShapeDtypeStruct((bsz, t, RET_WIDTH), BF16),
        scratch_shapes=[pltpu.VMEM((t, HEAD_DIM), F32),
                        pltpu.VMEM((t, HEAD_DIM), BF16), pltpu.VMEM((t, HEAD_DIM), BF16),
                        pltpu.VMEM((n_chunks, HEAD_DIM, HEAD_DIM), F32),
                        pltpu.VMEM((n_chunks, HEAD_DIM, HEAD_DIM), F32)],
        compiler_params=_cparams(("arbitrary", "arbitrary"), VMEM_LIMIT),
        name="ret_mixer",
    )(proj, proj, proj, proj, cosf, sinf, _ret_log_decays(), beta)


_HG_BASE = SUBLANES
_HG_LEVELS = tuple(m for m in (8, 16, 32, 64) if m < SEQ_CHUNK)


def _cumsum_rows(x, reverse):
    n = x.shape[0]
    row = lax.broadcasted_iota(I32, x.shape, 0)
    sh = 1
    while sh < n:
        if reverse:
            x = x + jnp.where(row < n - sh, pltpu.roll(x, n - sh, 0), 0.0)
        else:
            x = x + jnp.where(row >= sh, pltpu.roll(x, sh, 0), 0.0)
        sh *= 2
    return x


def _hg_masks(reverse):
    C = SEQ_CHUNK
    r = lax.broadcasted_iota(I32, (C, C), 0)
    c = lax.broadcasted_iota(I32, (C, C), 1)
    if reverse:
        r, c = c, r
    sh = _HG_BASE.bit_length() - 1
    masks = [((r >> sh) == (c >> sh)) & (c <= r)]
    for m in _HG_LEVELS:
        sh = m.bit_length() - 1
        masks.append(((r >> (sh + 1)) == (c >> (sh + 1)))
                     & (((r >> sh) & 1) == 1) & (((c >> sh) & 1) == 0))
    return [jnp.where(m, 1.0, 0.0) for m in masks]


def _hg_scores(q, kk, cum, mask_ref, reverse):
    C = SEQ_CHUNK
    d = 1 if reverse else 0

    def split(x, blk):
        return x.reshape(C // blk, blk, LANES)

    b = _HG_BASE
    ref_row = b // 2 if reverse else b // 2 - 1
    cum3 = split(cum, b)
    rel = cum3 - cum3[:, ref_row:ref_row + 1, :]
    qt = (split(q, b) * jnp.exp(rel)).reshape(C, LANES).astype(BF16)
    kt = (split(kk, b) * jnp.exp(-rel)).reshape(C, LANES).astype(BF16)
    a = jnp.where(mask_ref[d, 0] > 0.5, _dot_nt(qt, kt), 0.0)
    for li, m in enumerate(_HG_LEVELS):
        blk = 2 * m
        ref_row = m if reverse else m - 1
        cum3 = split(cum, blk)
        rel = cum3 - cum3[:, ref_row:ref_row + 1, :]
        dec = jnp.exp(-jnp.abs(rel)).reshape(C, LANES)
        halves = []
        for j in range(C // m):
            is_query = (j % 2 == 1) != reverse
            halves.append((q if is_query else kk)[j * m:(j + 1) * m])
        x = (jnp.concatenate(halves, axis=0) * dec).astype(BF16)
        a = a + mask_ref[d, li + 1] * _dot_nt(x, x)
    return a


def _hg_kernel(layer, q_ref, zf_ref, zb_ref, v_ref, g_ref, lbl_ref, beta_ref, o_ref,
               o_scr, qe_scr, u_scr, dec_scr, mask_scr):
    C = SEQ_CHUNK
    t = q_ref.shape[1]
    n = t // C
    for d, reverse in enumerate((False, True)):
        for li, m in enumerate(_hg_masks(reverse)):
            mask_scr[d, li] = m
    logits = lbl_ref[...].astype(F32)
    e = jnp.exp(logits - jnp.max(logits, axis=0, keepdims=True))
    p = e / jnp.sum(e, axis=0, keepdims=True)
    lb = p[0]
    for l in range(1, layer + 1):
        lb = lb + p[l]
    lb = lb - p[0]

    def gates(z_ref, sl, lb_row):
        f = lb_row + (1.0 - lb_row) * jax.nn.sigmoid(z_ref[0, sl, :].astype(F32))
        f = jnp.maximum(f, FORGET_FLOOR)
        return jnp.log(f), 1.0 - f

    def rows(ci):
        return pl.ds(pl.multiple_of(ci * C, C), C)

    def local_dir(ci, sl, q, v16, z_ref, d):
        reverse = d == 1
        logf, kk = gates(z_ref, sl, lb[d:d + 1, :])
        cum = _cumsum_rows(logf, reverse)
        a = _hg_scores(q, kk, cum, mask_scr, reverse)
        total = cum[0:1, :] if reverse else cum[C - 1:C, :]
        qe_scr[d, sl, :] = (q * jnp.exp(cum)).astype(BF16)
        k_tail = kk * jnp.exp(total - cum)
        u_scr[d, ci] = _dot_tn(v16, k_tail.astype(BF16))
        dec_scr[d, ci] = jnp.broadcast_to(jnp.exp(total), (SUBLANES, LANES))
        return _dot(a.astype(BF16), v16)

    def local_body(ci, carry):
        sl = rows(ci)
        q = q_ref[0, sl, :].astype(F32)
        v16 = v_ref[0, sl, :].astype(BF16)
        o_scr[sl, :] = (local_dir(ci, sl, q, v16, zf_ref, 0)
                        + local_dir(ci, sl, q, v16, zb_ref, 1))
        return carry

    lax.fori_loop(0, n, local_body, 0, unroll=2)

    def scan_body(i, states):
        sf, sb = states
        cf, cb = i, n - 1 - i
        slf, slb = rows(cf), rows(cb)
        o_scr[slf, :] += _dot_nt(qe_scr[0, slf, :], sf.astype(BF16))
        o_scr[slb, :] += _dot_nt(qe_scr[1, slb, :], sb.astype(BF16))
        return (sf * dec_scr[0, cf, 0:1, :] + u_scr[0, cf],
                sb * dec_scr[1, cb, 0:1, :] + u_scr[1, cb])

    zero = jnp.zeros((HEAD_DIM, HEAD_DIM), F32)
    lax.fori_loop(0, n, scan_body, (zero, zero), unroll=8)

    beta = beta_ref[...]

    def norm_body(ci, carry):
        sl = rows(ci)
        o = o_scr[sl, :]
        ms = jnp.mean(o * o, axis=-1, keepdims=True)
        y = o * lax.rsqrt(ms + HEAD_EPS) * _silu(g_ref[0, sl, :].astype(F32)) * beta
        o_ref[0, sl, :] = y.astype(o_ref.dtype)
        return carry

    lax.fori_loop(0, n, norm_body, 0, unroll=8)


def _hg_mixer(proj, lb_logits, beta, layer):
    bsz, t, _ = proj.shape
    n_chunks = t // SEQ_CHUNK
    assert n_chunks % 2 == 0

    def col(off):
        return pl.BlockSpec((1, t, HEAD_DIM), lambda b, h, off=off: (b, 0, off + h))

    return pl.pallas_call(
        functools.partial(_hg_kernel, layer),
        grid=(bsz, HG_HEADS),
        in_specs=[col(_HQ), col(_HFF), col(_HFB), col(_HI), col(_HG),
                  pl.BlockSpec((DEPTH, 2, HEAD_DIM), lambda b, h: (0, 0, h)),
                  pl.BlockSpec((1, HEAD_DIM), lambda b, h: (0, _OUT_HG + h))],
        out_specs=pl.BlockSpec((1, t, HEAD_DIM), lambda b, h: (b, 0, h)),
        out_shape=jax.ShapeDtypeStruct((bsz, t, HG_WIDTH), BF16),
        scratch_shapes=[pltpu.VMEM((t, HEAD_DIM), F32),
                        pltpu.VMEM((2, t, HEAD_DIM), BF16),
                        pltpu.VMEM((2, n_chunks, HEAD_DIM, HEAD_DIM), F32),
                        pltpu.VMEM((2, n_chunks, SUBLANES, LANES), F32),
                        pltpu.VMEM((2, 1 + len(_HG_LEVELS), SEQ_CHUNK, SEQ_CHUNK), F32)],
        compiler_params=_cparams(("arbitrary", "arbitrary"), VMEM_LIMIT),
        name="hg_mixer",
    )(proj, proj, proj, proj, proj, lb_logits, beta)


_OUT_PROJ_SPLIT = 2


def _out_proj_kernel(yc_ref, yr_ref, yh_ref, w_ref, x_ref, gate_ref, g_ref, b_ref,
                     sc_ref, sh_ref, wrh_ref, wrl_ref, xo_ref, h_ref, lo_ref):
    c0, c1 = CONV_WIDTH, CONV_WIDTH + RET_WIDTH
    tm = x_ref.shape[1]
    for half in range(_OUT_PROJ_SPLIT):
        sl = pl.ds(half * (tm // _OUT_PROJ_SPLIT), tm // _OUT_PROJ_SPLIT)
        y = (_dot(yc_ref[0, sl, :], w_ref[0, 0:c0, :]) + _dot(yr_ref[0, sl, :], w_ref[0, c0:c1, :])
             + _dot(yh_ref[0, sl, :], w_ref[0, c1:, :]))
        z = DEEPNORM_ALPHA * x_ref[0, sl, :] + (1.0 + gate_ref[0]) * y
        xn = _layer_norm_rows(z, g_ref[...], b_ref[...])
        xo_ref[0, sl, :] = xn
        h = xn * (1.0 + sc_ref[0]) + sh_ref[0]
        h_ref[0, sl, :] = h
        h_hi = h.astype(BF16)
        h_lo = (h - h_hi.astype(F32)).astype(BF16)
        wrh = wrh_ref[...]
        lo_ref[0, sl, :] = _dot(h_hi, wrh) + _dot(h_lo, wrh) + _dot(h_hi, wrl_ref[...])


def _out_proj(yc, yr, yh, w_out16, layer, x, gate, ln_g, ln_b, scale2, shift2, wr_hi, wr_lo):
    bsz, t, d = x.shape
    tm = 512

    def row(width):
        return pl.BlockSpec((1, tm, width), lambda b, i: (b, i, 0))

    vec = pl.BlockSpec((1, d), lambda b, i: (0, 0))
    mod = pl.BlockSpec((1, 1, d), lambda b, i: (b, 0, 0))
    wr = pl.BlockSpec((d, LANES), lambda b, i: (0, 0))
    return pl.pallas_call(
        _out_proj_kernel,
        grid=(bsz, t // tm),
        in_specs=[row(CONV_WIDTH), row(RET_WIDTH), row(HG_WIDTH),
                  pl.BlockSpec((1, d, d), lambda b, i: (layer, 0, 0)),
                  row(d), mod, vec, vec, mod, mod, wr, wr],
        out_specs=[row(d), row(d), row(LANES)],
        out_shape=[jax.ShapeDtypeStruct((bsz, t, d), F32),
                   jax.ShapeDtypeStruct((bsz, t, d), F32),
                   jax.ShapeDtypeStruct((bsz, t, LANES), F32)],
        compiler_params=_cparams(("arbitrary", "arbitrary"), VMEM_LIMIT),
        name="out_proj",
    )(yc, yr, yh, w_out16, x, gate, ln_g.reshape(1, d), ln_b.reshape(1, d),
      scale2, shift2, wr_hi, wr_lo)


def _route_kernel(n_tiles, lt_ref, bias_ref, pos1_ref, pos2_ref, w1_ref, w2_ref, tile_ref):
    E = N_EXPERTS
    logits = [lt_ref[e] for e in range(E)]
    shape = logits[0].shape
    mx = functools.reduce(jnp.maximum, logits)
    ex = [jnp.exp(l - mx) for l in logits]
    den = functools.reduce(lambda a, b: a + b, ex)
    scores = [x / den for x in ex]
    sel = [scores[e] + bias_ref[e] for e in range(E)]

    best_g = jnp.zeros(shape, I32)
    best_v = None
    for g in range(N_GROUPS):
        a, b, c, d = sel[EXPERTS_PER_GROUP * g: EXPERTS_PER_GROUP * (g + 1)]
        hi1, lo1 = jnp.maximum(a, b), jnp.minimum(a, b)
        hi2, lo2 = jnp.maximum(c, d), jnp.minimum(c, d)
        gs = jnp.maximum(hi1, hi2) + jnp.maximum(jnp.minimum(hi1, hi2), jnp.maximum(lo1, lo2))
        if g == 0:
            best_v = gs
        else:
            better = gs > best_v
            best_g = jnp.where(better, g, best_g)
            best_v = jnp.where(better, gs, best_v)

    masked = [jnp.where(best_g == (e // EXPERTS_PER_GROUP), sel[e], MASK_NEG) for e in range(E)]

    def arg_top(vals, exclude):
        idx = jnp.zeros(shape, I32)
        val = None
        for e in range(E):
            v = vals[e] if exclude is None else jnp.where(exclude == e, -jnp.inf, vals[e])
            if e == 0:
                val = v
            else:
                better = v > val
                idx = jnp.where(better, e, idx)
                val = jnp.where(better, v, val)
        return idx

    idx1 = arg_top(masked, None)
    idx2 = arg_top(masked, idx1)
    s1 = functools.reduce(lambda a, b: a + b, [jnp.where(idx1 == e, scores[e], 0.0) for e in range(E)])
    s2 = functools.reduce(lambda a, b: a + b, [jnp.where(idx2 == e, scores[e], 0.0) for e in range(E)])
    w1_ref[...] = s1 / (s1 + s2)
    w2_ref[...] = s2 / (s1 + s2)

    rows = shape[0]
    li = lax.broadcasted_iota(I32, (LANES, LANES), 0)
    lj = lax.broadcasted_iota(I32, (LANES, LANES), 1)
    upper = (li <= lj).astype(BF16)
    ri = lax.broadcasted_iota(I32, (rows, rows), 0)
    rj = lax.broadcasted_iota(I32, (rows, rows), 1)
    lower = (rj < ri).astype(BF16)
    tile_start = (lax.broadcasted_iota(I32, (1, LANES), 1) * MOE_TILE).astype(F32)
    start = jnp.zeros((1, 1), F32)
    pos1 = jnp.zeros(shape, F32)
    pos2 = jnp.zeros(shape, F32)
    tile_e = jnp.zeros((1, LANES), F32)
    for e in range(E):
        hit1 = idx1 == e
        hit2 = idx2 == e
        onehot = jnp.where(hit1 | hit2, 1.0, 0.0)
        pref = _dot(onehot.astype(BF16), upper)
        row_tot = pref[:, LANES - 1:LANES]
        row_off = _dot(lower, jnp.broadcast_to(row_tot, shape).astype(BF16))[:, 0:1]
        rank = pref - onehot + row_off
        count = jnp.sum(onehot, keepdims=True)
        dest = start + rank
        pos1 = jnp.where(hit1, dest, pos1)
        pos2 = jnp.where(hit2, dest, pos2)
        start = start + jnp.ceil(count / MOE_TILE) * MOE_TILE
        tile_e = tile_e + jnp.where(tile_start >= start, 1.0, 0.0)
    pos1_ref[...] = pos1.astype(I32)
    pos2_ref[...] = pos2.astype(I32)
    n_valid = start / MOE_TILE
    tile_id = lax.broadcasted_iota(I32, (1, LANES), 1).astype(F32)
    last_e = jnp.sum(jnp.where(tile_id == n_valid - 1.0, tile_e, 0.0), keepdims=True)
    tile_e = jnp.where(tile_id < n_valid, tile_e, last_e)
    sub = lax.broadcasted_iota(I32, (SUBLANES, LANES), 0)
    tile_ref[...] = jnp.where(sub == 0, tile_e, jnp.broadcast_to(n_valid, (SUBLANES, LANES))).astype(I32)


def _route(logits_t, router_bias, n_tiles):
    e, rows, lanes = logits_t.shape
    tok = jax.ShapeDtypeStruct((rows, lanes), I32)
    tokf = jax.ShapeDtypeStruct((rows, lanes), F32)
    full = pl.BlockSpec((rows, lanes), lambda i: (0, 0))
    return pl.pallas_call(
        functools.partial(_route_kernel, n_tiles),
        grid=(1,),
        in_specs=[pl.BlockSpec((e, rows, lanes), lambda i: (0, 0, 0)),
                  pl.BlockSpec(memory_space=pltpu.SMEM)],
        out_specs=[full, full, full, full, pl.BlockSpec((SUBLANES, LANES), lambda i: (0, 0))],
        out_shape=[tok, tok, tokf, tokf, jax.ShapeDtypeStruct((SUBLANES, LANES), I32)],
        compiler_params=_cparams(("arbitrary",)),
        name="route",
    )(logits_t, router_bias)


_ROWS_PER_STEP = 512


_DMA_UNROLL = 8


def _scatter_rows_kernel(n_tiles, p1_ref, p2_ref, fill_ref, src_ref, dst_ref, zero_scr, sem, zsem):
    def zero_copy(i):
        row0 = pl.multiple_of(i * MOE_TILE, MOE_TILE)
        return pltpu.make_async_copy(zero_scr, dst_ref.at[pl.ds(row0, MOE_TILE)], zsem)

    @pl.when(pl.program_id(0) == 0)
    def _():
        zero_scr[...] = jnp.zeros_like(zero_scr)

        def zstart(i, carry):
            @pl.when(fill_ref[i] == 1)
            def _():
                zero_copy(i).start()
            return carry

        def zwait(i, carry):
            @pl.when(fill_ref[i] == 1)
            def _():
                zero_copy(i).wait()
            return carry

        lax.fori_loop(0, n_tiles, zstart, 0)
        lax.fori_loop(0, n_tiles, zwait, 0)

    def copies(r):
        row = src_ref.at[pl.ds(r, 1)]
        return (pltpu.make_async_copy(row, dst_ref.at[pl.ds(p1_ref[0, 0, r], 1)], sem.at[0]),
                pltpu.make_async_copy(row, dst_ref.at[pl.ds(p2_ref[0, 0, r], 1)], sem.at[1]))

    def start(r, carry):
        for queue, cp in enumerate(copies(r)):
            cp.start(priority=queue)
        return carry

    def wait(r, carry):
        for cp in copies(r):
            cp.wait()
        return carry

    lax.fori_loop(0, _ROWS_PER_STEP, start, 0, unroll=_DMA_UNROLL)
    lax.fori_loop(0, _ROWS_PER_STEP, wait, 0, unroll=_DMA_UNROLL)


def _scatter_rows(src, pos1, pos2, fill_tile, n_tiles):
    n, d = src.shape
    steps = n // _ROWS_PER_STEP
    idx = pl.BlockSpec((1, 1, _ROWS_PER_STEP), lambda i: (i, 0, 0), memory_space=pltpu.SMEM)
    return pl.pallas_call(
        functools.partial(_scatter_rows_kernel, n_tiles),
        grid=(steps,),
        in_specs=[idx, idx, pl.BlockSpec(memory_space=pltpu.SMEM),
                  pl.BlockSpec((_ROWS_PER_STEP, d), lambda i: (i, 0))],
        out_specs=pl.BlockSpec(memory_space=pl.ANY),
        out_shape=jax.ShapeDtypeStruct((n_tiles * MOE_TILE, d), src.dtype),
        scratch_shapes=[pltpu.VMEM((MOE_TILE, d), src.dtype),
                        pltpu.SemaphoreType.DMA((2,)), pltpu.SemaphoreType.DMA],
        compiler_params=_cparams(("arbitrary",), VMEM_LIMIT),
        name="scatter_rows",
    )(pos1.reshape(steps, 1, _ROWS_PER_STEP), pos2.reshape(steps, 1, _ROWS_PER_STEP), fill_tile, src)


_W_CHUNK = 128
_W_RING = 11


def _moe_kernel(layer, te_ref, nv_ref, first_ref, slot_ref, nxt_ref,
                x_ref, wg_hbm, wu_hbm, wd_hbm, y_ref,
                cg_scr, cu_scr, cd_scr, stage_scr, sem, cnt_ref):
    i = pl.program_id(0)
    d, f = cg_scr.shape[1], cg_scr.shape[2]
    n_g = d // _W_CHUNK
    n_chunks = 2 * n_g + f // _W_CHUNK

    def for_chunk(c, e, slot, fn):
        k = c % _W_RING

        @pl.when(c < n_g)
        def _():
            row = pl.multiple_of(c * _W_CHUNK, _W_CHUNK)
            fn(wg_hbm.at[layer, e, pl.ds(row, _W_CHUNK), :], stage_scr.at[k, :, pl.ds(0, f)],
               sem.at[k], cg_scr.at[slot, pl.ds(row, _W_CHUNK), :])

        @pl.when((c >= n_g) & (c < 2 * n_g))
        def _():
            row = pl.multiple_of((c - n_g) * _W_CHUNK, _W_CHUNK)
            fn(wu_hbm.at[layer, e, pl.ds(row, _W_CHUNK), :], stage_scr.at[k, :, pl.ds(0, f)],
               sem.at[k], cu_scr.at[slot, pl.ds(row, _W_CHUNK), :])

        @pl.when(c >= 2 * n_g)
        def _():
            row = pl.multiple_of((c - 2 * n_g) * _W_CHUNK, _W_CHUNK)
            fn(wd_hbm.at[layer, e, pl.ds(row, _W_CHUNK), :], stage_scr.at[k],
               sem.at[k], cd_scr.at[slot, pl.ds(row, _W_CHUNK), :])

    def start(src, stage, s, dst):
        pltpu.make_async_copy(src, stage, s).start()

    def finish(src, stage, s, dst):
        pltpu.make_async_copy(src, stage, s).wait()
        dst[...] = stage[...].astype(BF16)

    def start_upto(e, slot, hi):
        hi = jnp.minimum(hi, n_chunks)

        def body(c, carry):
            for_chunk(c, e, slot, start)
            return carry

        lax.fori_loop(cnt_ref[0], hi, body, 0)
        cnt_ref[0] = jnp.maximum(cnt_ref[0], hi)

    def finish_started(e, slot):
        def body(c, carry):
            for_chunk(c, e, slot, finish)
            return carry

        lax.fori_loop(cnt_ref[1], cnt_ref[0], body, 0)
        cnt_ref[1] = cnt_ref[0]

    valid = i < nv_ref[0]

    @pl.when(i == 0)
    def _():
        cnt_ref[0] = 0
        cnt_ref[1] = 0

    @pl.when(valid & (first_ref[i] == 1))
    def _():
        e, slot = te_ref[i], slot_ref[i]

        def body(c, carry):
            start_upto(e, slot, c + _W_RING)
            for_chunk(c, e, slot, finish)
            return carry

        lax.fori_loop(cnt_ref[1], n_chunks, body, 0)
        cnt_ref[0] = 0
        cnt_ref[1] = 0

    @pl.when(valid & (nxt_ref[i] >= 0))
    def _():
        e, slot = nxt_ref[i], 1 - slot_ref[i]
        finish_started(e, slot)
        start_upto(e, slot, cnt_ref[0] + _W_RING)

    @pl.when(valid)
    def _():
        slot = slot_ref[i]
        x = x_ref[...].astype(BF16)
        a = _dot(x, cg_scr[slot])
        b = _dot(x, cu_scr[slot])
        hidden = (_silu(a) * b).astype(BF16)
        y_ref[...] = _dot(hidden, cd_scr[slot])

    @pl.when(jnp.logical_not(valid))
    def _():
        y_ref[...] = jnp.zeros_like(y_ref)


def _moe_schedule(tile_e, n_valid):
    n_tiles = tile_e.shape[0]
    idx = jnp.arange(n_tiles, dtype=I32)
    valid = idx < n_valid[0]
    first = valid & ((idx == 0) | (tile_e != jnp.roll(tile_e, 1)))
    run = jnp.cumsum(first.astype(I32)) - 1
    next_first = lax.cummin(jnp.where(first, idx, n_tiles), reverse=True)
    after = jnp.concatenate([next_first[1:], jnp.full((1,), n_tiles, I32)])
    nxt = jnp.where(valid & (after < n_tiles), tile_e[jnp.minimum(after, n_tiles - 1)], -1)
    last = valid & ((after == idx + 1) | (idx == n_valid[0] - 1))
    fill = last | jnp.logical_not(valid)
    return first.astype(I32), (run % 2).astype(I32), nxt.astype(I32), fill.astype(I32)


def _moe_experts(xs, tile_e, n_valid, schedule, w_gate, w_up, w_down, layer):
    n_rows, d = xs.shape
    f = w_gate.shape[3]
    tm = MOE_TILE
    n_tiles = n_rows // tm
    assert d % _W_CHUNK == 0 and f % _W_CHUNK == 0
    first, slot, nxt = schedule

    def xmap(i, te, nv, *_):
        return (jnp.minimum(i, nv[0] - 1), 0)

    any_spec = pl.BlockSpec(memory_space=pl.ANY)
    grid_spec = pltpu.PrefetchScalarGridSpec(
        num_scalar_prefetch=5,
        grid=(n_tiles,),
        in_specs=[pl.BlockSpec((tm, d), xmap), any_spec, any_spec, any_spec],
        out_specs=pl.BlockSpec((tm, d), lambda i, *_: (i, 0)),
        scratch_shapes=[pltpu.VMEM((2, d, f), BF16), pltpu.VMEM((2, d, f), BF16),
                        pltpu.VMEM((2, f, d), BF16),
                        pltpu.VMEM((_W_RING, _W_CHUNK, d), F32),
                        pltpu.SemaphoreType.DMA((_W_RING,)),
                        pltpu.SMEM((2,), I32)],
    )
    return pl.pallas_call(
        functools.partial(_moe_kernel, layer),
        grid_spec=grid_spec,
        out_shape=jax.ShapeDtypeStruct((n_rows, d), F32),
        compiler_params=_cparams(("arbitrary",), MOE_VMEM_LIMIT),
        name="moe_experts",
    )(tile_e, n_valid, first, slot, nxt, xs, w_gate, w_up, w_down)


_COMBINE_ROWS = 256


def _combine_kernel(with_next, steps, p1_ref, p2_ref, p1n_ref, p2n_ref, ys_ref, w1_ref, w2_ref,
                    x_ref, gate_ref, g_ref, b_ref, *rest):
    if with_next:
        sc_ref, sh_ref, xo_ref, h_ref, ya_scr, yb_scr, sem = rest
    else:
        xo_ref, ya_scr, yb_scr, sem = rest
    step = pl.program_id(0)

    def copies(pa, pb, buf, s, r):
        return (pltpu.make_async_copy(ys_ref.at[pl.ds(pa[0, 0, r], 1)],
                                      buf.at[0, pl.ds(r, 1)], sem.at[s, 0]),
                pltpu.make_async_copy(ys_ref.at[pl.ds(pb[0, 0, r], 1)],
                                      buf.at[1, pl.ds(r, 1)], sem.at[s, 1]))

    def wait_all(pa, pb, buf, s):
        def body(r, carry):
            for cp in copies(pa, pb, buf, s, r):
                cp.wait()
            return carry
        lax.fori_loop(0, _COMBINE_ROWS, body, 0, unroll=_DMA_UNROLL)

    @pl.when(step == 0)
    def _():
        def body(r, carry):
            for cp in copies(p1_ref, p2_ref, ya_scr, 0, r):
                cp.start()
            return carry
        lax.fori_loop(0, _COMBINE_ROWS, body, 0, unroll=_DMA_UNROLL)

    for par, (cur, nxt) in enumerate(((ya_scr, yb_scr), (yb_scr, ya_scr))):
        @pl.when(step % 2 == par)
        def _(par=par, cur=cur, nxt=nxt):
            wait_all(p1_ref, p2_ref, cur, par)
            for r in range(_COMBINE_ROWS):
                for cp in copies(p1n_ref, p2n_ref, nxt, 1 - par, r):
                    cp.start()
            y = w1_ref[0] * cur[0] + w2_ref[0] * cur[1]
            z = DEEPNORM_ALPHA * x_ref[0] + (1.0 + gate_ref[0]) * y
            xn = _layer_norm_rows(z, g_ref[...], b_ref[...])
            xo_ref[0] = xn
            if with_next:
                h_ref[0] = (xn * (1.0 + sc_ref[0]) + sh_ref[0]).astype(BF16)

            @pl.when(step == steps - 1)
            def _():
                wait_all(p1n_ref, p2n_ref, nxt, 1 - par)


def _combine(ys, pos1, pos2, w1, w2, x, gate, ln_g, ln_b, next_mod):
    bsz, t, d = x.shape
    tr = _COMBINE_ROWS
    per_b = t // tr
    steps = bsz * per_b
    row = pl.BlockSpec((1, tr, d), lambda s: (s // per_b, s % per_b, 0))
    vec = pl.BlockSpec((1, d), lambda s: (0, 0))
    mod = pl.BlockSpec((1, 1, d), lambda s: (s // per_b, 0, 0))
    wcol = pl.BlockSpec((1, tr, 1), lambda s: (s // per_b, s % per_b, 0))
    idx = pl.BlockSpec((1, 1, tr), lambda s: (s, 0, 0), memory_space=pltpu.SMEM)
    idx_next = pl.BlockSpec((1, 1, tr), lambda s: (jnp.minimum(s + 1, steps - 1), 0, 0),
                            memory_space=pltpu.SMEM)
    with_next = next_mod is not None
    p1 = pos1.reshape(steps, 1, tr)
    p2 = pos2.reshape(steps, 1, tr)
    in_specs = [idx, idx, idx_next, idx_next, pl.BlockSpec(memory_space=pl.ANY),
                wcol, wcol, row, mod, vec, vec]
    args = [p1, p2, p1, p2, ys, w1.reshape(bsz, t, 1), w2.reshape(bsz, t, 1), x, gate,
            ln_g.reshape(1, d), ln_b.reshape(1, d)]
    out_specs = [row]
    out_shape = [jax.ShapeDtypeStruct((bsz, t, d), F32)]
    if with_next:
        in_specs += [mod, mod]
        args += list(next_mod)
        out_specs.append(row)
        out_shape.append(jax.ShapeDtypeStruct((bsz, t, d), BF16))
    res = pl.pallas_call(
        functools.partial(_combine_kernel, with_next, steps),
        grid=(steps,),
        in_specs=in_specs,
        out_specs=out_specs,
        out_shape=out_shape,
        scratch_shapes=[pltpu.VMEM((2, tr, d), F32), pltpu.VMEM((2, tr, d), F32),
                        pltpu.SemaphoreType.DMA((2, 2))],
        compiler_params=_cparams(("arbitrary",), VMEM_LIMIT),
        name="combine_ln",
    )(*args)
    return res if with_next else (res[0], None)


def kernel(x, c, emb_ln_g, emb_ln_b, w_ada, b_ada, w_in, conv_w, mix_beta, w_out, hg_lb_logits,
           ln_g, ln_b, w_router, router_bias, w_gate, w_up, w_down):
    bsz, t, d = x.shape
    n = bsz * t
    assert n % LANES == 0 and t % SEQ_CHUNK == 0
    n_tiles = 2 * n // MOE_TILE + N_EXPERTS
    assert n_tiles <= LANES

    mod = _ada_mod(c, w_ada, b_ada)
    mod = mod.reshape(DEPTH, bsz, 6, 1, d)

    def mods(l):
        return [mod[l, :, i] for i in range(6)]

    wr_hi = jnp.pad(w_router, ((0, 0), (0, LANES - N_EXPERTS)))
    wr_hi16 = wr_hi.astype(BF16)
    wr_lo16 = (wr_hi - wr_hi16.astype(F32)).astype(BF16)

    w_out16 = w_out.astype(BF16)

    m = [mods(l) for l in range(DEPTH)]
    xcur, h = _ln_mod(x, emb_ln_g, emb_ln_b, m[0][1], m[0][0])
    for l in range(DEPTH):
        _, _, gate1, shift2, scale2, gate2 = m[l]
        beta = mix_beta[l].reshape(1, -1)
        proj = _in_proj(h.reshape(n, d), w_in, l, BF16).reshape(bsz, t, IN_COLS)
        yc = _conv_mixer(proj, conv_w[l], beta)
        yr = _ret_mixer(proj, beta)
        yh = _hg_mixer(proj, hg_lb_logits, beta, l)
        x1, h2, logits = _out_proj(yc, yr, yh, w_out16, l, xcur, gate1,
                                   ln_g[l, 0], ln_b[l, 0], scale2, shift2, wr_hi16, wr_lo16)
        logits_t = logits.reshape(n, LANES)[:, :N_EXPERTS].T.reshape(N_EXPERTS, n // LANES, LANES)
        pos1, pos2, w1, w2, tiles = _route(logits_t, router_bias, n_tiles)
        tile_e, n_valid = tiles[0, :n_tiles], tiles[1, :1]
        first, slot, nxt, fill = _moe_schedule(tile_e, n_valid)
        xs = _scatter_rows(h2.reshape(n, d), pos1.reshape(n), pos2.reshape(n), fill, n_tiles)
        ys = _moe_experts(xs, tile_e, n_valid, (first, slot, nxt), w_gate, w_up, w_down, l)
        next_mod = (m[l + 1][1], m[l + 1][0]) if l + 1 < DEPTH else None
        xcur, h = _combine(ys, pos1.reshape(n), pos2.reshape(n), w1, w2, x1, gate2,
                           ln_g[l, 1], ln_b[l, 1], next_mod)
    return xcur
```

```python
import functools
import math

import jax
import jax.numpy as jnp
from jax import lax
from jax.experimental import pallas as pl
from jax.experimental.pallas import tpu as pltpu

F32 = jnp.float32
BF16 = jnp.bfloat16
I32 = jnp.int32

DEPTH = 2
CONV_WIDTH = 512
RET_WIDTH = 768
HG_WIDTH = 768
HEAD_DIM = 128
RET_HEADS = RET_WIDTH // HEAD_DIM
HG_HEADS = HG_WIDTH // HEAD_DIM
IN_COLS = 3 * CONV_WIDTH + 4 * RET_WIDTH + 5 * HG_WIDTH
ROPE_BASE = 10000.0
N_EXPERTS = 16
N_GROUPS = 4
EXPERTS_PER_GROUP = N_EXPERTS // N_GROUPS
MASK_NEG = -1e9
DEEPNORM_ALPHA = (2.0 * DEPTH) ** 0.25
LN_EPS = 1e-5
HEAD_EPS = 1e-6
FORGET_FLOOR = 1e-6

LANES = 128
SUBLANES = 8

_CB, _CC, _CH = 0, 4, 8
_RQ, _RK, _RV, _RG = 12, 18, 24, 30
_HQ, _HFF, _HFB, _HI, _HG = 36, 42, 48, 54, 60
_OUT_RET, _OUT_HG = 4, 10

SEQ_CHUNK = 128
MOE_TILE = 256
VMEM_LIMIT = 56 * 1024 * 1024
MOE_VMEM_LIMIT = 60 * 1024 * 1024


def _cparams(sem, vmem=None):
    return pltpu.CompilerParams(dimension_semantics=sem, vmem_limit_bytes=vmem)


def _silu(x):
    return x * jax.nn.sigmoid(x)


def _dot(a, b):
    return jnp.dot(a, b, preferred_element_type=F32)


def _dot_nt(a, b):
    return lax.dot_general(a, b, (((1,), (1,)), ((), ())), preferred_element_type=F32)


def _dot_tn(a, b):
    return lax.dot_general(a, b, (((0,), (0,)), ((), ())), preferred_element_type=F32)


def _ada_kernel(c_ref, w_ref, b_ref, o_ref):
    cond = _silu(c_ref[...])
    o_ref[0] = _dot(cond.astype(BF16), w_ref[0].astype(BF16)) + b_ref[0]


def _ada_mod(c, w_ada, b_ada):
    depth, d, n6 = w_ada.shape
    b = c.shape[0]
    bp = -(-b // SUBLANES) * SUBLANES
    cp = jnp.pad(c, ((0, bp - b), (0, 0)))
    tn = 512
    out = pl.pallas_call(
        _ada_kernel,
        grid=(depth, n6 // tn),
        in_specs=[
            pl.BlockSpec((bp, d), lambda l, j: (0, 0)),
            pl.BlockSpec((1, d, tn), lambda l, j: (l, 0, j)),
            pl.BlockSpec((1, 1, tn), lambda l, j: (l, 0, j)),
        ],
        out_specs=pl.BlockSpec((1, bp, tn), lambda l, j: (l, 0, j)),
        out_shape=jax.ShapeDtypeStruct((depth, bp, n6), F32),
        compiler_params=_cparams(("arbitrary", "arbitrary")),
        name="ada_mod",
    )(cp, w_ada, b_ada.reshape(depth, 1, n6))
    return out[:, :b, :]


def _layer_norm_rows(z, g, b):
    mu = jnp.mean(z, axis=-1, keepdims=True)
    zc = z - mu
    var = jnp.mean(zc * zc, axis=-1, keepdims=True)
    return zc * lax.rsqrt(var + LN_EPS) * g + b


def _ln_mod_kernel(x_ref, g_ref, b_ref, sc_ref, sh_ref, xo_ref, h_ref):
    y = _layer_norm_rows(x_ref[0], g_ref[...], b_ref[...])
    xo_ref[0] = y
    h_ref[0] = (y * (1.0 + sc_ref[0]) + sh_ref[0]).astype(BF16)


def _ln_mod(x, g, b, scale, shift):
    bsz, t, d = x.shape
    tr = 512
    row = pl.BlockSpec((1, tr, d), lambda i, j: (i, j, 0))
    vec = pl.BlockSpec((1, d), lambda i, j: (0, 0))
    mod = pl.BlockSpec((1, 1, d), lambda i, j: (i, 0, 0))
    return pl.pallas_call(
        _ln_mod_kernel,
        grid=(bsz, t // tr),
        in_specs=[row, vec, vec, mod, mod],
        out_specs=[row, row],
        out_shape=[jax.ShapeDtypeStruct((bsz, t, d), F32),
                   jax.ShapeDtypeStruct((bsz, t, d), BF16)],
        compiler_params=_cparams(("arbitrary", "arbitrary")),
        name="ln_mod",
    )(x, g.reshape(1, d), b.reshape(1, d), scale, shift)


def _in_proj_kernel(a_ref, w_ref, o_ref, w16_scr):
    @pl.when(pl.program_id(1) == 0)
    def _():
        w16_scr[...] = w_ref[0].astype(BF16)

    o_ref[...] = _dot(a_ref[...], w16_scr[...]).astype(o_ref.dtype)


def _in_proj(h, w_in, layer, out_dtype):
    n, d = h.shape
    nc = w_in.shape[2]
    tm, tn = 1024, 1408
    tm = min(tm, n)
    return pl.pallas_call(
        _in_proj_kernel,
        grid=(nc // tn, n // tm),
        in_specs=[pl.BlockSpec((tm, d), lambda j, i: (i, 0)),
                  pl.BlockSpec((1, d, tn), lambda j, i: (layer, 0, j))],
        out_specs=pl.BlockSpec((tm, tn), lambda j, i: (i, j)),
        out_shape=jax.ShapeDtypeStruct((n, nc), out_dtype),
        scratch_shapes=[pltpu.VMEM((d, tn), BF16)],
        compiler_params=_cparams(("arbitrary", "arbitrary"), VMEM_LIMIT),
        name="in_proj",
    )(h, w_in)


def _conv_kernel(cb_ref, cc_ref, ch_ref, w_ref, beta_ref, o_ref):
    u = cc_ref[0].astype(F32) * ch_ref[0].astype(F32)
    t = u.shape[0]
    row = lax.broadcasted_iota(I32, u.shape, 0)
    prev = jnp.where(row == 0, 0.0, pltpu.roll(u, 1, 0))
    nxt = jnp.where(row == t - 1, 0.0, pltpu.roll(u, t - 1, 0))
    w = w_ref[...]
    y = cb_ref[0].astype(F32) * (prev * w[0:1] + u * w[1:2] + nxt * w[2:3])
    o_ref[0] = (y * beta_ref[...]).astype(o_ref.dtype)


def _conv_mixer(proj, conv_w, beta):
    bsz, t, _ = proj.shape
    cw = 256
    nb = CONV_WIDTH // cw
    per = LANES * 1

    def col(off):
        return pl.BlockSpec((1, t, cw), lambda b, j, off=off: (b, 0, off * per // cw + j))

    return pl.pallas_call(
        _conv_kernel,
        grid=(bsz, nb),
        in_specs=[col(_CB), col(_CC), col(_CH),
                  pl.BlockSpec((3, cw), lambda b, j: (0, j)),
                  pl.BlockSpec((1, cw), lambda b, j: (0, j))],
        out_specs=pl.BlockSpec((1, t, cw), lambda b, j: (b, 0, j)),
        out_shape=jax.ShapeDtypeStruct((bsz, t, CONV_WIDTH), BF16),
        compiler_params=_cparams(("arbitrary", "arbitrary"), VMEM_LIMIT),
        name="conv_mixer",
    )(proj, proj, proj, conv_w, beta)


def _ret_kernel(q_ref, k_ref, v_ref, g_ref, cos_ref, sin_ref, lg_ref, beta_ref,
                o_ref, o_scr, qf_scr, qb_scr, uf_scr, ub_scr):
    L = SEQ_CHUNK
    t = q_ref.shape[1]
    n = t // L
    lgf = lg_ref[0, 0:1, :]
    lgb = lg_ref[0, 1:2, :]
    r = lax.broadcasted_iota(I32, (L, L), 0).astype(F32)
    c = lax.broadcasted_iota(I32, (L, L), 1).astype(F32)
    rel = r - c
    dbi = jnp.where(rel > 0, jnp.exp(lgf * jnp.maximum(rel, 0.0)),
                    jnp.where(rel < 0, jnp.exp(lgb * jnp.maximum(-rel, 0.0)), 2.0))
    qf = jnp.exp(lgf * (r + 1.0))
    kf = jnp.exp(lgf * (L - 1.0 - r))
    qb = jnp.exp(lgb * (L - r))
    kb = jnp.exp(lgb * r)
    gf_l = jnp.exp(lgf * float(L))
    gb_l = jnp.exp(lgb * float(L))
    scale = HEAD_DIM ** -0.5

    def rot(x, cs, sn):
        return x * cs + pltpu.roll(x, HEAD_DIM // 2, 1) * sn

    def rows(ci):
        return pl.ds(pl.multiple_of(ci * L, L), L)

    def local_body(ci, carry):
        sl = rows(ci)
        cs = cos_ref[sl, :]
        sn = sin_ref[sl, :]
        q = rot(q_ref[0, sl, :].astype(F32), cs, sn)
        k = rot(k_ref[0, sl, :].astype(F32), cs, sn) * scale
        v16 = v_ref[0, sl, :].astype(BF16)
        s = _dot_nt(q.astype(BF16), k.astype(BF16)) * dbi
        o_scr[sl, :] = _dot(s.astype(BF16), v16)
        qf_scr[sl, :] = (q * qf).astype(BF16)
        qb_scr[sl, :] = (q * qb).astype(BF16)
        uf_scr[ci] = _dot_tn((k * kf).astype(BF16), v16)
        ub_scr[ci] = _dot_tn((k * kb).astype(BF16), v16)
        return carry

    lax.fori_loop(0, n, local_body, 0, unroll=8)

    def scan_body(i, states):
        sf, sb = states
        cf, cb = i, n - 1 - i
        slf, slb = rows(cf), rows(cb)
        o_scr[slf, :] += _dot(qf_scr[slf, :], sf.astype(BF16))
        o_scr[slb, :] += _dot(qb_scr[slb, :], sb.astype(BF16))
        return gf_l * sf + uf_scr[cf], gb_l * sb + ub_scr[cb]

    zero = jnp.zeros((HEAD_DIM, HEAD_DIM), F32)
    lax.fori_loop(0, n, scan_body, (zero, zero), unroll=8)

    beta = beta_ref[...]

    def norm_body(ci, carry):
        sl = rows(ci)
        o = o_scr[sl, :]
        mu = jnp.mean(o, axis=-1, keepdims=True)
        oc = o - mu
        var = jnp.mean(oc * oc, axis=-1, keepdims=True)
        y = oc * lax.rsqrt(var + HEAD_EPS) * _silu(g_ref[0, sl, :].astype(F32)) * beta
        o_ref[0, sl, :] = y.astype(o_ref.dtype)
        return carry

    lax.fori_loop(0, n, norm_body, 0, unroll=4)


def _rotary_tables(t):
    half = HEAD_DIM // 2
    inv_freq = ROPE_BASE ** (-jnp.arange(half, dtype=F32) / half)
    ang = jnp.arange(t, dtype=F32)[:, None] * inv_freq[None, :]
    cos, sin = jnp.cos(ang), jnp.sin(ang)
    return jnp.concatenate([cos, cos], -1), jnp.concatenate([-sin, sin], -1)


def _ret_log_decays():
    head = jnp.arange(RET_HEADS, dtype=F32)
    lg_f = jnp.log1p(-jnp.exp2(-5.0 - head))
    lg_b = jnp.log1p(-jnp.exp2(-5.5 - head))
    lg = jnp.stack([lg_f, lg_b], axis=1)
    return jnp.broadcast_to(lg[:, :, None], (RET_HEADS, 2, LANES))


def _ret_mixer(proj, beta):
    bsz, t, _ = proj.shape
    n_chunks = t // SEQ_CHUNK
    assert n_chunks % 2 == 0
    cosf, sinf = _rotary_tables(t)

    def col(off):
        return pl.BlockSpec((1, t, HEAD_DIM), lambda b, h, off=off: (b, 0, off + h))

    tab = pl.BlockSpec((t, HEAD_DIM), lambda b, h: (0, 0))
    return pl.pallas_call(
        _ret_kernel,
        grid=(bsz, RET_HEADS),
        in_specs=[col(_RQ), col(_RK), col(_RV), col(_RG), tab, tab,
                  pl.BlockSpec((1, 2, LANES), lambda b, h: (h, 0, 0)),
                  pl.BlockSpec((1, HEAD_DIM), lambda b, h: (0, _OUT_RET + h))],
        out_specs=pl.BlockSpec((1, t, HEAD_DIM), lambda b, h: (b, 0, h)),
        out_shape=jax.ShapeDtypeStruct((bsz, t, RET_WIDTH), BF16),
        scratch_shapes=[pltpu.VMEM((t, HEAD_DIM), F32),
                        pltpu.VMEM((t, HEAD_DIM), BF16), pltpu.VMEM((t, HEAD_DIM), BF16),
                        pltpu.VMEM((n_chunks, HEAD_DIM, HEAD_DIM), F32),
                        pltpu.VMEM((n_chunks, HEAD_DIM, HEAD_DIM), F32)],
        compiler_params=_cparams(("arbitrary", "arbitrary"), VMEM_LIMIT),
        name="ret_mixer",
    )(proj, proj, proj, proj, cosf, sinf, _ret_log_decays(), beta)


_HG_BASE = SUBLANES
_HG_LEVELS = tuple(m for m in (8, 16, 32, 64) if m < SEQ_CHUNK)


def _cumsum_rows(x, reverse):
    n = x.shape[0]
    row = lax.broadcasted_iota(I32, x.shape, 0)
    sh = 1
    while sh < n:
        if reverse:
            x = x + jnp.where(row < n - sh, pltpu.roll(x, n - sh, 0), 0.0)
        else:
            x = x + jnp.where(row >= sh, pltpu.roll(x, sh, 0), 0.0)
        sh *= 2
    return x


def _hg_masks(reverse):
    C = SEQ_CHUNK
    r = lax.broadcasted_iota(I32, (C, C), 0)
    c = lax.broadcasted_iota(I32, (C, C), 1)
    if reverse:
        r, c = c, r
    sh = _HG_BASE.bit_length() - 1
    masks = [((r >> sh) == (c >> sh)) & (c <= r)]
    for m in _HG_LEVELS:
        sh = m.bit_length() - 1
        masks.append(((r >> (sh + 1)) == (c >> (sh + 1)))
                     & (((r >> sh) & 1) == 1) & (((c >> sh) & 1) == 0))
    return [jnp.where(m, 1.0, 0.0) for m in masks]


def _hg_scores(q, kk, cum, mask_ref, reverse):
    C = SEQ_CHUNK
    d = 1 if reverse else 0

    def split(x, blk):
        return x.reshape(C // blk, blk, LANES)

    b = _HG_BASE
    ref_row = b // 2 if reverse else b // 2 - 1
    cum3 = split(cum, b)
    rel = cum3 - cum3[:, ref_row:ref_row + 1, :]
    qt = (split(q, b) * jnp.exp(rel)).reshape(C, LANES).astype(BF16)
    kt = (split(kk, b) * jnp.exp(-rel)).reshape(C, LANES).astype(BF16)
    a = jnp.where(mask_ref[d, 0] > 0.5, _dot_nt(qt, kt), 0.0)
    for li, m in enumerate(_HG_LEVELS):
        blk = 2 * m
        ref_row = m if reverse else m - 1
        cum3 = split(cum, blk)
        rel = cum3 - cum3[:, ref_row:ref_row + 1, :]
        dec = jnp.exp(-jnp.abs(rel)).reshape(C, LANES)
        halves = []
        for j in range(C // m):
            is_query = (j % 2 == 1) != reverse
            halves.append((q if is_query else kk)[j * m:(j + 1) * m])
        x = (jnp.concatenate(halves, axis=0) * dec).astype(BF16)
        a = a + mask_ref[d, li + 1] * _dot_nt(x, x)
    return a


def _hg_kernel(layer, q_ref, zf_ref, zb_ref, v_ref, g_ref, lbl_ref, beta_ref, o_ref,
               o_scr, qe_scr, u_scr, dec_scr, mask_scr):
    C = SEQ_CHUNK
    t = q_ref.shape[1]
    n = t // C
    for d, reverse in enumerate((False, True)):
        for li, m in enumerate(_hg_masks(reverse)):
            mask_scr[d, li] = m
    logits = lbl_ref[...].astype(F32)
    e = jnp.exp(logits - jnp.max(logits, axis=0, keepdims=True))
    p = e / jnp.sum(e, axis=0, keepdims=True)
    lb = p[0]
    for l in range(1, layer + 1):
        lb = lb + p[l]
    lb = lb - p[0]

    def gates(z_ref, sl, lb_row):
        f = lb_row + (1.0 - lb_row) * jax.nn.sigmoid(z_ref[0, sl, :].astype(F32))
        f = jnp.maximum(f, FORGET_FLOOR)
        return jnp.log(f), 1.0 - f

    def rows(ci):
        return pl.ds(pl.multiple_of(ci * C, C), C)

    def local_dir(ci, sl, q, v16, z_ref, d):
        reverse = d == 1
        logf, kk = gates(z_ref, sl, lb[d:d + 1, :])
        cum = _cumsum_rows(logf, reverse)
        a = _hg_scores(q, kk, cum, mask_scr, reverse)
        total = cum[0:1, :] if reverse else cum[C - 1:C, :]
        qe_scr[d, sl, :] = (q * jnp.exp(cum)).astype(BF16)
        k_tail = kk * jnp.exp(total - cum)
        u_scr[d, ci] = _dot_tn(v16, k_tail.astype(BF16))
        dec_scr[d, ci] = jnp.broadcast_to(jnp.exp(total), (SUBLANES, LANES))
        return _dot(a.astype(BF16), v16)

    def local_body(ci, carry):
        sl = rows(ci)
        q = q_ref[0, sl, :].astype(F32)
        v16 = v_ref[0, sl, :].astype(BF16)
        o_scr[sl, :] = (local_dir(ci, sl, q, v16, zf_ref, 0)
                        + local_dir(ci, sl, q, v16, zb_ref, 1))
        return carry

    lax.fori_loop(0, n, local_body, 0, unroll=4)

    def scan_body(i, states):
        sf, sb = states
        cf, cb = i, n - 1 - i
        slf, slb = rows(cf), rows(cb)
        o_scr[slf, :] += _dot_nt(qe_scr[0, slf, :], sf.astype(BF16))
        o_scr[slb, :] += _dot_nt(qe_scr[1, slb, :], sb.astype(BF16))
        return (sf * dec_scr[0, cf, 0:1, :] + u_scr[0, cf],
                sb * dec_scr[1, cb, 0:1, :] + u_scr[1, cb])

    zero = jnp.zeros((HEAD_DIM, HEAD_DIM), F32)
    lax.fori_loop(0, n, scan_body, (zero, zero), unroll=8)

    beta = beta_ref[...]

    def norm_body(ci, carry):
        sl = rows(ci)
        o = o_scr[sl, :]
        ms = jnp.mean(o * o, axis=-1, keepdims=True)
        y = o * lax.rsqrt(ms + HEAD_EPS) * _silu(g_ref[0, sl, :].astype(F32)) * beta
        o_ref[0, sl, :] = y.astype(o_ref.dtype)
        return carry

    lax.fori_loop(0, n, norm_body, 0, unroll=8)


def _hg_mixer(proj, lb_logits, beta, layer):
    bsz, t, _ = proj.shape
    n_chunks = t // SEQ_CHUNK
    assert n_chunks % 2 == 0

    def col(off):
        return pl.BlockSpec((1, t, HEAD_DIM), lambda b, h, off=off: (b, 0, off + h))

    return pl.pallas_call(
        functools.partial(_hg_kernel, layer),
        grid=(bsz, HG_HEADS),
        in_specs=[col(_HQ), col(_HFF), col(_HFB), col(_HI), col(_HG),
                  pl.BlockSpec((DEPTH, 2, HEAD_DIM), lambda b, h: (0, 0, h)),
                  pl.BlockSpec((1, HEAD_DIM), lambda b, h: (0, _OUT_HG + h))],
        out_specs=pl.BlockSpec((1, t, HEAD_DIM), lambda b, h: (b, 0, h)),
        out_shape=jax.ShapeDtypeStruct((bsz, t, HG_WIDTH), BF16),
        scratch_shapes=[pltpu.VMEM((t, HEAD_DIM), F32),
                        pltpu.VMEM((2, t, HEAD_DIM), BF16),
                        pltpu.VMEM((2, n_chunks, HEAD_DIM, HEAD_DIM), F32),
                        pltpu.VMEM((2, n_chunks, SUBLANES, LANES), F32),
                        pltpu.VMEM((2, 1 + len(_HG_LEVELS), SEQ_CHUNK, SEQ_CHUNK), F32)],
        compiler_params=_cparams(("arbitrary", "arbitrary"), VMEM_LIMIT),
        name="hg_mixer",
    )(proj, proj, proj, proj, proj, lb_logits, beta)


_OUT_PROJ_SPLIT = 2


def _out_proj_kernel(yc_ref, yr_ref, yh_ref, w_ref, x_ref, gate_ref, g_ref, b_ref,
                     sc_ref, sh_ref, wrh_ref, wrl_ref, xo_ref, h_ref, lo_ref):
    c0, c1 = CONV_WIDTH, CONV_WIDTH + RET_WIDTH
    tm = x_ref.shape[1]
    for half in range(_OUT_PROJ_SPLIT):
        sl = pl.ds(half * (tm // _OUT_PROJ_SPLIT), tm // _OUT_PROJ_SPLIT)
        y = (_dot(yc_ref[0, sl, :], w_ref[0, 0:c0, :]) + _dot(yr_ref[0, sl, :], w_ref[0, c0:c1, :])
             + _dot(yh_ref[0, sl, :], w_ref[0, c1:, :]))
        z = DEEPNORM_ALPHA * x_ref[0, sl, :] + (1.0 + gate_ref[0]) * y
        xn = _layer_norm_rows(z, g_ref[...], b_ref[...])
        xo_ref[0, sl, :] = xn
        h = xn * (1.0 + sc_ref[0]) + sh_ref[0]
        h_ref[0, sl, :] = h
        h_hi = h.astype(BF16)
        h_lo = (h - h_hi.astype(F32)).astype(BF16)
        wrh = wrh_ref[...]
        lo_ref[0, sl, :] = _dot(h_hi, wrh) + _dot(h_lo, wrh) + _dot(h_hi, wrl_ref[...])


def _out_proj(yc, yr, yh, w_out16, layer, x, gate, ln_g, ln_b, scale2, shift2, wr_hi, wr_lo):
    bsz, t, d = x.shape
    tm = 512

    def row(width):
        return pl.BlockSpec((1, tm, width), lambda b, i: (b, i, 0))

    vec = pl.BlockSpec((1, d), lambda b, i: (0, 0))
    mod = pl.BlockSpec((1, 1, d), lambda b, i: (b, 0, 0))
    wr = pl.BlockSpec((d, LANES), lambda b, i: (0, 0))
    return pl.pallas_call(
        _out_proj_kernel,
        grid=(bsz, t // tm),
        in_specs=[row(CONV_WIDTH), row(RET_WIDTH), row(HG_WIDTH),
                  pl.BlockSpec((1, d, d), lambda b, i: (layer, 0, 0)),
                  row(d), mod, vec, vec, mod, mod, wr, wr],
        out_specs=[row(d), row(d), row(LANES)],
        out_shape=[jax.ShapeDtypeStruct((bsz, t, d), F32),
                   jax.ShapeDtypeStruct((bsz, t, d), F32),
                   jax.ShapeDtypeStruct((bsz, t, LANES), F32)],
        compiler_params=_cparams(("arbitrary", "arbitrary"), VMEM_LIMIT),
        name="out_proj",
    )(yc, yr, yh, w_out16, x, gate, ln_g.reshape(1, d), ln_b.reshape(1, d),
      scale2, shift2, wr_hi, wr_lo)


def _route_kernel(n_tiles, lt_ref, bias_ref, pos1_ref, pos2_ref, w1_ref, w2_ref, tile_ref):
    E = N_EXPERTS
    logits = [lt_ref[e] for e in range(E)]
    shape = logits[0].shape
    mx = functools.reduce(jnp.maximum, logits)
    ex = [jnp.exp(l - mx) for l in logits]
    den = functools.reduce(lambda a, b: a + b, ex)
    scores = [x / den for x in ex]
    sel = [scores[e] + bias_ref[e] for e in range(E)]

    best_g = jnp.zeros(shape, I32)
    best_v = None
    for g in range(N_GROUPS):
        a, b, c, d = sel[EXPERTS_PER_GROUP * g: EXPERTS_PER_GROUP * (g + 1)]
        hi1, lo1 = jnp.maximum(a, b), jnp.minimum(a, b)
        hi2, lo2 = jnp.maximum(c, d), jnp.minimum(c, d)
        gs = jnp.maximum(hi1, hi2) + jnp.maximum(jnp.minimum(hi1, hi2), jnp.maximum(lo1, lo2))
        if g == 0:
            best_v = gs
        else:
            better = gs > best_v
            best_g = jnp.where(better, g, best_g)
            best_v = jnp.where(better, gs, best_v)

    masked = [jnp.where(best_g == (e // EXPERTS_PER_GROUP), sel[e], MASK_NEG) for e in range(E)]

    def arg_top(vals, exclude):
        idx = jnp.zeros(shape, I32)
        val = None
        for e in range(E):
            v = vals[e] if exclude is None else jnp.where(exclude == e, -jnp.inf, vals[e])
            if e == 0:
                val = v
            else:
                better = v > val
                idx = jnp.where(better, e, idx)
                val = jnp.where(better, v, val)
        return idx

    idx1 = arg_top(masked, None)
    idx2 = arg_top(masked, idx1)
    s1 = functools.reduce(lambda a, b: a + b, [jnp.where(idx1 == e, scores[e], 0.0) for e in range(E)])
    s2 = functools.reduce(lambda a, b: a + b, [jnp.where(idx2 == e, scores[e], 0.0) for e in range(E)])
    w1_ref[...] = s1 / (s1 + s2)
    w2_ref[...] = s2 / (s1 + s2)

    rows = shape[0]
    li = lax.broadcasted_iota(I32, (LANES, LANES), 0)
    lj = lax.broadcasted_iota(I32, (LANES, LANES), 1)
    upper = (li <= lj).astype(BF16)
    ri = lax.broadcasted_iota(I32, (rows, rows), 0)
    rj = lax.broadcasted_iota(I32, (rows, rows), 1)
    lower = (rj < ri).astype(BF16)
    tile_start = (lax.broadcasted_iota(I32, (1, LANES), 1) * MOE_TILE).astype(F32)
    start = jnp.zeros((1, 1), F32)
    pos1 = jnp.zeros(shape, F32)
    pos2 = jnp.zeros(shape, F32)
    tile_e = jnp.zeros((1, LANES), F32)
    for e in range(E):
        hit1 = idx1 == e
        hit2 = idx2 == e
        onehot = jnp.where(hit1 | hit2, 1.0, 0.0)
        pref = _dot(onehot.astype(BF16), upper)
        row_tot = pref[:, LANES - 1:LANES]
        row_off = _dot(lower, jnp.broadcast_to(row_tot, shape).astype(BF16))[:, 0:1]
        rank = pref - onehot + row_off
        count = jnp.sum(onehot, keepdims=True)
        dest = start + rank
        pos1 = jnp.where(hit1, dest, pos1)
        pos2 = jnp.where(hit2, dest, pos2)
        start = start + jnp.ceil(count / MOE_TILE) * MOE_TILE
        tile_e = tile_e + jnp.where(tile_start >= start, 1.0, 0.0)
    pos1_ref[...] = pos1.astype(I32)
    pos2_ref[...] = pos2.astype(I32)
    n_valid = start / MOE_TILE
    tile_id = lax.broadcasted_iota(I32, (1, LANES), 1).astype(F32)
    last_e = jnp.sum(jnp.where(tile_id == n_valid - 1.0, tile_e, 0.0), keepdims=True)
    tile_e = jnp.where(tile_id < n_valid, tile_e, last_e)
    sub = lax.broadcasted_iota(I32, (SUBLANES, LANES), 0)
    tile_ref[...] = jnp.where(sub == 0, tile_e, jnp.broadcast_to(n_valid, (SUBLANES, LANES))).astype(I32)


def _route(logits_t, router_bias, n_tiles):
    e, rows, lanes = logits_t.shape
    tok = jax.ShapeDtypeStruct((rows, lanes), I32)
    tokf = jax.ShapeDtypeStruct((rows, lanes), F32)
    full = pl.BlockSpec((rows, lanes), lambda i: (0, 0))
    return pl.pallas_call(
        functools.partial(_route_kernel, n_tiles),
        grid=(1,),
        in_specs=[pl.BlockSpec((e, rows, lanes), lambda i: (0, 0, 0)),
                  pl.BlockSpec(memory_space=pltpu.SMEM)],
        out_specs=[full, full, full, full, pl.BlockSpec((SUBLANES, LANES), lambda i: (0, 0))],
        out_shape=[tok, tok, tokf, tokf, jax.ShapeDtypeStruct((SUBLANES, LANES), I32)],
        compiler_params=_cparams(("arbitrary",)),
        name="route",
    )(logits_t, router_bias)


_ROWS_PER_STEP = 512


_DMA_UNROLL = 8


def _scatter_rows_kernel(n_tiles, p1_ref, p2_ref, fill_ref, src_ref, dst_ref, zero_scr, sem, zsem):
    def zero_copy(i):
        row0 = pl.multiple_of(i * MOE_TILE, MOE_TILE)
        return pltpu.make_async_copy(zero_scr, dst_ref.at[pl.ds(row0, MOE_TILE)], zsem)

    @pl.when(pl.program_id(0) == 0)
    def _():
        zero_scr[...] = jnp.zeros_like(zero_scr)

        def zstart(i, carry):
            @pl.when(fill_ref[i] == 1)
            def _():
                zero_copy(i).start()
            return carry

        def zwait(i, carry):
            @pl.when(fill_ref[i] == 1)
            def _():
                zero_copy(i).wait()
            return carry

        lax.fori_loop(0, n_tiles, zstart, 0)
        lax.fori_loop(0, n_tiles, zwait, 0)

    def copies(r):
        row = src_ref.at[pl.ds(r, 1)]
        return (pltpu.make_async_copy(row, dst_ref.at[pl.ds(p1_ref[0, 0, r], 1)], sem.at[0]),
                pltpu.make_async_copy(row, dst_ref.at[pl.ds(p2_ref[0, 0, r], 1)], sem.at[1]))

    def start(r, carry):
        for queue, cp in enumerate(copies(r)):
            cp.start(priority=queue)
        return carry

    def wait(r, carry):
        for cp in copies(r):
            cp.wait()
        return carry

    lax.fori_loop(0, _ROWS_PER_STEP, start, 0, unroll=_DMA_UNROLL)
    lax.fori_loop(0, _ROWS_PER_STEP, wait, 0, unroll=_DMA_UNROLL)


def _scatter_rows(src, pos1, pos2, fill_tile, n_tiles):
    n, d = src.shape
    steps = n // _ROWS_PER_STEP
    idx = pl.BlockSpec((1, 1, _ROWS_PER_STEP), lambda i: (i, 0, 0), memory_space=pltpu.SMEM)
    return pl.pallas_call(
        functools.partial(_scatter_rows_kernel, n_tiles),
        grid=(steps,),
        in_specs=[idx, idx, pl.BlockSpec(memory_space=pltpu.SMEM),
                  pl.BlockSpec((_ROWS_PER_STEP, d), lambda i: (i, 0))],
        out_specs=pl.BlockSpec(memory_space=pl.ANY),
        out_shape=jax.ShapeDtypeStruct((n_tiles * MOE_TILE, d), src.dtype),
        scratch_shapes=[pltpu.VMEM((MOE_TILE, d), src.dtype),
                        pltpu.SemaphoreType.DMA((2,)), pltpu.SemaphoreType.DMA],
        compiler_params=_cparams(("arbitrary",), VMEM_LIMIT),
        name="scatter_rows",
    )(pos1.reshape(steps, 1, _ROWS_PER_STEP), pos2.reshape(steps, 1, _ROWS_PER_STEP), fill_tile, src)


_W_CHUNK = 128
_W_RING = 11


def _moe_kernel(layer, te_ref, nv_ref, first_ref, slot_ref, nxt_ref,
                x_ref, wg_hbm, wu_hbm, wd_hbm, y_ref,
                cg_scr, cu_scr, cd_scr, stage_scr, sem, cnt_ref):
    i = pl.program_id(0)
    d, f = cg_scr.shape[1], cg_scr.shape[2]
    n_g = d // _W_CHUNK
    n_chunks = 2 * n_g + f // _W_CHUNK

    def for_chunk(c, e, slot, fn):
        k = c % _W_RING

        @pl.when(c < n_g)
        def _():
            row = pl.multiple_of(c * _W_CHUNK, _W_CHUNK)
            fn(wg_hbm.at[layer, e, pl.ds(row, _W_CHUNK), :], stage_scr.at[k, :, pl.ds(0, f)],
               sem.at[k], cg_scr.at[slot, pl.ds(row, _W_CHUNK), :])

        @pl.when((c >= n_g) & (c < 2 * n_g))
        def _():
            row = pl.multiple_of((c - n_g) * _W_CHUNK, _W_CHUNK)
            fn(wu_hbm.at[layer, e, pl.ds(row, _W_CHUNK), :], stage_scr.at[k, :, pl.ds(0, f)],
               sem.at[k], cu_scr.at[slot, pl.ds(row, _W_CHUNK), :])

        @pl.when(c >= 2 * n_g)
        def _():
            row = pl.multiple_of((c - 2 * n_g) * _W_CHUNK, _W_CHUNK)
            fn(wd_hbm.at[layer, e, pl.ds(row, _W_CHUNK), :], stage_scr.at[k],
               sem.at[k], cd_scr.at[slot, pl.ds(row, _W_CHUNK), :])

    def start(src, stage, s, dst):
        pltpu.make_async_copy(src, stage, s).start()

    def finish(src, stage, s, dst):
        pltpu.make_async_copy(src, stage, s).wait()
        dst[...] = stage[...].astype(BF16)

    def start_upto(e, slot, hi):
        hi = jnp.minimum(hi, n_chunks)

        def body(c, carry):
            for_chunk(c, e, slot, start)
            return carry

        lax.fori_loop(cnt_ref[0], hi, body, 0)
        cnt_ref[0] = jnp.maximum(cnt_ref[0], hi)

    def finish_started(e, slot):
        def body(c, carry):
            for_chunk(c, e, slot, finish)
            return carry

        lax.fori_loop(cnt_ref[1], cnt_ref[0], body, 0)
        cnt_ref[1] = cnt_ref[0]

    valid = i < nv_ref[0]

    @pl.when(i == 0)
    def _():
        cnt_ref[0] = 0
        cnt_ref[1] = 0

    @pl.when(valid & (first_ref[i] == 1))
    def _():
        e, slot = te_ref[i], slot_ref[i]

        def body(c, carry):
            start_upto(e, slot, c + _W_RING)
            for_chunk(c, e, slot, finish)
            return carry

        lax.fori_loop(cnt_ref[1], n_chunks, body, 0)
        cnt_ref[0] = 0
        cnt_ref[1] = 0

    @pl.when(valid & (nxt_ref[i] >= 0))
    def _():
        e, slot = nxt_ref[i], 1 - slot_ref[i]
        finish_started(e, slot)
        start_upto(e, slot, cnt_ref[0] + _W_RING)

    @pl.when(valid)
    def _():
        slot = slot_ref[i]
        x = x_ref[...].astype(BF16)
        a = _dot(x, cg_scr[slot])
        b = _dot(x, cu_scr[slot])
        hidden = (_silu(a) * b).astype(BF16)
        y_ref[...] = _dot(hidden, cd_scr[slot])

    @pl.when(jnp.logical_not(valid))
    def _():
        y_ref[...] = jnp.zeros_like(y_ref)


def _moe_schedule(tile_e, n_valid):
    n_tiles = tile_e.shape[0]
    idx = jnp.arange(n_tiles, dtype=I32)
    valid = idx < n_valid[0]
    first = valid & ((idx == 0) | (tile_e != jnp.roll(tile_e, 1)))
    run = jnp.cumsum(first.astype(I32)) - 1
    next_first = lax.cummin(jnp.where(first, idx, n_tiles), reverse=True)
    after = jnp.concatenate([next_first[1:], jnp.full((1,), n_tiles, I32)])
    nxt = jnp.where(valid & (after < n_tiles), tile_e[jnp.minimum(after, n_tiles - 1)], -1)
    last = valid & ((after == idx + 1) | (idx == n_valid[0] - 1))
    fill = last | jnp.logical_not(valid)
    return first.astype(I32), (run % 2).astype(I32), nxt.astype(I32), fill.astype(I32)


def _moe_experts(xs, tile_e, n_valid, schedule, w_gate, w_up, w_down, layer):
    n_rows, d = xs.shape
    f = w_gate.shape[3]
    tm = MOE_TILE
    n_tiles = n_rows // tm
    assert d % _W_CHUNK == 0 and f % _W_CHUNK == 0
    first, slot, nxt = schedule

    def xmap(i, te, nv, *_):
        return (jnp.minimum(i, nv[0] - 1), 0)

    any_spec = pl.BlockSpec(memory_space=pl.ANY)
    grid_spec = pltpu.PrefetchScalarGridSpec(
        num_scalar_prefetch=5,
        grid=(n_tiles,),
        in_specs=[pl.BlockSpec((tm, d), xmap), any_spec, any_spec, any_spec],
        out_specs=pl.BlockSpec((tm, d), lambda i, *_: (i, 0)),
        scratch_shapes=[pltpu.VMEM((2, d, f), BF16), pltpu.VMEM((2, d, f), BF16),
                        pltpu.VMEM((2, f, d), BF16),
                        pltpu.VMEM((_W_RING, _W_CHUNK, d), F32),
                        pltpu.SemaphoreType.DMA((_W_RING,)),
                        pltpu.SMEM((2,), I32)],
    )
    return pl.pallas_call(
        functools.partial(_moe_kernel, layer),
        grid_spec=grid_spec,
        out_shape=jax.ShapeDtypeStruct((n_rows, d), F32),
        compiler_params=_cparams(("arbitrary",), MOE_VMEM_LIMIT),
        name="moe_experts",
    )(tile_e, n_valid, first, slot, nxt, xs, w_gate, w_up, w_down)


_COMBINE_ROWS = 256


def _combine_kernel(with_next, steps, p1_ref, p2_ref, p1n_ref, p2n_ref, ys_ref, w1_ref, w2_ref,
                    x_ref, gate_ref, g_ref, b_ref, *rest):
    if with_next:
        sc_ref, sh_ref, xo_ref, h_ref, ya_scr, yb_scr, sem = rest
    else:
        xo_ref, ya_scr, yb_scr, sem = rest
    step = pl.program_id(0)

    def copies(pa, pb, buf, s, r):
        return (pltpu.make_async_copy(ys_ref.at[pl.ds(pa[0, 0, r], 1)],
                                      buf.at[0, pl.ds(r, 1)], sem.at[s, 0]),
                pltpu.make_async_copy(ys_ref.at[pl.ds(pb[0, 0, r], 1)],
                                      buf.at[1, pl.ds(r, 1)], sem.at[s, 1]))

    def wait_all(pa, pb, buf, s):
        def body(r, carry):
            for cp in copies(pa, pb, buf, s, r):
                cp.wait()
            return carry
        lax.fori_loop(0, _COMBINE_ROWS, body, 0, unroll=_DMA_UNROLL)

    @pl.when(step == 0)
    def _():
        def body(r, carry):
            for cp in copies(p1_ref, p2_ref, ya_scr, 0, r):
                cp.start()
            return carry
        lax.fori_loop(0, _COMBINE_ROWS, body, 0, unroll=_DMA_UNROLL)

    for par, (cur, nxt) in enumerate(((ya_scr, yb_scr), (yb_scr, ya_scr))):
        @pl.when(step % 2 == par)
        def _(par=par, cur=cur, nxt=nxt):
            wait_all(p1_ref, p2_ref, cur, par)
            for r in range(_COMBINE_ROWS):
                for cp in copies(p1n_ref, p2n_ref, nxt, 1 - par, r):
                    cp.start()
            y = w1_ref[0] * cur[0] + w2_ref[0] * cur[1]
            z = DEEPNORM_ALPHA * x_ref[0] + (1.0 + gate_ref[0]) * y
            xn = _layer_norm_rows(z, g_ref[...], b_ref[...])
            xo_ref[0] = xn
            if with_next:
                h_ref[0] = (xn * (1.0 + sc_ref[0]) + sh_ref[0]).astype(BF16)

            @pl.when(step == steps - 1)
            def _():
                wait_all(p1n_ref, p2n_ref, nxt, 1 - par)


def _combine(ys, pos1, pos2, w1, w2, x, gate, ln_g, ln_b, next_mod):
    bsz, t, d = x.shape
    tr = _COMBINE_ROWS
    per_b = t // tr
    steps = bsz * per_b
    row = pl.BlockSpec((1, tr, d), lambda s: (s // per_b, s % per_b, 0))
    vec = pl.BlockSpec((1, d), lambda s: (0, 0))
    mod = pl.BlockSpec((1, 1, d), lambda s: (s // per_b, 0, 0))
    wcol = pl.BlockSpec((1, tr, 1), lambda s: (s // per_b, s % per_b, 0))
    idx = pl.BlockSpec((1, 1, tr), lambda s: (s, 0, 0), memory_space=pltpu.SMEM)
    idx_next = pl.BlockSpec((1, 1, tr), lambda s: (jnp.minimum(s + 1, steps - 1), 0, 0),
                            memory_space=pltpu.SMEM)
    with_next = next_mod is not None
    p1 = pos1.reshape(steps, 1, tr)
    p2 = pos2.reshape(steps, 1, tr)
    in_specs = [idx, idx, idx_next, idx_next, pl.BlockSpec(memory_space=pl.ANY),
                wcol, wcol, row, mod, vec, vec]
    args = [p1, p2, p1, p2, ys, w1.reshape(bsz, t, 1), w2.reshape(bsz, t, 1), x, gate,
            ln_g.reshape(1, d), ln_b.reshape(1, d)]
    out_specs = [row]
    out_shape = [jax.ShapeDtypeStruct((bsz, t, d), F32)]
    if with_next:
        in_specs += [mod, mod]
        args += list(next_mod)
        out_specs.append(row)
        out_shape.append(jax.ShapeDtypeStruct((bsz, t, d), BF16))
    res = pl.pallas_call(
        functools.partial(_combine_kernel, with_next, steps),
        grid=(steps,),
        in_specs=in_specs,
        out_specs=out_specs,
        out_shape=out_shape,
        scratch_shapes=[pltpu.VMEM((2, tr, d), F32), pltpu.VMEM((2, tr, d), F32),
                        pltpu.SemaphoreType.DMA((2, 2))],
        compiler_params=_cparams(("arbitrary",), VMEM_LIMIT),
        name="combine_ln",
    )(*args)
    return res if with_next else (res[0], None)


def kernel(x, c, emb_ln_g, emb_ln_b, w_ada, b_ada, w_in, conv_w, mix_beta, w_out, hg_lb_logits,
           ln_g, ln_b, w_router, router_bias, w_gate, w_up, w_down):
    bsz, t, d = x.shape
    n = bsz * t
    assert n % LANES == 0 and t % SEQ_CHUNK == 0
    n_tiles = 2 * n // MOE_TILE + N_EXPERTS
    assert n_tiles <= LANES

    mod = _ada_mod(c, w_ada, b_ada)
    mod = mod.reshape(DEPTH, bsz, 6, 1, d)

    def mods(l):
        return [mod[l, :, i] for i in range(6)]

    wr_hi = jnp.pad(w_router, ((0, 0), (0, LANES - N_EXPERTS)))
    wr_hi16 = wr_hi.astype(BF16)
    wr_lo16 = (wr_hi - wr_hi16.astype(F32)).astype(BF16)

    w_out16 = w_out.astype(BF16)

    m = [mods(l) for l in range(DEPTH)]
    xcur, h = _ln_mod(x, emb_ln_g, emb_ln_b, m[0][1], m[0][0])
    for l in range(DEPTH):
        _, _, gate1, shift2, scale2, gate2 = m[l]
        beta = mix_beta[l].reshape(1, -1)
        proj = _in_proj(h.reshape(n, d), w_in, l, BF16).reshape(bsz, t, IN_COLS)
        yc = _conv_mixer(proj, conv_w[l], beta)
        yr = _ret_mixer(proj, beta)
        yh = _hg_mixer(proj, hg_lb_logits, beta, l)
        x1, h2, logits = _out_proj(yc, yr, yh, w_out16, l, xcur, gate1,
                                   ln_g[l, 0], ln_b[l, 0], scale2, shift2, wr_hi16, wr_lo16)
        logits_t = logits.reshape(n, LANES)[:, :N_EXPERTS].T.reshape(N_EXPERTS, n // LANES, LANES)
        pos1, pos2, w1, w2, tiles = _route(logits_t, router_bias, n_tiles)
        tile_e, n_valid = tiles[0, :n_tiles], tiles[1, :1]
        first, slot, nxt, fill = _moe_schedule(tile_e, n_valid)
        xs = _scatter_rows(h2.reshape(n, d), pos1.reshape(n), pos2.reshape(n), fill, n_tiles)
        ys = _moe_experts(xs, tile_e, n_valid, (first, slot, nxt), w_gate, w_up, w_down, l)
        next_mod = (m[l + 1][1], m[l + 1][0]) if l + 1 < DEPTH else None
        xcur, h = _combine(ys, pos1.reshape(n), pos2.reshape(n), w1, w2, x1, gate2,
                           ln_g[l, 1], ln_b[l, 1], next_mod)
    return xcur
```

```python
import functools
import math

import jax
import jax.numpy as jnp
from jax import lax
from jax.experimental import pallas as pl
from jax.experimental.pallas import tpu as pltpu

F32 = jnp.float32
BF16 = jnp.bfloat16
I32 = jnp.int32

DEPTH = 2
CONV_WIDTH = 512
RET_WIDTH = 768
HG_WIDTH = 768
HEAD_DIM = 128
RET_HEADS = RET_WIDTH // HEAD_DIM
HG_HEADS = HG_WIDTH // HEAD_DIM
IN_COLS = 3 * CONV_WIDTH + 4 * RET_WIDTH + 5 * HG_WIDTH
ROPE_BASE = 10000.0
N_EXPERTS = 16
N_GROUPS = 4
EXPERTS_PER_GROUP = N_EXPERTS // N_GROUPS
MASK_NEG = -1e9
DEEPNORM_ALPHA = (2.0 * DEPTH) ** 0.25
LN_EPS = 1e-5
HEAD_EPS = 1e-6
FORGET_FLOOR = 1e-6

LANES = 128
SUBLANES = 8

_CB, _CC, _CH = 0, 4, 8
_RQ, _RK, _RV, _RG = 12, 18, 24, 30
_HQ, _HFF, _HFB, _HI, _HG = 36, 42, 48, 54, 60
_OUT_RET, _OUT_HG = 4, 10

SEQ_CHUNK = 128
MOE_TILE = 256
VMEM_LIMIT = 56 * 1024 * 1024
MOE_VMEM_LIMIT = 60 * 1024 * 1024


def _cparams(sem, vmem=None):
    return pltpu.CompilerParams(dimension_semantics=sem, vmem_limit_bytes=vmem)


def _silu(x):
    return x * jax.nn.sigmoid(x)


def _dot(a, b):
    return jnp.dot(a, b, preferred_element_type=F32)


def _dot_nt(a, b):
    return lax.dot_general(a, b, (((1,), (1,)), ((), ())), preferred_element_type=F32)


def _dot_tn(a, b):
    return lax.dot_general(a, b, (((0,), (0,)), ((), ())), preferred_element_type=F32)


def _ada_kernel(c_ref, w_ref, b_ref, o_ref):
    cond = _silu(c_ref[...])
    o_ref[0] = _dot(cond.astype(BF16), w_ref[0].astype(BF16)) + b_ref[0]


def _ada_mod(c, w_ada, b_ada):
    depth, d, n6 = w_ada.shape
    b = c.shape[0]
    bp = -(-b // SUBLANES) * SUBLANES
    cp = jnp.pad(c, ((0, bp - b), (0, 0)))
    tn = 512
    out = pl.pallas_call(
        _ada_kernel,
        grid=(depth, n6 // tn),
        in_specs=[
            pl.BlockSpec((bp, d), lambda l, j: (0, 0)),
            pl.BlockSpec((1, d, tn), lambda l, j: (l, 0, j)),
            pl.BlockSpec((1, 1, tn), lambda l, j: (l, 0, j)),
        ],
        out_specs=pl.BlockSpec((1, bp, tn), lambda l, j: (l, 0, j)),
        out_shape=jax.ShapeDtypeStruct((depth, bp, n6), F32),
        compiler_params=_cparams(("arbitrary", "arbitrary")),
        name="ada_mod",
    )(cp, w_ada, b_ada.reshape(depth, 1, n6))
    return out[:, :b, :]


def _layer_norm_rows(z, g, b):
    mu = jnp.mean(z, axis=-1, keepdims=True)
    zc = z - mu
    var = jnp.mean(zc * zc, axis=-1, keepdims=True)
    return zc * lax.rsqrt(var + LN_EPS) * g + b


def _ln_mod_kernel(x_ref, g_ref, b_ref, sc_ref, sh_ref, xo_ref, h_ref):
    y = _layer_norm_rows(x_ref[0], g_ref[...], b_ref[...])
    xo_ref[0] = y
    h_ref[0] = (y * (1.0 + sc_ref[0]) + sh_ref[0]).astype(BF16)


def _ln_mod(x, g, b, scale, shift):
    bsz, t, d = x.shape
    tr = 512
    row = pl.BlockSpec((1, tr, d), lambda i, j: (i, j, 0))
    vec = pl.BlockSpec((1, d), lambda i, j: (0, 0))
    mod = pl.BlockSpec((1, 1, d), lambda i, j: (i, 0, 0))
    return pl.pallas_call(
        _ln_mod_kernel,
        grid=(bsz, t // tr),
        in_specs=[row, vec, vec, mod, mod],
        out_specs=[row, row],
        out_shape=[jax.ShapeDtypeStruct((bsz, t, d), F32),
                   jax.ShapeDtypeStruct((bsz, t, d), BF16)],
        compiler_params=_cparams(("arbitrary", "arbitrary")),
        name="ln_mod",
    )(x, g.reshape(1, d), b.reshape(1, d), scale, shift)


def _in_proj_kernel(a_ref, w_ref, o_ref, w16_scr):
    @pl.when(pl.program_id(1) == 0)
    def _():
        w16_scr[...] = w_ref[0].astype(BF16)

    o_ref[...] = _dot(a_ref[...], w16_scr[...]).astype(o_ref.dtype)


def _in_proj(h, w_in, layer, out_dtype):
    n, d = h.shape
    nc = w_in.shape[2]
    tm, tn = 1024, 1408
    tm = min(tm, n)
    return pl.pallas_call(
        _in_proj_kernel,
        grid=(nc // tn, n // tm),
        in_specs=[pl.BlockSpec((tm, d), lambda j, i: (i, 0)),
                  pl.BlockSpec((1, d, tn), lambda j, i: (layer, 0, j))],
        out_specs=pl.BlockSpec((tm, tn), lambda j, i: (i, j)),
        out_shape=jax.ShapeDtypeStruct((n, nc), out_dtype),
        scratch_shapes=[pltpu.VMEM((d, tn), BF16)],
        compiler_params=_cparams(("arbitrary", "arbitrary"), VMEM_LIMIT),
        name="in_proj",
    )(h, w_in)


def _conv_kernel(cb_ref, cc_ref, ch_ref, w_ref, beta_ref, o_ref):
    u = cc_ref[0].astype(F32) * ch_ref[0].astype(F32)
    t = u.shape[0]
    row = lax.broadcasted_iota(I32, u.shape, 0)
    prev = jnp.where(row == 0, 0.0, pltpu.roll(u, 1, 0))
    nxt = jnp.where(row == t - 1, 0.0, pltpu.roll(u, t - 1, 0))
    w = w_ref[...]
    y = cb_ref[0].astype(F32) * (prev * w[0:1] + u * w[1:2] + nxt * w[2:3])
    o_ref[0] = (y * beta_ref[...]).astype(o_ref.dtype)


def _conv_mixer(proj, conv_w, beta):
    bsz, t, _ = proj.shape
    cw = 256
    nb = CONV_WIDTH // cw
    per = LANES * 1

    def col(off):
        return pl.BlockSpec((1, t, cw), lambda b, j, off=off: (b, 0, off * per // cw + j))

    return pl.pallas_call(
        _conv_kernel,
        grid=(bsz, nb),
        in_specs=[col(_CB), col(_CC), col(_CH),
                  pl.BlockSpec((3, cw), lambda b, j: (0, j)),
                  pl.BlockSpec((1, cw), lambda b, j: (0, j))],
        out_specs=pl.BlockSpec((1, t, cw), lambda b, j: (b, 0, j)),
        out_shape=jax.ShapeDtypeStruct((bsz, t, CONV_WIDTH), BF16),
        compiler_params=_cparams(("arbitrary", "arbitrary"), VMEM_LIMIT),
        name="conv_mixer",
    )(proj, proj, proj, conv_w, beta)


def _ret_kernel(q_ref, k_ref, v_ref, g_ref, cos_ref, sin_ref, lg_ref, beta_ref,
                o_ref, o_scr, qf_scr, qb_scr, uf_scr, ub_scr):
    L = SEQ_CHUNK
    t = q_ref.shape[1]
    n = t // L
    lgf = lg_ref[0, 0:1, :]
    lgb = lg_ref[0, 1:2, :]
    r = lax.broadcasted_iota(I32, (L, L), 0).astype(F32)
    c = lax.broadcasted_iota(I32, (L, L), 1).astype(F32)
    rel = r - c
    dbi = jnp.where(rel > 0, jnp.exp(lgf * jnp.maximum(rel, 0.0)),
                    jnp.where(rel < 0, jnp.exp(lgb * jnp.maximum(-rel, 0.0)), 2.0))
    qf = jnp.exp(lgf * (r + 1.0))
    kf = jnp.exp(lgf * (L - 1.0 - r))
    qb = jnp.exp(lgb * (L - r))
    kb = jnp.exp(lgb * r)
    gf_l = jnp.exp(lgf * float(L))
    gb_l = jnp.exp(lgb * float(L))
    scale = HEAD_DIM ** -0.5

    def rot(x, cs, sn):
        return x * cs + pltpu.roll(x, HEAD_DIM // 2, 1) * sn

    def rows(ci):
        return pl.ds(pl.multiple_of(ci * L, L), L)

    def local_body(ci, carry):
        sl = rows(ci)
        cs = cos_ref[sl, :]
        sn = sin_ref[sl, :]
        q = rot(q_ref[0, sl, :].astype(F32), cs, sn)
        k = rot(k_ref[0, sl, :].astype(F32), cs, sn) * scale
        v16 = v_ref[0, sl, :].astype(BF16)
        s = _dot_nt(q.astype(BF16), k.astype(BF16)) * dbi
        o_scr[sl, :] = _dot(s.astype(BF16), v16)
        qf_scr[sl, :] = (q * qf).astype(BF16)
        qb_scr[sl, :] = (q * qb).astype(BF16)
        uf_scr[ci] = _dot_tn((k * kf).astype(BF16), v16)
        ub_scr[ci] = _dot_tn((k * kb).astype(BF16), v16)
        return carry

    lax.fori_loop(0, n, local_body, 0, unroll=8)

    def scan_body(i, states):
        sf, sb = states
        cf, cb = i, n - 1 - i
        slf, slb = rows(cf), rows(cb)
        o_scr[slf, :] += _dot(qf_scr[slf, :], sf.astype(BF16))
        o_scr[slb, :] += _dot(qb_scr[slb, :], sb.astype(BF16))
        return gf_l * sf + uf_scr[cf], gb_l * sb + ub_scr[cb]

    zero = jnp.zeros((HEAD_DIM, HEAD_DIM), F32)
    lax.fori_loop(0, n, scan_body, (zero, zero), unroll=16)

    beta = beta_ref[...]

    def norm_body(ci, carry):
        sl = rows(ci)
        o = o_scr[sl, :]
        mu = jnp.mean(o, axis=-1, keepdims=True)
        oc = o - mu
        var = jnp.mean(oc * oc, axis=-1, keepdims=True)
        y = oc * lax.rsqrt(var + HEAD_EPS) * _silu(g_ref[0, sl, :].astype(F32)) * beta
        o_ref[0, sl, :] = y.astype(o_ref.dtype)
        return carry

    lax.fori_loop(0, n, norm_body, 0, unroll=4)


def _rotary_tables(t):
    half = HEAD_DIM // 2
    inv_freq = ROPE_BASE ** (-jnp.arange(half, dtype=F32) / half)
    ang = jnp.arange(t, dtype=F32)[:, None] * inv_freq[None, :]
    cos, sin = jnp.cos(ang), jnp.sin(ang)
    return jnp.concatenate([cos, cos], -1), jnp.concatenate([-sin, sin], -1)


def _ret_log_decays():
    head = jnp.arange(RET_HEADS, dtype=F32)
    lg_f = jnp.log1p(-jnp.exp2(-5.0 - head))
    lg_b = jnp.log1p(-jnp.exp2(-5.5 - head))
    lg = jnp.stack([lg_f, lg_b], axis=1)
    return jnp.broadcast_to(lg[:, :, None], (RET_HEADS, 2, LANES))


def _ret_mixer(proj, beta):
    bsz, t, _ = proj.shape
    n_chunks = t // SEQ_CHUNK
    assert n_chunks % 2 == 0
    cosf, sinf = _rotary_tables(t)

    def col(off):
        return pl.BlockSpec((1, t, HEAD_DIM), lambda b, h, off=off: (b, 0, off + h))

    tab = pl.BlockSpec((t, HEAD_DIM), lambda b, h: (0, 0))
    return pl.pallas_call(
        _ret_kernel,
        grid=(bsz, RET_HEADS),
        in_specs=[col(_RQ), col(_RK), col(_RV), col(_RG), tab, tab,
                  pl.BlockSpec((1, 2, LANES), lambda b, h: (h, 0, 0)),
                  pl.BlockSpec((1, HEAD_DIM), lambda b, h: (0, _OUT_RET + h))],
        out_specs=pl.BlockSpec((1, t, HEAD_DIM), lambda b, h: (b, 0, h)),
        out_shape=jax.ShapeDtypeStruct((bsz, t, RET_WIDTH), BF16),
        scratch_shapes=[pltpu.VMEM((t, HEAD_DIM), F32),
                        pltpu.VMEM((t, HEAD_DIM), BF16), pltpu.VMEM((t, HEAD_DIM), BF16),
                        pltpu.VMEM((n_chunks, HEAD_DIM, HEAD_DIM), F32),
                        pltpu.VMEM((n_chunks, HEAD_DIM, HEAD_DIM), F32)],
        compiler_params=_cparams(("arbitrary", "arbitrary"), VMEM_LIMIT),
        name="ret_mixer",
    )(proj, proj, proj, proj, cosf, sinf, _ret_log_decays(), beta)


_HG_BASE = SUBLANES
_HG_LEVELS = tuple(m for m in (8, 16, 32, 64) if m < SEQ_CHUNK)


def _cumsum_rows(x, reverse):
    n = x.shape[0]
    row = lax.broadcasted_iota(I32, x.shape, 0)
    sh = 1
    while sh < n:
        if reverse:
            x = x + jnp.where(row < n - sh, pltpu.roll(x, n - sh, 0), 0.0)
        else:
            x = x + jnp.where(row >= sh, pltpu.roll(x, sh, 0), 0.0)
        sh *= 2
    return x


def _hg_masks(reverse):
    C = SEQ_CHUNK
    r = lax.broadcasted_iota(I32, (C, C), 0)
    c = lax.broadcasted_iota(I32, (C, C), 1)
    if reverse:
        r, c = c, r
    sh = _HG_BASE.bit_length() - 1
    masks = [((r >> sh) == (c >> sh)) & (c <= r)]
    for m in _HG_LEVELS:
        sh = m.bit_length() - 1
        masks.append(((r >> (sh + 1)) == (c >> (sh + 1)))
                     & (((r >> sh) & 1) == 1) & (((c >> sh) & 1) == 0))
    return [jnp.where(m, 1.0, 0.0) for m in masks]


def _hg_scores(q, kk, cum, mask_ref, reverse):
    C = SEQ_CHUNK
    d = 1 if reverse else 0

    def split(x, blk):
        return x.reshape(C // blk, blk, LANES)

    b = _HG_BASE
    ref_row = b // 2 if reverse else b // 2 - 1
    cum3 = split(cum, b)
    rel = cum3 - cum3[:, ref_row:ref_row + 1, :]
    qt = (split(q, b) * jnp.exp(rel)).reshape(C, LANES).astype(BF16)
    kt = (split(kk, b) * jnp.exp(-rel)).reshape(C, LANES).astype(BF16)
    a = jnp.where(mask_ref[d, 0] > 0.5, _dot_nt(qt, kt), 0.0)
    for li, m in enumerate(_HG_LEVELS):
        blk = 2 * m
        ref_row = m if reverse else m - 1
        cum3 = split(cum, blk)
        rel = cum3 - cum3[:, ref_row:ref_row + 1, :]
        dec = jnp.exp(-jnp.abs(rel)).reshape(C, LANES)
        halves = []
        for j in range(C // m):
            is_query = (j % 2 == 1) != reverse
            halves.append((q if is_query else kk)[j * m:(j + 1) * m])
        x = (jnp.concatenate(halves, axis=0) * dec).astype(BF16)
        a = a + mask_ref[d, li + 1] * _dot_nt(x, x)
    return a


def _hg_kernel(layer, q_ref, zf_ref, zb_ref, v_ref, g_ref, lbl_ref, beta_ref, o_ref,
               o_scr, qe_scr, u_scr, dec_scr, mask_scr):
    C = SEQ_CHUNK
    t = q_ref.shape[1]
    n = t // C
    for d, reverse in enumerate((False, True)):
        for li, m in enumerate(_hg_masks(reverse)):
            mask_scr[d, li] = m
    logits = lbl_ref[...].astype(F32)
    e = jnp.exp(logits - jnp.max(logits, axis=0, keepdims=True))
    p = e / jnp.sum(e, axis=0, keepdims=True)
    lb = p[0]
    for l in range(1, layer + 1):
        lb = lb + p[l]
    lb = lb - p[0]

    def gates(z_ref, sl, lb_row):
        f = lb_row + (1.0 - lb_row) * jax.nn.sigmoid(z_ref[0, sl, :].astype(F32))
        f = jnp.maximum(f, FORGET_FLOOR)
        return jnp.log(f), 1.0 - f

    def rows(ci):
        return pl.ds(pl.multiple_of(ci * C, C), C)

    def local_dir(ci, sl, q, v16, z_ref, d):
        reverse = d == 1
        logf, kk = gates(z_ref, sl, lb[d:d + 1, :])
        cum = _cumsum_rows(logf, reverse)
        a = _hg_scores(q, kk, cum, mask_scr, reverse)
        total = cum[0:1, :] if reverse else cum[C - 1:C, :]
        qe_scr[d, sl, :] = (q * jnp.exp(cum)).astype(BF16)
        k_tail = kk * jnp.exp(total - cum)
        u_scr[d, ci] = _dot_tn(v16, k_tail.astype(BF16))
        dec_scr[d, ci] = jnp.broadcast_to(jnp.exp(total), (SUBLANES, LANES))
        return _dot(a.astype(BF16), v16)

    def local_body(ci, carry):
        sl = rows(ci)
        q = q_ref[0, sl, :].astype(F32)
        v16 = v_ref[0, sl, :].astype(BF16)
        o_scr[sl, :] = (local_dir(ci, sl, q, v16, zf_ref, 0)
                        + local_dir(ci, sl, q, v16, zb_ref, 1))
        return carry

    lax.fori_loop(0, n, local_body, 0, unroll=8)

    def scan_body(i, states):
        sf, sb = states
        cf, cb = i, n - 1 - i
        slf, slb = rows(cf), rows(cb)
        o_scr[slf, :] += _dot_nt(qe_scr[0, slf, :], sf.astype(BF16))
        o_scr[slb, :] += _dot_nt(qe_scr[1, slb, :], sb.astype(BF16))
        return (sf * dec_scr[0, cf, 0:1, :] + u_scr[0, cf],
                sb * dec_scr[1, cb, 0:1, :] + u_scr[1, cb])

    zero = jnp.zeros((HEAD_DIM, HEAD_DIM), F32)
    lax.fori_loop(0, n, scan_body, (zero, zero), unroll=16)

    beta = beta_ref[...]

    def norm_body(ci, carry):
        sl = rows(ci)
        o = o_scr[sl, :]
        ms = jnp.mean(o * o, axis=-1, keepdims=True)
        y = o * lax.rsqrt(ms + HEAD_EPS) * _silu(g_ref[0, sl, :].astype(F32)) * beta
        o_ref[0, sl, :] = y.astype(o_ref.dtype)
        return carry

    lax.fori_loop(0, n, norm_body, 0, unroll=8)


def _hg_mixer(proj, lb_logits, beta, layer):
    bsz, t, _ = proj.shape
    n_chunks = t // SEQ_CHUNK
    assert n_chunks % 2 == 0

    def col(off):
        return pl.BlockSpec((1, t, HEAD_DIM), lambda b, h, off=off: (b, 0, off + h))

    return pl.pallas_call(
        functools.partial(_hg_kernel, layer),
        grid=(bsz, HG_HEADS),
        in_specs=[col(_HQ), col(_HFF), col(_HFB), col(_HI), col(_HG),
                  pl.BlockSpec((DEPTH, 2, HEAD_DIM), lambda b, h: (0, 0, h)),
                  pl.BlockSpec((1, HEAD_DIM), lambda b, h: (0, _OUT_HG + h))],
        out_specs=pl.BlockSpec((1, t, HEAD_DIM), lambda b, h: (b, 0, h)),
        out_shape=jax.ShapeDtypeStruct((bsz, t, HG_WIDTH), BF16),
        scratch_shapes=[pltpu.VMEM((t, HEAD_DIM), F32),
                        pltpu.VMEM((2, t, HEAD_DIM), BF16),
                        pltpu.VMEM((2, n_chunks, HEAD_DIM, HEAD_DIM), F32),
                        pltpu.VMEM((2, n_chunks, SUBLANES, LANES), F32),
                        pltpu.VMEM((2, 1 + len(_HG_LEVELS), SEQ_CHUNK, SEQ_CHUNK), F32)],
        compiler_params=_cparams(("arbitrary", "arbitrary"), VMEM_LIMIT),
        name="hg_mixer",
    )(proj, proj, proj, proj, proj, lb_logits, beta)


_OUT_PROJ_SPLIT = 2


def _out_proj_kernel(yc_ref, yr_ref, yh_ref, w_ref, x_ref, gate_ref, g_ref, b_ref,
                     sc_ref, sh_ref, wrh_ref, wrl_ref, xo_ref, h_ref, lo_ref):
    c0, c1 = CONV_WIDTH, CONV_WIDTH + RET_WIDTH
    tm = x_ref.shape[1]
    for half in range(_OUT_PROJ_SPLIT):
        sl = pl.ds(half * (tm // _OUT_PROJ_SPLIT), tm // _OUT_PROJ_SPLIT)
        y = (_dot(yc_ref[0, sl, :], w_ref[0, 0:c0, :]) + _dot(yr_ref[0, sl, :], w_ref[0, c0:c1, :])
             + _dot(yh_ref[0, sl, :], w_ref[0, c1:, :]))
        z = DEEPNORM_ALPHA * x_ref[0, sl, :] + (1.0 + gate_ref[0]) * y
        xn = _layer_norm_rows(z, g_ref[...], b_ref[...])
        xo_ref[0, sl, :] = xn
        h = xn * (1.0 + sc_ref[0]) + sh_ref[0]
        h_ref[0, sl, :] = h
        h_hi = h.astype(BF16)
        h_lo = (h - h_hi.astype(F32)).astype(BF16)
        wrh = wrh_ref[...]
        lo_ref[0, sl, :] = _dot(h_hi, wrh) + _dot(h_lo, wrh) + _dot(h_hi, wrl_ref[...])


def _out_proj(yc, yr, yh, w_out16, layer, x, gate, ln_g, ln_b, scale2, shift2, wr_hi, wr_lo):
    bsz, t, d = x.shape
    tm = 512

    def row(width):
        return pl.BlockSpec((1, tm, width), lambda b, i: (b, i, 0))

    vec = pl.BlockSpec((1, d), lambda b, i: (0, 0))
    mod = pl.BlockSpec((1, 1, d), lambda b, i: (b, 0, 0))
    wr = pl.BlockSpec((d, LANES), lambda b, i: (0, 0))
    return pl.pallas_call(
        _out_proj_kernel,
        grid=(bsz, t // tm),
        in_specs=[row(CONV_WIDTH), row(RET_WIDTH), row(HG_WIDTH),
                  pl.BlockSpec((1, d, d), lambda b, i: (layer, 0, 0)),
                  row(d), mod, vec, vec, mod, mod, wr, wr],
        out_specs=[row(d), row(d), row(LANES)],
        out_shape=[jax.ShapeDtypeStruct((bsz, t, d), F32),
                   jax.ShapeDtypeStruct((bsz, t, d), F32),
                   jax.ShapeDtypeStruct((bsz, t, LANES), F32)],
        compiler_params=_cparams(("arbitrary", "arbitrary"), VMEM_LIMIT),
        name="out_proj",
    )(yc, yr, yh, w_out16, x, gate, ln_g.reshape(1, d), ln_b.reshape(1, d),
      scale2, shift2, wr_hi, wr_lo)


def _route_kernel(n_tiles, lt_ref, bias_ref, pos1_ref, pos2_ref, w1_ref, w2_ref, tile_ref):
    E = N_EXPERTS
    logits = [lt_ref[e] for e in range(E)]
    shape = logits[0].shape
    mx = functools.reduce(jnp.maximum, logits)
    ex = [jnp.exp(l - mx) for l in logits]
    den = functools.reduce(lambda a, b: a + b, ex)
    scores = [x / den for x in ex]
    sel = [scores[e] + bias_ref[e] for e in range(E)]

    best_g = jnp.zeros(shape, I32)
    best_v = None
    for g in range(N_GROUPS):
        a, b, c, d = sel[EXPERTS_PER_GROUP * g: EXPERTS_PER_GROUP * (g + 1)]
        hi1, lo1 = jnp.maximum(a, b), jnp.minimum(a, b)
        hi2, lo2 = jnp.maximum(c, d), jnp.minimum(c, d)
        gs = jnp.maximum(hi1, hi2) + jnp.maximum(jnp.minimum(hi1, hi2), jnp.maximum(lo1, lo2))
        if g == 0:
            best_v = gs
        else:
            better = gs > best_v
            best_g = jnp.where(better, g, best_g)
            best_v = jnp.where(better, gs, best_v)

    masked = [jnp.where(best_g == (e // EXPERTS_PER_GROUP), sel[e], MASK_NEG) for e in range(E)]

    def arg_top(vals, exclude):
        idx = jnp.zeros(shape, I32)
        val = None
        for e in range(E):
            v = vals[e] if exclude is None else jnp.where(exclude == e, -jnp.inf, vals[e])
            if e == 0:
                val = v
            else:
                better = v > val
                idx = jnp.where(better, e, idx)
                val = jnp.where(better, v, val)
        return idx

    idx1 = arg_top(masked, None)
    idx2 = arg_top(masked, idx1)
    s1 = functools.reduce(lambda a, b: a + b, [jnp.where(idx1 == e, scores[e], 0.0) for e in range(E)])
    s2 = functools.reduce(lambda a, b: a + b, [jnp.where(idx2 == e, scores[e], 0.0) for e in range(E)])
    w1_ref[...] = s1 / (s1 + s2)
    w2_ref[...] = s2 / (s1 + s2)

    rows = shape[0]
    li = lax.broadcasted_iota(I32, (LANES, LANES), 0)
    lj = lax.broadcasted_iota(I32, (LANES, LANES), 1)
    upper = (li <= lj).astype(BF16)
    ri = lax.broadcasted_iota(I32, (rows, rows), 0)
    rj = lax.broadcasted_iota(I32, (rows, rows), 1)
    lower = (rj < ri).astype(BF16)
    tile_start = (lax.broadcasted_iota(I32, (1, LANES), 1) * MOE_TILE).astype(F32)
    start = jnp.zeros((1, 1), F32)
    pos1 = jnp.zeros(shape, F32)
    pos2 = jnp.zeros(shape, F32)
    tile_e = jnp.zeros((1, LANES), F32)
    for e in range(E):
        hit1 = idx1 == e
        hit2 = idx2 == e
        onehot = jnp.where(hit1 | hit2, 1.0, 0.0)
        pref = _dot(onehot.astype(BF16), upper)
        row_tot = pref[:, LANES - 1:LANES]
        row_off = _dot(lower, jnp.broadcast_to(row_tot, shape).astype(BF16))[:, 0:1]
        rank = pref - onehot + row_off
        count = jnp.sum(onehot, keepdims=True)
        dest = start + rank
        pos1 = jnp.where(hit1, dest, pos1)
        pos2 = jnp.where(hit2, dest, pos2)
        start = start + jnp.ceil(count / MOE_TILE) * MOE_TILE
        tile_e = tile_e + jnp.where(tile_start >= start, 1.0, 0.0)
    pos1_ref[...] = pos1.astype(I32)
    pos2_ref[...] = pos2.astype(I32)
    n_valid = start / MOE_TILE
    tile_id = lax.broadcasted_iota(I32, (1, LANES), 1).astype(F32)
    last_e = jnp.sum(jnp.where(tile_id == n_valid - 1.0, tile_e, 0.0), keepdims=True)
    tile_e = jnp.where(tile_id < n_valid, tile_e, last_e)
    sub = lax.broadcasted_iota(I32, (SUBLANES, LANES), 0)
    tile_ref[...] = jnp.where(sub == 0, tile_e, jnp.broadcast_to(n_valid, (SUBLANES, LANES))).astype(I32)


def _route(logits_t, router_bias, n_tiles):
    e, rows, lanes = logits_t.shape
    tok = jax.ShapeDtypeStruct((rows, lanes), I32)
    tokf = jax.ShapeDtypeStruct((rows, lanes), F32)
    full = pl.BlockSpec((rows, lanes), lambda i: (0, 0))
    return pl.pallas_call(
        functools.partial(_route_kernel, n_tiles),
        grid=(1,),
        in_specs=[pl.BlockSpec((e, rows, lanes), lambda i: (0, 0, 0)),
                  pl.BlockSpec(memory_space=pltpu.SMEM)],
        out_specs=[full, full, full, full, pl.BlockSpec((SUBLANES, LANES), lambda i: (0, 0))],
        out_shape=[tok, tok, tokf, tokf, jax.ShapeDtypeStruct((SUBLANES, LANES), I32)],
        compiler_params=_cparams(("arbitrary",)),
        name="route",
    )(logits_t, router_bias)


_ROWS_PER_STEP = 512


_DMA_UNROLL = 8


def _scatter_rows_kernel(n_tiles, p1_ref, p2_ref, fill_ref, src_ref, dst_ref, zero_scr, sem, zsem):
    def zero_copy(i):
        row0 = pl.multiple_of(i * MOE_TILE, MOE_TILE)
        return pltpu.make_async_copy(zero_scr, dst_ref.at[pl.ds(row0, MOE_TILE)], zsem)

    @pl.when(pl.program_id(0) == 0)
    def _():
        zero_scr[...] = jnp.zeros_like(zero_scr)

        def zstart(i, carry):
            @pl.when(fill_ref[i] == 1)
            def _():
                zero_copy(i).start()
            return carry

        def zwait(i, carry):
            @pl.when(fill_ref[i] == 1)
            def _():
                zero_copy(i).wait()
            return carry

        lax.fori_loop(0, n_tiles, zstart, 0)
        lax.fori_loop(0, n_tiles, zwait, 0)

    def copies(r):
        row = src_ref.at[pl.ds(r, 1)]
        return (pltpu.make_async_copy(row, dst_ref.at[pl.ds(p1_ref[0, 0, r], 1)], sem.at[0]),
                pltpu.make_async_copy(row, dst_ref.at[pl.ds(p2_ref[0, 0, r], 1)], sem.at[1]))

    def start(r, carry):
        for queue, cp in enumerate(copies(r)):
            cp.start(priority=queue)
        return carry

    def wait(r, carry):
        for cp in copies(r):
            cp.wait()
        return carry

    lax.fori_loop(0, _ROWS_PER_STEP, start, 0, unroll=_DMA_UNROLL)
    lax.fori_loop(0, _ROWS_PER_STEP, wait, 0, unroll=_DMA_UNROLL)


def _scatter_rows(src, pos1, pos2, fill_tile, n_tiles):
    n, d = src.shape
    steps = n // _ROWS_PER_STEP
    idx = pl.BlockSpec((1, 1, _ROWS_PER_STEP), lambda i: (i, 0, 0), memory_space=pltpu.SMEM)
    return pl.pallas_call(
        functools.partial(_scatter_rows_kernel, n_tiles),
        grid=(steps,),
        in_specs=[idx, idx, pl.BlockSpec(memory_space=pltpu.SMEM),
                  pl.BlockSpec((_ROWS_PER_STEP, d), lambda i: (i, 0))],
        out_specs=pl.BlockSpec(memory_space=pl.ANY),
        out_shape=jax.ShapeDtypeStruct((n_tiles * MOE_TILE, d), src.dtype),
        scratch_shapes=[pltpu.VMEM((MOE_TILE, d), src.dtype),
                        pltpu.SemaphoreType.DMA((2,)), pltpu.SemaphoreType.DMA],
        compiler_params=_cparams(("arbitrary",), VMEM_LIMIT),
        name="scatter_rows",
    )(pos1.reshape(steps, 1, _ROWS_PER_STEP), pos2.reshape(steps, 1, _ROWS_PER_STEP), fill_tile, src)


_W_CHUNK = 128
_W_RING = 11


def _moe_kernel(layer, te_ref, nv_ref, first_ref, slot_ref, nxt_ref,
                x_ref, wg_hbm, wu_hbm, wd_hbm, y_ref,
                cg_scr, cu_scr, cd_scr, stage_scr, sem, cnt_ref):
    i = pl.program_id(0)
    d, f = cg_scr.shape[1], cg_scr.shape[2]
    n_g = d // _W_CHUNK
    n_chunks = 2 * n_g + f // _W_CHUNK

    def for_chunk(c, e, slot, fn):
        k = c % _W_RING

        @pl.when(c < n_g)
        def _():
            row = pl.multiple_of(c * _W_CHUNK, _W_CHUNK)
            fn(wg_hbm.at[layer, e, pl.ds(row, _W_CHUNK), :], stage_scr.at[k, :, pl.ds(0, f)],
               sem.at[k], cg_scr.at[slot, pl.ds(row, _W_CHUNK), :])

        @pl.when((c >= n_g) & (c < 2 * n_g))
        def _():
            row = pl.multiple_of((c - n_g) * _W_CHUNK, _W_CHUNK)
            fn(wu_hbm.at[layer, e, pl.ds(row, _W_CHUNK), :], stage_scr.at[k, :, pl.ds(0, f)],
               sem.at[k], cu_scr.at[slot, pl.ds(row, _W_CHUNK), :])

        @pl.when(c >= 2 * n_g)
        def _():
            row = pl.multiple_of((c - 2 * n_g) * _W_CHUNK, _W_CHUNK)
            fn(wd_hbm.at[layer, e, pl.ds(row, _W_CHUNK), :], stage_scr.at[k],
               sem.at[k], cd_scr.at[slot, pl.ds(row, _W_CHUNK), :])

    def start(src, stage, s, dst):
        pltpu.make_async_copy(src, stage, s).start()

    def finish(src, stage, s, dst):
        pltpu.make_async_copy(src, stage, s).wait()
        dst[...] = stage[...].astype(BF16)

    def start_upto(e, slot, hi):
        hi = jnp.minimum(hi, n_chunks)

        def body(c, carry):
            for_chunk(c, e, slot, start)
            return carry

        lax.fori_loop(cnt_ref[0], hi, body, 0)
        cnt_ref[0] = jnp.maximum(cnt_ref[0], hi)

    def finish_started(e, slot):
        def body(c, carry):
            for_chunk(c, e, slot, finish)
            return carry

        lax.fori_loop(cnt_ref[1], cnt_ref[0], body, 0)
        cnt_ref[1] = cnt_ref[0]

    valid = i < nv_ref[0]

    @pl.when(i == 0)
    def _():
        cnt_ref[0] = 0
        cnt_ref[1] = 0

    @pl.when(valid & (first_ref[i] == 1))
    def _():
        e, slot = te_ref[i], slot_ref[i]

        def body(c, carry):
            start_upto(e, slot, c + _W_RING)
            for_chunk(c, e, slot, finish)
            return carry

        lax.fori_loop(cnt_ref[1], n_chunks, body, 0)
        cnt_ref[0] = 0
        cnt_ref[1] = 0

    @pl.when(valid & (nxt_ref[i] >= 0))
    def _():
        e, slot = nxt_ref[i], 1 - slot_ref[i]
        finish_started(e, slot)
        start_upto(e, slot, cnt_ref[0] + _W_RING)

    @pl.when(valid)
    def _():
        slot = slot_ref[i]
        x = x_ref[...].astype(BF16)
        a = _dot(x, cg_scr[slot])
        b = _dot(x, cu_scr[slot])
        hidden = (_silu(a) * b).astype(BF16)
        y_ref[...] = _dot(hidden, cd_scr[slot])

    @pl.when(jnp.logical_not(valid))
    def _():
        y_ref[...] = jnp.zeros_like(y_ref)


def _moe_schedule(tile_e, n_valid):
    n_tiles = tile_e.shape[0]
    idx = jnp.arange(n_tiles, dtype=I32)
    valid = idx < n_valid[0]
    first = valid & ((idx == 0) | (tile_e != jnp.roll(tile_e, 1)))
    run = jnp.cumsum(first.astype(I32)) - 1
    next_first = lax.cummin(jnp.where(first, idx, n_tiles), reverse=True)
    after = jnp.concatenate([next_first[1:], jnp.full((1,), n_tiles, I32)])
    nxt = jnp.where(valid & (after < n_tiles), tile_e[jnp.minimum(after, n_tiles - 1)], -1)
    last = valid & ((after == idx + 1) | (idx == n_valid[0] - 1))
    fill = last | jnp.logical_not(valid)
    return first.astype(I32), (run % 2).astype(I32), nxt.astype(I32), fill.astype(I32)


def _moe_experts(xs, tile_e, n_valid, schedule, w_gate, w_up, w_down, layer):
    n_rows, d = xs.shape
    f = w_gate.shape[3]
    tm = MOE_TILE
    n_tiles = n_rows // tm
    assert d % _W_CHUNK == 0 and f % _W_CHUNK == 0
    first, slot, nxt = schedule

    def xmap(i, te, nv, *_):
        return (jnp.minimum(i, nv[0] - 1), 0)

    any_spec = pl.BlockSpec(memory_space=pl.ANY)
    grid_spec = pltpu.PrefetchScalarGridSpec(
        num_scalar_prefetch=5,
        grid=(n_tiles,),
        in_specs=[pl.BlockSpec((tm, d), xmap), any_spec, any_spec, any_spec],
        out_specs=pl.BlockSpec((tm, d), lambda i, *_: (i, 0)),
        scratch_shapes=[pltpu.VMEM((2, d, f), BF16), pltpu.VMEM((2, d, f), BF16),
                        pltpu.VMEM((2, f, d), BF16),
                        pltpu.VMEM((_W_RING, _W_CHUNK, d), F32),
                        pltpu.SemaphoreType.DMA((_W_RING,)),
                        pltpu.SMEM((2,), I32)],
    )
    return pl.pallas_call(
        functools.partial(_moe_kernel, layer),
        grid_spec=grid_spec,
        out_shape=jax.ShapeDtypeStruct((n_rows, d), F32),
        compiler_params=_cparams(("arbitrary",), MOE_VMEM_LIMIT),
        name="moe_experts",
    )(tile_e, n_valid, first, slot, nxt, xs, w_gate, w_up, w_down)


_COMBINE_ROWS = 256


def _combine_kernel(with_next, steps, p1_ref, p2_ref, p1n_ref, p2n_ref, ys_ref, w1_ref, w2_ref,
                    x_ref, gate_ref, g_ref, b_ref, *rest):
    if with_next:
        sc_ref, sh_ref, xo_ref, h_ref, ya_scr, yb_scr, sem = rest
    else:
        xo_ref, ya_scr, yb_scr, sem = rest
    step = pl.program_id(0)

    def copies(pa, pb, buf, s, r):
        return (pltpu.make_async_copy(ys_ref.at[pl.ds(pa[0, 0, r], 1)],
                                      buf.at[0, pl.ds(r, 1)], sem.at[s, 0]),
                pltpu.make_async_copy(ys_ref.at[pl.ds(pb[0, 0, r], 1)],
                                      buf.at[1, pl.ds(r, 1)], sem.at[s, 1]))

    def wait_all(pa, pb, buf, s):
        def body(r, carry):
            for cp in copies(pa, pb, buf, s, r):
                cp.wait()
            return carry
        lax.fori_loop(0, _COMBINE_ROWS, body, 0, unroll=_DMA_UNROLL)

    @pl.when(step == 0)
    def _():
        def body(r, carry):
            for cp in copies(p1_ref, p2_ref, ya_scr, 0, r):
                cp.start()
            return carry
        lax.fori_loop(0, _COMBINE_ROWS, body, 0, unroll=_DMA_UNROLL)

    for par, (cur, nxt) in enumerate(((ya_scr, yb_scr), (yb_scr, ya_scr))):
        @pl.when(step % 2 == par)
        def _(par=par, cur=cur, nxt=nxt):
            wait_all(p1_ref, p2_ref, cur, par)
            for r in range(_COMBINE_ROWS):
                for cp in copies(p1n_ref, p2n_ref, nxt, 1 - par, r):
                    cp.start()
            y = w1_ref[0] * cur[0] + w2_ref[0] * cur[1]
            z = DEEPNORM_ALPHA * x_ref[0] + (1.0 + gate_ref[0]) * y
            xn = _layer_norm_rows(z, g_ref[...], b_ref[...])
            xo_ref[0] = xn
            if with_next:
                h_ref[0] = (xn * (1.0 + sc_ref[0]) + sh_ref[0]).astype(BF16)

            @pl.when(step == steps - 1)
            def _():
                wait_all(p1n_ref, p2n_ref, nxt, 1 - par)


def _combine(ys, pos1, pos2, w1, w2, x, gate, ln_g, ln_b, next_mod):
    bsz, t, d = x.shape
    tr = _COMBINE_ROWS
    per_b = t // tr
    steps = bsz * per_b
    row = pl.BlockSpec((1, tr, d), lambda s: (s // per_b, s % per_b, 0))
    vec = pl.BlockSpec((1, d), lambda s: (0, 0))
    mod = pl.BlockSpec((1, 1, d), lambda s: (s // per_b, 0, 0))
    wcol = pl.BlockSpec((1, tr, 1), lambda s: (s // per_b, s % per_b, 0))
    idx = pl.BlockSpec((1, 1, tr), lambda s: (s, 0, 0), memory_space=pltpu.SMEM)
    idx_next = pl.BlockSpec((1, 1, tr), lambda s: (jnp.minimum(s + 1, steps - 1), 0, 0),
                            memory_space=pltpu.SMEM)
    with_next = next_mod is not None
    p1 = pos1.reshape(steps, 1, tr)
    p2 = pos2.reshape(steps, 1, tr)
    in_specs = [idx, idx, idx_next, idx_next, pl.BlockSpec(memory_space=pl.ANY),
                wcol, wcol, row, mod, vec, vec]
    args = [p1, p2, p1, p2, ys, w1.reshape(bsz, t, 1), w2.reshape(bsz, t, 1), x, gate,
            ln_g.reshape(1, d), ln_b.reshape(1, d)]
    out_specs = [row]
    out_shape = [jax.ShapeDtypeStruct((bsz, t, d), F32)]
    if with_next:
        in_specs += [mod, mod]
        args += list(next_mod)
        out_specs.append(row)
        out_shape.append(jax.ShapeDtypeStruct((bsz, t, d), BF16))
    res = pl.pallas_call(
        functools.partial(_combine_kernel, with_next, steps),
        grid=(steps,),
        in_specs=in_specs,
        out_specs=out_specs,
        out_shape=out_shape,
        scratch_shapes=[pltpu.VMEM((2, tr, d), F32), pltpu.VMEM((2, tr, d), F32),
                        pltpu.SemaphoreType.DMA((2, 2))],
        compiler_params=_cparams(("arbitrary",), VMEM_LIMIT),
        name="combine_ln",
    )(*args)
    return res if with_next else (res[0], None)


def kernel(x, c, emb_ln_g, emb_ln_b, w_ada, b_ada, w_in, conv_w, mix_beta, w_out, hg_lb_logits,
           ln_g, ln_b, w_router, router_bias, w_gate, w_up, w_down):
    bsz, t, d = x.shape
    n = bsz * t
    assert n % LANES == 0 and t % SEQ_CHUNK == 0
    n_tiles = 2 * n // MOE_TILE + N_EXPERTS
    assert n_tiles <= LANES

    mod = _ada_mod(c, w_ada, b_ada)
    mod = mod.reshape(DEPTH, bsz, 6, 1, d)

    def mods(l):
        return [mod[l, :, i] for i in range(6)]

    wr_hi = jnp.pad(w_router, ((0, 0), (0, LANES - N_EXPERTS)))
    wr_hi16 = wr_hi.astype(BF16)
    wr_lo16 = (wr_hi - wr_hi16.astype(F32)).astype(BF16)

    w_out16 = w_out.astype(BF16)

    m = [mods(l) for l in range(DEPTH)]
    xcur, h = _ln_mod(x, emb_ln_g, emb_ln_b, m[0][1], m[0][0])
    for l in range(DEPTH):
        _, _, gate1, shift2, scale2, gate2 = m[l]
        beta = mix_beta[l].reshape(1, -1)
        proj = _in_proj(h.reshape(n, d), w_in, l, BF16).reshape(bsz, t, IN_COLS)
        yc = _conv_mixer(proj, conv_w[l], beta)
        yr = _ret_mixer(proj, beta)
        yh = _hg_mixer(proj, hg_lb_logits, beta, l)
        x1, h2, logits = _out_proj(yc, yr, yh, w_out16, l, xcur, gate1,
                                   ln_g[l, 0], ln_b[l, 0], scale2, shift2, wr_hi16, wr_lo16)
        logits_t = logits.reshape(n, LANES)[:, :N_EXPERTS].T.reshape(N_EXPERTS, n // LANES, LANES)
        pos1, pos2, w1, w2, tiles = _route(logits_t, router_bias, n_tiles)
        tile_e, n_valid = tiles[0, :n_tiles], tiles[1, :1]
        first, slot, nxt, fill = _moe_schedule(tile_e, n_valid)
        xs = _scatter_rows(h2.reshape(n, d), pos1.reshape(n), pos2.reshape(n), fill, n_tiles)
        ys = _moe_experts(xs, tile_e, n_valid, (first, slot, nxt), w_gate, w_up, w_down, l)
        next_mod = (m[l + 1][1], m[l + 1][0]) if l + 1 < DEPTH else None
        xcur, h = _combine(ys, pos1.reshape(n), pos2.reshape(n), w1, w2, x1, gate2,
                           ln_g[l, 1], ln_b[l, 1], next_mod)
    return xcur
```

```python
import functools
import math

import jax
import jax.numpy as jnp
from jax import lax
from jax.experimental import pallas as pl
from jax.experimental.pallas import tpu as pltpu

F32 = jnp.float32
BF16 = jnp.bfloat16
I32 = jnp.int32

DEPTH = 2
CONV_WIDTH = 512
RET_WIDTH = 768
HG_WIDTH = 768
HEAD_DIM = 128
RET_HEADS = RET_WIDTH // HEAD_DIM
HG_HEADS = HG_WIDTH // HEAD_DIM
IN_COLS = 3 * CONV_WIDTH + 4 * RET_WIDTH + 5 * HG_WIDTH
ROPE_BASE = 10000.0
N_EXPERTS = 16
N_GROUPS = 4
EXPERTS_PER_GROUP = N_EXPERTS // N_GROUPS
MASK_NEG = -1e9
DEEPNORM_ALPHA = (2.0 * DEPTH) ** 0.25
LN_EPS = 1e-5
HEAD_EPS = 1e-6
FORGET_FLOOR = 1e-6

LANES = 128
SUBLANES = 8

_CB, _CC, _CH = 0, 4, 8
_RQ, _RK, _RV, _RG = 12, 18, 24, 30
_HQ, _HFF, _HFB, _HI, _HG = 36, 42, 48, 54, 60
_OUT_RET, _OUT_HG = 4, 10

SEQ_CHUNK = 128
MOE_TILE = 256
VMEM_LIMIT = 56 * 1024 * 1024
MOE_VMEM_LIMIT = 60 * 1024 * 1024


def _cparams(sem, vmem=None):
    return pltpu.CompilerParams(dimension_semantics=sem, vmem_limit_bytes=vmem)


def _silu(x):
    return x * jax.nn.sigmoid(x)


def _dot(a, b):
    return jnp.dot(a, b, preferred_element_type=F32)


def _dot_nt(a, b):
    return lax.dot_general(a, b, (((1,), (1,)), ((), ())), preferred_element_type=F32)


def _dot_tn(a, b):
    return lax.dot_general(a, b, (((0,), (0,)), ((), ())), preferred_element_type=F32)


def _ada_kernel(c_ref, w_ref, b_ref, o_ref):
    cond = _silu(c_ref[...])
    o_ref[0] = _dot(cond.astype(BF16), w_ref[0].astype(BF16)) + b_ref[0]


def _ada_mod(c, w_ada, b_ada):
    depth, d, n6 = w_ada.shape
    b = c.shape[0]
    bp = -(-b // SUBLANES) * SUBLANES
    cp = jnp.pad(c, ((0, bp - b), (0, 0)))
    tn = 512
    out = pl.pallas_call(
        _ada_kernel,
        grid=(depth, n6 // tn),
        in_specs=[
            pl.BlockSpec((bp, d), lambda l, j: (0, 0)),
            pl.BlockSpec((1, d, tn), lambda l, j: (l, 0, j)),
            pl.BlockSpec((1, 1, tn), lambda l, j: (l, 0, j)),
        ],
        out_specs=pl.BlockSpec((1, bp, tn), lambda l, j: (l, 0, j)),
        out_shape=jax.ShapeDtypeStruct((depth, bp, n6), F32),
        compiler_params=_cparams(("arbitrary", "arbitrary")),
        name="ada_mod",
    )(cp, w_ada, b_ada.reshape(depth, 1, n6))
    return out[:, :b, :]


def _layer_norm_rows(z, g, b):
    mu = jnp.mean(z, axis=-1, keepdims=True)
    zc = z - mu
    var = jnp.mean(zc * zc, axis=-1, keepdims=True)
    return zc * lax.rsqrt(var + LN_EPS) * g + b


def _ln_mod_kernel(x_ref, g_ref, b_ref, sc_ref, sh_ref, xo_ref, h_ref):
    y = _layer_norm_rows(x_ref[0], g_ref[...], b_ref[...])
    xo_ref[0] = y
    h_ref[0] = (y * (1.0 + sc_ref[0]) + sh_ref[0]).astype(BF16)


def _ln_mod(x, g, b, scale, shift):
    bsz, t, d = x.shape
    tr = 512
    row = pl.BlockSpec((1, tr, d), lambda i, j: (i, j, 0))
    vec = pl.BlockSpec((1, d), lambda i, j: (0, 0))
    mod = pl.BlockSpec((1, 1, d), lambda i, j: (i, 0, 0))
    return pl.pallas_call(
        _ln_mod_kernel,
        grid=(bsz, t // tr),
        in_specs=[row, vec, vec, mod, mod],
        out_specs=[row, row],
        out_shape=[jax.ShapeDtypeStruct((bsz, t, d), F32),
                   jax.ShapeDtypeStruct((bsz, t, d), BF16)],
        compiler_params=_cparams(("arbitrary", "arbitrary")),
        name="ln_mod",
    )(x, g.reshape(1, d), b.reshape(1, d), scale, shift)


def _in_proj_kernel(a_ref, w_ref, o_ref, w16_scr):
    @pl.when(pl.program_id(1) == 0)
    def _():
        w16_scr[...] = w_ref[0].astype(BF16)

    o_ref[...] = _dot(a_ref[...], w16_scr[...]).astype(o_ref.dtype)


def _in_proj(h, w_in, layer, out_dtype):
    n, d = h.shape
    nc = w_in.shape[2]
    tm, tn = 1024, 1408
    tm = min(tm, n)
    return pl.pallas_call(
        _in_proj_kernel,
        grid=(nc // tn, n // tm),
        in_specs=[pl.BlockSpec((tm, d), lambda j, i: (i, 0)),
                  pl.BlockSpec((1, d, tn), lambda j, i: (layer, 0, j))],
        out_specs=pl.BlockSpec((tm, tn), lambda j, i: (i, j)),
        out_shape=jax.ShapeDtypeStruct((n, nc), out_dtype),
        scratch_shapes=[pltpu.VMEM((d, tn), BF16)],
        compiler_params=_cparams(("arbitrary", "arbitrary"), VMEM_LIMIT),
        name="in_proj",
    )(h, w_in)


def _conv_kernel(cb_ref, cc_ref, ch_ref, w_ref, beta_ref, o_ref):
    u = cc_ref[0].astype(F32) * ch_ref[0].astype(F32)
    t = u.shape[0]
    row = lax.broadcasted_iota(I32, u.shape, 0)
    prev = jnp.where(row == 0, 0.0, pltpu.roll(u, 1, 0))
    nxt = jnp.where(row == t - 1, 0.0, pltpu.roll(u, t - 1, 0))
    w = w_ref[...]
    y = cb_ref[0].astype(F32) * (prev * w[0:1] + u * w[1:2] + nxt * w[2:3])
    o_ref[0] = (y * beta_ref[...]).astype(o_ref.dtype)


def _conv_mixer(proj, conv_w, beta):
    bsz, t, _ = proj.shape
    cw = 256
    nb = CONV_WIDTH // cw
    per = LANES * 1

    def col(off):
        return pl.BlockSpec((1, t, cw), lambda b, j, off=off: (b, 0, off * per // cw + j))

    return pl.pallas_call(
        _conv_kernel,
        grid=(bsz, nb),
        in_specs=[col(_CB), col(_CC), col(_CH),
                  pl.BlockSpec((3, cw), lambda b, j: (0, j)),
                  pl.BlockSpec((1, cw), lambda b, j: (0, j))],
        out_specs=pl.BlockSpec((1, t, cw), lambda b, j: (b, 0, j)),
        out_shape=jax.ShapeDtypeStruct((bsz, t, CONV_WIDTH), BF16),
        compiler_params=_cparams(("arbitrary", "arbitrary"), VMEM_LIMIT),
        name="conv_mixer",
    )(proj, proj, proj, conv_w, beta)


def _ret_kernel(q_ref, k_ref, v_ref, g_ref, cos_ref, sin_ref, lg_ref, beta_ref,
                o_ref, o_scr, qf_scr, qb_scr, uf_scr, ub_scr):
    L = SEQ_CHUNK
    t = q_ref.shape[1]
    n = t // L
    lgf = lg_ref[0, 0:1, :]
    lgb = lg_ref[0, 1:2, :]
    r = lax.broadcasted_iota(I32, (L, L), 0).astype(F32)
    c = lax.broadcasted_iota(I32, (L, L), 1).astype(F32)
    rel = r - c
    dbi = jnp.where(rel > 0, jnp.exp(lgf * jnp.maximum(rel, 0.0)),
                    jnp.where(rel < 0, jnp.exp(lgb * jnp.maximum(-rel, 0.0)), 2.0))
    qf = jnp.exp(lgf * (r + 1.0))
    kf = jnp.exp(lgf * (L - 1.0 - r))
    qb = jnp.exp(lgb * (L - r))
    kb = jnp.exp(lgb * r)
    gf_l = jnp.exp(lgf * float(L))
    gb_l = jnp.exp(lgb * float(L))
    scale = HEAD_DIM ** -0.5

    def rot(x, cs, sn):
        return x * cs + pltpu.roll(x, HEAD_DIM // 2, 1) * sn

    def rows(ci):
        return pl.ds(pl.multiple_of(ci * L, L), L)

    def local_body(ci, carry):
        sl = rows(ci)
        cs = cos_ref[sl, :]
        sn = sin_ref[sl, :]
        q = rot(q_ref[0, sl, :].astype(F32), cs, sn)
        k = rot(k_ref[0, sl, :].astype(F32), cs, sn) * scale
        v16 = v_ref[0, sl, :].astype(BF16)
        s = _dot_nt(q.astype(BF16), k.astype(BF16)) * dbi
        o_scr[sl, :] = _dot(s.astype(BF16), v16)
        qf_scr[sl, :] = (q * qf).astype(BF16)
        qb_scr[sl, :] = (q * qb).astype(BF16)
        uf_scr[ci] = _dot_tn((k * kf).astype(BF16), v16)
        ub_scr[ci] = _dot_tn((k * kb).astype(BF16), v16)
        return carry

    lax.fori_loop(0, n, local_body, 0, unroll=16)

    def scan_body(i, states):
        sf, sb = states
        cf, cb = i, n - 1 - i
        slf, slb = rows(cf), rows(cb)
        o_scr[slf, :] += _dot(qf_scr[slf, :], sf.astype(BF16))
        o_scr[slb, :] += _dot(qb_scr[slb, :], sb.astype(BF16))
        return gf_l * sf + uf_scr[cf], gb_l * sb + ub_scr[cb]

    zero = jnp.zeros((HEAD_DIM, HEAD_DIM), F32)
    lax.fori_loop(0, n, scan_body, (zero, zero), unroll=16)

    beta = beta_ref[...]

    def norm_body(ci, carry):
        sl = rows(ci)
        o = o_scr[sl, :]
        mu = jnp.mean(o, axis=-1, keepdims=True)
        oc = o - mu
        var = jnp.mean(oc * oc, axis=-1, keepdims=True)
        y = oc * lax.rsqrt(var + HEAD_EPS) * _silu(g_ref[0, sl, :].astype(F32)) * beta
        o_ref[0, sl, :] = y.astype(o_ref.dtype)
        return carry

    lax.fori_loop(0, n, norm_body, 0, unroll=4)


def _rotary_tables(t):
    half = HEAD_DIM // 2
    inv_freq = ROPE_BASE ** (-jnp.arange(half, dtype=F32) / half)
    ang = jnp.arange(t, dtype=F32)[:, None] * inv_freq[None, :]
    cos, sin = jnp.cos(ang), jnp.sin(ang)
    return jnp.concatenate([cos, cos], -1), jnp.concatenate([-sin, sin], -1)


def _ret_log_decays():
    head = jnp.arange(RET_HEADS, dtype=F32)
    lg_f = jnp.log1p(-jnp.exp2(-5.0 - head))
    lg_b = jnp.log1p(-jnp.exp2(-5.5 - head))
    lg = jnp.stack([lg_f, lg_b], axis=1)
    return jnp.broadcast_to(lg[:, :, None], (RET_HEADS, 2, LANES))


def _ret_mixer(proj, beta):
    bsz, t, _ = proj.shape
    n_chunks = t // SEQ_CHUNK
    assert n_chunks % 2 == 0
    cosf, sinf = _rotary_tables(t)

    def col(off):
        return pl.BlockSpec((1, t, HEAD_DIM), lambda b, h, off=off: (b, 0, off + h))

    tab = pl.BlockSpec((t, HEAD_DIM), lambda b, h: (0, 0))
    return pl.pallas_call(
        _ret_kernel,
        grid=(bsz, RET_HEADS),
        in_specs=[col(_RQ), col(_RK), col(_RV), col(_RG), tab, tab,
                  pl.BlockSpec((1, 2, LANES), lambda b, h: (h, 0, 0)),
                  pl.BlockSpec((1, HEAD_DIM), lambda b, h: (0, _OUT_RET + h))],
        out_specs=pl.BlockSpec((1, t, HEAD_DIM), lambda b, h: (b, 0, h)),
        out_shape=jax.ShapeDtypeStruct((bsz, t, RET_WIDTH), BF16),
        scratch_shapes=[pltpu.VMEM((t, HEAD_DIM), F32),
                        pltpu.VMEM((t, HEAD_DIM), BF16), pltpu.VMEM((t, HEAD_DIM), BF16),
                        pltpu.VMEM((n_chunks, HEAD_DIM, HEAD_DIM), F32),
                        pltpu.VMEM((n_chunks, HEAD_DIM, HEAD_DIM), F32)],
        compiler_params=_cparams(("arbitrary", "arbitrary"), VMEM_LIMIT),
        name="ret_mixer",
    )(proj, proj, proj, proj, cosf, sinf, _ret_log_decays(), beta)


_HG_BASE = SUBLANES
_HG_LEVELS = tuple(m for m in (8, 16, 32, 64) if m < SEQ_CHUNK)


def _cumsum_rows(x, reverse):
    n = x.shape[0]
    row = lax.broadcasted_iota(I32, x.shape, 0)
    sh = 1
    while sh < n:
        if reverse:
            x = x + jnp.where(row < n - sh, pltpu.roll(x, n - sh, 0), 0.0)
        else:
            x = x + jnp.where(row >= sh, pltpu.roll(x, sh, 0), 0.0)
        sh *= 2
    return x


def _hg_masks(reverse):
    C = SEQ_CHUNK
    r = lax.broadcasted_iota(I32, (C, C), 0)
    c = lax.broadcasted_iota(I32, (C, C), 1)
    if reverse:
        r, c = c, r
    sh = _HG_BASE.bit_length() - 1
    masks = [((r >> sh) == (c >> sh)) & (c <= r)]
    for m in _HG_LEVELS:
        sh = m.bit_length() - 1
        masks.append(((r >> (sh + 1)) == (c >> (sh + 1)))
                     & (((r >> sh) & 1) == 1) & (((c >> sh) & 1) == 0))
    return [jnp.where(m, 1.0, 0.0) for m in masks]


def _hg_scores(q, kk, cum, mask_ref, reverse):
    C = SEQ_CHUNK
    d = 1 if reverse else 0

    def split(x, blk):
        return x.reshape(C // blk, blk, LANES)

    b = _HG_BASE
    ref_row = b // 2 if reverse else b // 2 - 1
    cum3 = split(cum, b)
    rel = cum3 - cum3[:, ref_row:ref_row + 1, :]
    qt = (split(q, b) * jnp.exp(rel)).reshape(C, LANES).astype(BF16)
    kt = (split(kk, b) * jnp.exp(-rel)).reshape(C, LANES).astype(BF16)
    a = jnp.where(mask_ref[d, 0] > 0.5, _dot_nt(qt, kt), 0.0)
    for li, m in enumerate(_HG_LEVELS):
        blk = 2 * m
        ref_row = m if reverse else m - 1
        cum3 = split(cum, blk)
        rel = cum3 - cum3[:, ref_row:ref_row + 1, :]
        dec = jnp.exp(-jnp.abs(rel)).reshape(C, LANES)
        halves = []
        for j in range(C // m):
            is_query = (j % 2 == 1) != reverse
            halves.append((q if is_query else kk)[j * m:(j + 1) * m])
        x = (jnp.concatenate(halves, axis=0) * dec).astype(BF16)
        a = a + mask_ref[d, li + 1] * _dot_nt(x, x)
    return a


def _hg_kernel(layer, q_ref, zf_ref, zb_ref, v_ref, g_ref, lbl_ref, beta_ref, o_ref,
               o_scr, qe_scr, u_scr, dec_scr, mask_scr):
    C = SEQ_CHUNK
    t = q_ref.shape[1]
    n = t // C
    for d, reverse in enumerate((False, True)):
        for li, m in enumerate(_hg_masks(reverse)):
            mask_scr[d, li] = m
    logits = lbl_ref[...].astype(F32)
    e = jnp.exp(logits - jnp.max(logits, axis=0, keepdims=True))
    p = e / jnp.sum(e, axis=0, keepdims=True)
    lb = p[0]
    for l in range(1, layer + 1):
        lb = lb + p[l]
    lb = lb - p[0]

    def gates(z_ref, sl, lb_row):
        f = lb_row + (1.0 - lb_row) * jax.nn.sigmoid(z_ref[0, sl, :].astype(F32))
        f = jnp.maximum(f, FORGET_FLOOR)
        return jnp.log(f), 1.0 - f

    def rows(ci):
        return pl.ds(pl.multiple_of(ci * C, C), C)

    def local_dir(ci, sl, q, v16, z_ref, d):
        reverse = d == 1
        logf, kk = gates(z_ref, sl, lb[d:d + 1, :])
        cum = _cumsum_rows(logf, reverse)
        a = _hg_scores(q, kk, cum, mask_scr, reverse)
        total = cum[0:1, :] if reverse else cum[C - 1:C, :]
        qe_scr[d, sl, :] = (q * jnp.exp(cum)).astype(BF16)
        k_tail = kk * jnp.exp(total - cum)
        u_scr[d, ci] = _dot_tn(v16, k_tail.astype(BF16))
        dec_scr[d, ci] = jnp.broadcast_to(jnp.exp(total), (SUBLANES, LANES))
        return _dot(a.astype(BF16), v16)

    def local_body(ci, carry):
        sl = rows(ci)
        q = q_ref[0, sl, :].astype(F32)
        v16 = v_ref[0, sl, :].astype(BF16)
        o_scr[sl, :] = (local_dir(ci, sl, q, v16, zf_ref, 0)
                        + local_dir(ci, sl, q, v16, zb_ref, 1))
        return carry

    lax.fori_loop(0, n, local_body, 0, unroll=16)

    def scan_body(i, states):
        sf, sb = states
        cf, cb = i, n - 1 - i
        slf, slb = rows(cf), rows(cb)
        o_scr[slf, :] += _dot_nt(qe_scr[0, slf, :], sf.astype(BF16))
        o_scr[slb, :] += _dot_nt(qe_scr[1, slb, :], sb.astype(BF16))
        return (sf * dec_scr[0, cf, 0:1, :] + u_scr[0, cf],
                sb * dec_scr[1, cb, 0:1, :] + u_scr[1, cb])

    zero = jnp.zeros((HEAD_DIM, HEAD_DIM), F32)
    lax.fori_loop(0, n, scan_body, (zero, zero), unroll=16)

    beta = beta_ref[...]

    def norm_body(ci, carry):
        sl = rows(ci)
        o = o_scr[sl, :]
        ms = jnp.mean(o * o, axis=-1, keepdims=True)
        y = o * lax.rsqrt(ms + HEAD_EPS) * _silu(g_ref[0, sl, :].astype(F32)) * beta
        o_ref[0, sl, :] = y.astype(o_ref.dtype)
        return carry

    lax.fori_loop(0, n, norm_body, 0, unroll=16)


def _hg_mixer(proj, lb_logits, beta, layer):
    bsz, t, _ = proj.shape
    n_chunks = t // SEQ_CHUNK
    assert n_chunks % 2 == 0

    def col(off):
        return pl.BlockSpec((1, t, HEAD_DIM), lambda b, h, off=off: (b, 0, off + h))

    return pl.pallas_call(
        functools.partial(_hg_kernel, layer),
        grid=(bsz, HG_HEADS),
        in_specs=[col(_HQ), col(_HFF), col(_HFB), col(_HI), col(_HG),
                  pl.BlockSpec((DEPTH, 2, HEAD_DIM), lambda b, h: (0, 0, h)),
                  pl.BlockSpec((1, HEAD_DIM), lambda b, h: (0, _OUT_HG + h))],
        out_specs=pl.BlockSpec((1, t, HEAD_DIM), lambda b, h: (b, 0, h)),
        out_shape=jax.ShapeDtypeStruct((bsz, t, HG_WIDTH), BF16),
        scratch_shapes=[pltpu.VMEM((t, HEAD_DIM), F32),
                        pltpu.VMEM((2, t, HEAD_DIM), BF16),
                        pltpu.VMEM((2, n_chunks, HEAD_DIM, HEAD_DIM), F32),
                        pltpu.VMEM((2, n_chunks, SUBLANES, LANES), F32),
                        pltpu.VMEM((2, 1 + len(_HG_LEVELS), SEQ_CHUNK, SEQ_CHUNK), F32)],
        compiler_params=_cparams(("arbitrary", "arbitrary"), VMEM_LIMIT),
        name="hg_mixer",
    )(proj, proj, proj, proj, proj, lb_logits, beta)


_OUT_PROJ_SPLIT = 2


def _out_proj_kernel(yc_ref, yr_ref, yh_ref, w_ref, x_ref, gate_ref, g_ref, b_ref,
                     sc_ref, sh_ref, wrh_ref, wrl_ref, xo_ref, h_ref, lo_ref):
    c0, c1 = CONV_WIDTH, CONV_WIDTH + RET_WIDTH
    tm = x_ref.shape[1]
    for half in range(_OUT_PROJ_SPLIT):
        sl = pl.ds(half * (tm // _OUT_PROJ_SPLIT), tm // _OUT_PROJ_SPLIT)
        y = (_dot(yc_ref[0, sl, :], w_ref[0, 0:c0, :]) + _dot(yr_ref[0, sl, :], w_ref[0, c0:c1, :])
             + _dot(yh_ref[0, sl, :], w_ref[0, c1:, :]))
        z = DEEPNORM_ALPHA * x_ref[0, sl, :] + (1.0 + gate_ref[0]) * y
        xn = _layer_norm_rows(z, g_ref[...], b_ref[...])
        xo_ref[0, sl, :] = xn
        h = xn * (1.0 + sc_ref[0]) + sh_ref[0]
        h_ref[0, sl, :] = h
        h_hi = h.astype(BF16)
        h_lo = (h - h_hi.astype(F32)).astype(BF16)
        wrh = wrh_ref[...]
        lo_ref[0, sl, :] = _dot(h_hi, wrh) + _dot(h_lo, wrh) + _dot(h_hi, wrl_ref[...])


def _out_proj(yc, yr, yh, w_out16, layer, x, gate, ln_g, ln_b, scale2, shift2, wr_hi, wr_lo):
    bsz, t, d = x.shape
    tm = 512

    def row(width):
        return pl.BlockSpec((1, tm, width), lambda b, i: (b, i, 0))

    vec = pl.BlockSpec((1, d), lambda b, i: (0, 0))
    mod = pl.BlockSpec((1, 1, d), lambda b, i: (b, 0, 0))
    wr = pl.BlockSpec((d, LANES), lambda b, i: (0, 0))
    return pl.pallas_call(
        _out_proj_kernel,
        grid=(bsz, t // tm),
        in_specs=[row(CONV_WIDTH), row(RET_WIDTH), row(HG_WIDTH),
                  pl.BlockSpec((1, d, d), lambda b, i: (layer, 0, 0)),
                  row(d), mod, vec, vec, mod, mod, wr, wr],
        out_specs=[row(d), row(d), row(LANES)],
        out_shape=[jax.ShapeDtypeStruct((bsz, t, d), F32),
                   jax.ShapeDtypeStruct((bsz, t, d), F32),
                   jax.ShapeDtypeStruct((bsz, t, LANES), F32)],
        compiler_params=_cparams(("arbitrary", "arbitrary"), VMEM_LIMIT),
        name="out_proj",
    )(yc, yr, yh, w_out16, x, gate, ln_g.reshape(1, d), ln_b.reshape(1, d),
      scale2, shift2, wr_hi, wr_lo)


def _route_kernel(n_tiles, lt_ref, bias_ref, pos1_ref, pos2_ref, w1_ref, w2_ref, tile_ref):
    E = N_EXPERTS
    logits = [lt_ref[e] for e in range(E)]
    shape = logits[0].shape
    mx = functools.reduce(jnp.maximum, logits)
    ex = [jnp.exp(l - mx) for l in logits]
    den = functools.reduce(lambda a, b: a + b, ex)
    scores = [x / den for x in ex]
    sel = [scores[e] + bias_ref[e] for e in range(E)]

    best_g = jnp.zeros(shape, I32)
    best_v = None
    for g in range(N_GROUPS):
        a, b, c, d = sel[EXPERTS_PER_GROUP * g: EXPERTS_PER_GROUP * (g + 1)]
        hi1, lo1 = jnp.maximum(a, b), jnp.minimum(a, b)
        hi2, lo2 = jnp.maximum(c, d), jnp.minimum(c, d)
        gs = jnp.maximum(hi1, hi2) + jnp.maximum(jnp.minimum(hi1, hi2), jnp.maximum(lo1, lo2))
        if g == 0:
            best_v = gs
        else:
            better = gs > best_v
            best_g = jnp.where(better, g, best_g)
            best_v = jnp.where(better, gs, best_v)

    masked = [jnp.where(best_g == (e // EXPERTS_PER_GROUP), sel[e], MASK_NEG) for e in range(E)]

    def arg_top(vals, exclude):
        idx = jnp.zeros(shape, I32)
        val = None
        for e in range(E):
            v = vals[e] if exclude is None else jnp.where(exclude == e, -jnp.inf, vals[e])
            if e == 0:
                val = v
            else:
                better = v > val
                idx = jnp.where(better, e, idx)
                val = jnp.where(better, v, val)
        return idx

    idx1 = arg_top(masked, None)
    idx2 = arg_top(masked, idx1)
    s1 = functools.reduce(lambda a, b: a + b, [jnp.where(idx1 == e, scores[e], 0.0) for e in range(E)])
    s2 = functools.reduce(lambda a, b: a + b, [jnp.where(idx2 == e, scores[e], 0.0) for e in range(E)])
    w1_ref[...] = s1 / (s1 + s2)
    w2_ref[...] = s2 / (s1 + s2)

    rows = shape[0]
    li = lax.broadcasted_iota(I32, (LANES, LANES), 0)
    lj = lax.broadcasted_iota(I32, (LANES, LANES), 1)
    upper = (li <= lj).astype(BF16)
    ri = lax.broadcasted_iota(I32, (rows, rows), 0)
    rj = lax.broadcasted_iota(I32, (rows, rows), 1)
    lower = (rj < ri).astype(BF16)
    tile_start = (lax.broadcasted_iota(I32, (1, LANES), 1) * MOE_TILE).astype(F32)
    start = jnp.zeros((1, 1), F32)
    pos1 = jnp.zeros(shape, F32)
    pos2 = jnp.zeros(shape, F32)
    tile_e = jnp.zeros((1, LANES), F32)
    for e in range(E):
        hit1 = idx1 == e
        hit2 = idx2 == e
        onehot = jnp.where(hit1 | hit2, 1.0, 0.0)
        pref = _dot(onehot.astype(BF16), upper)
        row_tot = pref[:, LANES - 1:LANES]
        row_off = _dot(lower, jnp.broadcast_to(row_tot, shape).astype(BF16))[:, 0:1]
        rank = pref - onehot + row_off
        count = jnp.sum(onehot, keepdims=True)
        dest = start + rank
        pos1 = jnp.where(hit1, dest, pos1)
        pos2 = jnp.where(hit2, dest, pos2)
        start = start + jnp.ceil(count / MOE_TILE) * MOE_TILE
        tile_e = tile_e + jnp.where(tile_start >= start, 1.0, 0.0)
    pos1_ref[...] = pos1.astype(I32)
    pos2_ref[...] = pos2.astype(I32)
    n_valid = start / MOE_TILE
    tile_id = lax.broadcasted_iota(I32, (1, LANES), 1).astype(F32)
    last_e = jnp.sum(jnp.where(tile_id == n_valid - 1.0, tile_e, 0.0), keepdims=True)
    tile_e = jnp.where(tile_id < n_valid, tile_e, last_e)
    sub = lax.broadcasted_iota(I32, (SUBLANES, LANES), 0)
    tile_ref[...] = jnp.where(sub == 0, tile_e, jnp.broadcast_to(n_valid, (SUBLANES, LANES))).astype(I32)


def _route(logits_t, router_bias, n_tiles):
    e, rows, lanes = logits_t.shape
    tok = jax.ShapeDtypeStruct((rows, lanes), I32)
    tokf = jax.ShapeDtypeStruct((rows, lanes), F32)
    full = pl.BlockSpec((rows, lanes), lambda i: (0, 0))
    return pl.pallas_call(
        functools.partial(_route_kernel, n_tiles),
        grid=(1,),
        in_specs=[pl.BlockSpec((e, rows, lanes), lambda i: (0, 0, 0)),
                  pl.BlockSpec(memory_space=pltpu.SMEM)],
        out_specs=[full, full, full, full, pl.BlockSpec((SUBLANES, LANES), lambda i: (0, 0))],
        out_shape=[tok, tok, tokf, tokf, jax.ShapeDtypeStruct((SUBLANES, LANES), I32)],
        compiler_params=_cparams(("arbitrary",)),
        name="route",
    )(logits_t, router_bias)


_ROWS_PER_STEP = 512


_DMA_UNROLL = 8


def _scatter_rows_kernel(n_tiles, p1_ref, p2_ref, fill_ref, src_ref, dst_ref, zero_scr, sem, zsem):
    def zero_copy(i):
        row0 = pl.multiple_of(i * MOE_TILE, MOE_TILE)
        return pltpu.make_async_copy(zero_scr, dst_ref.at[pl.ds(row0, MOE_TILE)], zsem)

    @pl.when(pl.program_id(0) == 0)
    def _():
        zero_scr[...] = jnp.zeros_like(zero_scr)

        def zstart(i, carry):
            @pl.when(fill_ref[i] == 1)
            def _():
                zero_copy(i).start()
            return carry

        def zwait(i, carry):
            @pl.when(fill_ref[i] == 1)
            def _():
                zero_copy(i).wait()
            return carry

        lax.fori_loop(0, n_tiles, zstart, 0)
        lax.fori_loop(0, n_tiles, zwait, 0)

    def copies(r):
        row = src_ref.at[pl.ds(r, 1)]
        return (pltpu.make_async_copy(row, dst_ref.at[pl.ds(p1_ref[0, 0, r], 1)], sem.at[0]),
                pltpu.make_async_copy(row, dst_ref.at[pl.ds(p2_ref[0, 0, r], 1)], sem.at[1]))

    def start(r, carry):
        for queue, cp in enumerate(copies(r)):
            cp.start(priority=queue)
        return carry

    def wait(r, carry):
        for cp in copies(r):
            cp.wait()
        return carry

    lax.fori_loop(0, _ROWS_PER_STEP, start, 0, unroll=_DMA_UNROLL)
    lax.fori_loop(0, _ROWS_PER_STEP, wait, 0, unroll=_DMA_UNROLL)


def _scatter_rows(src, pos1, pos2, fill_tile, n_tiles):
    n, d = src.shape
    steps = n // _ROWS_PER_STEP
    idx = pl.BlockSpec((1, 1, _ROWS_PER_STEP), lambda i: (i, 0, 0), memory_space=pltpu.SMEM)
    return pl.pallas_call(
        functools.partial(_scatter_rows_kernel, n_tiles),
        grid=(steps,),
        in_specs=[idx, idx, pl.BlockSpec(memory_space=pltpu.SMEM),
                  pl.BlockSpec((_ROWS_PER_STEP, d), lambda i: (i, 0))],
        out_specs=pl.BlockSpec(memory_space=pl.ANY),
        out_shape=jax.ShapeDtypeStruct((n_tiles * MOE_TILE, d), src.dtype),
        scratch_shapes=[pltpu.VMEM((MOE_TILE, d), src.dtype),
                        pltpu.SemaphoreType.DMA((2,)), pltpu.SemaphoreType.DMA],
        compiler_params=_cparams(("arbitrary",), VMEM_LIMIT),
        name="scatter_rows",
    )(pos1.reshape(steps, 1, _ROWS_PER_STEP), pos2.reshape(steps, 1, _ROWS_PER_STEP), fill_tile, src)


_W_CHUNK = 128
_W_RING = 11


def _moe_kernel(layer, te_ref, nv_ref, first_ref, slot_ref, nxt_ref,
                x_ref, wg_hbm, wu_hbm, wd_hbm, y_ref,
                cg_scr, cu_scr, cd_scr, stage_scr, sem, cnt_ref):
    i = pl.program_id(0)
    d, f = cg_scr.shape[1], cg_scr.shape[2]
    n_g = d // _W_CHUNK
    n_chunks = 2 * n_g + f // _W_CHUNK

    def for_chunk(c, e, slot, fn):
        k = c % _W_RING

        @pl.when(c < n_g)
        def _():
            row = pl.multiple_of(c * _W_CHUNK, _W_CHUNK)
            fn(wg_hbm.at[layer, e, pl.ds(row, _W_CHUNK), :], stage_scr.at[k, :, pl.ds(0, f)],
               sem.at[k], cg_scr.at[slot, pl.ds(row, _W_CHUNK), :])

        @pl.when((c >= n_g) & (c < 2 * n_g))
        def _():
            row = pl.multiple_of((c - n_g) * _W_CHUNK, _W_CHUNK)
            fn(wu_hbm.at[layer, e, pl.ds(row, _W_CHUNK), :], stage_scr.at[k, :, pl.ds(0, f)],
               sem.at[k], cu_scr.at[slot, pl.ds(row, _W_CHUNK), :])

        @pl.when(c >= 2 * n_g)
        def _():
            row = pl.multiple_of((c - 2 * n_g) * _W_CHUNK, _W_CHUNK)
            fn(wd_hbm.at[layer, e, pl.ds(row, _W_CHUNK), :], stage_scr.at[k],
               sem.at[k], cd_scr.at[slot, pl.ds(row, _W_CHUNK), :])

    def start(src, stage, s, dst):
        pltpu.make_async_copy(src, stage, s).start()

    def finish(src, stage, s, dst):
        pltpu.make_async_copy(src, stage, s).wait()
        dst[...] = stage[...].astype(BF16)

    def start_upto(e, slot, hi):
        hi = jnp.minimum(hi, n_chunks)

        def body(c, carry):
            for_chunk(c, e, slot, start)
            return carry

        lax.fori_loop(cnt_ref[0], hi, body, 0)
        cnt_ref[0] = jnp.maximum(cnt_ref[0], hi)

    def finish_started(e, slot):
        def body(c, carry):
            for_chunk(c, e, slot, finish)
            return carry

        lax.fori_loop(cnt_ref[1], cnt_ref[0], body, 0)
        cnt_ref[1] = cnt_ref[0]

    valid = i < nv_ref[0]

    @pl.when(i == 0)
    def _():
        cnt_ref[0] = 0
        cnt_ref[1] = 0

    @pl.when(valid & (first_ref[i] == 1))
    def _():
        e, slot = te_ref[i], slot_ref[i]

        def body(c, carry):
            start_upto(e, slot, c + _W_RING)
            for_chunk(c, e, slot, finish)
            return carry

        lax.fori_loop(cnt_ref[1], n_chunks, body, 0)
        cnt_ref[0] = 0
        cnt_ref[1] = 0

    @pl.when(valid & (nxt_ref[i] >= 0))
    def _():
        e, slot = nxt_ref[i], 1 - slot_ref[i]
        finish_started(e, slot)
        start_upto(e, slot, cnt_ref[0] + _W_RING)

    @pl.when(valid)
    def _():
        slot = slot_ref[i]
        x = x_ref[...].astype(BF16)
        a = _dot(x, cg_scr[slot])
        b = _dot(x, cu_scr[slot])
        hidden = (_silu(a) * b).astype(BF16)
        y_ref[...] = _dot(hidden, cd_scr[slot])

    @pl.when(jnp.logical_not(valid))
    def _():
        y_ref[...] = jnp.zeros_like(y_ref)


def _moe_schedule(tile_e, n_valid):
    n_tiles = tile_e.shape[0]
    idx = jnp.arange(n_tiles, dtype=I32)
    valid = idx < n_valid[0]
    first = valid & ((idx == 0) | (tile_e != jnp.roll(tile_e, 1)))
    run = jnp.cumsum(first.astype(I32)) - 1
    next_first = lax.cummin(jnp.where(first, idx, n_tiles), reverse=True)
    after = jnp.concatenate([next_first[1:], jnp.full((1,), n_tiles, I32)])
    nxt = jnp.where(valid & (after < n_tiles), tile_e[jnp.minimum(after, n_tiles - 1)], -1)
    last = valid & ((after == idx + 1) | (idx == n_valid[0] - 1))
    fill = last | jnp.logical_not(valid)
    return first.astype(I32), (run % 2).astype(I32), nxt.astype(I32), fill.astype(I32)


def _moe_experts(xs, tile_e, n_valid, schedule, w_gate, w_up, w_down, layer):
    n_rows, d = xs.shape
    f = w_gate.shape[3]
    tm = MOE_TILE
    n_tiles = n_rows // tm
    assert d % _W_CHUNK == 0 and f % _W_CHUNK == 0
    first, slot, nxt = schedule

    def xmap(i, te, nv, *_):
        return (jnp.minimum(i, nv[0] - 1), 0)

    any_spec = pl.BlockSpec(memory_space=pl.ANY)
    grid_spec = pltpu.PrefetchScalarGridSpec(
        num_scalar_prefetch=5,
        grid=(n_tiles,),
        in_specs=[pl.BlockSpec((tm, d), xmap), any_spec, any_spec, any_spec],
        out_specs=pl.BlockSpec((tm, d), lambda i, *_: (i, 0)),
        scratch_shapes=[pltpu.VMEM((2, d, f), BF16), pltpu.VMEM((2, d, f), BF16),
                        pltpu.VMEM((2, f, d), BF16),
                        pltpu.VMEM((_W_RING, _W_CHUNK, d), F32),
                        pltpu.SemaphoreType.DMA((_W_RING,)),
                        pltpu.SMEM((2,), I32)],
    )
    return pl.pallas_call(
        functools.partial(_moe_kernel, layer),
        grid_spec=grid_spec,
        out_shape=jax.ShapeDtypeStruct((n_rows, d), F32),
        compiler_params=_cparams(("arbitrary",), MOE_VMEM_LIMIT),
        name="moe_experts",
    )(tile_e, n_valid, first, slot, nxt, xs, w_gate, w_up, w_down)


_COMBINE_ROWS = 256


def _combine_kernel(with_next, steps, p1_ref, p2_ref, p1n_ref, p2n_ref, ys_ref, w1_ref, w2_ref,
                    x_ref, gate_ref, g_ref, b_ref, *rest):
    if with_next:
        sc_ref, sh_ref, xo_ref, h_ref, ya_scr, yb_scr, sem = rest
    else:
        xo_ref, ya_scr, yb_scr, sem = rest
    step = pl.program_id(0)

    def copies(pa, pb, buf, s, r):
        return (pltpu.make_async_copy(ys_ref.at[pl.ds(pa[0, 0, r], 1)],
                                      buf.at[0, pl.ds(r, 1)], sem.at[s, 0]),
                pltpu.make_async_copy(ys_ref.at[pl.ds(pb[0, 0, r], 1)],
                                      buf.at[1, pl.ds(r, 1)], sem.at[s, 1]))

    def wait_all(pa, pb, buf, s):
        def body(r, carry):
            for cp in copies(pa, pb, buf, s, r):
                cp.wait()
            return carry
        lax.fori_loop(0, _COMBINE_ROWS, body, 0, unroll=_DMA_UNROLL)

    @pl.when(step == 0)
    def _():
        def body(r, carry):
            for cp in copies(p1_ref, p2_ref, ya_scr, 0, r):
                cp.start()
            return carry
        lax.fori_loop(0, _COMBINE_ROWS, body, 0, unroll=_DMA_UNROLL)

    for par, (cur, nxt) in enumerate(((ya_scr, yb_scr), (yb_scr, ya_scr))):
        @pl.when(step % 2 == par)
        def _(par=par, cur=cur, nxt=nxt):
            wait_all(p1_ref, p2_ref, cur, par)
            for r in range(_COMBINE_ROWS):
                for cp in copies(p1n_ref, p2n_ref, nxt, 1 - par, r):
                    cp.start()
            y = w1_ref[0] * cur[0] + w2_ref[0] * cur[1]
            z = DEEPNORM_ALPHA * x_ref[0] + (1.0 + gate_ref[0]) * y
            xn = _layer_norm_rows(z, g_ref[...], b_ref[...])
            xo_ref[0] = xn
            if with_next:
                h_ref[0] = (xn * (1.0 + sc_ref[0]) + sh_ref[0]).astype(BF16)

            @pl.when(step == steps - 1)
            def _():
                wait_all(p1n_ref, p2n_ref, nxt, 1 - par)


def _combine(ys, pos1, pos2, w1, w2, x, gate, ln_g, ln_b, next_mod):
    bsz, t, d = x.shape
    tr = _COMBINE_ROWS
    per_b = t // tr
    steps = bsz * per_b
    row = pl.BlockSpec((1, tr, d), lambda s: (s // per_b, s % per_b, 0))
    vec = pl.BlockSpec((1, d), lambda s: (0, 0))
    mod = pl.BlockSpec((1, 1, d), lambda s: (s // per_b, 0, 0))
    wcol = pl.BlockSpec((1, tr, 1), lambda s: (s // per_b, s % per_b, 0))
    idx = pl.BlockSpec((1, 1, tr), lambda s: (s, 0, 0), memory_space=pltpu.SMEM)
    idx_next = pl.BlockSpec((1, 1, tr), lambda s: (jnp.minimum(s + 1, steps - 1), 0, 0),
                            memory_space=pltpu.SMEM)
    with_next = next_mod is not None
    p1 = pos1.reshape(steps, 1, tr)
    p2 = pos2.reshape(steps, 1, tr)
    in_specs = [idx, idx, idx_next, idx_next, pl.BlockSpec(memory_space=pl.ANY),
                wcol, wcol, row, mod, vec, vec]
    args = [p1, p2, p1, p2, ys, w1.reshape(bsz, t, 1), w2.reshape(bsz, t, 1), x, gate,
            ln_g.reshape(1, d), ln_b.reshape(1, d)]
    out_specs = [row]
    out_shape = [jax.ShapeDtypeStruct((bsz, t, d), F32)]
    if with_next:
        in_specs += [mod, mod]
        args += list(next_mod)
        out_specs.append(row)
        out_shape.append(jax.ShapeDtypeStruct((bsz, t, d), BF16))
    res = pl.pallas_call(
        functools.partial(_combine_kernel, with_next, steps),
        grid=(steps,),
        in_specs=in_specs,
        out_specs=out_specs,
        out_shape=out_shape,
        scratch_shapes=[pltpu.VMEM((2, tr, d), F32), pltpu.VMEM((2, tr, d), F32),
                        pltpu.SemaphoreType.DMA((2, 2))],
        compiler_params=_cparams(("arbitrary",), VMEM_LIMIT),
        name="combine_ln",
    )(*args)
    return res if with_next else (res[0], None)


def kernel(x, c, emb_ln_g, emb_ln_b, w_ada, b_ada, w_in, conv_w, mix_beta, w_out, hg_lb_logits,
           ln_g, ln_b, w_router, router_bias, w_gate, w_up, w_down):
    bsz, t, d = x.shape
    n = bsz * t
    assert n % LANES == 0 and t % SEQ_CHUNK == 0
    n_tiles = 2 * n // MOE_TILE + N_EXPERTS
    assert n_tiles <= LANES

    mod = _ada_mod(c, w_ada, b_ada)
    mod = mod.reshape(DEPTH, bsz, 6, 1, d)

    def mods(l):
        return [mod[l, :, i] for i in range(6)]

    wr_hi = jnp.pad(w_router, ((0, 0), (0, LANES - N_EXPERTS)))
    wr_hi16 = wr_hi.astype(BF16)
    wr_lo16 = (wr_hi - wr_hi16.astype(F32)).astype(BF16)

    w_out16 = w_out.astype(BF16)

    m = [mods(l) for l in range(DEPTH)]
    xcur, h = _ln_mod(x, emb_ln_g, emb_ln_b, m[0][1], m[0][0])
    for l in range(DEPTH):
        _, _, gate1, shift2, scale2, gate2 = m[l]
        beta = mix_beta[l].reshape(1, -1)
        proj = _in_proj(h.reshape(n, d), w_in, l, BF16).reshape(bsz, t, IN_COLS)
        yc = _conv_mixer(proj, conv_w[l], beta)
        yr = _ret_mixer(proj, beta)
        yh = _hg_mixer(proj, hg_lb_logits, beta, l)
        x1, h2, logits = _out_proj(yc, yr, yh, w_out16, l, xcur, gate1,
                                   ln_g[l, 0], ln_b[l, 0], scale2, shift2, wr_hi16, wr_lo16)
        logits_t = logits.reshape(n, LANES)[:, :N_EXPERTS].T.reshape(N_EXPERTS, n // LANES, LANES)
        pos1, pos2, w1, w2, tiles = _route(logits_t, router_bias, n_tiles)
        tile_e, n_valid = tiles[0, :n_tiles], tiles[1, :1]
        first, slot, nxt, fill = _moe_schedule(tile_e, n_valid)
        xs = _scatter_rows(h2.reshape(n, d), pos1.reshape(n), pos2.reshape(n), fill, n_tiles)
        ys = _moe_experts(xs, tile_e, n_valid, (first, slot, nxt), w_gate, w_up, w_down, l)
        next_mod = (m[l + 1][1], m[l + 1][0]) if l + 1 < DEPTH else None
        xcur, h = _combine(ys, pos1.reshape(n), pos2.reshape(n), w1, w2, x1, gate2,
                           ln_g[l, 1], ln_b[l, 1], next_mod)
    return xcur
```

```python
import functools
import math

import jax
import jax.numpy as jnp
from jax import lax
from jax.experimental import pallas as pl
from jax.experimental.pallas import tpu as pltpu

F32 = jnp.float32
BF16 = jnp.bfloat16
I32 = jnp.int32

DEPTH = 2
CONV_WIDTH = 512
RET_WIDTH = 768
HG_WIDTH = 768
HEAD_DIM = 128
RET_HEADS = RET_WIDTH // HEAD_DIM
HG_HEADS = HG_WIDTH // HEAD_DIM
IN_COLS = 3 * CONV_WIDTH + 4 * RET_WIDTH + 5 * HG_WIDTH
ROPE_BASE = 10000.0
N_EXPERTS = 16
N_GROUPS = 4
EXPERTS_PER_GROUP = N_EXPERTS // N_GROUPS
MASK_NEG = -1e9
DEEPNORM_ALPHA = (2.0 * DEPTH) ** 0.25
LN_EPS = 1e-5
HEAD_EPS = 1e-6
FORGET_FLOOR = 1e-6

LANES = 128
SUBLANES = 8

_CB, _CC, _CH = 0, 4, 8
_RQ, _RK, _RV, _RG = 12, 18, 24, 30
_HQ, _HFF, _HFB, _HI, _HG = 36, 42, 48, 54, 60
_OUT_RET, _OUT_HG = 4, 10

SEQ_CHUNK = 128
MOE_TILE = 256
VMEM_LIMIT = 56 * 1024 * 1024
MOE_VMEM_LIMIT = 60 * 1024 * 1024


def _cparams(sem, vmem=None):
    return pltpu.CompilerParams(dimension_semantics=sem, vmem_limit_bytes=vmem)


def _silu(x):
    return x * jax.nn.sigmoid(x)


def _dot(a, b):
    return jnp.dot(a, b, preferred_element_type=F32)


def _dot_nt(a, b):
    return lax.dot_general(a, b, (((1,), (1,)), ((), ())), preferred_element_type=F32)


def _dot_tn(a, b):
    return lax.dot_general(a, b, (((0,), (0,)), ((), ())), preferred_element_type=F32)


def _ada_kernel(c_ref, w_ref, b_ref, o_ref):
    cond = _silu(c_ref[...])
    o_ref[0] = _dot(cond.astype(BF16), w_ref[0].astype(BF16)) + b_ref[0]


def _ada_mod(c, w_ada, b_ada):
    depth, d, n6 = w_ada.shape
    b = c.shape[0]
    bp = -(-b // SUBLANES) * SUBLANES
    cp = jnp.pad(c, ((0, bp - b), (0, 0)))
    tn = 512
    out = pl.pallas_call(
        _ada_kernel,
        grid=(depth, n6 // tn),
        in_specs=[
            pl.BlockSpec((bp, d), lambda l, j: (0, 0)),
            pl.BlockSpec((1, d, tn), lambda l, j: (l, 0, j)),
            pl.BlockSpec((1, 1, tn), lambda l, j: (l, 0, j)),
        ],
        out_specs=pl.BlockSpec((1, bp, tn), lambda l, j: (l, 0, j)),
        out_shape=jax.ShapeDtypeStruct((depth, bp, n6), F32),
        compiler_params=_cparams(("arbitrary", "arbitrary")),
        name="ada_mod",
    )(cp, w_ada, b_ada.reshape(depth, 1, n6))
    return out[:, :b, :]


def _layer_norm_rows(z, g, b):
    mu = jnp.mean(z, axis=-1, keepdims=True)
    zc = z - mu
    var = jnp.mean(zc * zc, axis=-1, keepdims=True)
    return zc * lax.rsqrt(var + LN_EPS) * g + b


def _ln_mod_kernel(x_ref, g_ref, b_ref, sc_ref, sh_ref, xo_ref, h_ref):
    y = _layer_norm_rows(x_ref[0], g_ref[...], b_ref[...])
    xo_ref[0] = y
    h_ref[0] = (y * (1.0 + sc_ref[0]) + sh_ref[0]).astype(BF16)


def _ln_mod(x, g, b, scale, shift):
    bsz, t, d = x.shape
    tr = 512
    row = pl.BlockSpec((1, tr, d), lambda i, j: (i, j, 0))
    vec = pl.BlockSpec((1, d), lambda i, j: (0, 0))
    mod = pl.BlockSpec((1, 1, d), lambda i, j: (i, 0, 0))
    return pl.pallas_call(
        _ln_mod_kernel,
        grid=(bsz, t // tr),
        in_specs=[row, vec, vec, mod, mod],
        out_specs=[row, row],
        out_shape=[jax.ShapeDtypeStruct((bsz, t, d), F32),
                   jax.ShapeDtypeStruct((bsz, t, d), BF16)],
        compiler_params=_cparams(("arbitrary", "arbitrary")),
        name="ln_mod",
    )(x, g.reshape(1, d), b.reshape(1, d), scale, shift)


def _in_proj_kernel(a_ref, w_ref, o_ref, w16_scr):
    @pl.when(pl.program_id(1) == 0)
    def _():
        w16_scr[...] = w_ref[0].astype(BF16)

    o_ref[...] = _dot(a_ref[...], w16_scr[...]).astype(o_ref.dtype)


def _in_proj(h, w_in, layer, out_dtype):
    n, d = h.shape
    nc = w_in.shape[2]
    tm, tn = 1024, 1408
    tm = min(tm, n)
    return pl.pallas_call(
        _in_proj_kernel,
        grid=(nc // tn, n // tm),
        in_specs=[pl.BlockSpec((tm, d), lambda j, i: (i, 0)),
                  pl.BlockSpec((1, d, tn), lambda j, i: (layer, 0, j))],
        out_specs=pl.BlockSpec((tm, tn), lambda j, i: (i, j)),
        out_shape=jax.ShapeDtypeStruct((n, nc), out_dtype),
        scratch_shapes=[pltpu.VMEM((d, tn), BF16)],
        compiler_params=_cparams(("arbitrary", "arbitrary"), VMEM_LIMIT),
        name="in_proj",
    )(h, w_in)


def _conv_kernel(cb_ref, cc_ref, ch_ref, w_ref, beta_ref, o_ref):
    u = cc_ref[0].astype(F32) * ch_ref[0].astype(F32)
    t = u.shape[0]
    row = lax.broadcasted_iota(I32, u.shape, 0)
    prev = jnp.where(row == 0, 0.0, pltpu.roll(u, 1, 0))
    nxt = jnp.where(row == t - 1, 0.0, pltpu.roll(u, t - 1, 0))
    w = w_ref[...]
    y = cb_ref[0].astype(F32) * (prev * w[0:1] + u * w[1:2] + nxt * w[2:3])
    o_ref[0] = (y * beta_ref[...]).astype(o_ref.dtype)


def _conv_mixer(proj, conv_w, beta):
    bsz, t, _ = proj.shape
    cw = 256
    nb = CONV_WIDTH // cw
    per = LANES * 1

    def col(off):
        return pl.BlockSpec((1, t, cw), lambda b, j, off=off: (b, 0, off * per // cw + j))

    return pl.pallas_call(
        _conv_kernel,
        grid=(bsz, nb),
        in_specs=[col(_CB), col(_CC), col(_CH),
                  pl.BlockSpec((3, cw), lambda b, j: (0, j)),
                  pl.BlockSpec((1, cw), lambda b, j: (0, j))],
        out_specs=pl.BlockSpec((1, t, cw), lambda b, j: (b, 0, j)),
        out_shape=jax.ShapeDtypeStruct((bsz, t, CONV_WIDTH), BF16),
        compiler_params=_cparams(("arbitrary", "arbitrary"), VMEM_LIMIT),
        name="conv_mixer",
    )(proj, proj, proj, conv_w, beta)


def _ret_kernel(q_ref, k_ref, v_ref, g_ref, cos_ref, sin_ref, lg_ref, beta_ref,
                o_ref, o_scr, qf_scr, qb_scr, uf_scr, ub_scr):
    L = SEQ_CHUNK
    t = q_ref.shape[1]
    n = t // L
    lgf = lg_ref[0, 0:1, :]
    lgb = lg_ref[0, 1:2, :]
    r = lax.broadcasted_iota(I32, (L, L), 0).astype(F32)
    c = lax.broadcasted_iota(I32, (L, L), 1).astype(F32)
    rel = r - c
    dbi = jnp.where(rel > 0, jnp.exp(lgf * jnp.maximum(rel, 0.0)),
                    jnp.where(rel < 0, jnp.exp(lgb * jnp.maximum(-rel, 0.0)), 2.0))
    qf = jnp.exp(lgf * (r + 1.0))
    kf = jnp.exp(lgf * (L - 1.0 - r))
    qb = jnp.exp(lgb * (L - r))
    kb = jnp.exp(lgb * r)
    gf_l = jnp.exp(lgf * float(L))
    gb_l = jnp.exp(lgb * float(L))
    scale = HEAD_DIM ** -0.5

    def rot(x, cs, sn):
        return x * cs + pltpu.roll(x, HEAD_DIM // 2, 1) * sn

    def rows(ci):
        return pl.ds(pl.multiple_of(ci * L, L), L)

    def local_body(ci, carry):
        sl = rows(ci)
        cs = cos_ref[sl, :]
        sn = sin_ref[sl, :]
        q = rot(q_ref[0, sl, :].astype(F32), cs, sn)
        k = rot(k_ref[0, sl, :].astype(F32), cs, sn) * scale
        v16 = v_ref[0, sl, :].astype(BF16)
        s = _dot_nt(q.astype(BF16), k.astype(BF16)) * dbi
        o_scr[sl, :] = _dot(s.astype(BF16), v16)
        qf_scr[sl, :] = (q * qf).astype(BF16)
        qb_scr[sl, :] = (q * qb).astype(BF16)
        uf_scr[ci] = _dot_tn((k * kf).astype(BF16), v16)
        ub_scr[ci] = _dot_tn((k * kb).astype(BF16), v16)
        return carry

    lax.fori_loop(0, n, local_body, 0, unroll=16)

    def scan_body(i, states):
        sf, sb = states
        cf, cb = i, n - 1 - i
        slf, slb = rows(cf), rows(cb)
        o_scr[slf, :] += _dot(qf_scr[slf, :], sf.astype(BF16))
        o_scr[slb, :] += _dot(qb_scr[slb, :], sb.astype(BF16))
        return gf_l * sf + uf_scr[cf], gb_l * sb + ub_scr[cb]

    zero = jnp.zeros((HEAD_DIM, HEAD_DIM), F32)
    lax.fori_loop(0, n, scan_body, (zero, zero), unroll=16)

    beta = beta_ref[...]

    def norm_body(ci, carry):
        sl = rows(ci)
        o = o_scr[sl, :]
        mu = jnp.mean(o, axis=-1, keepdims=True)
        oc = o - mu
        var = jnp.mean(oc * oc, axis=-1, keepdims=True)
        y = oc * lax.rsqrt(var + HEAD_EPS) * _silu(g_ref[0, sl, :].astype(F32)) * beta
        o_ref[0, sl, :] = y.astype(o_ref.dtype)
        return carry

    lax.fori_loop(0, n, norm_body, 0, unroll=4)


def _rotary_tables(t):
    half = HEAD_DIM // 2
    inv_freq = ROPE_BASE ** (-jnp.arange(half, dtype=F32) / half)
    ang = jnp.arange(t, dtype=F32)[:, None] * inv_freq[None, :]
    cos, sin = jnp.cos(ang), jnp.sin(ang)
    return jnp.concatenate([cos, cos], -1), jnp.concatenate([-sin, sin], -1)


def _ret_log_decays():
    head = jnp.arange(RET_HEADS, dtype=F32)
    lg_f = jnp.log1p(-jnp.exp2(-5.0 - head))
    lg_b = jnp.log1p(-jnp.exp2(-5.5 - head))
    lg = jnp.stack([lg_f, lg_b], axis=1)
    return jnp.broadcast_to(lg[:, :, None], (RET_HEADS, 2, LANES))


def _ret_mixer(proj, beta):
    bsz, t, _ = proj.shape
    n_chunks = t // SEQ_CHUNK
    assert n_chunks % 2 == 0
    cosf, sinf = _rotary_tables(t)

    def col(off):
        return pl.BlockSpec((1, t, HEAD_DIM), lambda b, h, off=off: (b, 0, off + h))

    tab = pl.BlockSpec((t, HEAD_DIM), lambda b, h: (0, 0))
    return pl.pallas_call(
        _ret_kernel,
        grid=(bsz, RET_HEADS),
        in_specs=[col(_RQ), col(_RK), col(_RV), col(_RG), tab, tab,
                  pl.BlockSpec((1, 2, LANES), lambda b, h: (h, 0, 0)),
                  pl.BlockSpec((1, HEAD_DIM), lambda b, h: (0, _OUT_RET + h))],
        out_specs=pl.BlockSpec((1, t, HEAD_DIM), lambda b, h: (b, 0, h)),
        out_shape=jax.ShapeDtypeStruct((bsz, t, RET_WIDTH), BF16),
        scratch_shapes=[pltpu.VMEM((t, HEAD_DIM), F32),
                        pltpu.VMEM((t, HEAD_DIM), BF16), pltpu.VMEM((t, HEAD_DIM), BF16),
                        pltpu.VMEM((n_chunks, HEAD_DIM, HEAD_DIM), F32),
                        pltpu.VMEM((n_chunks, HEAD_DIM, HEAD_DIM), F32)],
        compiler_params=_cparams(("arbitrary", "arbitrary"), VMEM_LIMIT),
        name="ret_mixer",
    )(proj, proj, proj, proj, cosf, sinf, _ret_log_decays(), beta)


_HG_BASE = SUBLANES
_HG_LEVELS = tuple(m for m in (8, 16, 32, 64) if m < SEQ_CHUNK)


def _cumsum_rows(x, reverse):
    n = x.shape[0]
    row = lax.broadcasted_iota(I32, x.shape, 0)
    sh = 1
    while sh < n:
        if reverse:
            x = x + jnp.where(row < n - sh, pltpu.roll(x, n - sh, 0), 0.0)
        else:
            x = x + jnp.where(row >= sh, pltpu.roll(x, sh, 0), 0.0)
        sh *= 2
    return x


def _hg_masks(reverse):
    C = SEQ_CHUNK
    r = lax.broadcasted_iota(I32, (C, C), 0)
    c = lax.broadcasted_iota(I32, (C, C), 1)
    if reverse:
        r, c = c, r
    sh = _HG_BASE.bit_length() - 1
    masks = [((r >> sh) == (c >> sh)) & (c <= r)]
    for m in _HG_LEVELS:
        sh = m.bit_length() - 1
        masks.append(((r >> (sh + 1)) == (c >> (sh + 1)))
                     & (((r >> sh) & 1) == 1) & (((c >> sh) & 1) == 0))
    return [jnp.where(m, 1.0, 0.0) for m in masks]


def _hg_scores(q, kk, cum, mask_ref, reverse):
    C = SEQ_CHUNK
    d = 1 if reverse else 0

    def split(x, blk):
        return x.reshape(C // blk, blk, LANES)

    b = _HG_BASE
    ref_row = b // 2 if reverse else b // 2 - 1
    cum3 = split(cum, b)
    rel = cum3 - cum3[:, ref_row:ref_row + 1, :]
    qt = (split(q, b) * jnp.exp(rel)).reshape(C, LANES).astype(BF16)
    kt = (split(kk, b) * jnp.exp(-rel)).reshape(C, LANES).astype(BF16)
    a = jnp.where(mask_ref[d, 0] > 0.5, _dot_nt(qt, kt), 0.0)
    for li, m in enumerate(_HG_LEVELS):
        blk = 2 * m
        ref_row = m if reverse else m - 1
        cum3 = split(cum, blk)
        rel = cum3 - cum3[:, ref_row:ref_row + 1, :]
        dec = jnp.exp(-jnp.abs(rel)).reshape(C, LANES)
        halves = []
        for j in range(C // m):
            is_query = (j % 2 == 1) != reverse
            halves.append((q if is_query else kk)[j * m:(j + 1) * m])
        x = (jnp.concatenate(halves, axis=0) * dec).astype(BF16)
        a = a + mask_ref[d, li + 1] * _dot_nt(x, x)
    return a


def _hg_kernel(layer, q_ref, zf_ref, zb_ref, v_ref, g_ref, lbl_ref, beta_ref, o_ref,
               o_scr, qe_scr, u_scr, dec_scr, mask_scr):
    C = SEQ_CHUNK
    t = q_ref.shape[1]
    n = t // C
    for d, reverse in enumerate((False, True)):
        for li, m in enumerate(_hg_masks(reverse)):
            mask_scr[d, li] = m
    logits = lbl_ref[...].astype(F32)
    e = jnp.exp(logits - jnp.max(logits, axis=0, keepdims=True))
    p = e / jnp.sum(e, axis=0, keepdims=True)
    lb = p[0]
    for l in range(1, layer + 1):
        lb = lb + p[l]
    lb = lb - p[0]

    def gates(z_ref, sl, lb_row):
        f = lb_row + (1.0 - lb_row) * jax.nn.sigmoid(z_ref[0, sl, :].astype(F32))
        f = jnp.maximum(f, FORGET_FLOOR)
        return jnp.log(f), 1.0 - f

    def rows(ci):
        return pl.ds(pl.multiple_of(ci * C, C), C)

    def local_dir(ci, sl, q, v16, z_ref, d):
        reverse = d == 1
        logf, kk = gates(z_ref, sl, lb[d:d + 1, :])
        cum = _cumsum_rows(logf, reverse)
        a = _hg_scores(q, kk, cum, mask_scr, reverse)
        total = cum[0:1, :] if reverse else cum[C - 1:C, :]
        qe_scr[d, sl, :] = (q * jnp.exp(cum)).astype(BF16)
        k_tail = kk * jnp.exp(total - cum)
        u_scr[d, ci] = _dot_tn(v16, k_tail.astype(BF16))
        dec_scr[d, ci] = jnp.broadcast_to(jnp.exp(total), (SUBLANES, LANES))
        return _dot(a.astype(BF16), v16)

    def local_body(ci, carry):
        sl = rows(ci)
        q = q_ref[0, sl, :].astype(F32)
        v16 = v_ref[0, sl, :].astype(BF16)
        o_scr[sl, :] = (local_dir(ci, sl, q, v16, zf_ref, 0)
                        + local_dir(ci, sl, q, v16, zb_ref, 1))
        return carry

    lax.fori_loop(0, n, local_body, 0, unroll=16)

    def scan_body(i, states):
        sf, sb = states
        cf, cb = i, n - 1 - i
        slf, slb = rows(cf), rows(cb)
        o_scr[slf, :] += _dot_nt(qe_scr[0, slf, :], sf.astype(BF16))
        o_scr[slb, :] += _dot_nt(qe_scr[1, slb, :], sb.astype(BF16))
        return (sf * dec_scr[0, cf, 0:1, :] + u_scr[0, cf],
                sb * dec_scr[1, cb, 0:1, :] + u_scr[1, cb])

    zero = jnp.zeros((HEAD_DIM, HEAD_DIM), F32)
    lax.fori_loop(0, n, scan_body, (zero, zero), unroll=16)

    beta = beta_ref[...]

    def norm_body(ci, carry):
        sl = rows(ci)
        o = o_scr[sl, :]
        ms = jnp.mean(o * o, axis=-1, keepdims=True)
        y = o * lax.rsqrt(ms + HEAD_EPS) * _silu(g_ref[0, sl, :].astype(F32)) * beta
        o_ref[0, sl, :] = y.astype(o_ref.dtype)
        return carry

    lax.fori_loop(0, n, norm_body, 0, unroll=16)


def _hg_mixer(proj, lb_logits, beta, layer):
    bsz, t, _ = proj.shape
    n_chunks = t // SEQ_CHUNK
    assert n_chunks % 2 == 0

    def col(off):
        return pl.BlockSpec((1, t, HEAD_DIM), lambda b, h, off=off: (b, 0, off + h))

    return pl.pallas_call(
        functools.partial(_hg_kernel, layer),
        grid=(bsz, HG_HEADS),
        in_specs=[col(_HQ), col(_HFF), col(_HFB), col(_HI), col(_HG),
                  pl.BlockSpec((DEPTH, 2, HEAD_DIM), lambda b, h: (0, 0, h)),
                  pl.BlockSpec((1, HEAD_DIM), lambda b, h: (0, _OUT_HG + h))],
        out_specs=pl.BlockSpec((1, t, HEAD_DIM), lambda b, h: (b, 0, h)),
        out_shape=jax.ShapeDtypeStruct((bsz, t, HG_WIDTH), BF16),
        scratch_shapes=[pltpu.VMEM((t, HEAD_DIM), F32),
                        pltpu.VMEM((2, t, HEAD_DIM), BF16),
                        pltpu.VMEM((2, n_chunks, HEAD_DIM, HEAD_DIM), F32),
                        pltpu.VMEM((2, n_chunks, SUBLANES, LANES), F32),
                        pltpu.VMEM((2, 1 + len(_HG_LEVELS), SEQ_CHUNK, SEQ_CHUNK), F32)],
        compiler_params=_cparams(("arbitrary", "arbitrary"), VMEM_LIMIT),
        name="hg_mixer",
    )(proj, proj, proj, proj, proj, lb_logits, beta)


_OUT_PROJ_SPLIT = 2


def _out_proj_kernel(yc_ref, yr_ref, yh_ref, w_ref, x_ref, gate_ref, g_ref, b_ref,
                     sc_ref, sh_ref, wrh_ref, wrl_ref, xo_ref, h_ref, lo_ref):
    c0, c1 = CONV_WIDTH, CONV_WIDTH + RET_WIDTH
    tm = x_ref.shape[1]
    for half in range(_OUT_PROJ_SPLIT):
        sl = pl.ds(half * (tm // _OUT_PROJ_SPLIT), tm // _OUT_PROJ_SPLIT)
        y = (_dot(yc_ref[0, sl, :], w_ref[0, 0:c0, :]) + _dot(yr_ref[0, sl, :], w_ref[0, c0:c1, :])
             + _dot(yh_ref[0, sl, :], w_ref[0, c1:, :]))
        z = DEEPNORM_ALPHA * x_ref[0, sl, :] + (1.0 + gate_ref[0]) * y
        xn = _layer_norm_rows(z, g_ref[...], b_ref[...])
        xo_ref[0, sl, :] = xn
        h = xn * (1.0 + sc_ref[0]) + sh_ref[0]
        h_ref[0, sl, :] = h
        h_hi = h.astype(BF16)
        h_lo = (h - h_hi.astype(F32)).astype(BF16)
        wrh = wrh_ref[...]
        lo_ref[0, sl, :] = _dot(h_hi, wrh) + _dot(h_lo, wrh) + _dot(h_hi, wrl_ref[...])


def _out_proj(yc, yr, yh, w_out16, layer, x, gate, ln_g, ln_b, scale2, shift2, wr_hi, wr_lo):
    bsz, t, d = x.shape
    tm = 512

    def row(width):
        return pl.BlockSpec((1, tm, width), lambda b, i: (b, i, 0))

    vec = pl.BlockSpec((1, d), lambda b, i: (0, 0))
    mod = pl.BlockSpec((1, 1, d), lambda b, i: (b, 0, 0))
    wr = pl.BlockSpec((d, LANES), lambda b, i: (0, 0))
    return pl.pallas_call(
        _out_proj_kernel,
        grid=(bsz, t // tm),
        in_specs=[row(CONV_WIDTH), row(RET_WIDTH), row(HG_WIDTH),
                  pl.BlockSpec((1, d, d), lambda b, i: (layer, 0, 0)),
                  row(d), mod, vec, vec, mod, mod, wr, wr],
        out_specs=[row(d), row(d), row(LANES)],
        out_shape=[jax.ShapeDtypeStruct((bsz, t, d), F32),
                   jax.ShapeDtypeStruct((bsz, t, d), F32),
                   jax.ShapeDtypeStruct((bsz, t, LANES), F32)],
        compiler_params=_cparams(("arbitrary", "arbitrary"), VMEM_LIMIT),
        name="out_proj",
    )(yc, yr, yh, w_out16, x, gate, ln_g.reshape(1, d), ln_b.reshape(1, d),
      scale2, shift2, wr_hi, wr_lo)


def _route_kernel(n_tiles, lt_ref, bias_ref, pos1_ref, pos2_ref, w1_ref, w2_ref, tile_ref):
    E = N_EXPERTS
    logits = [lt_ref[e] for e in range(E)]
    shape = logits[0].shape
    mx = functools.reduce(jnp.maximum, logits)
    ex = [jnp.exp(l - mx) for l in logits]
    den = functools.reduce(lambda a, b: a + b, ex)
    scores = [x / den for x in ex]
    sel = [scores[e] + bias_ref[e] for e in range(E)]

    best_g = jnp.zeros(shape, I32)
    best_v = None
    for g in range(N_GROUPS):
        a, b, c, d = sel[EXPERTS_PER_GROUP * g: EXPERTS_PER_GROUP * (g + 1)]
        hi1, lo1 = jnp.maximum(a, b), jnp.minimum(a, b)
        hi2, lo2 = jnp.maximum(c, d), jnp.minimum(c, d)
        gs = jnp.maximum(hi1, hi2) + jnp.maximum(jnp.minimum(hi1, hi2), jnp.maximum(lo1, lo2))
        if g == 0:
            best_v = gs
        else:
            better = gs > best_v
            best_g = jnp.where(better, g, best_g)
            best_v = jnp.where(better, gs, best_v)

    masked = [jnp.where(best_g == (e // EXPERTS_PER_GROUP), sel[e], MASK_NEG) for e in range(E)]

    def arg_top(vals, exclude):
        idx = jnp.zeros(shape, I32)
        val = None
        for e in range(E):
            v = vals[e] if exclude is None else jnp.where(exclude == e, -jnp.inf, vals[e])
            if e == 0:
                val = v
            else:
                better = v > val
                idx = jnp.where(better, e, idx)
                val = jnp.where(better, v, val)
        return idx

    idx1 = arg_top(masked, None)
    idx2 = arg_top(masked, idx1)
    s1 = functools.reduce(lambda a, b: a + b, [jnp.where(idx1 == e, scores[e], 0.0) for e in range(E)])
    s2 = functools.reduce(lambda a, b: a + b, [jnp.where(idx2 == e, scores[e], 0.0) for e in range(E)])
    w1_ref[...] = s1 / (s1 + s2)
    w2_ref[...] = s2 / (s1 + s2)

    rows = shape[0]
    li = lax.broadcasted_iota(I32, (LANES, LANES), 0)
    lj = lax.broadcasted_iota(I32, (LANES, LANES), 1)
    upper = (li <= lj).astype(BF16)
    ri = lax.broadcasted_iota(I32, (rows, rows), 0)
    rj = lax.broadcasted_iota(I32, (rows, rows), 1)
    lower = (rj < ri).astype(BF16)
    tile_start = (lax.broadcasted_iota(I32, (1, LANES), 1) * MOE_TILE).astype(F32)
    start = jnp.zeros((1, 1), F32)
    pos1 = jnp.zeros(shape, F32)
    pos2 = jnp.zeros(shape, F32)
    tile_e = jnp.zeros((1, LANES), F32)
    for e in range(E):
        hit1 = idx1 == e
        hit2 = idx2 == e
        onehot = jnp.where(hit1 | hit2, 1.0, 0.0)
        pref = _dot(onehot.astype(BF16), upper)
        row_tot = pref[:, LANES - 1:LANES]
        row_off = _dot(lower, jnp.broadcast_to(row_tot, shape).astype(BF16))[:, 0:1]
        rank = pref - onehot + row_off
        count = jnp.sum(onehot, keepdims=True)
        dest = start + rank
        pos1 = jnp.where(hit1, dest, pos1)
        pos2 = jnp.where(hit2, dest, pos2)
        start = start + jnp.ceil(count / MOE_TILE) * MOE_TILE
        tile_e = tile_e + jnp.where(tile_start >= start, 1.0, 0.0)
    pos1_ref[...] = pos1.astype(I32)
    pos2_ref[...] = pos2.astype(I32)
    n_valid = start / MOE_TILE
    tile_id = lax.broadcasted_iota(I32, (1, LANES), 1).astype(F32)
    last_e = jnp.sum(jnp.where(tile_id == n_valid - 1.0, tile_e, 0.0), keepdims=True)
    tile_e = jnp.where(tile_id < n_valid, tile_e, last_e)
    sub = lax.broadcasted_iota(I32, (SUBLANES, LANES), 0)
    tile_ref[...] = jnp.where(sub == 0, tile_e, jnp.broadcast_to(n_valid, (SUBLANES, LANES))).astype(I32)


def _route(logits_t, router_bias, n_tiles):
    e, rows, lanes = logits_t.shape
    tok = jax.ShapeDtypeStruct((rows, lanes), I32)
    tokf = jax.ShapeDtypeStruct((rows, lanes), F32)
    full = pl.BlockSpec((rows, lanes), lambda i: (0, 0))
    return pl.pallas_call(
        functools.partial(_route_kernel, n_tiles),
        grid=(1,),
        in_specs=[pl.BlockSpec((e, rows, lanes), lambda i: (0, 0, 0)),
                  pl.BlockSpec(memory_space=pltpu.SMEM)],
        out_specs=[full, full, full, full, pl.BlockSpec((SUBLANES, LANES), lambda i: (0, 0))],
        out_shape=[tok, tok, tokf, tokf, jax.ShapeDtypeStruct((SUBLANES, LANES), I32)],
        compiler_params=_cparams(("arbitrary",)),
        name="route",
    )(logits_t, router_bias)


_ROWS_PER_STEP = 512


_DMA_UNROLL = 8


def _scatter_rows_kernel(n_tiles, p1_ref, p2_ref, fill_ref, src_ref, dst_ref, zero_scr, sem, zsem):
    def zero_copy(i):
        row0 = pl.multiple_of(i * MOE_TILE, MOE_TILE)
        return pltpu.make_async_copy(zero_scr, dst_ref.at[pl.ds(row0, MOE_TILE)], zsem)

    @pl.when(pl.program_id(0) == 0)
    def _():
        zero_scr[...] = jnp.zeros_like(zero_scr)

        def zstart(i, carry):
            @pl.when(fill_ref[i] == 1)
            def _():
                zero_copy(i).start()
            return carry

        def zwait(i, carry):
            @pl.when(fill_ref[i] == 1)
            def _():
                zero_copy(i).wait()
            return carry

        lax.fori_loop(0, n_tiles, zstart, 0)
        lax.fori_loop(0, n_tiles, zwait, 0)

    def copies(r):
        row = src_ref.at[pl.ds(r, 1)]
        return (pltpu.make_async_copy(row, dst_ref.at[pl.ds(p1_ref[0, 0, r], 1)], sem.at[0]),
                pltpu.make_async_copy(row, dst_ref.at[pl.ds(p2_ref[0, 0, r], 1)], sem.at[1]))

    def start(r, carry):
        for queue, cp in enumerate(copies(r)):
            cp.start(priority=queue)
        return carry

    def wait(r, carry):
        for cp in copies(r):
            cp.wait()
        return carry

    lax.fori_loop(0, _ROWS_PER_STEP, start, 0, unroll=_DMA_UNROLL)
    lax.fori_loop(0, _ROWS_PER_STEP, wait, 0, unroll=_DMA_UNROLL)


def _scatter_rows(src, pos1, pos2, fill_tile, n_tiles):
    n, d = src.shape
    steps = n // _ROWS_PER_STEP
    idx = pl.BlockSpec((1, 1, _ROWS_PER_STEP), lambda i: (i, 0, 0), memory_space=pltpu.SMEM)
    return pl.pallas_call(
        functools.partial(_scatter_rows_kernel, n_tiles),
        grid=(steps,),
        in_specs=[idx, idx, pl.BlockSpec(memory_space=pltpu.SMEM),
                  pl.BlockSpec((_ROWS_PER_STEP, d), lambda i: (i, 0))],
        out_specs=pl.BlockSpec(memory_space=pl.ANY),
        out_shape=jax.ShapeDtypeStruct((n_tiles * MOE_TILE, d), src.dtype),
        scratch_shapes=[pltpu.VMEM((MOE_TILE, d), src.dtype),
                        pltpu.SemaphoreType.DMA((2,)), pltpu.SemaphoreType.DMA],
        compiler_params=_cparams(("arbitrary",), VMEM_LIMIT),
        name="scatter_rows",
    )(pos1.reshape(steps, 1, _ROWS_PER_STEP), pos2.reshape(steps, 1, _ROWS_PER_STEP), fill_tile, src)


_W_CHUNK = 128
_W_RING = 13


def _moe_kernel(layer, te_ref, nv_ref, first_ref, slot_ref, nxt_ref,
                x_ref, wg_hbm, wu_hbm, wd_hbm, y_ref,
                cg_scr, cu_scr, cd_scr, stage_scr, sem, cnt_ref):
    i = pl.program_id(0)
    d, f = cg_scr.shape[1], cg_scr.shape[2]
    n_g = d // _W_CHUNK
    n_chunks = 2 * n_g + f // _W_CHUNK

    def for_chunk(c, e, slot, fn):
        k = c % _W_RING

        @pl.when(c < n_g)
        def _():
            row = pl.multiple_of(c * _W_CHUNK, _W_CHUNK)
            fn(wg_hbm.at[layer, e, pl.ds(row, _W_CHUNK), :], stage_scr.at[k, :, pl.ds(0, f)],
               sem.at[k], cg_scr.at[slot, pl.ds(row, _W_CHUNK), :])

        @pl.when((c >= n_g) & (c < 2 * n_g))
        def _():
            row = pl.multiple_of((c - n_g) * _W_CHUNK, _W_CHUNK)
            fn(wu_hbm.at[layer, e, pl.ds(row, _W_CHUNK), :], stage_scr.at[k, :, pl.ds(0, f)],
               sem.at[k], cu_scr.at[slot, pl.ds(row, _W_CHUNK), :])

        @pl.when(c >= 2 * n_g)
        def _():
            row = pl.multiple_of((c - 2 * n_g) * _W_CHUNK, _W_CHUNK)
            fn(wd_hbm.at[layer, e, pl.ds(row, _W_CHUNK), :], stage_scr.at[k],
               sem.at[k], cd_scr.at[slot, pl.ds(row, _W_CHUNK), :])

    def start(src, stage, s, dst):
        pltpu.make_async_copy(src, stage, s).start()

    def finish(src, stage, s, dst):
        pltpu.make_async_copy(src, stage, s).wait()
        dst[...] = stage[...].astype(BF16)

    def start_upto(e, slot, hi):
        hi = jnp.minimum(hi, n_chunks)

        def body(c, carry):
            for_chunk(c, e, slot, start)
            return carry

        lax.fori_loop(cnt_ref[0], hi, body, 0)
        cnt_ref[0] = jnp.maximum(cnt_ref[0], hi)

    def finish_started(e, slot):
        def body(c, carry):
            for_chunk(c, e, slot, finish)
            return carry

        lax.fori_loop(cnt_ref[1], cnt_ref[0], body, 0)
        cnt_ref[1] = cnt_ref[0]

    valid = i < nv_ref[0]

    @pl.when(i == 0)
    def _():
        cnt_ref[0] = 0
        cnt_ref[1] = 0

    @pl.when(valid & (first_ref[i] == 1))
    def _():
        e, slot = te_ref[i], slot_ref[i]

        def body(c, carry):
            start_upto(e, slot, c + _W_RING)
            for_chunk(c, e, slot, finish)
            return carry

        lax.fori_loop(cnt_ref[1], n_chunks, body, 0)
        cnt_ref[0] = 0
        cnt_ref[1] = 0

    @pl.when(valid & (nxt_ref[i] >= 0))
    def _():
        e, slot = nxt_ref[i], 1 - slot_ref[i]
        finish_started(e, slot)
        start_upto(e, slot, cnt_ref[0] + _W_RING)

    @pl.when(valid)
    def _():
        slot = slot_ref[i]
        x = x_ref[...].astype(BF16)
        a = _dot(x, cg_scr[slot])
        b = _dot(x, cu_scr[slot])
        hidden = (_silu(a) * b).astype(BF16)
        y_ref[...] = _dot(hidden, cd_scr[slot])

    @pl.when(jnp.logical_not(valid))
    def _():
        y_ref[...] = jnp.zeros_like(y_ref)


def _moe_schedule(tile_e, n_valid):
    n_tiles = tile_e.shape[0]
    idx = jnp.arange(n_tiles, dtype=I32)
    valid = idx < n_valid[0]
    first = valid & ((idx == 0) | (tile_e != jnp.roll(tile_e, 1)))
    run = jnp.cumsum(first.astype(I32)) - 1
    next_first = lax.cummin(jnp.where(first, idx, n_tiles), reverse=True)
    after = jnp.concatenate([next_first[1:], jnp.full((1,), n_tiles, I32)])
    nxt = jnp.where(valid & (after < n_tiles), tile_e[jnp.minimum(after, n_tiles - 1)], -1)
    last = valid & ((after == idx + 1) | (idx == n_valid[0] - 1))
    fill = last | jnp.logical_not(valid)
    return first.astype(I32), (run % 2).astype(I32), nxt.astype(I32), fill.astype(I32)


def _moe_experts(xs, tile_e, n_valid, schedule, w_gate, w_up, w_down, layer):
    n_rows, d = xs.shape
    f = w_gate.shape[3]
    tm = MOE_TILE
    n_tiles = n_rows // tm
    assert d % _W_CHUNK == 0 and f % _W_CHUNK == 0
    first, slot, nxt = schedule

    def xmap(i, te, nv, *_):
        return (jnp.minimum(i, nv[0] - 1), 0)

    any_spec = pl.BlockSpec(memory_space=pl.ANY)
    grid_spec = pltpu.PrefetchScalarGridSpec(
        num_scalar_prefetch=5,
        grid=(n_tiles,),
        in_specs=[pl.BlockSpec((tm, d), xmap), any_spec, any_spec, any_spec],
        out_specs=pl.BlockSpec((tm, d), lambda i, *_: (i, 0)),
        scratch_shapes=[pltpu.VMEM((2, d, f), BF16), pltpu.VMEM((2, d, f), BF16),
                        pltpu.VMEM((2, f, d), BF16),
                        pltpu.VMEM((_W_RING, _W_CHUNK, d), F32),
                        pltpu.SemaphoreType.DMA((_W_RING,)),
                        pltpu.SMEM((2,), I32)],
    )
    return pl.pallas_call(
        functools.partial(_moe_kernel, layer),
        grid_spec=grid_spec,
        out_shape=jax.ShapeDtypeStruct((n_rows, d), F32),
        compiler_params=_cparams(("arbitrary",), MOE_VMEM_LIMIT),
        name="moe_experts",
    )(tile_e, n_valid, first, slot, nxt, xs, w_gate, w_up, w_down)


_COMBINE_ROWS = 256


def _combine_kernel(with_next, steps, p1_ref, p2_ref, p1n_ref, p2n_ref, ys_ref, w1_ref, w2_ref,
                    x_ref, gate_ref, g_ref, b_ref, *rest):
    if with_next:
        sc_ref, sh_ref, xo_ref, h_ref, ya_scr, yb_scr, sem = rest
    else:
        xo_ref, ya_scr, yb_scr, sem = rest
    step = pl.program_id(0)

    def copies(pa, pb, buf, s, r):
        return (pltpu.make_async_copy(ys_ref.at[pl.ds(pa[0, 0, r], 1)],
                                      buf.at[0, pl.ds(r, 1)], sem.at[s, 0]),
                pltpu.make_async_copy(ys_ref.at[pl.ds(pb[0, 0, r], 1)],
                                      buf.at[1, pl.ds(r, 1)], sem.at[s, 1]))

    def wait_all(pa, pb, buf, s):
        def body(r, carry):
            for cp in copies(pa, pb, buf, s, r):
                cp.wait()
            return carry
        lax.fori_loop(0, _COMBINE_ROWS, body, 0, unroll=_DMA_UNROLL)

    @pl.when(step == 0)
    def _():
        def body(r, carry):
            for cp in copies(p1_ref, p2_ref, ya_scr, 0, r):
                cp.start()
            return carry
        lax.fori_loop(0, _COMBINE_ROWS, body, 0, unroll=_DMA_UNROLL)

    for par, (cur, nxt) in enumerate(((ya_scr, yb_scr), (yb_scr, ya_scr))):
        @pl.when(step % 2 == par)
        def _(par=par, cur=cur, nxt=nxt):
            wait_all(p1_ref, p2_ref, cur, par)
            for r in range(_COMBINE_ROWS):
                for cp in copies(p1n_ref, p2n_ref, nxt, 1 - par, r):
                    cp.start()
            y = w1_ref[0] * cur[0] + w2_ref[0] * cur[1]
            z = DEEPNORM_ALPHA * x_ref[0] + (1.0 + gate_ref[0]) * y
            xn = _layer_norm_rows(z, g_ref[...], b_ref[...])
            xo_ref[0] = xn
            if with_next:
                h_ref[0] = (xn * (1.0 + sc_ref[0]) + sh_ref[0]).astype(BF16)

            @pl.when(step == steps - 1)
            def _():
                wait_all(p1n_ref, p2n_ref, nxt, 1 - par)


def _combine(ys, pos1, pos2, w1, w2, x, gate, ln_g, ln_b, next_mod):
    bsz, t, d = x.shape
    tr = _COMBINE_ROWS
    per_b = t // tr
    steps = bsz * per_b
    row = pl.BlockSpec((1, tr, d), lambda s: (s // per_b, s % per_b, 0))
    vec = pl.BlockSpec((1, d), lambda s: (0, 0))
    mod = pl.BlockSpec((1, 1, d), lambda s: (s // per_b, 0, 0))
    wcol = pl.BlockSpec((1, tr, 1), lambda s: (s // per_b, s % per_b, 0))
    idx = pl.BlockSpec((1, 1, tr), lambda s: (s, 0, 0), memory_space=pltpu.SMEM)
    idx_next = pl.BlockSpec((1, 1, tr), lambda s: (jnp.minimum(s + 1, steps - 1), 0, 0),
                            memory_space=pltpu.SMEM)
    with_next = next_mod is not None
    p1 = pos1.reshape(steps, 1, tr)
    p2 = pos2.reshape(steps, 1, tr)
    in_specs = [idx, idx, idx_next, idx_next, pl.BlockSpec(memory_space=pl.ANY),
                wcol, wcol, row, mod, vec, vec]
    args = [p1, p2, p1, p2, ys, w1.reshape(bsz, t, 1), w2.reshape(bsz, t, 1), x, gate,
            ln_g.reshape(1, d), ln_b.reshape(1, d)]
    out_specs = [row]
    out_shape = [jax.ShapeDtypeStruct((bsz, t, d), F32)]
    if with_next:
        in_specs += [mod, mod]
        args += list(next_mod)
        out_specs.append(row)
        out_shape.append(jax.ShapeDtypeStruct((bsz, t, d), BF16))
    res = pl.pallas_call(
        functools.partial(_combine_kernel, with_next, steps),
        grid=(steps,),
        in_specs=in_specs,
        out_specs=out_specs,
        out_shape=out_shape,
        scratch_shapes=[pltpu.VMEM((2, tr, d), F32), pltpu.VMEM((2, tr, d), F32),
                        pltpu.SemaphoreType.DMA((2, 2))],
        compiler_params=_cparams(("arbitrary",), VMEM_LIMIT),
        name="combine_ln",
    )(*args)
    return res if with_next else (res[0], None)


def kernel(x, c, emb_ln_g, emb_ln_b, w_ada, b_ada, w_in, conv_w, mix_beta, w_out, hg_lb_logits,
           ln_g, ln_b, w_router, router_bias, w_gate, w_up, w_down):
    bsz, t, d = x.shape
    n = bsz * t
    assert n % LANES == 0 and t % SEQ_CHUNK == 0
    n_tiles = 2 * n // MOE_TILE + N_EXPERTS
    assert n_tiles <= LANES

    mod = _ada_mod(c, w_ada, b_ada)
    mod = mod.reshape(DEPTH, bsz, 6, 1, d)

    def mods(l):
        return [mod[l, :, i] for i in range(6)]

    wr_hi = jnp.pad(w_router, ((0, 0), (0, LANES - N_EXPERTS)))
    wr_hi16 = wr_hi.astype(BF16)
    wr_lo16 = (wr_hi - wr_hi16.astype(F32)).astype(BF16)

    w_out16 = w_out.astype(BF16)

    m = [mods(l) for l in range(DEPTH)]
    xcur, h = _ln_mod(x, emb_ln_g, emb_ln_b, m[0][1], m[0][0])
    for l in range(DEPTH):
        _, _, gate1, shift2, scale2, gate2 = m[l]
        beta = mix_beta[l].reshape(1, -1)
        proj = _in_proj(h.reshape(n, d), w_in, l, BF16).reshape(bsz, t, IN_COLS)
        yc = _conv_mixer(proj, conv_w[l], beta)
        yr = _ret_mixer(proj, beta)
        yh = _hg_mixer(proj, hg_lb_logits, beta, l)
        x1, h2, logits = _out_proj(yc, yr, yh, w_out16, l, xcur, gate1,
                                   ln_g[l, 0], ln_b[l, 0], scale2, shift2, wr_hi16, wr_lo16)
        logits_t = logits.reshape(n, LANES)[:, :N_EXPERTS].T.reshape(N_EXPERTS, n // LANES, LANES)
        pos1, pos2, w1, w2, tiles = _route(logits_t, router_bias, n_tiles)
        tile_e, n_valid = tiles[0, :n_tiles], tiles[1, :1]
        first, slot, nxt, fill = _moe_schedule(tile_e, n_valid)
        xs = _scatter_rows(h2.reshape(n, d), pos1.reshape(n), pos2.reshape(n), fill, n_tiles)
        ys = _moe_experts(xs, tile_e, n_valid, (first, slot, nxt), w_gate, w_up, w_down, l)
        next_mod = (m[l + 1][1], m[l + 1][0]) if l + 1 < DEPTH else None
        xcur, h = _combine(ys, pos1.reshape(n), pos2.reshape(n), w1, w2, x1, gate2,
                           ln_g[l, 1], ln_b[l, 1], next_mod)
    return xcur
```
